```python
import math
import jax, jax.numpy as jnp
from jax import lax
import numpy as np

D_MODEL = 1024
BATCH = 2
SEQ = 8192
DEPTH = 2

GRID_W = 64
CTX_LEN = 256
EPS = 1e-6

D_MIX = D_MODEL
D_GROUP = D_MIX // 4
CONV_A = 31
CHUNK_B = 128
HEADS_B = 4
HEAD_DIM_C = 64
HEADS_C = D_GROUP // HEAD_DIM_C
GROUPS_C = 2
STATE_C = 128
CONV_C = 5
CHUNK_C = 128
HEAD_DIM_D = 64
HEADS_D = D_GROUP // HEAD_DIM_D
LORA_W = 64
LORA_A = 64
LORA_G = 128
GN_EPS_D = 64e-5
N_KEYS = 128
N_EXPERTS = N_KEYS * N_KEYS
PEER_HEADS = 8
PEER_DK = 256
PEER_TOPK = 16
PEER_BLOCK = 128

C_XBC = D_GROUP + 2 * GROUPS_C * STATE_C
A_COLS = 2 * D_GROUP
B_COLS = 2 * D_GROUP
C_COLS = D_GROUP + C_XBC + 2 * HEADS_C
D_COLS = 3 * D_GROUP + 2 * LORA_W + 2 * LORA_A + LORA_G
OFF_B = A_COLS
OFF_C = OFF_B + B_COLS
OFF_D = OFF_C + C_COLS
IN_COLS = OFF_D + D_COLS

kernel_name = "hybrid_headgroup_diffusion_block"


def rmsnorm(x, g):
    xf = x.astype(jnp.float32)
    y = xf * lax.rsqrt(jnp.mean(xf * xf, -1, keepdims=True) + EPS)
    return (y * g).astype(x.dtype)


def layernorm(x, g, b, eps=1e-5):
    xf = x.astype(jnp.float32)
    mu = jnp.mean(xf, -1, keepdims=True)
    var = jnp.mean(jnp.square(xf - mu), -1, keepdims=True)
    return ((xf - mu) * lax.rsqrt(var + eps) * g + b).astype(x.dtype)


def modulate(h, shift, scale):
    return h * (1 + scale) + shift


def depthwise_conv(x, w, b):
    k = w.shape[0]
    y = lax.conv_general_dilated(x, w[:, None, :].astype(x.dtype), window_strides=(1,),
                                 padding=[((k - 1) // 2, (k - 1) // 2)],
                                 dimension_numbers=('NWC', 'WIO', 'NWC'),
                                 feature_group_count=x.shape[-1])
    return y + b.astype(x.dtype)


def sincos_2d(t_len):
    rows = t_len // GRID_W
    row = jnp.repeat(jnp.arange(rows), GRID_W).astype(jnp.float32)
    col = jnp.tile(jnp.arange(GRID_W), rows).astype(jnp.float32)
    q = D_MODEL // 4
    freq = 10000.0 ** (-jnp.arange(q, dtype=jnp.float32) / q)
    ar = row[:, None] * freq
    ac = col[:, None] * freq
    return jnp.concatenate([jnp.sin(ar), jnp.cos(ar), jnp.sin(ac), jnp.cos(ac)], -1)


def mixer_conformer(p, conv_w, conv_b, ln_g, ln_b):
    hg = p[..., :D_GROUP] * jax.nn.sigmoid(p[..., D_GROUP:])
    hg = depthwise_conv(hg, conv_w, conv_b)
    return jax.nn.silu(layernorm(hg, ln_g, ln_b))


def mixer_chunk_mlp(p, ln_g, ln_b, w_s, b_s):
    bsz, t_len, _ = p.shape
    u, v = p[..., :D_GROUP], p[..., D_GROUP:]
    v = layernorm(v, ln_g, ln_b)
    v = v.reshape(bsz, t_len // CHUNK_B, CHUNK_B, HEADS_B, D_GROUP // HEADS_B)
    mixed = jnp.einsum('hij,bcjhd->bcihd', w_s, v) + b_s.T[None, None, :, :, None]
    return u * mixed.reshape(bsz, t_len, D_GROUP)


def segsum_exp(a):
    l = a.shape[-1]
    xe = jnp.broadcast_to(a[..., :, None], a.shape + (l,))
    strict = jnp.tril(jnp.ones((l, l), bool), -1)
    ss = jnp.cumsum(jnp.where(strict, xe, 0.0), axis=-2)
    return jnp.where(jnp.tril(jnp.ones((l, l), bool)), jnp.exp(ss), 0.0)


def ssd_scan(xdt, da, bm, cm, init, emit):
    b, t_len, h, pd = xdt.shape
    nc = t_len // CHUNK_C
    xc = xdt.reshape(b, nc, CHUNK_C, h, pd)
    bc = bm.reshape(b, nc, CHUNK_C, h, -1)
    cc = cm.reshape(b, nc, CHUNK_C, h, -1)
    ac = da.reshape(b, nc, CHUNK_C, h).transpose(0, 3, 1, 2).astype(jnp.float32)
    a_cs = jnp.cumsum(ac, -1)
    decay_states = jnp.exp(a_cs[..., -1:] - a_cs)
    states = jnp.einsum('bclhn,bhcl,bclhp->bchpn', bc, decay_states, xc)
    states = jnp.concatenate([init[:, None].astype(states.dtype), states], 1)
    decay_chunk = segsum_exp(jnp.pad(a_cs[..., -1], ((0, 0), (0, 0), (1, 0))))
    new_states = jnp.einsum('bhzc,bchpn->bzhpn', decay_chunk, states)
    final = new_states[:, -1]
    if not emit:
        return None, final
    lmat = segsum_exp(ac)
    y_diag = jnp.einsum('bclhn,bcshn,bhcls,bcshp->bclhp', cc, bc, lmat, xc)
    y_off = jnp.einsum('bclhn,bchpn,bhcl->bclhp', cc, new_states[:, :-1], jnp.exp(a_cs))
    return (y_diag + y_off).reshape(b, t_len, h, pd), final


def mamba_prepare(p, conv_w, conv_b):
    b, t_len, _ = p.shape
    z = p[..., :D_GROUP]
    xbc = jax.nn.silu(depthwise_conv(p[..., D_GROUP:D_GROUP + C_XBC], conv_w, conv_b))
    dt_raw = p[..., D_GROUP + C_XBC:].reshape(b, t_len, 2, HEADS_C)
    xs = xbc[..., :D_GROUP].reshape(b, t_len, HEADS_C, HEAD_DIM_C)
    gs = GROUPS_C * STATE_C
    bm = xbc[..., D_GROUP:D_GROUP + gs].reshape(b, t_len, GROUPS_C, STATE_C)
    cm = xbc[..., D_GROUP + gs:].reshape(b, t_len, GROUPS_C, STATE_C)
    rep = HEADS_C // GROUPS_C
    return z, xs, jnp.repeat(bm, rep, 2), jnp.repeat(cm, rep, 2), dt_raw


def mamba_output(prep, y, d_skip, norm_g):
    z, xs, _, _, _ = prep
    b, t_len = xs.shape[:2]
    y = (y + d_skip[:, None] * xs).reshape(b, t_len, D_GROUP)
    return rmsnorm(y * jax.nn.silu(z), norm_g)


def mixer_mamba(pl, pc, conv_w, conv_b, dt_bias, a_log, d_skip, norm_g, emit_ctx):
    lat = mamba_prepare(pl, conv_w, conv_b)
    cx = mamba_prepare(pc, conv_w, conv_b)
    init = jnp.zeros((pl.shape[0], HEADS_C, HEAD_DIM_C, STATE_C), jnp.float32)
    yl, yc = [], []
    for d in range(2):
        flip = (lambda t: jnp.flip(t, 1)) if d == 1 else (lambda t: t)
        a_d = -jnp.exp(a_log[d])

        def ssd_args(prep):
            _, xs, bm, cm, dt_raw = prep
            dt = jax.nn.softplus(dt_raw[:, :, d].astype(jnp.float32) + dt_bias[d])
            return flip(xs * dt[..., None]), flip(dt * a_d), flip(bm), flip(cm)

        y_c, s_ctx = ssd_scan(*ssd_args(cx), init, emit_ctx)
        y_l, _ = ssd_scan(*ssd_args(lat), s_ctx, True)
        yl.append(flip(y_l))
        if emit_ctx:
            yc.append(flip(y_c))
    out_l = mamba_output(lat, yl[0] + yl[1], d_skip, norm_g)
    out_c = mamba_output(cx, yc[0] + yc[1], d_skip, norm_g) if emit_ctx else None
    return out_l, out_c


def shift_lerp(p, mu_prev, mu_next):
    prev = jnp.pad(p, ((0, 0), (1, 0), (0, 0)))[:, :-1]
    nxt = jnp.pad(p, ((0, 0), (0, 1), (0, 0)))[:, 1:]
    return p + mu_prev * (prev - p) + mu_next * (nxt - p)


def rwkv_prepare(p, mu_prev, mu_next, w0, w2, a0, a2, k_k, k_a):
    p = shift_lerp(p, mu_prev, mu_next)
    b, t_len, _ = p.shape
    heads = lambda t: t.reshape(t.shape[:-1] + (HEADS_D, HEAD_DIM_D))
    r = p[..., :D_GROUP]
    k = p[..., D_GROUP:2 * D_GROUP]
    v = p[..., 2 * D_GROUP:3 * D_GROUP]
    o = 3 * D_GROUP
    wd = p[..., o:o + 2 * LORA_W].reshape(b, t_len, 2, LORA_W)
    o += 2 * LORA_W
    ad = p[..., o:o + 2 * LORA_A].reshape(b, t_len, 2, LORA_A)
    o += 2 * LORA_A
    gd = p[..., o:]
    w = w0 + jnp.einsum('btdr,drc->btdc', jnp.tanh(wd), w2)
    decay = jnp.exp(-jnp.exp(-jax.nn.softplus(-w.astype(jnp.float32)) - 0.5))
    a = jax.nn.sigmoid(a0 + jnp.einsum('btdr,drc->btdc', ad, a2))
    kk = heads(k * k_k).astype(jnp.float32)
    kk = kk * lax.rsqrt(jnp.sum(kk * kk, -1, keepdims=True) + 1e-12)
    kd = k[:, :, None] * (1 + (a - 1) * k_a)
    return heads(r), heads(v), heads(decay), heads(a), kk, heads(kd), gd


def rwkv_scan(prep, d, init, reverse, emit):
    r, v, decay, a, kk, kd, _ = prep
    xs = (decay[:, :, d], kk, kk * a[:, :, d], kd[:, :, d], v) + ((r,) if emit else ())
    xs = tuple(jnp.moveaxis(t, 1, 0) for t in xs)

    def step(s, inp):
        w_t, kk_t, kka_t, k_t, v_t = inp[:5]
        s = (s * w_t[:, :, None, :]
             - jnp.einsum('bhvk,bhk->bhv', s, kk_t)[..., None] * kka_t[:, :, None, :]
             + v_t[..., None] * k_t[:, :, None, :])
        y = jnp.einsum('bhvk,bhk->bhv', s, inp[5]) if emit else None
        return s, y

    s, ys = lax.scan(step, init, xs, reverse=reverse)
    return s, (jnp.moveaxis(ys, 0, 1) if emit else None)


def rwkv_output(prep, y, g2, r_k, gn_g, gn_b):
    r, v, _, _, _, kd, gd = prep
    b, t_len = r.shape[:2]
    mu = jnp.mean(y, -1, keepdims=True)
    var = jnp.mean(jnp.square(y - mu), -1, keepdims=True)
    yn = ((y - mu) * lax.rsqrt(var + GN_EPS_D)).reshape(b, t_len, D_GROUP) * gn_g + gn_b
    bonus = jnp.sum(r[:, :, None] * kd * r_k, axis=(2, 4))[..., None] * v
    g = jax.nn.sigmoid(gd) @ g2
    return (yn + bonus.reshape(b, t_len, D_GROUP)) * g


def mixer_rwkv(pl, pc, mu_prev, mu_next, w0, w2, a0, a2, g2, k_k, k_a, r_k, gn_g, gn_b, emit_ctx):
    lat = rwkv_prepare(pl, mu_prev, mu_next, w0, w2, a0, a2, k_k, k_a)
    cx = rwkv_prepare(pc, mu_prev, mu_next, w0, w2, a0, a2, k_k, k_a)
    init = jnp.zeros((pl.shape[0], HEADS_D, HEAD_DIM_D, HEAD_DIM_D), jnp.float32)
    yl, yc = [], []
    for d in range(2):
        s_ctx, y_c = rwkv_scan(cx, d, init, d == 1, emit_ctx)
        _, y_l = rwkv_scan(lat, d, s_ctx, d == 1, True)
        yl.append(y_l)
        if emit_ctx:
            yc.append(y_c)
    out_l = rwkv_output(lat, yl[0] + yl[1], g2, r_k, gn_g, gn_b)
    out_c = rwkv_output(cx, yc[0] + yc[1], g2, r_k, gn_g, gn_b) if emit_ctx else None
    return out_l, out_c


def peer_ffn(f, wq, k1, k2, u_tab, v_tab):
    b, t_len, _ = f.shape
    blocks = f.reshape(-1, PEER_BLOCK, D_MODEL)
    nc = PEER_TOPK * PEER_TOPK

    def block(fb):
        q = (fb @ wq).reshape(PEER_BLOCK, PEER_HEADS, 2, PEER_DK // 2)
        s1 = jnp.einsum('thd,kd->thk', q[:, :, 0], k1).astype(jnp.float32)
        s2 = jnp.einsum('thd,kd->thk', q[:, :, 1], k2).astype(jnp.float32)
        v1, i1 = lax.top_k(s1, PEER_TOPK)
        v2, i2 = lax.top_k(s2, PEER_TOPK)
        cand = (v1[..., :, None] + v2[..., None, :]).reshape(PEER_BLOCK, PEER_HEADS, nc)
        cand_idx = (i1[..., :, None] * N_KEYS + i2[..., None, :]).reshape(PEER_BLOCK, PEER_HEADS, nc)
        top, pos = lax.top_k(cand, PEER_TOPK)
        idx = jnp.take_along_axis(cand_idx, pos, axis=-1)
        gate = jax.nn.softmax(top, axis=-1)
        act = jax.nn.gelu(jnp.einsum('thkd,td->thk', u_tab[idx], fb)) * gate
        return jnp.einsum('thk,thkd->td', act, v_tab[idx])

    return lax.map(block, blocks).reshape(b, t_len, D_MODEL)


def setup_inputs(seed: int = 0) -> dict:
    key = jax.random.key(seed)
    ks = iter(jax.random.split(key, 64))
    f32 = jnp.float32
    L = DEPTH

    def nrm(shape, scale=1.0):
        return jax.random.normal(next(ks), shape, f32) * scale

    def gain(shape):
        return 1.0 + nrm(shape, 0.05)

    def unif(shape, lo, hi):
        return jax.random.uniform(next(ks), shape, f32, lo, hi)

    dt = jnp.exp(unif((L, 2, HEADS_C), math.log(1e-3), math.log(1e-1)))
    return {
        'x': nrm((BATCH, SEQ, D_MODEL)),
        'c': nrm((BATCH, D_MODEL)),
        'ctx': nrm((BATCH, CTX_LEN, D_MODEL)),
        'c_ctx': nrm((D_MODEL,)),
        'ada_w': nrm((L, D_MODEL, 6 * D_MODEL), 0.5 * D_MODEL ** -0.5),
        'ada_b': nrm((L, 6 * D_MODEL), 0.02),
        'norm_mix_g': gain((L, D_MODEL)),
        'norm_ffn_g': gain((L, D_MODEL)),
        'w_in': nrm((L, D_MODEL, IN_COLS), D_MODEL ** -0.5),
        'w_out': nrm((L, D_MIX, D_MODEL), D_MIX ** -0.5),
        'a_conv_w': nrm((L, CONV_A, D_GROUP), CONV_A ** -0.5),
        'a_conv_b': nrm((L, D_GROUP), 0.02),
        'a_ln_g': gain((L, D_GROUP)),
        'a_ln_b': nrm((L, D_GROUP), 0.02),
        'b_ln_g': gain((L, D_GROUP)),
        'b_ln_b': nrm((L, D_GROUP), 0.02),
        'b_ws': nrm((L, HEADS_B, CHUNK_B, CHUNK_B), CHUNK_B ** -0.5),
        'b_bs': gain((L, HEADS_B, CHUNK_B)),
        'c_conv_w': nrm((L, CONV_C, C_XBC), CONV_C ** -0.5),
        'c_conv_b': nrm((L, C_XBC), 0.02),
        'c_dt_bias': dt + jnp.log(-jnp.expm1(-dt)),
        'c_a_log': jnp.log(unif((L, 2, HEADS_C), 1.0, 16.0)),
        'c_d': 1.0 + nrm((L, HEADS_C), 0.1),
        'c_norm_g': gain((L, D_GROUP)),
        'd_mu_prev': unif((L, D_COLS), 0.0, 0.5),
        'd_mu_next': unif((L, D_COLS), 0.0, 0.5),
        'd_w0': nrm((L, 2, D_GROUP), 0.5),
        'd_w2': nrm((L, 2, LORA_W, D_GROUP), 0.1),
        'd_a0': nrm((L, 2, D_GROUP), 0.5),
        'd_a2': nrm((L, 2, LORA_A, D_GROUP), 0.1),
        'd_g2': nrm((L, LORA_G, D_GROUP), LORA_G ** -0.5),
        'd_k_k': 0.85 + nrm((L, D_GROUP), 0.05),
        'd_k_a': gain((L, D_GROUP)),
        'd_r_k': nrm((L, HEADS_D, HEAD_DIM_D), 0.1),
        'd_gn_g': gain((L, D_GROUP)),
        'd_gn_b': nrm((L, D_GROUP), 0.02),
        'peer_wq': nrm((L, D_MODEL, PEER_HEADS * PEER_DK), D_MODEL ** -0.5),
        'peer_k1': nrm((L, N_KEYS, PEER_DK // 2), (PEER_DK // 2) ** -0.5),
        'peer_k2': nrm((L, N_KEYS, PEER_DK // 2), (PEER_DK // 2) ** -0.5),
        'peer_u': nrm((L, N_EXPERTS, D_MODEL), D_MODEL ** -0.5),
        'peer_v': nrm((L, N_EXPERTS, D_MODEL), 0.5),
        'final_g': gain((D_MODEL,)),
    }


def reference(x, c, ctx, c_ctx, ada_w, ada_b, norm_mix_g, norm_ffn_g, w_in, w_out,
              a_conv_w, a_conv_b, a_ln_g, a_ln_b, b_ln_g, b_ln_b, b_ws, b_bs,
              c_conv_w, c_conv_b, c_dt_bias, c_a_log, c_d, c_norm_g,
              d_mu_prev, d_mu_next, d_w0, d_w2, d_a0, d_a2, d_g2, d_k_k, d_k_a, d_r_k,
              d_gn_g, d_gn_b, peer_wq, peer_k1, peer_k2, peer_u, peer_v, final_g):
    h = x + sincos_2d(x.shape[1]).astype(x.dtype)
    hc = ctx
    for i in range(DEPTH):
        emit = i < DEPTH - 1
        m = (jax.nn.silu(c) @ ada_w[i] + ada_b[i]).reshape(c.shape[0], 6, 1, D_MODEL)
        mc = (jax.nn.silu(c_ctx) @ ada_w[i] + ada_b[i]).reshape(6, 1, D_MODEL)
        col0 = 0 if emit else OFF_C
        pl = modulate(rmsnorm(h, norm_mix_g[i]), m[:, 0], m[:, 1]) @ w_in[i]
        pc = modulate(rmsnorm(hc, norm_mix_g[i]), mc[0], mc[1]) @ w_in[i][:, col0:]
        y_a = mixer_conformer(pl[..., :OFF_B], a_conv_w[i], a_conv_b[i], a_ln_g[i], a_ln_b[i])
        y_b = mixer_chunk_mlp(pl[..., OFF_B:OFF_C], b_ln_g[i], b_ln_b[i], b_ws[i], b_bs[i])
        y_c, yc_c = mixer_mamba(pl[..., OFF_C:OFF_D], pc[..., OFF_C - col0:OFF_D - col0],
                                c_conv_w[i], c_conv_b[i], c_dt_bias[i], c_a_log[i], c_d[i],
                                c_norm_g[i], emit)
        y_d, yc_d = mixer_rwkv(pl[..., OFF_D:], pc[..., OFF_D - col0:], d_mu_prev[i], d_mu_next[i],
                               d_w0[i], d_w2[i], d_a0[i], d_a2[i], d_g2[i], d_k_k[i], d_k_a[i],
                               d_r_k[i], d_gn_g[i], d_gn_b[i], emit)
        h = h + m[:, 2] * (jnp.concatenate([y_a, y_b, y_c, y_d], -1) @ w_out[i])
        f = modulate(rmsnorm(h, norm_ffn_g[i]), m[:, 3], m[:, 4])
        h = h + m[:, 5] * peer_ffn(f, peer_wq[i], peer_k1[i], peer_k2[i], peer_u[i], peer_v[i])
        if emit:
            yc_a = mixer_conformer(pc[..., :OFF_B], a_conv_w[i], a_conv_b[i], a_ln_g[i], a_ln_b[i])
            yc_b = mixer_chunk_mlp(pc[..., OFF_B:OFF_C], b_ln_g[i], b_ln_b[i], b_ws[i], b_bs[i])
            hc = hc + mc[2] * (jnp.concatenate([yc_a, yc_b, yc_c, yc_d], -1) @ w_out[i])
            fc = modulate(rmsnorm(hc, norm_ffn_g[i]), mc[3], mc[4])
            hc = hc + mc[5] * peer_ffn(fc, peer_wq[i], peer_k1[i], peer_k2[i], peer_u[i], peer_v[i])
    return rmsnorm(h, final_g)
```

```python
import functools
import math

import jax
import jax.numpy as jnp
from jax import lax
from jax.experimental import pallas as pl
from jax.experimental.pallas import tpu as pltpu

F32 = jnp.float32
BF16 = jnp.bfloat16
HIGHEST = lax.Precision.HIGHEST

D_MODEL = 1024
DEPTH = 2
GRID_W = 64
EPS = 1e-6
D_GROUP = 256
CONV_A = 31
CHUNK_B = 128
HEADS_B = 4
HEADS_C = 4
HEAD_DIM_C = 64
STATE_C = 128
CONV_C = 5
CHUNK_C = 128
HEADS_D = 4
HEAD_DIM_D = 64
LORA_W = 64
LORA_A = 64
LORA_G = 128
GN_EPS_D = 64e-5
N_KEYS = 128
N_EXPERTS = N_KEYS * N_KEYS
PEER_HEADS = 8
PEER_DK = 256
PEER_TOPK = 16

C_XBC = D_GROUP + 2 * 2 * STATE_C
OFF_B = 2 * D_GROUP
OFF_C = OFF_B + 2 * D_GROUP
OFF_D = OFF_C + D_GROUP + C_XBC + 2 * HEADS_C
D_COLS = 3 * D_GROUP + 2 * LORA_W + 2 * LORA_A + LORA_G
PC_COLS = D_GROUP + C_XBC + 2 * 128

LANES = 128
SEQ_TILE = 256
TOK_TILE = 512
CHUNK_D = 64
EXP_TILE = 1024
VMEM_LIMIT = 56 * 1024 * 1024


def _cparams(n_axes):
    return pltpu.CompilerParams(dimension_semantics=("arbitrary",) * n_axes,
                                vmem_limit_bytes=VMEM_LIMIT)


def _silu(x):
    return x * jax.nn.sigmoid(x)


def _dot(a, b, dims=None):
    a = a.astype(BF16)
    b = b.astype(BF16)
    if dims is None:
        return jnp.dot(a, b, preferred_element_type=F32)
    return lax.dot_general(a, b, (dims, ((), ())), preferred_element_type=F32)


def _dot_hi(a, b):
    return jnp.dot(a, b, precision=HIGHEST, preferred_element_type=F32)


NT = ((1,), (1,))
TN = ((0,), (0,))


def _iota(shape, axis):
    return lax.broadcasted_iota(jnp.int32, shape, axis)


def _head_block_ones(n, width):
    return (_iota((n, n), 0) // width == _iota((n, n), 1) // width).astype(F32)


def _ada_body(cs_ref, w_ref, b_ref, o_ref):
    o_ref[...] = _dot_hi(_silu(cs_ref[...]), w_ref[...]) + b_ref[...]


def _ada(cs, ada_w, ada_b):
    L = ada_w.shape[0]
    nb = 1536
    return pl.pallas_call(
        _ada_body,
        grid=(L, 6 * D_MODEL // nb),
        in_specs=[pl.BlockSpec((8, D_MODEL), lambda l, n: (0, 0)),
                  pl.BlockSpec((None, D_MODEL, nb), lambda l, n: (l, 0, n)),
                  pl.BlockSpec((None, 1, nb), lambda l, n: (l, 0, n))],
        out_specs=pl.BlockSpec((None, 8, nb), lambda l, n: (l, 0, n)),
        out_shape=jax.ShapeDtypeStruct((L, 8, 6 * D_MODEL), F32),
        compiler_params=_cparams(2),
        name="ada",
    )(cs, ada_w, ada_b.reshape(L, 1, 6 * D_MODEL))


def _norm_mod(h, g, shift, scale):
    xn = h * lax.rsqrt(jnp.mean(h * h, axis=-1, keepdims=True) + EPS) * g
    return xn * (1.0 + scale) + shift


def _inproj_tail(h, sh_ref, sc_ref, g_ref, wa, wb, wc, wd, oa, ob, oc, od):
    xm = _norm_mod(h, g_ref[...], sh_ref[...], sc_ref[...]).astype(BF16)
    oa[...] = jnp.dot(xm, wa[...], preferred_element_type=F32)
    ob[...] = jnp.dot(xm, wb[...], preferred_element_type=F32)
    oc[...] = jnp.dot(xm, wc[...], preferred_element_type=F32)
    od[...] = jnp.dot(xm, wd[...], preferred_element_type=F32)


def _inproj_first_body(n_lat, x_ref, ctx_ref, pos_ref, sh_ref, sc_ref, g_ref, wa, wb, wc, wd,
                       oh, oa, ob, oc, od):
    i = pl.program_id(0)
    h = jnp.where(i < n_lat, x_ref[...] + pos_ref[...], ctx_ref[...])
    oh[...] = h
    _inproj_tail(h, sh_ref, sc_ref, g_ref, wa, wb, wc, wd, oa, ob, oc, od)


def _inproj_body(h_ref, sh_ref, sc_ref, g_ref, wa, wb, wc, wd, oa, ob, oc, od):
    _inproj_tail(h_ref[...], sh_ref, sc_ref, g_ref, wa, wb, wc, wd, oa, ob, oc, od)


def _mod_spec(n_lat, per_seq, k):
    def imap(i):
        row = jnp.where(i < n_lat, i // per_seq, 2)
        return (row * 6 + k, 0, 0)
    return pl.BlockSpec((None, 1, D_MODEL), imap)


def _inproj(h_or_parts, mods, g, ws, n_lat, per_seq, first):
    wa, wb, wc, wd = ws
    n_tiles = n_lat + 1
    rows = n_tiles * TOK_TILE
    tile = lambda c: pl.BlockSpec((TOK_TILE, c), lambda i: (i, 0))
    full = lambda a: pl.BlockSpec(a.shape, lambda i: (0,) * a.ndim)
    common_specs = [_mod_spec(n_lat, per_seq, 0), _mod_spec(n_lat, per_seq, 1), full(g),
                    full(wa), full(wb), full(wc), full(wd)]
    out_specs = [tile(512), tile(512), tile(PC_COLS), tile(D_COLS)]
    out_shape = [jax.ShapeDtypeStruct((rows, c), F32) for c in (512, 512, PC_COLS, D_COLS)]
    if first:
        x2, ctx2, pos = h_or_parts
        in_specs = [pl.BlockSpec((TOK_TILE, D_MODEL), lambda i: (jnp.minimum(i, n_lat - 1), 0)),
                    pl.BlockSpec((TOK_TILE, D_MODEL), lambda i: (0, 0)),
                    pl.BlockSpec((TOK_TILE, D_MODEL), lambda i: (i % per_seq, 0))] + common_specs
        return pl.pallas_call(
            functools.partial(_inproj_first_body, n_lat),
            grid=(n_tiles,), in_specs=in_specs,
            out_specs=[tile(D_MODEL)] + out_specs,
            out_shape=[jax.ShapeDtypeStruct((rows, D_MODEL), F32)] + out_shape,
            compiler_params=_cparams(1), name="inproj_first",
        )(x2, ctx2, pos, mods, mods, g, wa, wb, wc, wd)
    return pl.pallas_call(
        _inproj_body, grid=(n_tiles,), in_specs=[tile(D_MODEL)] + common_specs,
        out_specs=out_specs, out_shape=out_shape,
        compiler_params=_cparams(1), name="inproj",
    )(h_or_parts, mods, mods, g, wa, wb, wc, wd)


def _layernorm(x, g, b, eps=1e-5):
    mu = jnp.mean(x, axis=-1, keepdims=True)
    xc = x - mu
    var = jnp.mean(xc * xc, axis=-1, keepdims=True)
    return xc * lax.rsqrt(var + eps) * g + b


def _mixab_body(n_lat, n_l, pa, pa_prev, pa_next, pb, cw, cb, alg, alb, blg, blb, ws, bsm,
                out, ext):
    i = pl.program_id(0)
    is_ctx = i >= n_lat
    tpos = i % n_l
    first = jnp.logical_or(is_ctx, tpos == 0)
    last = jnp.logical_or(is_ctx, tpos == n_l - 1)

    def glu(x):
        return x[:, :D_GROUP] * jax.nn.sigmoid(x[:, D_GROUP:])

    halo = 16
    ext[0:halo, :] = jnp.where(first, 0.0, glu(pa_prev[...]))
    ext[halo:halo + SEQ_TILE, :] = glu(pa[...])
    ext[halo + SEQ_TILE:, :] = jnp.where(last, 0.0, glu(pa_next[...]))
    acc = jnp.zeros((SEQ_TILE, D_GROUP), F32) + cb[...]
    for j in range(CONV_A):
        acc = acc + cw[j:j + 1, :] * ext[pl.ds(halo - (CONV_A - 1) // 2 + j, SEQ_TILE), :]
    out[:, :D_GROUP] = _silu(_layernorm(acc, alg[...], alb[...]))

    x = pb[...]
    u = x[:, :D_GROUP]
    v = _layernorm(x[:, D_GROUP:], blg[...], blb[...]).astype(BF16)
    hd = D_GROUP // HEADS_B
    rows = []
    for c in range(SEQ_TILE // CHUNK_B):
        vc = v[c * CHUNK_B:(c + 1) * CHUNK_B, :]
        heads = [jnp.dot(ws[hh], vc[:, hh * hd:(hh + 1) * hd], preferred_element_type=F32)
                 for hh in range(HEADS_B)]
        rows.append(jnp.concatenate(heads, axis=1) + bsm[...])
    out[:, D_GROUP:] = u * jnp.concatenate(rows, axis=0)


def _mixab(pa, pb, prm, n_lat, n_l):
    rows = pa.shape[0]
    n_tiles = rows // SEQ_TILE
    halo = 16
    per = SEQ_TILE // halo
    full = lambda a: pl.BlockSpec(a.shape, lambda i: (0,) * a.ndim)
    in_specs = [pl.BlockSpec((SEQ_TILE, 512), lambda i: (i, 0)),
                pl.BlockSpec((halo, 512), lambda i: (jnp.maximum(i * per - 1, 0), 0)),
                pl.BlockSpec((halo, 512), lambda i: (jnp.minimum((i + 1) * per, rows // halo - 1), 0)),
                pl.BlockSpec((SEQ_TILE, 512), lambda i: (i, 0))] + [full(a) for a in prm]
    return pl.pallas_call(
        functools.partial(_mixab_body, n_lat, n_l),
        grid=(n_tiles,), in_specs=in_specs,
        out_specs=pl.BlockSpec((SEQ_TILE, 512), lambda i: (i, 0)),
        out_shape=jax.ShapeDtypeStruct((rows, 512), F32),
        scratch_shapes=[pltpu.VMEM((SEQ_TILE + 2 * halo, D_GROUP), F32)],
        compiler_params=_cparams(1), name="mixab",
    )(pa, pa, pa, pb, *prm)


def _scan_tile(n_b, n_l, b, ph, j):
    lat = b * n_l + jnp.where(ph == 0, j - 1, n_l - j)
    return jnp.where(j == 0, n_b * n_l + b, lat)


def _scan_specs(n_b, n_l, rows, cols):
    tid = functools.partial(_scan_tile, n_b, n_l)
    per = SEQ_TILE // 8
    cur = pl.BlockSpec((SEQ_TILE, cols), lambda b, ph, j: (tid(b, ph, j), 0))
    prev = pl.BlockSpec((8, cols), lambda b, ph, j: (jnp.maximum(tid(b, ph, j) * per - 1, 0), 0))
    nxt = pl.BlockSpec((8, cols),
                       lambda b, ph, j: (jnp.minimum((tid(b, ph, j) + 1) * per, rows // 8 - 1), 0))
    return cur, prev, nxt


def _scan_out_spec(n_b, n_l, cols):
    def imap(b, ph, j):
        return (jnp.where(ph == 0, n_b * n_l + b, _scan_tile(n_b, n_l, b, 1, j)), 0)
    return pl.BlockSpec((SEQ_TILE, cols), imap)


def _scan_flags(n_l):
    ph = pl.program_id(1)
    j = pl.program_id(2)
    is_ctx = j == 0
    tpos = jnp.where(ph == 0, j - 1, n_l - j)
    first = jnp.logical_or(is_ctx, tpos == 0)
    last = jnp.logical_or(is_ctx, tpos == n_l - 1)
    slot = jnp.where(is_ctx, 0, tpos + 1)
    return ph, j, first, last, slot


def _mamba_body(n_l, cur_ref, prev_ref, next_ref, dt_ref, cw, cb, dtb, alog, dsk, ng,
                out, ext, yf, st, xbc_s, a_s, dt_s, y_s):
    ph, j, first, last, slot = _scan_flags(n_l)
    fwd = ph == 0

    @pl.when(j == 0)
    def _():
        st[...] = jnp.zeros_like(st)

    zx = D_GROUP
    ext[0:8, :] = jnp.where(first, 0.0, prev_ref[:, zx:])
    ext[8:8 + SEQ_TILE, :] = cur_ref[:, zx:]
    ext[8 + SEQ_TILE:, :] = jnp.where(last, 0.0, next_ref[:, zx:])
    acc = jnp.zeros((SEQ_TILE, C_XBC), F32) + cb[...]
    for jj in range(CONV_C):
        acc = acc + cw[jj:jj + 1, :] * ext[pl.ds(8 - (CONV_C - 1) // 2 + jj, SEQ_TILE), :]
    xbc_s[...] = _silu(acc)
    dt = jax.nn.softplus(dt_ref[...] + dtb[ph])
    dt_s[...] = dt
    a_s[...] = dt * (-jnp.exp(alog[ph]))

    L = CHUNK_C
    r_i = _iota((L, L), 0)
    c_i = _iota((L, L), 1)
    ltri = (r_i >= c_i).astype(F32)
    mask = jnp.where(fwd, r_i - c_i, c_i - r_i) >= 0
    sgn = jnp.where(fwd, 1.0, -1.0)
    hd = HEAD_DIM_C

    def chunk(it, carry):
        q = jnp.where(fwd, it, SEQ_TILE // L - 1 - it)
        rows = pl.ds(pl.multiple_of(q * L, L), L)
        a_q = a_s[rows, :]
        cs = _dot_hi(ltri, a_q)
        tot = cs[L - 1:L, :]
        e = jnp.where(fwd, cs, cs - a_q)
        e_t = e.T
        dt_q = dt_s[rows, :]
        xbc = xbc_s[rows, :]
        ys = []
        for g in range(2):
            bm = xbc[:, D_GROUP + g * STATE_C:D_GROUP + (g + 1) * STATE_C]
            cm = xbc[:, D_GROUP + 2 * STATE_C + g * STATE_C:D_GROUP + 2 * STATE_C + (g + 1) * STATE_C]
            bm_t = bm.T.astype(BF16)
            cm_b = cm.astype(BF16)
            gmat = jnp.dot(cm_b, bm_t, preferred_element_type=F32)
            for h in (2 * g, 2 * g + 1):
                ecol = e[:, h:h + 1]
                erow = e_t[h:h + 1, :]
                totc = tot[:, h:h + 1]
                xdt = xbc[:, h * hd:(h + 1) * hd] * dt_q[:, h:h + 1]
                lmat = jnp.exp(jnp.where(mask, sgn * (ecol - erow), -1e30))
                offs = jnp.exp(jnp.where(fwd, ecol, totc - ecol))
                stw = jnp.exp(jnp.where(fwd, totc - ecol, ecol))
                s_prev = st[h]
                y = _dot(gmat * lmat, xdt) + offs * _dot(cm_b, s_prev)
                st[h] = jnp.exp(totc) * s_prev + _dot(bm_t, xdt * stw)
                ys.append(y)
        y_s[rows, :] = jnp.concatenate(ys, axis=1)
        return carry

    lax.fori_loop(0, SEQ_TILE // L, chunk, 0)

    @pl.when(fwd)
    def _():
        yf[slot] = y_s[...]

    @pl.when(ph == 1)
    def _():
        y = yf[slot] + y_s[...] + dsk[...] * xbc_s[:, :D_GROUP]
        t = y * _silu(cur_ref[:, :D_GROUP])
        out[...] = t * lax.rsqrt(jnp.mean(t * t, axis=-1, keepdims=True) + EPS) * ng[...]


def _mamba(pc, prm, n_b, n_l):
    rows = pc.shape[0]
    cur, prev, nxt = _scan_specs(n_b, n_l, rows, D_GROUP + C_XBC)
    tid = functools.partial(_scan_tile, n_b, n_l)
    dt_spec = pl.BlockSpec((SEQ_TILE, LANES),
                           lambda b, ph, j: (tid(b, ph, j), (D_GROUP + C_XBC) // LANES + ph))
    full = lambda a: pl.BlockSpec(a.shape, lambda b, ph, j: (0,) * a.ndim)
    return pl.pallas_call(
        functools.partial(_mamba_body, n_l),
        grid=(n_b, 2, n_l + 1),
        in_specs=[cur, prev, nxt, dt_spec] + [full(a) for a in prm],
        out_specs=_scan_out_spec(n_b, n_l, D_GROUP),
        out_shape=jax.ShapeDtypeStruct((rows, D_GROUP), F32),
        scratch_shapes=[pltpu.VMEM((SEQ_TILE + 16, C_XBC), F32),
                        pltpu.VMEM((n_l + 1, SEQ_TILE, D_GROUP), F32),
                        pltpu.VMEM((HEADS_C, STATE_C, HEAD_DIM_C), F32),
                        pltpu.VMEM((SEQ_TILE, C_XBC), F32),
                        pltpu.VMEM((SEQ_TILE, LANES), F32),
                        pltpu.VMEM((SEQ_TILE, LANES), F32),
                        pltpu.VMEM((SEQ_TILE, D_GROUP), F32)],
        compiler_params=_cparams(3), name="mamba",
    )(pc, pc, pc, pc, *prm)


def _rwkv_body(n_l, cur_ref, prev_ref, next_ref, mup, mun, w0, w2, a0, a2, g2, kkw, kaw, rkw,
               gng, gnb, out, yf, st, r_s, v_s, kap_s, alp_s, kd_s, lw_s, y_s):
    ph, j, first, last, slot = _scan_flags(n_l)
    fwd = ph == 0

    @pl.when(j == 0)
    def _():
        st[...] = jnp.zeros_like(st)

    cur = cur_ref[...]
    prv = jnp.concatenate([jnp.where(first, 0.0, prev_ref[7:8, :]), cur[:SEQ_TILE - 1, :]], axis=0)
    nxt = jnp.concatenate([cur[1:, :], jnp.where(last, 0.0, next_ref[0:1, :])], axis=0)
    p = cur + mup[...] * (prv - cur) + mun[...] * (nxt - cur)
    G = D_GROUP
    r = p[:, :G]
    k = p[:, G:2 * G]
    v = p[:, 2 * G:3 * G]
    blk = _head_block_ones(G, HEAD_DIM_D)
    kkr = k * kkw[...]
    kk = kkr * lax.rsqrt(_dot_hi(kkr * kkr, blk) + 1e-12)

    def rate(d_static=None):
        if d_static is None:
            ad = jnp.where(fwd, p[:, 3 * G + 2 * LORA_W:3 * G + 2 * LORA_W + LORA_A],
                           p[:, 3 * G + 2 * LORA_W + LORA_A:3 * G + 2 * LORA_W + 2 * LORA_A])
            a = jax.nn.sigmoid(a0[ph] + _dot(ad, a2[ph]))
        else:
            o = 3 * G + 2 * LORA_W + d_static * LORA_A
            a = jax.nn.sigmoid(a0[d_static] + _dot(p[:, o:o + LORA_A], a2[d_static]))
        return a, k * (1.0 + (a - 1.0) * kaw[...])

    a_d, kd_d = rate()
    wd = jnp.where(fwd, p[:, 3 * G:3 * G + LORA_W], p[:, 3 * G + LORA_W:3 * G + 2 * LORA_W])
    w = w0[ph] + _dot(jnp.tanh(wd), w2[ph])
    lw_s[...] = -math.exp(-0.5) * jax.nn.sigmoid(w)
    r_s[...] = r
    v_s[...] = v
    kap_s[...] = kk
    alp_s[...] = a_d * kk
    kd_s[...] = kd_d

    C = CHUNK_D
    r_i = _iota((C, C), 0)
    c_i = _iota((C, C), 1)
    lag = jnp.where(fwd, r_i - c_i, c_i - r_i)
    tri = lag >= 0
    tri_f = tri.astype(F32)
    strict = lag > 0
    eye = (r_i == c_i).astype(F32)
    hd = HEAD_DIM_D

    def chunk(it, carry):
        q = jnp.where(fwd, it, SEQ_TILE // C - 1 - it)
        rows = pl.ds(pl.multiple_of(q * C, C), C)
        lw = lw_s[rows, :]
        incl = _dot_hi(tri_f, lw)
        tot = jnp.where(fwd, incl[C - 1:C, :], incl[0:1, :])
        p_in = jnp.exp(incl)
        p_ex = jnp.exp(incl - lw)
        p_inv = jnp.exp(-incl)
        p_tot = jnp.exp(tot)
        kap_h = kap_s[rows, :] * p_ex
        r_h = r_s[rows, :] * p_in
        alp_b = alp_s[rows, :] * p_inv
        k_b = kd_s[rows, :] * p_inv
        vv = v_s[rows, :]
        ys = []
        for h in range(HEADS_D):
            sl = slice(h * hd, (h + 1) * hd)
            left = jnp.concatenate([kap_h[:, sl], r_h[:, sl]], axis=0)
            right = jnp.concatenate([alp_b[:, sl], k_b[:, sl]], axis=0)
            m1 = _dot(left, right, NT)
            a_uu = jnp.where(strict, m1[:C, :C], 0.0)
            a_vk = jnp.where(strict, m1[:C, C:], 0.0)
            a_ra = jnp.where(tri, m1[C:, :C], 0.0)
            a_rk = jnp.where(tri, m1[C:, C:], 0.0)
            pw = -a_uu
            tinv = eye + pw
            for _ in range(5):
                pw = _dot(pw, pw)
                tinv = tinv + _dot(tinv, pw)
            s0 = st[h]
            vh = vv[:, sl]
            x0 = _dot(left, s0, NT)
            x1 = _dot(jnp.concatenate([a_vk, a_rk], axis=0), vh)
            u = _dot(tinv, x0[:C] + x1[:C])
            ys.append(x0[C:] + x1[C:] - _dot(a_ra, u))
            upd = _dot(jnp.concatenate([vh, -u], axis=0),
                       jnp.concatenate([k_b[:, sl], alp_b[:, sl]], axis=0), TN)
            st[h] = (s0 + upd) * p_tot[:, sl]
        y_s[rows, :] = jnp.concatenate(ys, axis=1)
        return carry

    lax.fori_loop(0, SEQ_TILE // C, chunk, 0)

    @pl.when(fwd)
    def _():
        yf[slot] = y_s[...]

    @pl.when(ph == 1)
    def _():
        y = yf[slot] + y_s[...]
        inv = 1.0 / HEAD_DIM_D
        mu = _dot_hi(y, blk) * inv
        yc = y - mu
        var = _dot_hi(yc * yc, blk) * inv
        yn = yc * lax.rsqrt(var + GN_EPS_D) * gng[...] + gnb[...]
        _, kd_f = rate(0)
        bonus = _dot_hi(r * (kd_f + kd_d) * rkw[...], blk) * v
        gate = _dot(jax.nn.sigmoid(p[:, 3 * G + 2 * LORA_W + 2 * LORA_A:]), g2[...])
        out[...] = (yn + bonus) * gate


def _rwkv(pd, prm, n_b, n_l):
    rows = pd.shape[0]
    cur, prev, nxt = _scan_specs(n_b, n_l, rows, D_COLS)
    full = lambda a: pl.BlockSpec(a.shape, lambda b, ph, j: (0,) * a.ndim)
    tile = lambda: pltpu.VMEM((SEQ_TILE, D_GROUP), F32)
    return pl.pallas_call(
        functools.partial(_rwkv_body, n_l),
        grid=(n_b, 2, n_l + 1),
        in_specs=[cur, prev, nxt] + [full(a) for a in prm],
        out_specs=_scan_out_spec(n_b, n_l, D_GROUP),
        out_shape=jax.ShapeDtypeStruct((rows, D_GROUP), F32),
        scratch_shapes=[pltpu.VMEM((n_l + 1, SEQ_TILE, D_GROUP), F32),
                        pltpu.VMEM((HEADS_D, HEAD_DIM_D, HEAD_DIM_D), F32)] + [tile() for _ in range(7)],
        compiler_params=_cparams(3), name="rwkv",
    )(pd, pd, pd, *prm)


def _outproj_body(yab, yc, yd, h_ref, gate, sh, sc, g, wab, wc, wd, oh, of):
    mix = (_dot(yab[...], wab[...]) + _dot(yc[...], wc[...]) + _dot(yd[...], wd[...]))
    h = h_ref[...] + gate[...] * mix
    oh[...] = h
    of[...] = _norm_mod(h, g[...], sh[...], sc[...]).astype(BF16)


def _outproj(yab, yc, yd, h, mods, g, ws, n_lat, per_seq):
    rows = h.shape[0]
    tile = lambda c: pl.BlockSpec((TOK_TILE, c), lambda i: (i, 0))
    full = lambda a: pl.BlockSpec(a.shape, lambda i: (0,) * a.ndim)
    return pl.pallas_call(
        _outproj_body, grid=(rows // TOK_TILE,),
        in_specs=[tile(512), tile(D_GROUP), tile(D_GROUP), tile(D_MODEL),
                  _mod_spec(n_lat, per_seq, 2), _mod_spec(n_lat, per_seq, 3),
                  _mod_spec(n_lat, per_seq, 4), full(g)] + [full(a) for a in ws],
        out_specs=[tile(D_MODEL), tile(D_MODEL)],
        out_shape=[jax.ShapeDtypeStruct((rows, D_MODEL), F32),
                   jax.ShapeDtypeStruct((rows, D_MODEL), BF16)],
        compiler_params=_cparams(1), name="outproj",
    )(yab, yc, yd, h, mods, mods, mods, g, *ws)


TOPK_TILE = 256


def _extract_top(work, out_ref):
    n = work.shape[0]
    io = _iota(work.shape, 0)

    def step(kk, w):
        m = jnp.max(w, axis=0, keepdims=True)
        idx = jnp.min(jnp.where(w == m, io, n), axis=0, keepdims=True)
        out_ref[pl.ds(kk, 1), :] = m
        return jnp.where(io == idx, -jnp.inf, w)

    lax.fori_loop(0, PEER_TOPK, step, work)


def _topk_body(f_ref, wq_ref, k1_ref, k2_ref, s1_o, s2_o, e1_o, e2_o, thr_o, q_s, v1_s, v2_s, t_s):
    q_s[...] = lax.dot_general(wq_ref[...], f_ref[...], (NT, ((), ())),
                               preferred_element_type=F32).astype(BF16)
    half = PEER_DK // 2

    def head(h, carry):
        base = pl.multiple_of(h * PEER_DK, PEER_DK)
        s1 = jnp.dot(k1_ref[...], q_s[pl.ds(base, half), :], preferred_element_type=F32)
        s2 = jnp.dot(k2_ref[...], q_s[pl.ds(base + half, half), :], preferred_element_type=F32)
        _extract_top(s1, v1_s)
        _extract_top(s2, v2_s)
        v1 = v1_s[...]
        v2 = v2_s[...]
        cand = jnp.concatenate([v1[a:a + 1, :] + v2 for a in range(PEER_TOPK)], axis=0)
        _extract_top(cand, t_s)
        top = t_s[...]
        z = jnp.sum(jnp.exp(top - top[0:1, :]), axis=0, keepdims=True)
        s1_o[h] = s1
        s2_o[h] = s2
        e1_o[h] = jnp.exp(s1 - v1[0:1, :]) / z
        e2_o[h] = jnp.exp(s2 - v2[0:1, :])
        thr_o[pl.ds(h, 1), :] = top[PEER_TOPK - 1:PEER_TOPK, :]
        return carry

    lax.fori_loop(0, PEER_HEADS, head, 0)


def _topk(f, wq_t, k1, k2):
    rows = f.shape[0]
    T = TOPK_TILE
    full = lambda a: pl.BlockSpec(a.shape, lambda i: (0,) * a.ndim)
    big = pl.BlockSpec((PEER_HEADS, N_KEYS, T), lambda i: (0, 0, i))
    big_shape = jax.ShapeDtypeStruct((PEER_HEADS, N_KEYS, rows), F32)
    return pl.pallas_call(
        _topk_body, grid=(rows // T,),
        in_specs=[pl.BlockSpec((T, D_MODEL), lambda i: (i, 0)), full(wq_t), full(k1), full(k2)],
        out_specs=[big, big, big, big, pl.BlockSpec((PEER_HEADS, T), lambda i: (0, i))],
        out_shape=[big_shape] * 4 + [jax.ShapeDtypeStruct((PEER_HEADS, rows), F32)],
        scratch_shapes=[pltpu.VMEM((PEER_HEADS * PEER_DK, T), BF16),
                        pltpu.VMEM((PEER_TOPK, T), F32), pltpu.VMEM((PEER_TOPK, T), F32),
                        pltpu.VMEM((PEER_TOPK, T), F32)],
        compiler_params=_cparams(1), name="peer_topk",
    )(f, wq_t, k1, k2)


def _peer_body(final, f_ref, u_ref, v_ref, s1_ref, e1_ref, s2_ref, e2_ref, thr_ref, h_ref,
               gate_ref, fg_ref, out, st_s, at_s, acc):
    jx = pl.program_id(1)

    @pl.when(jx == 0)
    def _():
        acc[...] = jnp.zeros_like(acc)

    st_s[...] = lax.dot_general(u_ref[...], f_ref[...], (NT, ((), ())),
                                preferred_element_type=F32)
    n_rows = EXP_TILE // N_KEYS

    def lane_group(lg, carry):
        ln = pl.ds(pl.multiple_of(lg * LANES, LANES), LANES)
        for ii in range(n_rows):
            rows = slice(ii * N_KEYS, (ii + 1) * N_KEYS)
            gsum = jnp.zeros((N_KEYS, LANES), F32)
            for h in range(PEER_HEADS):
                cand = s1_ref[h, ii:ii + 1, ln] + s2_ref[h, :, ln]
                wgt = e1_ref[h, ii:ii + 1, ln] * e2_ref[h, :, ln]
                gsum = gsum + jnp.where(cand >= thr_ref[h:h + 1, ln], wgt, 0.0)
            at_s[rows, ln] = (jax.nn.gelu(st_s[rows, ln]) * gsum).astype(BF16)
        return carry

    lax.fori_loop(0, TOK_TILE // LANES, lane_group, 0)
    acc[...] += lax.dot_general(at_s[...], v_ref[...], (TN, ((), ())), preferred_element_type=F32)

    @pl.when(jx == pl.num_programs(1) - 1)
    def _():
        h = h_ref[...] + gate_ref[...] * acc[...]
        if final:
            h = h * lax.rsqrt(jnp.mean(h * h, axis=-1, keepdims=True) + EPS) * fg_ref[...]
        out[...] = h


def _peer(f, u_b, v_b, s1, e1, s2, e2, thr, h, mods, fg, n_lat, per_seq, n_tok_tiles, final):
    n_e = N_EXPERTS // EXP_TILE
    rpt = EXP_TILE // N_KEYS
    tok = lambda c: pl.BlockSpec((TOK_TILE, c), lambda i, j: (i, 0))
    exp = pl.BlockSpec((EXP_TILE, D_MODEL), lambda i, j: (j, 0))
    rowblk = pl.BlockSpec((PEER_HEADS, rpt, TOK_TILE), lambda i, j: (0, j, i))
    allkeys = pl.BlockSpec((PEER_HEADS, N_KEYS, TOK_TILE), lambda i, j: (0, 0, i))

    def gate_map(i, j):
        row = jnp.where(i < n_lat, i // per_seq, 2)
        return (row * 6 + 5, 0, 0)

    return pl.pallas_call(
        functools.partial(_peer_body, final),
        grid=(n_tok_tiles, n_e),
        in_specs=[tok(D_MODEL), exp, exp, rowblk, rowblk, allkeys, allkeys,
                  pl.BlockSpec((PEER_HEADS, TOK_TILE), lambda i, j: (0, i)), tok(D_MODEL),
                  pl.BlockSpec((None, 1, D_MODEL), gate_map),
                  pl.BlockSpec((1, D_MODEL), lambda i, j: (0, 0))],
        out_specs=tok(D_MODEL),
        out_shape=jax.ShapeDtypeStruct((n_tok_tiles * TOK_TILE, D_MODEL), F32),
        scratch_shapes=[pltpu.VMEM((EXP_TILE, TOK_TILE), F32),
                        pltpu.VMEM((EXP_TILE, TOK_TILE), BF16),
                        pltpu.VMEM((TOK_TILE, D_MODEL), F32)],
        compiler_params=_cparams(2), name="peer",
    )(f, u_b, v_b, s1, e1, s2, e2, thr, h, mods, fg)


def _sincos_2d(t_len):
    rows = t_len // GRID_W
    row = jnp.repeat(jnp.arange(rows), GRID_W).astype(F32)
    col = jnp.tile(jnp.arange(GRID_W), rows).astype(F32)
    q = D_MODEL // 4
    freq = 10000.0 ** (-jnp.arange(q, dtype=F32) / q)
    ar = row[:, None] * freq
    ac = col[:, None] * freq
    return jnp.concatenate([jnp.sin(ar), jnp.cos(ar), jnp.sin(ac), jnp.cos(ac)], -1)


def _row(a):
    return a.reshape(1, -1).astype(F32)


def _pad_lanes(a, width=LANES):
    return jnp.pad(a, ((0, 0), (0, width - a.shape[-1])))


def kernel(x, c, ctx, c_ctx, ada_w, ada_b, norm_mix_g, norm_ffn_g, w_in, w_out, a_conv_w, a_conv_b, a_ln_g, a_ln_b, b_ln_g, b_ln_b, b_ws, b_bs, c_conv_w, c_conv_b, c_dt_bias, c_a_log, c_d, c_norm_g, d_mu_prev, d_mu_next, d_w0, d_w2, d_a0, d_a2, d_g2, d_k_k, d_k_a, d_r_k, d_gn_g, d_gn_b, peer_wq, peer_k1, peer_k2, peer_u, peer_v, final_g):
    n_b, t_lat, _ = x.shape
    t_ctx = ctx.shape[1]
    assert n_b == 2 and t_ctx == SEQ_TILE and t_lat % TOK_TILE == 0
    n_l = t_lat // SEQ_TILE
    per_seq = t_lat // TOK_TILE
    n_lat = n_b * per_seq
    n_lat256 = n_b * n_l

    cs = jnp.zeros((8, D_MODEL), F32).at[:n_b].set(c).at[n_b].set(c_ctx)
    mods_all = _ada(cs, ada_w, ada_b)
    pos = _sincos_2d(t_lat)
    x2 = x.reshape(n_b * t_lat, D_MODEL)
    ctx2 = ctx.reshape(n_b * t_ctx, D_MODEL)

    h = None
    for i in range(DEPTH):
        last_layer = i == DEPTH - 1
        mods = mods_all[i].reshape(8 * 6, 1, D_MODEL)
        w = w_in[i]
        dtc = OFF_C + D_GROUP + C_XBC
        w_c = jnp.concatenate([w[:, OFF_C:dtc], _pad_lanes(w[:, dtc:dtc + HEADS_C]),
                               _pad_lanes(w[:, dtc + HEADS_C:dtc + 2 * HEADS_C])], axis=1)
        ws_in = tuple(a.astype(BF16) for a in (w[:, :OFF_B], w[:, OFF_B:OFF_C], w_c, w[:, OFF_D:]))
        g_mix = _row(norm_mix_g[i])
        if i == 0:
            h, pa, pb, pc, pd = _inproj((x2, ctx2, pos), mods, g_mix, ws_in, n_lat, per_seq, True)
        else:
            pa, pb, pc, pd = _inproj(h, mods, g_mix, ws_in, n_lat, per_seq, False)

        conv_a = jnp.pad(a_conv_w[i], ((0, 32 - CONV_A), (0, 0)))
        bsm = jnp.repeat(b_bs[i].T, D_GROUP // HEADS_B, axis=1)
        yab = _mixab(pa, pb, (conv_a, _row(a_conv_b[i]), _row(a_ln_g[i]), _row(a_ln_b[i]),
                              _row(b_ln_g[i]), _row(b_ln_b[i]), b_ws[i].astype(BF16), bsm),
                     n_lat256, n_l)

        conv_c = jnp.pad(c_conv_w[i], ((0, 8 - CONV_C), (0, 0)))
        dtb = _pad_lanes(c_dt_bias[i]).reshape(2, 1, LANES)
        alog = _pad_lanes(c_a_log[i]).reshape(2, 1, LANES)
        dsk = _row(jnp.repeat(c_d[i], HEAD_DIM_C))
        yc = _mamba(pc, (conv_c, _row(c_conv_b[i]), dtb, alog, dsk, _row(c_norm_g[i])), n_b, n_l)

        yd = _rwkv(pd, (_row(d_mu_prev[i]), _row(d_mu_next[i]), d_w0[i].reshape(2, 1, D_GROUP),
                        d_w2[i].astype(BF16), d_a0[i].reshape(2, 1, D_GROUP), d_a2[i].astype(BF16),
                        d_g2[i].astype(BF16), _row(d_k_k[i]), _row(d_k_a[i]), _row(d_r_k[i]),
                        _row(d_gn_g[i]), _row(d_gn_b[i])), n_b, n_l)

        wo = w_out[i].astype(BF16)
        h, f = _outproj(yab, yc, yd, h, mods, _row(norm_ffn_g[i]),
                        (wo[:512], wo[512:768], wo[768:]), n_lat, per_seq)

        s1, s2, e1, e2, thr = _topk(f, peer_wq[i].T.astype(BF16), peer_k1[i].astype(BF16),
                                    peer_k2[i].astype(BF16))
        n_tok = n_lat if last_layer else n_lat + 1
        h = _peer(f, peer_u[i].astype(BF16), peer_v[i].astype(BF16), s1, e1, s2, e2, thr, h,
                  mods, _row(final_g), n_lat, per_seq, n_tok, last_layer)
    return h.reshape(n_b, t_lat, D_MODEL)
```

```python
import functools
import math

import jax
import jax.numpy as jnp
from jax import lax
from jax.experimental import pallas as pl
from jax.experimental.pallas import tpu as pltpu

F32 = jnp.float32
BF16 = jnp.bfloat16
HIGHEST = lax.Precision.HIGHEST

D_MODEL = 1024
DEPTH = 2
GRID_W = 64
EPS = 1e-6
D_GROUP = 256
CONV_A = 31
CHUNK_B = 128
HEADS_B = 4
HEADS_C = 4
HEAD_DIM_C = 64
STATE_C = 128
CONV_C = 5
CHUNK_C = 128
HEADS_D = 4
HEAD_DIM_D = 64
LORA_W = 64
LORA_A = 64
LORA_G = 128
GN_EPS_D = 64e-5
N_KEYS = 128
N_EXPERTS = N_KEYS * N_KEYS
PEER_HEADS = 8
PEER_DK = 256
PEER_TOPK = 16

C_XBC = D_GROUP + 2 * 2 * STATE_C
OFF_B = 2 * D_GROUP
OFF_C = OFF_B + 2 * D_GROUP
OFF_D = OFF_C + D_GROUP + C_XBC + 2 * HEADS_C
D_COLS = 3 * D_GROUP + 2 * LORA_W + 2 * LORA_A + LORA_G
PC_COLS = D_GROUP + C_XBC + 2 * 128

LANES = 128
SEQ_TILE = 256
TOK_TILE = 512
CHUNK_D = 64
EXP_TILE = 1024
VMEM_LIMIT = 56 * 1024 * 1024


def _cparams(n_axes):
    return pltpu.CompilerParams(dimension_semantics=("arbitrary",) * n_axes,
                                vmem_limit_bytes=VMEM_LIMIT)


def _silu(x):
    return x * jax.nn.sigmoid(x)


def _dot(a, b, dims=None):
    a = a.astype(BF16)
    b = b.astype(BF16)
    if dims is None:
        return jnp.dot(a, b, preferred_element_type=F32)
    return lax.dot_general(a, b, (dims, ((), ())), preferred_element_type=F32)


def _dot_hi(a, b):
    return jnp.dot(a, b, precision=HIGHEST, preferred_element_type=F32)


NT = ((1,), (1,))
TN = ((0,), (0,))


def _iota(shape, axis):
    return lax.broadcasted_iota(jnp.int32, shape, axis)


def _head_block_ones(n, width):
    return (_iota((n, n), 0) // width == _iota((n, n), 1) // width).astype(F32)


def _ada_body(cs_ref, w_ref, b_ref, o_ref):
    o_ref[...] = _dot_hi(_silu(cs_ref[...]), w_ref[...]) + b_ref[...]


def _ada(cs, ada_w, ada_b):
    L = ada_w.shape[0]
    nb = 1536
    return pl.pallas_call(
        _ada_body,
        grid=(L, 6 * D_MODEL // nb),
        in_specs=[pl.BlockSpec((8, D_MODEL), lambda l, n: (0, 0)),
                  pl.BlockSpec((None, D_MODEL, nb), lambda l, n: (l, 0, n)),
                  pl.BlockSpec((None, 1, nb), lambda l, n: (l, 0, n))],
        out_specs=pl.BlockSpec((None, 8, nb), lambda l, n: (l, 0, n)),
        out_shape=jax.ShapeDtypeStruct((L, 8, 6 * D_MODEL), F32),
        compiler_params=_cparams(2),
        name="ada",
    )(cs, ada_w, ada_b.reshape(L, 1, 6 * D_MODEL))


def _norm_mod(h, g, shift, scale):
    xn = h * lax.rsqrt(jnp.mean(h * h, axis=-1, keepdims=True) + EPS) * g
    return xn * (1.0 + scale) + shift


def _inproj_tail(h, sh_ref, sc_ref, g_ref, wa, wb, wc, wd, oa, ob, oc, od):
    xm = _norm_mod(h, g_ref[...], sh_ref[...], sc_ref[...]).astype(BF16)
    oa[...] = jnp.dot(xm, wa[...], preferred_element_type=F32)
    ob[...] = jnp.dot(xm, wb[...], preferred_element_type=F32)
    oc[...] = jnp.dot(xm, wc[...], preferred_element_type=F32)
    od[...] = jnp.dot(xm, wd[...], preferred_element_type=F32)


def _inproj_first_body(n_lat, x_ref, ctx_ref, pos_ref, sh_ref, sc_ref, g_ref, wa, wb, wc, wd,
                       oh, oa, ob, oc, od):
    i = pl.program_id(0)
    h = jnp.where(i < n_lat, x_ref[...] + pos_ref[...], ctx_ref[...])
    oh[...] = h
    _inproj_tail(h, sh_ref, sc_ref, g_ref, wa, wb, wc, wd, oa, ob, oc, od)


def _inproj_body(h_ref, sh_ref, sc_ref, g_ref, wa, wb, wc, wd, oa, ob, oc, od):
    _inproj_tail(h_ref[...], sh_ref, sc_ref, g_ref, wa, wb, wc, wd, oa, ob, oc, od)


def _mod_spec(n_lat, per_seq, k):
    def imap(i):
        row = jnp.where(i < n_lat, i // per_seq, 2)
        return (row * 6 + k, 0, 0)
    return pl.BlockSpec((None, 1, D_MODEL), imap)


def _inproj(h_or_parts, mods, g, ws, n_lat, per_seq, first):
    wa, wb, wc, wd = ws
    n_tiles = n_lat + 1
    rows = n_tiles * TOK_TILE
    tile = lambda c: pl.BlockSpec((TOK_TILE, c), lambda i: (i, 0))
    full = lambda a: pl.BlockSpec(a.shape, lambda i: (0,) * a.ndim)
    common_specs = [_mod_spec(n_lat, per_seq, 0), _mod_spec(n_lat, per_seq, 1), full(g),
                    full(wa), full(wb), full(wc), full(wd)]
    out_specs = [tile(512), tile(512), tile(PC_COLS), tile(D_COLS)]
    out_shape = [jax.ShapeDtypeStruct((rows, c), F32) for c in (512, 512, PC_COLS, D_COLS)]
    if first:
        x2, ctx2, pos = h_or_parts
        in_specs = [pl.BlockSpec((TOK_TILE, D_MODEL), lambda i: (jnp.minimum(i, n_lat - 1), 0)),
                    pl.BlockSpec((TOK_TILE, D_MODEL), lambda i: (0, 0)),
                    pl.BlockSpec((TOK_TILE, D_MODEL), lambda i: (i % per_seq, 0))] + common_specs
        return pl.pallas_call(
            functools.partial(_inproj_first_body, n_lat),
            grid=(n_tiles,), in_specs=in_specs,
            out_specs=[tile(D_MODEL)] + out_specs,
            out_shape=[jax.ShapeDtypeStruct((rows, D_MODEL), F32)] + out_shape,
            compiler_params=_cparams(1), name="inproj_first",
        )(x2, ctx2, pos, mods, mods, g, wa, wb, wc, wd)
    return pl.pallas_call(
        _inproj_body, grid=(n_tiles,), in_specs=[tile(D_MODEL)] + common_specs,
        out_specs=out_specs, out_shape=out_shape,
        compiler_params=_cparams(1), name="inproj",
    )(h_or_parts, mods, mods, g, wa, wb, wc, wd)


def _layernorm(x, g, b, eps=1e-5):
    mu = jnp.mean(x, axis=-1, keepdims=True)
    xc = x - mu
    var = jnp.mean(xc * xc, axis=-1, keepdims=True)
    return xc * lax.rsqrt(var + eps) * g + b


def _mixab_body(n_lat, n_l, pa, pa_prev, pa_next, pb, cw, cb, alg, alb, blg, blb, ws, bsm,
                out, ext):
    i = pl.program_id(0)
    is_ctx = i >= n_lat
    tpos = i % n_l
    first = jnp.logical_or(is_ctx, tpos == 0)
    last = jnp.logical_or(is_ctx, tpos == n_l - 1)

    def glu(x):
        return x[:, :D_GROUP] * jax.nn.sigmoid(x[:, D_GROUP:])

    halo = 16
    ext[0:halo, :] = jnp.where(first, 0.0, glu(pa_prev[...]))
    ext[halo:halo + SEQ_TILE, :] = glu(pa[...])
    ext[halo + SEQ_TILE:, :] = jnp.where(last, 0.0, glu(pa_next[...]))
    acc = jnp.zeros((SEQ_TILE, D_GROUP), F32) + cb[...]
    for j in range(CONV_A):
        acc = acc + cw[j:j + 1, :] * ext[pl.ds(halo - (CONV_A - 1) // 2 + j, SEQ_TILE), :]
    out[:, :D_GROUP] = _silu(_layernorm(acc, alg[...], alb[...]))

    x = pb[...]
    u = x[:, :D_GROUP]
    v = _layernorm(x[:, D_GROUP:], blg[...], blb[...]).astype(BF16)
    hd = D_GROUP // HEADS_B
    rows = []
    for c in range(SEQ_TILE // CHUNK_B):
        vc = v[c * CHUNK_B:(c + 1) * CHUNK_B, :]
        heads = [jnp.dot(ws[hh], vc[:, hh * hd:(hh + 1) * hd], preferred_element_type=F32)
                 for hh in range(HEADS_B)]
        rows.append(jnp.concatenate(heads, axis=1) + bsm[...])
    out[:, D_GROUP:] = u * jnp.concatenate(rows, axis=0)


def _mixab(pa, pb, prm, n_lat, n_l):
    rows = pa.shape[0]
    n_tiles = rows // SEQ_TILE
    halo = 16
    per = SEQ_TILE // halo
    full = lambda a: pl.BlockSpec(a.shape, lambda i: (0,) * a.ndim)
    in_specs = [pl.BlockSpec((SEQ_TILE, 512), lambda i: (i, 0)),
                pl.BlockSpec((halo, 512), lambda i: (jnp.maximum(i * per - 1, 0), 0)),
                pl.BlockSpec((halo, 512), lambda i: (jnp.minimum((i + 1) * per, rows // halo - 1), 0)),
                pl.BlockSpec((SEQ_TILE, 512), lambda i: (i, 0))] + [full(a) for a in prm]
    return pl.pallas_call(
        functools.partial(_mixab_body, n_lat, n_l),
        grid=(n_tiles,), in_specs=in_specs,
        out_specs=pl.BlockSpec((SEQ_TILE, 512), lambda i: (i, 0)),
        out_shape=jax.ShapeDtypeStruct((rows, 512), F32),
        scratch_shapes=[pltpu.VMEM((SEQ_TILE + 2 * halo, D_GROUP), F32)],
        compiler_params=_cparams(1), name="mixab",
    )(pa, pa, pa, pb, *prm)


def _scan_tile(n_b, n_l, b, ph, j):
    lat = b * n_l + jnp.where(ph == 0, j - 1, n_l - j)
    return jnp.where(j == 0, n_b * n_l + b, lat)


def _scan_specs(n_b, n_l, rows, cols):
    tid = functools.partial(_scan_tile, n_b, n_l)
    per = SEQ_TILE // 8
    cur = pl.BlockSpec((SEQ_TILE, cols), lambda b, ph, j: (tid(b, ph, j), 0))
    prev = pl.BlockSpec((8, cols), lambda b, ph, j: (jnp.maximum(tid(b, ph, j) * per - 1, 0), 0))
    nxt = pl.BlockSpec((8, cols),
                       lambda b, ph, j: (jnp.minimum((tid(b, ph, j) + 1) * per, rows // 8 - 1), 0))
    return cur, prev, nxt


def _scan_out_spec(n_b, n_l, cols):
    def imap(b, ph, j):
        return (jnp.where(ph == 0, n_b * n_l + b, _scan_tile(n_b, n_l, b, 1, j)), 0)
    return pl.BlockSpec((SEQ_TILE, cols), imap)


def _scan_flags(n_l):
    ph = pl.program_id(1)
    j = pl.program_id(2)
    is_ctx = j == 0
    tpos = jnp.where(ph == 0, j - 1, n_l - j)
    first = jnp.logical_or(is_ctx, tpos == 0)
    last = jnp.logical_or(is_ctx, tpos == n_l - 1)
    slot = jnp.where(is_ctx, 0, tpos + 1)
    return ph, j, first, last, slot


def _mamba_body(n_l, cur_ref, prev_ref, next_ref, dt_ref, cw, cb, dtb, alog, dsk, ng,
                out, ext, yf, st, xbc_s, a_s, dt_s, y_s):
    ph, j, first, last, slot = _scan_flags(n_l)
    fwd = ph == 0

    @pl.when(j == 0)
    def _():
        st[...] = jnp.zeros_like(st)

    zx = D_GROUP
    ext[0:8, :] = jnp.where(first, 0.0, prev_ref[:, zx:])
    ext[8:8 + SEQ_TILE, :] = cur_ref[:, zx:]
    ext[8 + SEQ_TILE:, :] = jnp.where(last, 0.0, next_ref[:, zx:])
    acc = jnp.zeros((SEQ_TILE, C_XBC), F32) + cb[...]
    for jj in range(CONV_C):
        acc = acc + cw[jj:jj + 1, :] * ext[pl.ds(8 - (CONV_C - 1) // 2 + jj, SEQ_TILE), :]
    xbc_s[...] = _silu(acc)
    dt = jax.nn.softplus(dt_ref[...] + dtb[ph])
    dt_s[...] = dt
    a_s[...] = dt * (-jnp.exp(alog[ph]))

    L = CHUNK_C
    r_i = _iota((L, L), 0)
    c_i = _iota((L, L), 1)
    ltri = (r_i >= c_i).astype(F32)
    mask = jnp.where(fwd, r_i - c_i, c_i - r_i) >= 0
    sgn = jnp.where(fwd, 1.0, -1.0)
    hd = HEAD_DIM_C

    def chunk(it, carry):
        q = jnp.where(fwd, it, SEQ_TILE // L - 1 - it)
        rows = pl.ds(pl.multiple_of(q * L, L), L)
        a_q = a_s[rows, :]
        cs = _dot_hi(ltri, a_q)
        tot = cs[L - 1:L, :]
        e = jnp.where(fwd, cs, cs - a_q)
        e_t = e.T
        dt_q = dt_s[rows, :]
        xbc = xbc_s[rows, :]
        ys = []
        for g in range(2):
            bm = xbc[:, D_GROUP + g * STATE_C:D_GROUP + (g + 1) * STATE_C]
            cm = xbc[:, D_GROUP + 2 * STATE_C + g * STATE_C:D_GROUP + 2 * STATE_C + (g + 1) * STATE_C]
            bm_t = bm.T.astype(BF16)
            cm_b = cm.astype(BF16)
            gmat = jnp.dot(cm_b, bm_t, preferred_element_type=F32)
            for h in (2 * g, 2 * g + 1):
                ecol = e[:, h:h + 1]
                erow = e_t[h:h + 1, :]
                totc = tot[:, h:h + 1]
                xdt = xbc[:, h * hd:(h + 1) * hd] * dt_q[:, h:h + 1]
                lmat = jnp.exp(jnp.where(mask, sgn * (ecol - erow), -1e30))
                offs = jnp.exp(jnp.where(fwd, ecol, totc - ecol))
                stw = jnp.exp(jnp.where(fwd, totc - ecol, ecol))
                s_prev = st[h]
                y = _dot(gmat * lmat, xdt) + offs * _dot(cm_b, s_prev)
                st[h] = jnp.exp(totc) * s_prev + _dot(bm_t, xdt * stw)
                ys.append(y)
        y_s[rows, :] = jnp.concatenate(ys, axis=1)
        return carry

    lax.fori_loop(0, SEQ_TILE // L, chunk, 0)

    @pl.when(fwd)
    def _():
        yf[slot] = y_s[...]

    @pl.when(ph == 1)
    def _():
        y = yf[slot] + y_s[...] + dsk[...] * xbc_s[:, :D_GROUP]
        t = y * _silu(cur_ref[:, :D_GROUP])
        out[...] = t * lax.rsqrt(jnp.mean(t * t, axis=-1, keepdims=True) + EPS) * ng[...]


def _mamba(pc, prm, n_b, n_l):
    rows = pc.shape[0]
    cur, prev, nxt = _scan_specs(n_b, n_l, rows, D_GROUP + C_XBC)
    tid = functools.partial(_scan_tile, n_b, n_l)
    dt_spec = pl.BlockSpec((SEQ_TILE, LANES),
                           lambda b, ph, j: (tid(b, ph, j), (D_GROUP + C_XBC) // LANES + ph))
    full = lambda a: pl.BlockSpec(a.shape, lambda b, ph, j: (0,) * a.ndim)
    return pl.pallas_call(
        functools.partial(_mamba_body, n_l),
        grid=(n_b, 2, n_l + 1),
        in_specs=[cur, prev, nxt, dt_spec] + [full(a) for a in prm],
        out_specs=_scan_out_spec(n_b, n_l, D_GROUP),
        out_shape=jax.ShapeDtypeStruct((rows, D_GROUP), F32),
        scratch_shapes=[pltpu.VMEM((SEQ_TILE + 16, C_XBC), F32),
                        pltpu.VMEM((n_l + 1, SEQ_TILE, D_GROUP), F32),
                        pltpu.VMEM((HEADS_C, STATE_C, HEAD_DIM_C), F32),
                        pltpu.VMEM((SEQ_TILE, C_XBC), F32),
                        pltpu.VMEM((SEQ_TILE, LANES), F32),
                        pltpu.VMEM((SEQ_TILE, LANES), F32),
                        pltpu.VMEM((SEQ_TILE, D_GROUP), F32)],
        compiler_params=_cparams(3), name="mamba",
    )(pc, pc, pc, pc, *prm)


def _rwkv_body(n_l, cur_ref, prev_ref, next_ref, mup, mun, w0, w2, a0, a2, g2, kkw, kaw, rkw,
               gng, gnb, out, yf, st, r_s, v_s, kap_s, alp_s, kd_s, lw_s, y_s,
               phi_s, psi_s, sin_s, left_s, amat_s, ara_s, tinv_s):
    ph, j, first, last, slot = _scan_flags(n_l)
    fwd = ph == 0

    @pl.when(j == 0)
    def _():
        st[...] = jnp.zeros_like(st)

    cur = cur_ref[...]
    prv = jnp.concatenate([jnp.where(first, 0.0, prev_ref[7:8, :]), cur[:SEQ_TILE - 1, :]], axis=0)
    nxt = jnp.concatenate([cur[1:, :], jnp.where(last, 0.0, next_ref[0:1, :])], axis=0)
    p = cur + mup[...] * (prv - cur) + mun[...] * (nxt - cur)
    G = D_GROUP
    r = p[:, :G]
    k = p[:, G:2 * G]
    v = p[:, 2 * G:3 * G]
    blk = _head_block_ones(G, HEAD_DIM_D)
    kkr = k * kkw[...]
    kk = kkr * lax.rsqrt(_dot_hi(kkr * kkr, blk) + 1e-12)

    def rate(d_static=None):
        if d_static is None:
            ad = jnp.where(fwd, p[:, 3 * G + 2 * LORA_W:3 * G + 2 * LORA_W + LORA_A],
                           p[:, 3 * G + 2 * LORA_W + LORA_A:3 * G + 2 * LORA_W + 2 * LORA_A])
            a = jax.nn.sigmoid(a0[ph] + _dot(ad, a2[ph]))
        else:
            o = 3 * G + 2 * LORA_W + d_static * LORA_A
            a = jax.nn.sigmoid(a0[d_static] + _dot(p[:, o:o + LORA_A], a2[d_static]))
        return a, k * (1.0 + (a - 1.0) * kaw[...])

    a_d, kd_d = rate()
    wd = jnp.where(fwd, p[:, 3 * G:3 * G + LORA_W], p[:, 3 * G + LORA_W:3 * G + 2 * LORA_W])
    w = w0[ph] + _dot(jnp.tanh(wd), w2[ph])
    lw_s[...] = -math.exp(-0.5) * jax.nn.sigmoid(w)
    r_s[...] = r
    v_s[...] = v
    kap_s[...] = kk
    alp_s[...] = a_d * kk
    kd_s[...] = kd_d

    C = CHUNK_D
    r_i = _iota((C, C), 0)
    c_i = _iota((C, C), 1)
    lag = jnp.where(fwd, r_i - c_i, c_i - r_i)
    tri = lag >= 0
    tri_f = tri.astype(F32)
    strict = lag > 0
    eye = (r_i == c_i).astype(F32)
    hd = HEAD_DIM_D

    n_q = SEQ_TILE // C
    eye_k = (_iota((hd, hd), 0) == _iota((hd, hd), 1)).astype(F32)

    units = [(q, h) for q in range(n_q) for h in range(HEADS_D)]
    rows_of = lambda q: slice(q * C, (q + 1) * C)
    lanes_of = lambda h: slice(h * hd, (h + 1) * hd)
    left, right, p_tot = {}, {}, {}
    for q in range(n_q):
        rows = rows_of(q)
        lw = lw_s[rows, :]
        incl = _dot_hi(tri_f, lw)
        tot = jnp.where(fwd, incl[C - 1:C, :], incl[0:1, :])
        p_inv = jnp.exp(-incl)
        kap_h = kap_s[rows, :] * jnp.exp(incl - lw)
        r_h = r_s[rows, :] * jnp.exp(incl)
        alp_b = alp_s[rows, :] * p_inv
        k_b = kd_s[rows, :] * p_inv
        ptq = jnp.exp(tot)
        for h in range(HEADS_D):
            sl = lanes_of(h)
            left[q, h] = jnp.concatenate([kap_h[:, sl], r_h[:, sl]], axis=0)
            right[q, h] = jnp.concatenate([alp_b[:, sl], k_b[:, sl]], axis=0)
            p_tot[q, h] = ptq[:, sl]
            left_s[q * HEADS_D + h] = left[q, h]
    m1 = {u: _dot(left[u], right[u], NT) for u in units}
    a_vk, pw, tinv = {}, {}, {}
    for u in units:
        qh = u[0] * HEADS_D + u[1]
        a_vk[u] = jnp.where(strict, m1[u][:C, C:], 0.0)
        amat_s[qh] = jnp.concatenate([a_vk[u], jnp.where(tri, m1[u][C:, C:], 0.0)], axis=0)
        ara_s[qh] = jnp.where(tri, m1[u][C:, :C], 0.0)
        pw[u] = jnp.where(strict, -m1[u][:C, :C], 0.0)
        tinv[u] = eye + pw[u]
    for _ in range(5):
        pw = {u: _dot(pw[u], pw[u]) for u in units}
        upd = {u: _dot(tinv[u], pw[u]) for u in units}
        tinv = {u: tinv[u] + upd[u] for u in units}
    wmat = {u: _dot(tinv[u], right[u][:C], TN) for u in units}
    kw = {u: _dot(left[u][:C], wmat[u], TN) for u in units}
    aw = {u: _dot(a_vk[u], wmat[u], TN) for u in units}
    vk = {u: _dot(v_s[rows_of(u[0]), lanes_of(u[1])], right[u][C:] - aw[u], TN) for u in units}
    for u in units:
        qh = u[0] * HEADS_D + u[1]
        phi_s[qh] = (eye_k - kw[u]) * p_tot[u]
        psi_s[qh] = vk[u] * p_tot[u]
        tinv_s[qh] = tinv[u]

    for it in range(n_q):
        q = jnp.where(fwd, it, n_q - 1 - it)
        for h in range(HEADS_D):
            qh = q * HEADS_D + h
            s_in = st[h]
            sin_s[qh] = s_in
            st[h] = _dot(s_in, phi_s[qh]) + psi_s[qh]

    qh_of = lambda u: u[0] * HEADS_D + u[1]
    x0 = {u: _dot(left_s[qh_of(u)], sin_s[qh_of(u)], NT) for u in units}
    x1 = {u: _dot(amat_s[qh_of(u)], v_s[rows_of(u[0]), lanes_of(u[1])]) for u in units}
    uu = {u: _dot(tinv_s[qh_of(u)], x0[u][:C] + x1[u][:C]) for u in units}
    au = {u: _dot(ara_s[qh_of(u)], uu[u]) for u in units}
    for q in range(n_q):
        y_s[rows_of(q), :] = jnp.concatenate(
            [x0[q, h][C:] + x1[q, h][C:] - au[q, h] for h in range(HEADS_D)], axis=1)

    @pl.when(fwd)
    def _():
        yf[slot] = y_s[...]

    @pl.when(ph == 1)
    def _():
        y = yf[slot] + y_s[...]
        inv = 1.0 / HEAD_DIM_D
        mu = _dot_hi(y, blk) * inv
        yc = y - mu
        var = _dot_hi(yc * yc, blk) * inv
        yn = yc * lax.rsqrt(var + GN_EPS_D) * gng[...] + gnb[...]
        _, kd_f = rate(0)
        bonus = _dot_hi(r * (kd_f + kd_d) * rkw[...], blk) * v
        gate = _dot(jax.nn.sigmoid(p[:, 3 * G + 2 * LORA_W + 2 * LORA_A:]), g2[...])
        out[...] = (yn + bonus) * gate


def _rwkv(pd, prm, n_b, n_l):
    rows = pd.shape[0]
    cur, prev, nxt = _scan_specs(n_b, n_l, rows, D_COLS)
    full = lambda a: pl.BlockSpec(a.shape, lambda b, ph, j: (0,) * a.ndim)
    tile = lambda: pltpu.VMEM((SEQ_TILE, D_GROUP), F32)
    per_qh = lambda r, c: pltpu.VMEM((SEQ_TILE // CHUNK_D * HEADS_D, r, c), F32)
    return pl.pallas_call(
        functools.partial(_rwkv_body, n_l),
        grid=(n_b, 2, n_l + 1),
        in_specs=[cur, prev, nxt] + [full(a) for a in prm],
        out_specs=_scan_out_spec(n_b, n_l, D_GROUP),
        out_shape=jax.ShapeDtypeStruct((rows, D_GROUP), F32),
        scratch_shapes=[pltpu.VMEM((n_l + 1, SEQ_TILE, D_GROUP), F32),
                        pltpu.VMEM((HEADS_D, HEAD_DIM_D, HEAD_DIM_D), F32)] + [tile() for _ in range(7)]
        + [per_qh(HEAD_DIM_D, HEAD_DIM_D) for _ in range(3)]
        + [per_qh(2 * CHUNK_D, HEAD_DIM_D), per_qh(2 * CHUNK_D, CHUNK_D),
           per_qh(CHUNK_D, CHUNK_D), per_qh(CHUNK_D, CHUNK_D)],
        compiler_params=_cparams(3), name="rwkv",
    )(pd, pd, pd, *prm)


def _outproj_body(yab, yc, yd, h_ref, gate, sh, sc, g, wab, wc, wd, oh, of):
    mix = (_dot(yab[...], wab[...]) + _dot(yc[...], wc[...]) + _dot(yd[...], wd[...]))
    h = h_ref[...] + gate[...] * mix
    oh[...] = h
    of[...] = _norm_mod(h, g[...], sh[...], sc[...]).astype(BF16)


def _outproj(yab, yc, yd, h, mods, g, ws, n_lat, per_seq):
    rows = h.shape[0]
    tile = lambda c: pl.BlockSpec((TOK_TILE, c), lambda i: (i, 0))
    full = lambda a: pl.BlockSpec(a.shape, lambda i: (0,) * a.ndim)
    return pl.pallas_call(
        _outproj_body, grid=(rows // TOK_TILE,),
        in_specs=[tile(512), tile(D_GROUP), tile(D_GROUP), tile(D_MODEL),
                  _mod_spec(n_lat, per_seq, 2), _mod_spec(n_lat, per_seq, 3),
                  _mod_spec(n_lat, per_seq, 4), full(g)] + [full(a) for a in ws],
        out_specs=[tile(D_MODEL), tile(D_MODEL)],
        out_shape=[jax.ShapeDtypeStruct((rows, D_MODEL), F32),
                   jax.ShapeDtypeStruct((rows, D_MODEL), BF16)],
        compiler_params=_cparams(1), name="outproj",
    )(yab, yc, yd, h, mods, mods, mods, g, *ws)


TOPK_TILE = 256


SUBLANES = 8


def _cmpx(lst, i, j):
    a, b = lst[i], lst[j]
    lst[i] = jnp.maximum(a, b)
    lst[j] = jnp.minimum(a, b)


def _bitonic_sort_desc(lst):
    n = len(lst)
    k = 2
    while k <= n:
        j = k // 2
        while j >= 1:
            for i in range(n):
                p = i ^ j
                if p > i:
                    if (i & k) == 0:
                        _cmpx(lst, i, p)
                    else:
                        _cmpx(lst, p, i)
            j //= 2
        k *= 2


def _bitonic_merge_desc(lst):
    n = len(lst)
    j = n // 2
    while j >= 1:
        for i in range(n):
            p = i ^ j
            if p > i:
                _cmpx(lst, i, p)
        j //= 2


def _merge_top(a, b):
    n = len(a)
    c = [jnp.maximum(a[i], b[n - 1 - i]) for i in range(n)]
    _bitonic_merge_desc(c)
    return c


def _merge_sublanes(lst):
    for shift in (4, 2, 1):
        lst = _merge_top(lst, [pltpu.roll(a, shift, axis=0) for a in lst])
    return lst


def _top16_rows(s):
    lst = [s[SUBLANES * v:SUBLANES * (v + 1), :] for v in range(s.shape[0] // SUBLANES)]
    _bitonic_sort_desc(lst)
    return _merge_sublanes(lst)


def _topk_body(f_ref, wq_ref, k1_ref, k2_ref, s1_o, s2_o, e1_o, e2_o, thr_o, q_s):
    q_s[...] = lax.dot_general(wq_ref[...], f_ref[...], (NT, ((), ())),
                               preferred_element_type=F32).astype(BF16)
    half = PEER_DK // 2
    T = f_ref.shape[0]
    sub = _iota((SUBLANES, T), 0)

    def stack(rows):
        out = rows[SUBLANES - 1]
        for b in range(SUBLANES - 2, -1, -1):
            out = jnp.where(sub == b, rows[b], out)
        return out

    def head(h, carry):
        base = pl.multiple_of(h * PEER_DK, PEER_DK)
        s1 = jnp.dot(k1_ref[...], q_s[pl.ds(base, half), :], preferred_element_type=F32)
        s2 = jnp.dot(k2_ref[...], q_s[pl.ds(base + half, half), :], preferred_element_type=F32)
        t1 = _top16_rows(s1)
        t2 = _top16_rows(s2)
        lo = stack(t2[:SUBLANES])
        hi = stack(t2[SUBLANES:])
        top = _merge_sublanes(_merge_top([t + lo for t in t1], [t + hi for t in t1]))
        z = jnp.exp(top[0] - top[0])
        for kk in range(1, PEER_TOPK):
            z = z + jnp.exp(top[kk] - top[0])
        s1_o[h] = s1
        s2_o[h] = s2
        e1_o[h] = jnp.exp(s1 - t1[0][0:1, :]) / z[0:1, :]
        e2_o[h] = jnp.exp(s2 - t2[0][0:1, :])
        thr_o[pl.ds(h, 1), :] = top[PEER_TOPK - 1][0:1, :]
        return carry

    lax.fori_loop(0, PEER_HEADS, head, 0)


def _topk(f, wq_t, k1, k2):
    rows = f.shape[0]
    T = TOPK_TILE
    full = lambda a: pl.BlockSpec(a.shape, lambda i: (0,) * a.ndim)
    big = pl.BlockSpec((PEER_HEADS, N_KEYS, T), lambda i: (0, 0, i))
    big_shape = jax.ShapeDtypeStruct((PEER_HEADS, N_KEYS, rows), F32)
    return pl.pallas_call(
        _topk_body, grid=(rows // T,),
        in_specs=[pl.BlockSpec((T, D_MODEL), lambda i: (i, 0)), full(wq_t), full(k1), full(k2)],
        out_specs=[big, big, big, big, pl.BlockSpec((PEER_HEADS, T), lambda i: (0, i))],
        out_shape=[big_shape] * 4 + [jax.ShapeDtypeStruct((PEER_HEADS, rows), F32)],
        scratch_shapes=[pltpu.VMEM((PEER_HEADS * PEER_DK, T), BF16)],
        compiler_params=_cparams(1), name="peer_topk",
    )(f, wq_t, k1, k2)


def _peer_body(final, f_ref, u_ref, v_ref, s1_ref, e1_ref, s2_ref, e2_ref, thr_ref, h_ref,
               gate_ref, fg_ref, out, st_s, at_s, acc):
    jx = pl.program_id(1)

    @pl.when(jx == 0)
    def _():
        acc[...] = jnp.zeros_like(acc)

    st_s[...] = lax.dot_general(u_ref[...], f_ref[...], (NT, ((), ())),
                                preferred_element_type=F32)
    n_rows = EXP_TILE // N_KEYS

    def lane_group(lg, carry):
        ln = pl.ds(pl.multiple_of(lg * LANES, LANES), LANES)
        for ii in range(n_rows):
            rows = slice(ii * N_KEYS, (ii + 1) * N_KEYS)
            gsum = jnp.zeros((N_KEYS, LANES), F32)
            for h in range(PEER_HEADS):
                cand = s1_ref[h, ii:ii + 1, ln] + s2_ref[h, :, ln]
                wgt = e1_ref[h, ii:ii + 1, ln] * e2_ref[h, :, ln]
                gsum = gsum + jnp.where(cand >= thr_ref[h:h + 1, ln], wgt, 0.0)
            at_s[rows, ln] = (jax.nn.gelu(st_s[rows, ln]) * gsum).astype(BF16)
        return carry

    lax.fori_loop(0, TOK_TILE // LANES, lane_group, 0)
    acc[...] += lax.dot_general(at_s[...], v_ref[...], (TN, ((), ())), preferred_element_type=F32)

    @pl.when(jx == pl.num_programs(1) - 1)
    def _():
        h = h_ref[...] + gate_ref[...] * acc[...]
        if final:
            h = h * lax.rsqrt(jnp.mean(h * h, axis=-1, keepdims=True) + EPS) * fg_ref[...]
        out[...] = h


def _peer(f, u_b, v_b, s1, e1, s2, e2, thr, h, mods, fg, n_lat, per_seq, n_tok_tiles, final):
    n_e = N_EXPERTS // EXP_TILE
    rpt = EXP_TILE // N_KEYS
    tok = lambda c: pl.BlockSpec((TOK_TILE, c), lambda i, j: (i, 0))
    exp = pl.BlockSpec((EXP_TILE, D_MODEL), lambda i, j: (j, 0))
    rowblk = pl.BlockSpec((PEER_HEADS, rpt, TOK_TILE), lambda i, j: (0, j, i))
    allkeys = pl.BlockSpec((PEER_HEADS, N_KEYS, TOK_TILE), lambda i, j: (0, 0, i))

    def gate_map(i, j):
        row = jnp.where(i < n_lat, i // per_seq, 2)
        return (row * 6 + 5, 0, 0)

    return pl.pallas_call(
        functools.partial(_peer_body, final),
        grid=(n_tok_tiles, n_e),
        in_specs=[tok(D_MODEL), exp, exp, rowblk, rowblk, allkeys, allkeys,
                  pl.BlockSpec((PEER_HEADS, TOK_TILE), lambda i, j: (0, i)), tok(D_MODEL),
                  pl.BlockSpec((None, 1, D_MODEL), gate_map),
                  pl.BlockSpec((1, D_MODEL), lambda i, j: (0, 0))],
        out_specs=tok(D_MODEL),
        out_shape=jax.ShapeDtypeStruct((n_tok_tiles * TOK_TILE, D_MODEL), F32),
        scratch_shapes=[pltpu.VMEM((EXP_TILE, TOK_TILE), F32),
                        pltpu.VMEM((EXP_TILE, TOK_TILE), BF16),
                        pltpu.VMEM((TOK_TILE, D_MODEL), F32)],
        compiler_params=_cparams(2), name="peer",
    )(f, u_b, v_b, s1, e1, s2, e2, thr, h, mods, fg)


def _sincos_2d(t_len):
    rows = t_len // GRID_W
    row = jnp.repeat(jnp.arange(rows), GRID_W).astype(F32)
    col = jnp.tile(jnp.arange(GRID_W), rows).astype(F32)
    q = D_MODEL // 4
    freq = 10000.0 ** (-jnp.arange(q, dtype=F32) / q)
    ar = row[:, None] * freq
    ac = col[:, None] * freq
    return jnp.concatenate([jnp.sin(ar), jnp.cos(ar), jnp.sin(ac), jnp.cos(ac)], -1)


def _row(a):
    return a.reshape(1, -1).astype(F32)


def _pad_lanes(a, width=LANES):
    return jnp.pad(a, ((0, 0), (0, width - a.shape[-1])))


def kernel(x, c, ctx, c_ctx, ada_w, ada_b, norm_mix_g, norm_ffn_g, w_in, w_out, a_conv_w, a_conv_b, a_ln_g, a_ln_b, b_ln_g, b_ln_b, b_ws, b_bs, c_conv_w, c_conv_b, c_dt_bias, c_a_log, c_d, c_norm_g, d_mu_prev, d_mu_next, d_w0, d_w2, d_a0, d_a2, d_g2, d_k_k, d_k_a, d_r_k, d_gn_g, d_gn_b, peer_wq, peer_k1, peer_k2, peer_u, peer_v, final_g):
    n_b, t_lat, _ = x.shape
    t_ctx = ctx.shape[1]
    assert n_b == 2 and t_ctx == SEQ_TILE and t_lat % TOK_TILE == 0
    n_l = t_lat // SEQ_TILE
    per_seq = t_lat // TOK_TILE
    n_lat = n_b * per_seq
    n_lat256 = n_b * n_l

    cs = jnp.zeros((8, D_MODEL), F32).at[:n_b].set(c).at[n_b].set(c_ctx)
    mods_all = _ada(cs, ada_w, ada_b)
    pos = _sincos_2d(t_lat)
    x2 = x.reshape(n_b * t_lat, D_MODEL)
    ctx2 = ctx.reshape(n_b * t_ctx, D_MODEL)

    h = None
    for i in range(DEPTH):
        last_layer = i == DEPTH - 1
        mods = mods_all[i].reshape(8 * 6, 1, D_MODEL)
        w = w_in[i]
        dtc = OFF_C + D_GROUP + C_XBC
        w_c = jnp.concatenate([w[:, OFF_C:dtc], _pad_lanes(w[:, dtc:dtc + HEADS_C]),
                               _pad_lanes(w[:, dtc + HEADS_C:dtc + 2 * HEADS_C])], axis=1)
        ws_in = tuple(a.astype(BF16) for a in (w[:, :OFF_B], w[:, OFF_B:OFF_C], w_c, w[:, OFF_D:]))
        g_mix = _row(norm_mix_g[i])
        if i == 0:
            h, pa, pb, pc, pd = _inproj((x2, ctx2, pos), mods, g_mix, ws_in, n_lat, per_seq, True)
        else:
            pa, pb, pc, pd = _inproj(h, mods, g_mix, ws_in, n_lat, per_seq, False)

        conv_a = jnp.pad(a_conv_w[i], ((0, 32 - CONV_A), (0, 0)))
        bsm = jnp.repeat(b_bs[i].T, D_GROUP // HEADS_B, axis=1)
        yab = _mixab(pa, pb, (conv_a, _row(a_conv_b[i]), _row(a_ln_g[i]), _row(a_ln_b[i]),
                              _row(b_ln_g[i]), _row(b_ln_b[i]), b_ws[i].astype(BF16), bsm),
                     n_lat256, n_l)

        conv_c = jnp.pad(c_conv_w[i], ((0, 8 - CONV_C), (0, 0)))
        dtb = _pad_lanes(c_dt_bias[i]).reshape(2, 1, LANES)
        alog = _pad_lanes(c_a_log[i]).reshape(2, 1, LANES)
        dsk = _row(jnp.repeat(c_d[i], HEAD_DIM_C))
        yc = _mamba(pc, (conv_c, _row(c_conv_b[i]), dtb, alog, dsk, _row(c_norm_g[i])), n_b, n_l)

        yd = _rwkv(pd, (_row(d_mu_prev[i]), _row(d_mu_next[i]), d_w0[i].reshape(2, 1, D_GROUP),
                        d_w2[i].astype(BF16), d_a0[i].reshape(2, 1, D_GROUP), d_a2[i].astype(BF16),
                        d_g2[i].astype(BF16), _row(d_k_k[i]), _row(d_k_a[i]), _row(d_r_k[i]),
                        _row(d_gn_g[i]), _row(d_gn_b[i])), n_b, n_l)

        wo = w_out[i].astype(BF16)
        h, f = _outproj(yab, yc, yd, h, mods, _row(norm_ffn_g[i]),
                        (wo[:512], wo[512:768], wo[768:]), n_lat, per_seq)

        s1, s2, e1, e2, thr = _topk(f, peer_wq[i].T.astype(BF16), peer_k1[i].astype(BF16),
                                    peer_k2[i].astype(BF16))
        n_tok = n_lat if last_layer else n_lat + 1
        h = _peer(f, peer_u[i].astype(BF16), peer_v[i].astype(BF16), s1, e1, s2, e2, thr, h,
                  mods, _row(final_g), n_lat, per_seq, n_tok, last_layer)
    return h.reshape(n_b, t_lat, D_MODEL)
```

```python
import functools
import math

import jax
import jax.numpy as jnp
from jax import lax
from jax.experimental import pallas as pl
from jax.experimental.pallas import tpu as pltpu

F32 = jnp.float32
BF16 = jnp.bfloat16
U32 = jnp.uint32
HIGHEST = lax.Precision.HIGHEST

D_MODEL = 1024
DEPTH = 2
GRID_W = 64
EPS = 1e-6
D_GROUP = 256
CONV_A = 31
CHUNK_B = 128
HEADS_B = 4
HEADS_C = 4
HEAD_DIM_C = 64
STATE_C = 128
CONV_C = 5
CHUNK_C = 128
HEADS_D = 4
HEAD_DIM_D = 64
LORA_W = 64
LORA_A = 64
LORA_G = 128
GN_EPS_D = 64e-5
N_KEYS = 128
N_EXPERTS = N_KEYS * N_KEYS
PEER_HEADS = 8
PEER_DK = 256
PEER_TOPK = 16

C_XBC = D_GROUP + 2 * 2 * STATE_C
OFF_B = 2 * D_GROUP
OFF_C = OFF_B + 2 * D_GROUP
OFF_D = OFF_C + D_GROUP + C_XBC + 2 * HEADS_C
D_COLS = 3 * D_GROUP + 2 * LORA_W + 2 * LORA_A + LORA_G
PC_COLS = D_GROUP + C_XBC + 2 * 128

LANES = 128
SEQ_TILE = 256
TOK_TILE = 512
CHUNK_D = 64
EXP_TILE = 1024
VMEM_LIMIT = 56 * 1024 * 1024


def _cparams(n_axes):
    return pltpu.CompilerParams(dimension_semantics=("arbitrary",) * n_axes,
                                vmem_limit_bytes=VMEM_LIMIT)


def _silu(x):
    return x * jax.nn.sigmoid(x)


def _dot(a, b, dims=None):
    a = a.astype(BF16)
    b = b.astype(BF16)
    if dims is None:
        return jnp.dot(a, b, preferred_element_type=F32)
    return lax.dot_general(a, b, (dims, ((), ())), preferred_element_type=F32)


def _dot_hi(a, b):
    return jnp.dot(a, b, precision=HIGHEST, preferred_element_type=F32)


NT = ((1,), (1,))
TN = ((0,), (0,))


def _gelu_tanh(x):
    k1 = -2.0 * math.sqrt(2.0 / math.pi) * math.log2(math.e)
    k3 = 0.044715 * k1
    return x / (1.0 + jnp.exp2(x * (k1 + k3 * (x * x))))


def _pack_rows(x):
    return pltpu.bitcast(x.astype(BF16), U32)


def _unpack_rows(x):
    return pltpu.bitcast(x, BF16)


def _iota(shape, axis):
    return lax.broadcasted_iota(jnp.int32, shape, axis)


def _head_block_ones(n, width):
    return (_iota((n, n), 0) // width == _iota((n, n), 1) // width).astype(F32)


def _ada_body(cs_ref, w_ref, b_ref, o_ref):
    o_ref[...] = _dot_hi(_silu(cs_ref[...]), w_ref[...]) + b_ref[...]


def _ada(cs, ada_w, ada_b):
    L = ada_w.shape[0]
    nb = 1536
    return pl.pallas_call(
        _ada_body,
        grid=(L, 6 * D_MODEL // nb),
        in_specs=[pl.BlockSpec((8, D_MODEL), lambda l, n: (0, 0)),
                  pl.BlockSpec((None, D_MODEL, nb), lambda l, n: (l, 0, n)),
                  pl.BlockSpec((None, 1, nb), lambda l, n: (l, 0, n))],
        out_specs=pl.BlockSpec((None, 8, nb), lambda l, n: (l, 0, n)),
        out_shape=jax.ShapeDtypeStruct((L, 8, 6 * D_MODEL), F32),
        compiler_params=_cparams(2),
        name="ada",
    )(cs, ada_w, ada_b.reshape(L, 1, 6 * D_MODEL))


def _norm_mod(h, g, shift, scale):
    xn = h * lax.rsqrt(jnp.mean(h * h, axis=-1, keepdims=True) + EPS) * g
    return xn * (1.0 + scale) + shift


def _inproj_tail(h, sh_ref, sc_ref, g_ref, wa, wb, wc, wd, oa, ob, oc, od):
    xm = _norm_mod(h, g_ref[...], sh_ref[...], sc_ref[...]).astype(BF16)
    oa[...] = jnp.dot(xm, wa[...], preferred_element_type=F32)
    ob[...] = jnp.dot(xm, wb[...], preferred_element_type=F32)
    oc[...] = jnp.dot(xm, wc[...], preferred_element_type=F32)
    od[...] = jnp.dot(xm, wd[...], preferred_element_type=F32)


def _inproj_first_body(n_lat, x_ref, ctx_ref, pos_ref, sh_ref, sc_ref, g_ref, wa, wb, wc, wd,
                       oh, oa, ob, oc, od):
    i = pl.program_id(0)
    h = jnp.where(i < n_lat, x_ref[...] + pos_ref[...], ctx_ref[...])
    oh[...] = h
    _inproj_tail(h, sh_ref, sc_ref, g_ref, wa, wb, wc, wd, oa, ob, oc, od)


def _inproj_body(h_ref, sh_ref, sc_ref, g_ref, wa, wb, wc, wd, oa, ob, oc, od):
    _inproj_tail(h_ref[...], sh_ref, sc_ref, g_ref, wa, wb, wc, wd, oa, ob, oc, od)


def _mod_spec(n_lat, per_seq, k):
    def imap(i):
        row = jnp.where(i < n_lat, i // per_seq, 2)
        return (row * 6 + k, 0, 0)
    return pl.BlockSpec((None, 1, D_MODEL), imap)


def _inproj(h_or_parts, mods, g, ws, n_lat, per_seq, first):
    wa, wb, wc, wd = ws
    n_tiles = n_lat + 1
    rows = n_tiles * TOK_TILE
    tile = lambda c: pl.BlockSpec((TOK_TILE, c), lambda i: (i, 0))
    full = lambda a: pl.BlockSpec(a.shape, lambda i: (0,) * a.ndim)
    common_specs = [_mod_spec(n_lat, per_seq, 0), _mod_spec(n_lat, per_seq, 1), full(g),
                    full(wa), full(wb), full(wc), full(wd)]
    out_specs = [tile(512), tile(512), tile(PC_COLS), tile(D_COLS)]
    out_shape = [jax.ShapeDtypeStruct((rows, c), F32) for c in (512, 512, PC_COLS, D_COLS)]
    if first:
        x2, ctx2, pos = h_or_parts
        in_specs = [pl.BlockSpec((TOK_TILE, D_MODEL), lambda i: (jnp.minimum(i, n_lat - 1), 0)),
                    pl.BlockSpec((TOK_TILE, D_MODEL), lambda i: (0, 0)),
                    pl.BlockSpec((TOK_TILE, D_MODEL), lambda i: (i % per_seq, 0))] + common_specs
        return pl.pallas_call(
            functools.partial(_inproj_first_body, n_lat),
            grid=(n_tiles,), in_specs=in_specs,
            out_specs=[tile(D_MODEL)] + out_specs,
            out_shape=[jax.ShapeDtypeStruct((rows, D_MODEL), F32)] + out_shape,
            compiler_params=_cparams(1), name="inproj_first",
        )(x2, ctx2, pos, mods, mods, g, wa, wb, wc, wd)
    return pl.pallas_call(
        _inproj_body, grid=(n_tiles,), in_specs=[tile(D_MODEL)] + common_specs,
        out_specs=out_specs, out_shape=out_shape,
        compiler_params=_cparams(1), name="inproj",
    )(h_or_parts, mods, mods, g, wa, wb, wc, wd)


def _layernorm(x, g, b, eps=1e-5):
    mu = jnp.mean(x, axis=-1, keepdims=True)
    xc = x - mu
    var = jnp.mean(xc * xc, axis=-1, keepdims=True)
    return xc * lax.rsqrt(var + eps) * g + b


def _mixab_body(n_lat, n_l, pa, pa_prev, pa_next, pb, cw, cb, alg, alb, blg, blb, ws, bsm,
                out, ext):
    i = pl.program_id(0)
    is_ctx = i >= n_lat
    tpos = i % n_l
    first = jnp.logical_or(is_ctx, tpos == 0)
    last = jnp.logical_or(is_ctx, tpos == n_l - 1)

    def glu(x):
        return x[:, :D_GROUP] * jax.nn.sigmoid(x[:, D_GROUP:])

    halo = 16
    ext[0:halo, :] = jnp.where(first, 0.0, glu(pa_prev[...]))
    ext[halo:halo + SEQ_TILE, :] = glu(pa[...])
    ext[halo + SEQ_TILE:, :] = jnp.where(last, 0.0, glu(pa_next[...]))
    acc = jnp.zeros((SEQ_TILE, D_GROUP), F32) + cb[...]
    for j in range(CONV_A):
        acc = acc + cw[j:j + 1, :] * ext[pl.ds(halo - (CONV_A - 1) // 2 + j, SEQ_TILE), :]
    out[:, :D_GROUP] = _silu(_layernorm(acc, alg[...], alb[...]))

    x = pb[...]
    u = x[:, :D_GROUP]
    v = _layernorm(x[:, D_GROUP:], blg[...], blb[...]).astype(BF16)
    hd = D_GROUP // HEADS_B
    rows = []
    for c in range(SEQ_TILE // CHUNK_B):
        vc = v[c * CHUNK_B:(c + 1) * CHUNK_B, :]
        heads = [jnp.dot(ws[hh], vc[:, hh * hd:(hh + 1) * hd], preferred_element_type=F32)
                 for hh in range(HEADS_B)]
        rows.append(jnp.concatenate(heads, axis=1) + bsm[...])
    out[:, D_GROUP:] = u * jnp.concatenate(rows, axis=0)


def _mixab(pa, pb, prm, n_lat, n_l):
    rows = pa.shape[0]
    n_tiles = rows // SEQ_TILE
    halo = 16
    per = SEQ_TILE // halo
    full = lambda a: pl.BlockSpec(a.shape, lambda i: (0,) * a.ndim)
    in_specs = [pl.BlockSpec((SEQ_TILE, 512), lambda i: (i, 0)),
                pl.BlockSpec((halo, 512), lambda i: (jnp.maximum(i * per - 1, 0), 0)),
                pl.BlockSpec((halo, 512), lambda i: (jnp.minimum((i + 1) * per, rows // halo - 1), 0)),
                pl.BlockSpec((SEQ_TILE, 512), lambda i: (i, 0))] + [full(a) for a in prm]
    return pl.pallas_call(
        functools.partial(_mixab_body, n_lat, n_l),
        grid=(n_tiles,), in_specs=in_specs,
        out_specs=pl.BlockSpec((SEQ_TILE, 512), lambda i: (i, 0)),
        out_shape=jax.ShapeDtypeStruct((rows, 512), F32),
        scratch_shapes=[pltpu.VMEM((SEQ_TILE + 2 * halo, D_GROUP), F32)],
        compiler_params=_cparams(1), name="mixab",
    )(pa, pa, pa, pb, *prm)


def _scan_tile(n_b, n_l, b, ph, j):
    lat = b * n_l + jnp.where(ph == 0, j - 1, n_l - j)
    return jnp.where(j == 0, n_b * n_l + b, lat)


def _scan_specs(n_b, n_l, rows, cols):
    tid = functools.partial(_scan_tile, n_b, n_l)
    per = SEQ_TILE // 8
    cur = pl.BlockSpec((SEQ_TILE, cols), lambda b, ph, j: (tid(b, ph, j), 0))
    prev = pl.BlockSpec((8, cols), lambda b, ph, j: (jnp.maximum(tid(b, ph, j) * per - 1, 0), 0))
    nxt = pl.BlockSpec((8, cols),
                       lambda b, ph, j: (jnp.minimum((tid(b, ph, j) + 1) * per, rows // 8 - 1), 0))
    return cur, prev, nxt


def _scan_out_spec(n_b, n_l, cols):
    def imap(b, ph, j):
        return (jnp.where(ph == 0, n_b * n_l + b, _scan_tile(n_b, n_l, b, 1, j)), 0)
    return pl.BlockSpec((SEQ_TILE, cols), imap)


def _scan_flags(n_l):
    ph = pl.program_id(1)
    j = pl.program_id(2)
    is_ctx = j == 0
    tpos = jnp.where(ph == 0, j - 1, n_l - j)
    first = jnp.logical_or(is_ctx, tpos == 0)
    last = jnp.logical_or(is_ctx, tpos == n_l - 1)
    slot = jnp.where(is_ctx, 0, tpos + 1)
    return ph, j, first, last, slot


def _mamba_body(n_l, cur_ref, prev_ref, next_ref, dt_ref, cw, cb, dtb, alog, dsk, ng,
                out, ext, yf, st, xbc_s, a_s, dt_s, y_s):
    ph, j, first, last, slot = _scan_flags(n_l)
    fwd = ph == 0

    @pl.when(j == 0)
    def _():
        st[...] = jnp.zeros_like(st)

    zx = D_GROUP
    ext[0:8, :] = jnp.where(first, 0.0, prev_ref[:, zx:])
    ext[8:8 + SEQ_TILE, :] = cur_ref[:, zx:]
    ext[8 + SEQ_TILE:, :] = jnp.where(last, 0.0, next_ref[:, zx:])
    acc = jnp.zeros((SEQ_TILE, C_XBC), F32) + cb[...]
    for jj in range(CONV_C):
        acc = acc + cw[jj:jj + 1, :] * ext[pl.ds(8 - (CONV_C - 1) // 2 + jj, SEQ_TILE), :]
    xbc_s[...] = _silu(acc)
    dt = jax.nn.softplus(dt_ref[...] + dtb[ph])
    dt_s[...] = dt
    a_s[...] = dt * (-jnp.exp(alog[ph]))

    L = CHUNK_C
    r_i = _iota((L, L), 0)
    c_i = _iota((L, L), 1)
    ltri = (r_i >= c_i).astype(F32)
    mask = jnp.where(fwd, r_i - c_i, c_i - r_i) >= 0
    sgn = jnp.where(fwd, 1.0, -1.0)
    hd = HEAD_DIM_C

    def chunk(it, carry):
        q = jnp.where(fwd, it, SEQ_TILE // L - 1 - it)
        rows = pl.ds(pl.multiple_of(q * L, L), L)
        a_q = a_s[rows, :]
        cs = _dot_hi(ltri, a_q)
        tot = cs[L - 1:L, :]
        e = jnp.where(fwd, cs, cs - a_q)
        e_t = e.T
        dt_q = dt_s[rows, :]
        xbc = xbc_s[rows, :]
        ys = []
        for g in range(2):
            bm = xbc[:, D_GROUP + g * STATE_C:D_GROUP + (g + 1) * STATE_C]
            cm = xbc[:, D_GROUP + 2 * STATE_C + g * STATE_C:D_GROUP + 2 * STATE_C + (g + 1) * STATE_C]
            bm_t = bm.T.astype(BF16)
            cm_b = cm.astype(BF16)
            gmat = jnp.dot(cm_b, bm_t, preferred_element_type=F32)
            for h in (2 * g, 2 * g + 1):
                ecol = e[:, h:h + 1]
                erow = e_t[h:h + 1, :]
                totc = tot[:, h:h + 1]
                xdt = xbc[:, h * hd:(h + 1) * hd] * dt_q[:, h:h + 1]
                lmat = jnp.exp(jnp.where(mask, sgn * (ecol - erow), -1e30))
                offs = jnp.exp(jnp.where(fwd, ecol, totc - ecol))
                stw = jnp.exp(jnp.where(fwd, totc - ecol, ecol))
                s_prev = st[h]
                y = _dot(gmat * lmat, xdt) + offs * _dot(cm_b, s_prev)
                st[h] = jnp.exp(totc) * s_prev + _dot(bm_t, xdt * stw)
                ys.append(y)
        y_s[rows, :] = jnp.concatenate(ys, axis=1)
        return carry

    lax.fori_loop(0, SEQ_TILE // L, chunk, 0)

    @pl.when(fwd)
    def _():
        yf[slot] = y_s[...]

    @pl.when(ph == 1)
    def _():
        y = yf[slot] + y_s[...] + dsk[...] * xbc_s[:, :D_GROUP]
        t = y * _silu(cur_ref[:, :D_GROUP])
        out[...] = t * lax.rsqrt(jnp.mean(t * t, axis=-1, keepdims=True) + EPS) * ng[...]


def _mamba(pc, prm, n_b, n_l):
    rows = pc.shape[0]
    cur, prev, nxt = _scan_specs(n_b, n_l, rows, D_GROUP + C_XBC)
    tid = functools.partial(_scan_tile, n_b, n_l)
    dt_spec = pl.BlockSpec((SEQ_TILE, LANES),
                           lambda b, ph, j: (tid(b, ph, j), (D_GROUP + C_XBC) // LANES + ph))
    full = lambda a: pl.BlockSpec(a.shape, lambda b, ph, j: (0,) * a.ndim)
    return pl.pallas_call(
        functools.partial(_mamba_body, n_l),
        grid=(n_b, 2, n_l + 1),
        in_specs=[cur, prev, nxt, dt_spec] + [full(a) for a in prm],
        out_specs=_scan_out_spec(n_b, n_l, D_GROUP),
        out_shape=jax.ShapeDtypeStruct((rows, D_GROUP), F32),
        scratch_shapes=[pltpu.VMEM((SEQ_TILE + 16, C_XBC), F32),
                        pltpu.VMEM((n_l + 1, SEQ_TILE, D_GROUP), F32),
                        pltpu.VMEM((HEADS_C, STATE_C, HEAD_DIM_C), F32),
                        pltpu.VMEM((SEQ_TILE, C_XBC), F32),
                        pltpu.VMEM((SEQ_TILE, LANES), F32),
                        pltpu.VMEM((SEQ_TILE, LANES), F32),
                        pltpu.VMEM((SEQ_TILE, D_GROUP), F32)],
        compiler_params=_cparams(3), name="mamba",
    )(pc, pc, pc, pc, *prm)


def _rwkv_body(n_l, cur_ref, prev_ref, next_ref, mup, mun, w0, w2, a0, a2, g2, kkw, kaw, rkw,
               gng, gnb, out, yf, st, r_s, v_s, kap_s, alp_s, kd_s, lw_s, y_s,
               phi_s, psi_s, sin_s, left_s, amat_s, ara_s, tinv_s):
    ph, j, first, last, slot = _scan_flags(n_l)
    fwd = ph == 0

    @pl.when(j == 0)
    def _():
        st[...] = jnp.zeros_like(st)

    cur = cur_ref[...]
    prv = jnp.concatenate([jnp.where(first, 0.0, prev_ref[7:8, :]), cur[:SEQ_TILE - 1, :]], axis=0)
    nxt = jnp.concatenate([cur[1:, :], jnp.where(last, 0.0, next_ref[0:1, :])], axis=0)
    p = cur + mup[...] * (prv - cur) + mun[...] * (nxt - cur)
    G = D_GROUP
    r = p[:, :G]
    k = p[:, G:2 * G]
    v = p[:, 2 * G:3 * G]
    blk = _head_block_ones(G, HEAD_DIM_D)
    kkr = k * kkw[...]
    kk = kkr * lax.rsqrt(_dot_hi(kkr * kkr, blk) + 1e-12)

    def rate(d_static=None):
        if d_static is None:
            ad = jnp.where(fwd, p[:, 3 * G + 2 * LORA_W:3 * G + 2 * LORA_W + LORA_A],
                           p[:, 3 * G + 2 * LORA_W + LORA_A:3 * G + 2 * LORA_W + 2 * LORA_A])
            a = jax.nn.sigmoid(a0[ph] + _dot(ad, a2[ph]))
        else:
            o = 3 * G + 2 * LORA_W + d_static * LORA_A
            a = jax.nn.sigmoid(a0[d_static] + _dot(p[:, o:o + LORA_A], a2[d_static]))
        return a, k * (1.0 + (a - 1.0) * kaw[...])

    a_d, kd_d = rate()
    wd = jnp.where(fwd, p[:, 3 * G:3 * G + LORA_W], p[:, 3 * G + LORA_W:3 * G + 2 * LORA_W])
    w = w0[ph] + _dot(jnp.tanh(wd), w2[ph])
    lw_s[...] = -math.exp(-0.5) * jax.nn.sigmoid(w)
    r_s[...] = r
    v_s[...] = v
    kap_s[...] = kk
    alp_s[...] = a_d * kk
    kd_s[...] = kd_d

    C = CHUNK_D
    r_i = _iota((C, C), 0)
    c_i = _iota((C, C), 1)
    lag = jnp.where(fwd, r_i - c_i, c_i - r_i)
    tri = lag >= 0
    tri_f = tri.astype(F32)
    strict = lag > 0
    eye = (r_i == c_i).astype(F32)
    hd = HEAD_DIM_D

    n_q = SEQ_TILE // C
    eye_k = (_iota((hd, hd), 0) == _iota((hd, hd), 1)).astype(F32)

    units = [(q, h) for q in range(n_q) for h in range(HEADS_D)]
    rows_of = lambda q: slice(q * C, (q + 1) * C)
    lanes_of = lambda h: slice(h * hd, (h + 1) * hd)
    left, right, p_tot = {}, {}, {}
    for q in range(n_q):
        rows = rows_of(q)
        lw = lw_s[rows, :]
        incl = _dot_hi(tri_f, lw)
        tot = jnp.where(fwd, incl[C - 1:C, :], incl[0:1, :])
        p_inv = jnp.exp(-incl)
        kap_h = kap_s[rows, :] * jnp.exp(incl - lw)
        r_h = r_s[rows, :] * jnp.exp(incl)
        alp_b = alp_s[rows, :] * p_inv
        k_b = kd_s[rows, :] * p_inv
        ptq = jnp.exp(tot)
        for h in range(HEADS_D):
            sl = lanes_of(h)
            left[q, h] = jnp.concatenate([kap_h[:, sl], r_h[:, sl]], axis=0)
            right[q, h] = jnp.concatenate([alp_b[:, sl], k_b[:, sl]], axis=0)
            p_tot[q, h] = ptq[:, sl]
            left_s[q * HEADS_D + h] = left[q, h]
    m1 = {u: _dot(left[u], right[u], NT) for u in units}
    a_vk, pw, tinv = {}, {}, {}
    for u in units:
        qh = u[0] * HEADS_D + u[1]
        a_vk[u] = jnp.where(strict, m1[u][:C, C:], 0.0)
        amat_s[qh] = jnp.concatenate([a_vk[u], jnp.where(tri, m1[u][C:, C:], 0.0)], axis=0)
        ara_s[qh] = jnp.where(tri, m1[u][C:, :C], 0.0)
        pw[u] = jnp.where(strict, -m1[u][:C, :C], 0.0)
        tinv[u] = eye + pw[u]
    for _ in range(5):
        pw = {u: _dot(pw[u], pw[u]) for u in units}
        upd = {u: _dot(tinv[u], pw[u]) for u in units}
        tinv = {u: tinv[u] + upd[u] for u in units}
    wmat = {u: _dot(tinv[u], right[u][:C], TN) for u in units}
    kw = {u: _dot(left[u][:C], wmat[u], TN) for u in units}
    aw = {u: _dot(a_vk[u], wmat[u], TN) for u in units}
    vk = {u: _dot(v_s[rows_of(u[0]), lanes_of(u[1])], right[u][C:] - aw[u], TN) for u in units}
    for u in units:
        qh = u[0] * HEADS_D + u[1]
        phi_s[qh] = (eye_k - kw[u]) * p_tot[u]
        psi_s[qh] = vk[u] * p_tot[u]
        tinv_s[qh] = tinv[u]

    for it in range(n_q):
        q = jnp.where(fwd, it, n_q - 1 - it)
        for h in range(HEADS_D):
            qh = q * HEADS_D + h
            s_in = st[h]
            sin_s[qh] = s_in
            st[h] = _dot(s_in, phi_s[qh]) + psi_s[qh]

    qh_of = lambda u: u[0] * HEADS_D + u[1]
    x0 = {u: _dot(left_s[qh_of(u)], sin_s[qh_of(u)], NT) for u in units}
    x1 = {u: _dot(amat_s[qh_of(u)], v_s[rows_of(u[0]), lanes_of(u[1])]) for u in units}
    uu = {u: _dot(tinv_s[qh_of(u)], x0[u][:C] + x1[u][:C]) for u in units}
    au = {u: _dot(ara_s[qh_of(u)], uu[u]) for u in units}
    for q in range(n_q):
        y_s[rows_of(q), :] = jnp.concatenate(
            [x0[q, h][C:] + x1[q, h][C:] - au[q, h] for h in range(HEADS_D)], axis=1)

    @pl.when(fwd)
    def _():
        yf[slot] = y_s[...]

    @pl.when(ph == 1)
    def _():
        y = yf[slot] + y_s[...]
        inv = 1.0 / HEAD_DIM_D
        mu = _dot_hi(y, blk) * inv
        yc = y - mu
        var = _dot_hi(yc * yc, blk) * inv
        yn = yc * lax.rsqrt(var + GN_EPS_D) * gng[...] + gnb[...]
        _, kd_f = rate(0)
        bonus = _dot_hi(r * (kd_f + kd_d) * rkw[...], blk) * v
        gate = _dot(jax.nn.sigmoid(p[:, 3 * G + 2 * LORA_W + 2 * LORA_A:]), g2[...])
        out[...] = (yn + bonus) * gate


def _rwkv(pd, prm, n_b, n_l):
    rows = pd.shape[0]
    cur, prev, nxt = _scan_specs(n_b, n_l, rows, D_COLS)
    full = lambda a: pl.BlockSpec(a.shape, lambda b, ph, j: (0,) * a.ndim)
    tile = lambda: pltpu.VMEM((SEQ_TILE, D_GROUP), F32)
    per_qh = lambda r, c: pltpu.VMEM((SEQ_TILE // CHUNK_D * HEADS_D, r, c), F32)
    return pl.pallas_call(
        functools.partial(_rwkv_body, n_l),
        grid=(n_b, 2, n_l + 1),
        in_specs=[cur, prev, nxt] + [full(a) for a in prm],
        out_specs=_scan_out_spec(n_b, n_l, D_GROUP),
        out_shape=jax.ShapeDtypeStruct((rows, D_GROUP), F32),
        scratch_shapes=[pltpu.VMEM((n_l + 1, SEQ_TILE, D_GROUP), F32),
                        pltpu.VMEM((HEADS_D, HEAD_DIM_D, HEAD_DIM_D), F32)] + [tile() for _ in range(7)]
        + [per_qh(HEAD_DIM_D, HEAD_DIM_D) for _ in range(3)]
        + [per_qh(2 * CHUNK_D, HEAD_DIM_D), per_qh(2 * CHUNK_D, CHUNK_D),
           per_qh(CHUNK_D, CHUNK_D), per_qh(CHUNK_D, CHUNK_D)],
        compiler_params=_cparams(3), name="rwkv",
    )(pd, pd, pd, *prm)


def _outproj_body(yab, yc, yd, h_ref, gate, sh, sc, g, wab, wc, wd, oh, of):
    mix = (_dot(yab[...], wab[...]) + _dot(yc[...], wc[...]) + _dot(yd[...], wd[...]))
    h = h_ref[...] + gate[...] * mix
    oh[...] = h
    of[...] = _norm_mod(h, g[...], sh[...], sc[...]).astype(BF16)


def _outproj(yab, yc, yd, h, mods, g, ws, n_lat, per_seq):
    rows = h.shape[0]
    tile = lambda c: pl.BlockSpec((TOK_TILE, c), lambda i: (i, 0))
    full = lambda a: pl.BlockSpec(a.shape, lambda i: (0,) * a.ndim)
    return pl.pallas_call(
        _outproj_body, grid=(rows // TOK_TILE,),
        in_specs=[tile(512), tile(D_GROUP), tile(D_GROUP), tile(D_MODEL),
                  _mod_spec(n_lat, per_seq, 2), _mod_spec(n_lat, per_seq, 3),
                  _mod_spec(n_lat, per_seq, 4), full(g)] + [full(a) for a in ws],
        out_specs=[tile(D_MODEL), tile(D_MODEL)],
        out_shape=[jax.ShapeDtypeStruct((rows, D_MODEL), F32),
                   jax.ShapeDtypeStruct((rows, D_MODEL), BF16)],
        compiler_params=_cparams(1), name="outproj",
    )(yab, yc, yd, h, mods, mods, mods, g, *ws)


TOPK_TILE = 256


SUBLANES = 8


def _cmpx(lst, i, j):
    a, b = lst[i], lst[j]
    lst[i] = jnp.maximum(a, b)
    lst[j] = jnp.minimum(a, b)


def _bitonic_sort_desc(lst):
    n = len(lst)
    k = 2
    while k <= n:
        j = k // 2
        while j >= 1:
            for i in range(n):
                p = i ^ j
                if p > i:
                    if (i & k) == 0:
                        _cmpx(lst, i, p)
                    else:
                        _cmpx(lst, p, i)
            j //= 2
        k *= 2


def _bitonic_merge_desc(lst):
    n = len(lst)
    j = n // 2
    while j >= 1:
        for i in range(n):
            p = i ^ j
            if p > i:
                _cmpx(lst, i, p)
        j //= 2


def _merge_top(a, b):
    n = len(a)
    c = [jnp.maximum(a[i], b[n - 1 - i]) for i in range(n)]
    _bitonic_merge_desc(c)
    return c


def _merge_sublanes(lst):
    for shift in (4, 2, 1):
        lst = _merge_top(lst, [pltpu.roll(a, shift, axis=0) for a in lst])
    return lst


def _top16_rows(s):
    lst = [s[SUBLANES * v:SUBLANES * (v + 1), :] for v in range(s.shape[0] // SUBLANES)]
    _bitonic_sort_desc(lst)
    return _merge_sublanes(lst)


def _topk_body(f_ref, wq_ref, k1_ref, k2_ref, e1_o, n_o, r2_o, e2_o, q_s):
    q_s[...] = lax.dot_general(wq_ref[...], f_ref[...], (NT, ((), ())),
                               preferred_element_type=F32).astype(BF16)
    half = PEER_DK // 2
    T = f_ref.shape[0]
    sub = _iota((SUBLANES, T), 0)

    def stack(rows):
        out = rows[SUBLANES - 1]
        for b in range(SUBLANES - 2, -1, -1):
            out = jnp.where(sub == b, rows[b], out)
        return out

    def rep(a):
        return jnp.concatenate([a] * (N_KEYS // SUBLANES), axis=0)

    def head(h, carry):
        base = pl.multiple_of(h * PEER_DK, PEER_DK)
        s1 = jnp.dot(k1_ref[...], q_s[pl.ds(base, half), :], preferred_element_type=F32)
        s2 = jnp.dot(k2_ref[...], q_s[pl.ds(base + half, half), :], preferred_element_type=F32)
        t1 = _top16_rows(s1)
        t2 = _top16_rows(s2)
        lo = stack(t2[:SUBLANES])
        hi = stack(t2[SUBLANES:])
        top = _merge_sublanes(_merge_top([t + lo for t in t1], [t + hi for t in t1]))
        z = jnp.exp(top[0] - top[0])
        for kk in range(1, PEER_TOPK):
            z = z + jnp.exp(top[kk] - top[0])
        thr = rep(top[PEER_TOPK - 1])
        r2 = jnp.zeros_like(s2)
        n1 = jnp.zeros_like(s1)
        for b in range(PEER_TOPK):
            t2b = rep(t2[b])
            r2 = r2 + jnp.where(t2b > s2, 1.0, 0.0)
            n1 = n1 + jnp.where(s1 + t2b >= thr, 1.0, 0.0)
        r2_o[h] = _pack_rows(r2)
        n_o[h] = n1
        e1_o[h] = jnp.exp(s1 - rep(t1[0])) / rep(z)
        e2_o[h] = _pack_rows(jnp.exp(s2 - rep(t2[0])))
        return carry

    lax.fori_loop(0, PEER_HEADS, head, 0)


def _topk(f, wq_t, k1, k2):
    rows = f.shape[0]
    T = TOPK_TILE
    full = lambda a: pl.BlockSpec(a.shape, lambda i: (0,) * a.ndim)
    big = pl.BlockSpec((PEER_HEADS, N_KEYS, T), lambda i: (0, 0, i))
    big_shape = lambda dt: jax.ShapeDtypeStruct((PEER_HEADS, N_KEYS, rows), dt)
    packed = pl.BlockSpec((PEER_HEADS, N_KEYS // 2, T), lambda i: (0, 0, i))
    packed_shape = jax.ShapeDtypeStruct((PEER_HEADS, N_KEYS // 2, rows), U32)
    return pl.pallas_call(
        _topk_body, grid=(rows // T,),
        in_specs=[pl.BlockSpec((T, D_MODEL), lambda i: (i, 0)), full(wq_t), full(k1), full(k2)],
        out_specs=[big, big, packed, packed],
        out_shape=[big_shape(F32), big_shape(F32), packed_shape, packed_shape],
        scratch_shapes=[pltpu.VMEM((PEER_HEADS * PEER_DK, T), BF16)],
        compiler_params=_cparams(1), name="peer_topk",
    )(f, wq_t, k1, k2)


def _peer_body(final, n_e, f_ref, u_ref, v_ref, e1_ref, n_ref, r2_ref, e2_ref, h_ref,
               gate_ref, fg_ref, out, st_a, st_b, at_s, acc):
    s = pl.program_id(0)
    j_prev = jnp.maximum(s - 1, 0) % n_e
    cur = s % 2

    @pl.when(s == 0)
    def _():
        at_s[...] = jnp.zeros_like(at_s)

    @pl.when(j_prev == 0)
    def _():
        acc[...] = jnp.zeros_like(acc)

    half = TOK_TILE // 2
    st_halves = (st_a, st_b)
    for hh in range(2):
        st_halves[hh][...] = lax.dot_general(u_ref[...], f_ref[hh * half:(hh + 1) * half, :],
                                             (NT, ((), ())), preferred_element_type=F32)
    acc[...] += lax.dot_general(_unpack_rows(at_s[1 - cur]), v_ref[...], (TN, ((), ())),
                                preferred_element_type=F32)

    blk = (N_KEYS, LANES)
    for lg in range(TOK_TILE // LANES):
        ln = slice(lg * LANES, (lg + 1) * LANES)
        st_ref = st_halves[lg * LANES // half]
        st_ln = slice(lg * LANES % half, lg * LANES % half + LANES)
        for ii in range(EXP_TILE // N_KEYS):
            rows = slice(ii * N_KEYS, (ii + 1) * N_KEYS)
            g = None
            for h in range(PEER_HEADS):
                n_row = jnp.broadcast_to(n_ref[h, ii:ii + 1, ln], blk).astype(BF16)
                e1_row = jnp.broadcast_to(e1_ref[h, ii:ii + 1, ln], blk).astype(BF16)
                w = jnp.where(_unpack_rows(r2_ref[h, :, ln]) < n_row,
                              _unpack_rows(e2_ref[h, :, ln]), 0.0) * e1_row
                g = w if g is None else g + w
            at_s[cur, ii * N_KEYS // 2:(ii + 1) * N_KEYS // 2, ln] = _pack_rows(
                _gelu_tanh(st_ref[rows, st_ln]).astype(BF16) * g)

    @pl.when(jnp.logical_and(s >= 1, j_prev == n_e - 1))
    def _():
        h = h_ref[...] + gate_ref[...] * acc[...]
        if final:
            h = h * lax.rsqrt(jnp.mean(h * h, axis=-1, keepdims=True) + EPS) * fg_ref[...]
        out[...] = h


def _peer(f, u_b, v_b, e1, n1, r2, e2, h, mods, fg, n_lat, per_seq, n_tok_tiles, final):
    n_e = N_EXPERTS // EXP_TILE
    rpt = EXP_TILE // N_KEYS
    n_steps = n_tok_tiles * n_e

    def cur(s):
        sc = jnp.minimum(s, n_steps - 1)
        return sc // n_e, sc % n_e

    def prev(s):
        sp = jnp.maximum(s - 1, 0)
        return sp // n_e, sp % n_e

    tok_cur = pl.BlockSpec((TOK_TILE, D_MODEL), lambda s: (cur(s)[0], 0))
    tok_prev = pl.BlockSpec((TOK_TILE, D_MODEL), lambda s: (prev(s)[0], 0))
    exp_cur = pl.BlockSpec((EXP_TILE, D_MODEL), lambda s: (cur(s)[1], 0))
    exp_prev = pl.BlockSpec((EXP_TILE, D_MODEL), lambda s: (prev(s)[1], 0))
    rowblk = pl.BlockSpec((PEER_HEADS, rpt, TOK_TILE), lambda s: (0, cur(s)[1], cur(s)[0]))
    allkeys = pl.BlockSpec((PEER_HEADS, N_KEYS // 2, TOK_TILE), lambda s: (0, 0, cur(s)[0]))

    def gate_map(s):
        i = prev(s)[0]
        row = jnp.where(i < n_lat, i // per_seq, 2)
        return (row * 6 + 5, 0, 0)

    return pl.pallas_call(
        functools.partial(_peer_body, final, n_e),
        grid=(n_steps + 1,),
        in_specs=[tok_cur, exp_cur, exp_prev, rowblk, rowblk, allkeys, allkeys, tok_prev,
                  pl.BlockSpec((None, 1, D_MODEL), gate_map),
                  pl.BlockSpec((1, D_MODEL), lambda s: (0, 0))],
        out_specs=tok_prev,
        out_shape=jax.ShapeDtypeStruct((n_tok_tiles * TOK_TILE, D_MODEL), F32),
        scratch_shapes=[pltpu.VMEM((EXP_TILE, TOK_TILE // 2), F32),
                        pltpu.VMEM((EXP_TILE, TOK_TILE // 2), F32),
                        pltpu.VMEM((2, EXP_TILE // 2, TOK_TILE), U32),
                        pltpu.VMEM((TOK_TILE, D_MODEL), F32)],
        compiler_params=_cparams(1), name="peer",
    )(f, u_b, v_b, e1, n1, r2, e2, h, mods, fg)


def _sincos_2d(t_len):
    rows = t_len // GRID_W
    row = jnp.repeat(jnp.arange(rows), GRID_W).astype(F32)
    col = jnp.tile(jnp.arange(GRID_W), rows).astype(F32)
    q = D_MODEL // 4
    freq = 10000.0 ** (-jnp.arange(q, dtype=F32) / q)
    ar = row[:, None] * freq
    ac = col[:, None] * freq
    return jnp.concatenate([jnp.sin(ar), jnp.cos(ar), jnp.sin(ac), jnp.cos(ac)], -1)


def _row(a):
    return a.reshape(1, -1).astype(F32)


def _pad_lanes(a, width=LANES):
    return jnp.pad(a, ((0, 0), (0, width - a.shape[-1])))


def kernel(x, c, ctx, c_ctx, ada_w, ada_b, norm_mix_g, norm_ffn_g, w_in, w_out, a_conv_w, a_conv_b, a_ln_g, a_ln_b, b_ln_g, b_ln_b, b_ws, b_bs, c_conv_w, c_conv_b, c_dt_bias, c_a_log, c_d, c_norm_g, d_mu_prev, d_mu_next, d_w0, d_w2, d_a0, d_a2, d_g2, d_k_k, d_k_a, d_r_k, d_gn_g, d_gn_b, peer_wq, peer_k1, peer_k2, peer_u, peer_v, final_g):
    n_b, t_lat, _ = x.shape
    t_ctx = ctx.shape[1]
    assert n_b == 2 and t_ctx == SEQ_TILE and t_lat % TOK_TILE == 0
    n_l = t_lat // SEQ_TILE
    per_seq = t_lat // TOK_TILE
    n_lat = n_b * per_seq
    n_lat256 = n_b * n_l

    cs = jnp.zeros((8, D_MODEL), F32).at[:n_b].set(c).at[n_b].set(c_ctx)
    mods_all = _ada(cs, ada_w, ada_b)
    pos = _sincos_2d(t_lat)
    x2 = x.reshape(n_b * t_lat, D_MODEL)
    ctx2 = ctx.reshape(n_b * t_ctx, D_MODEL)

    h = None
    for i in range(DEPTH):
        last_layer = i == DEPTH - 1
        mods = mods_all[i].reshape(8 * 6, 1, D_MODEL)
        w = w_in[i]
        dtc = OFF_C + D_GROUP + C_XBC
        w_c = jnp.concatenate([w[:, OFF_C:dtc], _pad_lanes(w[:, dtc:dtc + HEADS_C]),
                               _pad_lanes(w[:, dtc + HEADS_C:dtc + 2 * HEADS_C])], axis=1)
        ws_in = tuple(a.astype(BF16) for a in (w[:, :OFF_B], w[:, OFF_B:OFF_C], w_c, w[:, OFF_D:]))
        g_mix = _row(norm_mix_g[i])
        if i == 0:
            h, pa, pb, pc, pd = _inproj((x2, ctx2, pos), mods, g_mix, ws_in, n_lat, per_seq, True)
        else:
            pa, pb, pc, pd = _inproj(h, mods, g_mix, ws_in, n_lat, per_seq, False)

        conv_a = jnp.pad(a_conv_w[i], ((0, 32 - CONV_A), (0, 0)))
        bsm = jnp.repeat(b_bs[i].T, D_GROUP // HEADS_B, axis=1)
        yab = _mixab(pa, pb, (conv_a, _row(a_conv_b[i]), _row(a_ln_g[i]), _row(a_ln_b[i]),
                              _row(b_ln_g[i]), _row(b_ln_b[i]), b_ws[i].astype(BF16), bsm),
                     n_lat256, n_l)

        conv_c = jnp.pad(c_conv_w[i], ((0, 8 - CONV_C), (0, 0)))
        dtb = _pad_lanes(c_dt_bias[i]).reshape(2, 1, LANES)
        alog = _pad_lanes(c_a_log[i]).reshape(2, 1, LANES)
        dsk = _row(jnp.repeat(c_d[i], HEAD_DIM_C))
        yc = _mamba(pc, (conv_c, _row(c_conv_b[i]), dtb, alog, dsk, _row(c_norm_g[i])), n_b, n_l)

        yd = _rwkv(pd, (_row(d_mu_prev[i]), _row(d_mu_next[i]), d_w0[i].reshape(2, 1, D_GROUP),
                        d_w2[i].astype(BF16), d_a0[i].reshape(2, 1, D_GROUP), d_a2[i].astype(BF16),
                        d_g2[i].astype(BF16), _row(d_k_k[i]), _row(d_k_a[i]), _row(d_r_k[i]),
                        _row(d_gn_g[i]), _row(d_gn_b[i])), n_b, n_l)

        wo = w_out[i].astype(BF16)
        h, f = _outproj(yab, yc, yd, h, mods, _row(norm_ffn_g[i]),
                        (wo[:512], wo[512:768], wo[768:]), n_lat, per_seq)

        e1, n1, r2, e2 = _topk(f, peer_wq[i].T.astype(BF16), peer_k1[i].astype(BF16),
                               peer_k2[i].astype(BF16))
        n_tok = n_lat if last_layer else n_lat + 1
        h = _peer(f, peer_u[i].astype(BF16), peer_v[i].astype(BF16), e1, n1, r2, e2, h,
                  mods, _row(final_g), n_lat, per_seq, n_tok, last_layer)
    return h.reshape(n_b, t_lat, D_MODEL)
```

```python
import functools
import math

import jax
import jax.numpy as jnp
from jax import lax
from jax.experimental import pallas as pl
from jax.experimental.pallas import tpu as pltpu

F32 = jnp.float32
BF16 = jnp.bfloat16
U32 = jnp.uint32
HIGHEST = lax.Precision.HIGHEST

D_MODEL = 1024
DEPTH = 2
GRID_W = 64
EPS = 1e-6
D_GROUP = 256
CONV_A = 31
CHUNK_B = 128
HEADS_B = 4
HEADS_C = 4
HEAD_DIM_C = 64
STATE_C = 128
CONV_C = 5
CHUNK_C = 128
HEADS_D = 4
HEAD_DIM_D = 64
LORA_W = 64
LORA_A = 64
LORA_G = 128
GN_EPS_D = 64e-5
N_KEYS = 128
N_EXPERTS = N_KEYS * N_KEYS
PEER_HEADS = 8
PEER_DK = 256
PEER_TOPK = 16

C_XBC = D_GROUP + 2 * 2 * STATE_C
OFF_B = 2 * D_GROUP
OFF_C = OFF_B + 2 * D_GROUP
OFF_D = OFF_C + D_GROUP + C_XBC + 2 * HEADS_C
D_COLS = 3 * D_GROUP + 2 * LORA_W + 2 * LORA_A + LORA_G
PC_COLS = D_GROUP + C_XBC + 2 * 128

LANES = 128
SEQ_TILE = 256
TOK_TILE = 512
CHUNK_D = 64
EXP_TILE = 1024
VMEM_LIMIT = 56 * 1024 * 1024


def _cparams(n_axes):
    return pltpu.CompilerParams(dimension_semantics=("arbitrary",) * n_axes,
                                vmem_limit_bytes=VMEM_LIMIT)


def _silu(x):
    return x * jax.nn.sigmoid(x)


def _dot(a, b, dims=None):
    a = a.astype(BF16)
    b = b.astype(BF16)
    if dims is None:
        return jnp.dot(a, b, preferred_element_type=F32)
    return lax.dot_general(a, b, (dims, ((), ())), preferred_element_type=F32)


def _dot_hi(a, b):
    return jnp.dot(a, b, precision=HIGHEST, preferred_element_type=F32)


NT = ((1,), (1,))
TN = ((0,), (0,))


def _gelu_tanh(x):
    k1 = -2.0 * math.sqrt(2.0 / math.pi) * math.log2(math.e)
    k3 = 0.044715 * k1
    return x / (1.0 + jnp.exp2(x * (k1 + k3 * (x * x))))


def _pack_rows(x):
    return pltpu.bitcast(x.astype(BF16), U32)


def _unpack_rows(x):
    return pltpu.bitcast(x, BF16)


def _dup_bf16(x):
    hi = pltpu.bitcast(x.astype(BF16).astype(F32), U32)
    return hi | (hi >> 16)


def _iota(shape, axis):
    return lax.broadcasted_iota(jnp.int32, shape, axis)


def _head_block_ones(n, width):
    return (_iota((n, n), 0) // width == _iota((n, n), 1) // width).astype(F32)


def _ada_body(cs_ref, w_ref, b_ref, o_ref):
    o_ref[...] = _dot_hi(_silu(cs_ref[...]), w_ref[...]) + b_ref[...]


def _ada(cs, ada_w, ada_b):
    L = ada_w.shape[0]
    nb = 1536
    return pl.pallas_call(
        _ada_body,
        grid=(L, 6 * D_MODEL // nb),
        in_specs=[pl.BlockSpec((8, D_MODEL), lambda l, n: (0, 0)),
                  pl.BlockSpec((None, D_MODEL, nb), lambda l, n: (l, 0, n)),
                  pl.BlockSpec((None, 1, nb), lambda l, n: (l, 0, n))],
        out_specs=pl.BlockSpec((None, 8, nb), lambda l, n: (l, 0, n)),
        out_shape=jax.ShapeDtypeStruct((L, 8, 6 * D_MODEL), F32),
        compiler_params=_cparams(2),
        name="ada",
    )(cs, ada_w, ada_b.reshape(L, 1, 6 * D_MODEL))


def _norm_mod(h, g, shift, scale):
    xn = h * lax.rsqrt(jnp.mean(h * h, axis=-1, keepdims=True) + EPS) * g
    return xn * (1.0 + scale) + shift


def _inproj_tail(h, sh_ref, sc_ref, g_ref, wa, wb, wc, wd, oa, ob, oc, od):
    xm = _norm_mod(h, g_ref[...], sh_ref[...], sc_ref[...]).astype(BF16)
    oa[...] = jnp.dot(xm, wa[...], preferred_element_type=F32)
    ob[...] = jnp.dot(xm, wb[...], preferred_element_type=F32)
    oc[...] = jnp.dot(xm, wc[...], preferred_element_type=F32)
    od[...] = jnp.dot(xm, wd[...], preferred_element_type=F32)


def _inproj_first_body(n_lat, x_ref, ctx_ref, pos_ref, sh_ref, sc_ref, g_ref, wa, wb, wc, wd,
                       oh, oa, ob, oc, od):
    i = pl.program_id(0)
    h = jnp.where(i < n_lat, x_ref[...] + pos_ref[...], ctx_ref[...])
    oh[...] = h
    _inproj_tail(h, sh_ref, sc_ref, g_ref, wa, wb, wc, wd, oa, ob, oc, od)


def _inproj_body(h_ref, sh_ref, sc_ref, g_ref, wa, wb, wc, wd, oa, ob, oc, od):
    _inproj_tail(h_ref[...], sh_ref, sc_ref, g_ref, wa, wb, wc, wd, oa, ob, oc, od)


def _mod_spec(n_lat, per_seq, k):
    def imap(i):
        row = jnp.where(i < n_lat, i // per_seq, 2)
        return (row * 6 + k, 0, 0)
    return pl.BlockSpec((None, 1, D_MODEL), imap)


def _inproj(h_or_parts, mods, g, ws, n_lat, per_seq, first):
    wa, wb, wc, wd = ws
    n_tiles = n_lat + 1
    rows = n_tiles * TOK_TILE
    tile = lambda c: pl.BlockSpec((TOK_TILE, c), lambda i: (i, 0))
    full = lambda a: pl.BlockSpec(a.shape, lambda i: (0,) * a.ndim)
    common_specs = [_mod_spec(n_lat, per_seq, 0), _mod_spec(n_lat, per_seq, 1), full(g),
                    full(wa), full(wb), full(wc), full(wd)]
    out_specs = [tile(512), tile(512), tile(PC_COLS), tile(D_COLS)]
    out_shape = [jax.ShapeDtypeStruct((rows, c), F32) for c in (512, 512, PC_COLS, D_COLS)]
    if first:
        x2, ctx2, pos = h_or_parts
        in_specs = [pl.BlockSpec((TOK_TILE, D_MODEL), lambda i: (jnp.minimum(i, n_lat - 1), 0)),
                    pl.BlockSpec((TOK_TILE, D_MODEL), lambda i: (0, 0)),
                    pl.BlockSpec((TOK_TILE, D_MODEL), lambda i: (i % per_seq, 0))] + common_specs
        return pl.pallas_call(
            functools.partial(_inproj_first_body, n_lat),
            grid=(n_tiles,), in_specs=in_specs,
            out_specs=[tile(D_MODEL)] + out_specs,
            out_shape=[jax.ShapeDtypeStruct((rows, D_MODEL), F32)] + out_shape,
            compiler_params=_cparams(1), name="inproj_first",
        )(x2, ctx2, pos, mods, mods, g, wa, wb, wc, wd)
    return pl.pallas_call(
        _inproj_body, grid=(n_tiles,), in_specs=[tile(D_MODEL)] + common_specs,
        out_specs=out_specs, out_shape=out_shape,
        compiler_params=_cparams(1), name="inproj",
    )(h_or_parts, mods, mods, g, wa, wb, wc, wd)


def _layernorm(x, g, b, eps=1e-5):
    mu = jnp.mean(x, axis=-1, keepdims=True)
    xc = x - mu
    var = jnp.mean(xc * xc, axis=-1, keepdims=True)
    return xc * lax.rsqrt(var + eps) * g + b


def _mixab_body(n_lat, n_l, pa, pa_prev, pa_next, pb, cw, cb, alg, alb, blg, blb, ws, bsm,
                out, ext):
    i = pl.program_id(0)
    is_ctx = i >= n_lat
    tpos = i % n_l
    first = jnp.logical_or(is_ctx, tpos == 0)
    last = jnp.logical_or(is_ctx, tpos == n_l - 1)

    def glu(x):
        return x[:, :D_GROUP] * jax.nn.sigmoid(x[:, D_GROUP:])

    halo = 16
    ext[0:halo, :] = jnp.where(first, 0.0, glu(pa_prev[...]))
    ext[halo:halo + SEQ_TILE, :] = glu(pa[...])
    ext[halo + SEQ_TILE:, :] = jnp.where(last, 0.0, glu(pa_next[...]))
    acc = jnp.zeros((SEQ_TILE, D_GROUP), F32) + cb[...]
    for j in range(CONV_A):
        acc = acc + cw[j:j + 1, :] * ext[pl.ds(halo - (CONV_A - 1) // 2 + j, SEQ_TILE), :]
    out[:, :D_GROUP] = _silu(_layernorm(acc, alg[...], alb[...]))

    x = pb[...]
    u = x[:, :D_GROUP]
    v = _layernorm(x[:, D_GROUP:], blg[...], blb[...]).astype(BF16)
    hd = D_GROUP // HEADS_B
    rows = []
    for c in range(SEQ_TILE // CHUNK_B):
        vc = v[c * CHUNK_B:(c + 1) * CHUNK_B, :]
        heads = [jnp.dot(ws[hh], vc[:, hh * hd:(hh + 1) * hd], preferred_element_type=F32)
                 for hh in range(HEADS_B)]
        rows.append(jnp.concatenate(heads, axis=1) + bsm[...])
    out[:, D_GROUP:] = u * jnp.concatenate(rows, axis=0)


def _mixab(pa, pb, prm, n_lat, n_l):
    rows = pa.shape[0]
    n_tiles = rows // SEQ_TILE
    halo = 16
    per = SEQ_TILE // halo
    full = lambda a: pl.BlockSpec(a.shape, lambda i: (0,) * a.ndim)
    in_specs = [pl.BlockSpec((SEQ_TILE, 512), lambda i: (i, 0)),
                pl.BlockSpec((halo, 512), lambda i: (jnp.maximum(i * per - 1, 0), 0)),
                pl.BlockSpec((halo, 512), lambda i: (jnp.minimum((i + 1) * per, rows // halo - 1), 0)),
                pl.BlockSpec((SEQ_TILE, 512), lambda i: (i, 0))] + [full(a) for a in prm]
    return pl.pallas_call(
        functools.partial(_mixab_body, n_lat, n_l),
        grid=(n_tiles,), in_specs=in_specs,
        out_specs=pl.BlockSpec((SEQ_TILE, 512), lambda i: (i, 0)),
        out_shape=jax.ShapeDtypeStruct((rows, 512), F32),
        scratch_shapes=[pltpu.VMEM((SEQ_TILE + 2 * halo, D_GROUP), F32)],
        compiler_params=_cparams(1), name="mixab",
    )(pa, pa, pa, pb, *prm)


def _scan_tile(n_b, n_l, b, ph, j):
    lat = b * n_l + jnp.where(ph == 0, j - 1, n_l - j)
    return jnp.where(j == 0, n_b * n_l + b, lat)


def _scan_specs(n_b, n_l, rows, cols):
    tid = functools.partial(_scan_tile, n_b, n_l)
    per = SEQ_TILE // 8
    cur = pl.BlockSpec((SEQ_TILE, cols), lambda b, ph, j: (tid(b, ph, j), 0))
    prev = pl.BlockSpec((8, cols), lambda b, ph, j: (jnp.maximum(tid(b, ph, j) * per - 1, 0), 0))
    nxt = pl.BlockSpec((8, cols),
                       lambda b, ph, j: (jnp.minimum((tid(b, ph, j) + 1) * per, rows // 8 - 1), 0))
    return cur, prev, nxt


def _scan_out_spec(n_b, n_l, cols):
    def imap(b, ph, j):
        return (jnp.where(ph == 0, n_b * n_l + b, _scan_tile(n_b, n_l, b, 1, j)), 0)
    return pl.BlockSpec((SEQ_TILE, cols), imap)


def _scan_flags(n_l):
    ph = pl.program_id(1)
    j = pl.program_id(2)
    is_ctx = j == 0
    tpos = jnp.where(ph == 0, j - 1, n_l - j)
    first = jnp.logical_or(is_ctx, tpos == 0)
    last = jnp.logical_or(is_ctx, tpos == n_l - 1)
    slot = jnp.where(is_ctx, 0, tpos + 1)
    return ph, j, first, last, slot


def _mamba_body(n_l, cur_ref, prev_ref, next_ref, dt_ref, cw, cb, dtb, alog, dsk, ng,
                out, ext, yf, st, xbc_s, a_s, dt_s, y_s):
    ph, j, first, last, slot = _scan_flags(n_l)
    fwd = ph == 0

    @pl.when(j == 0)
    def _():
        st[...] = jnp.zeros_like(st)

    zx = D_GROUP
    ext[0:8, :] = jnp.where(first, 0.0, prev_ref[:, zx:])
    ext[8:8 + SEQ_TILE, :] = cur_ref[:, zx:]
    ext[8 + SEQ_TILE:, :] = jnp.where(last, 0.0, next_ref[:, zx:])
    acc = jnp.zeros((SEQ_TILE, C_XBC), F32) + cb[...]
    for jj in range(CONV_C):
        acc = acc + cw[jj:jj + 1, :] * ext[pl.ds(8 - (CONV_C - 1) // 2 + jj, SEQ_TILE), :]
    xbc_s[...] = _silu(acc)
    dt = jax.nn.softplus(dt_ref[...] + dtb[ph])
    dt_s[...] = dt
    a_s[...] = dt * (-jnp.exp(alog[ph]))

    L = CHUNK_C
    r_i = _iota((L, L), 0)
    c_i = _iota((L, L), 1)
    ltri = (r_i >= c_i).astype(F32)
    mask = jnp.where(fwd, r_i - c_i, c_i - r_i) >= 0
    sgn = jnp.where(fwd, 1.0, -1.0)
    hd = HEAD_DIM_C

    def chunk(it, carry):
        q = jnp.where(fwd, it, SEQ_TILE // L - 1 - it)
        rows = pl.ds(pl.multiple_of(q * L, L), L)
        a_q = a_s[rows, :]
        cs = _dot_hi(ltri, a_q)
        tot = cs[L - 1:L, :]
        e = jnp.where(fwd, cs, cs - a_q)
        e_t = e.T
        dt_q = dt_s[rows, :]
        xbc = xbc_s[rows, :]
        ys = []
        for g in range(2):
            bm = xbc[:, D_GROUP + g * STATE_C:D_GROUP + (g + 1) * STATE_C]
            cm = xbc[:, D_GROUP + 2 * STATE_C + g * STATE_C:D_GROUP + 2 * STATE_C + (g + 1) * STATE_C]
            bm_t = bm.T.astype(BF16)
            cm_b = cm.astype(BF16)
            gmat = jnp.dot(cm_b, bm_t, preferred_element_type=F32)
            for h in (2 * g, 2 * g + 1):
                ecol = e[:, h:h + 1]
                erow = e_t[h:h + 1, :]
                totc = tot[:, h:h + 1]
                xdt = xbc[:, h * hd:(h + 1) * hd] * dt_q[:, h:h + 1]
                lmat = jnp.exp(jnp.where(mask, sgn * (ecol - erow), -1e30))
                offs = jnp.exp(jnp.where(fwd, ecol, totc - ecol))
                stw = jnp.exp(jnp.where(fwd, totc - ecol, ecol))
                s_prev = st[h]
                y = _dot(gmat * lmat, xdt) + offs * _dot(cm_b, s_prev)
                st[h] = jnp.exp(totc) * s_prev + _dot(bm_t, xdt * stw)
                ys.append(y)
        y_s[rows, :] = jnp.concatenate(ys, axis=1)
        return carry

    lax.fori_loop(0, SEQ_TILE // L, chunk, 0)

    @pl.when(fwd)
    def _():
        yf[slot] = y_s[...]

    @pl.when(ph == 1)
    def _():
        y = yf[slot] + y_s[...] + dsk[...] * xbc_s[:, :D_GROUP]
        t = y * _silu(cur_ref[:, :D_GROUP])
        out[...] = t * lax.rsqrt(jnp.mean(t * t, axis=-1, keepdims=True) + EPS) * ng[...]


def _mamba(pc, prm, n_b, n_l):
    rows = pc.shape[0]
    cur, prev, nxt = _scan_specs(n_b, n_l, rows, D_GROUP + C_XBC)
    tid = functools.partial(_scan_tile, n_b, n_l)
    dt_spec = pl.BlockSpec((SEQ_TILE, LANES),
                           lambda b, ph, j: (tid(b, ph, j), (D_GROUP + C_XBC) // LANES + ph))
    full = lambda a: pl.BlockSpec(a.shape, lambda b, ph, j: (0,) * a.ndim)
    return pl.pallas_call(
        functools.partial(_mamba_body, n_l),
        grid=(n_b, 2, n_l + 1),
        in_specs=[cur, prev, nxt, dt_spec] + [full(a) for a in prm],
        out_specs=_scan_out_spec(n_b, n_l, D_GROUP),
        out_shape=jax.ShapeDtypeStruct((rows, D_GROUP), F32),
        scratch_shapes=[pltpu.VMEM((SEQ_TILE + 16, C_XBC), F32),
                        pltpu.VMEM((n_l + 1, SEQ_TILE, D_GROUP), F32),
                        pltpu.VMEM((HEADS_C, STATE_C, HEAD_DIM_C), F32),
                        pltpu.VMEM((SEQ_TILE, C_XBC), F32),
                        pltpu.VMEM((SEQ_TILE, LANES), F32),
                        pltpu.VMEM((SEQ_TILE, LANES), F32),
                        pltpu.VMEM((SEQ_TILE, D_GROUP), F32)],
        compiler_params=_cparams(3), name="mamba",
    )(pc, pc, pc, pc, *prm)


def _rwkv_body(n_l, cur_ref, prev_ref, next_ref, mup, mun, w0, w2, a0, a2, g2, kkw, kaw, rkw,
               gng, gnb, out, yf, st, r_s, v_s, kap_s, alp_s, kd_s, lw_s, y_s,
               phi_s, psi_s, sin_s, left_s, amat_s, ara_s, tinv_s):
    ph, j, first, last, slot = _scan_flags(n_l)
    fwd = ph == 0

    @pl.when(j == 0)
    def _():
        st[...] = jnp.zeros_like(st)

    cur = cur_ref[...]
    prv = jnp.concatenate([jnp.where(first, 0.0, prev_ref[7:8, :]), cur[:SEQ_TILE - 1, :]], axis=0)
    nxt = jnp.concatenate([cur[1:, :], jnp.where(last, 0.0, next_ref[0:1, :])], axis=0)
    p = cur + mup[...] * (prv - cur) + mun[...] * (nxt - cur)
    G = D_GROUP
    r = p[:, :G]
    k = p[:, G:2 * G]
    v = p[:, 2 * G:3 * G]
    blk = _head_block_ones(G, HEAD_DIM_D)
    kkr = k * kkw[...]
    kk = kkr * lax.rsqrt(_dot_hi(kkr * kkr, blk) + 1e-12)

    def rate(d_static=None):
        if d_static is None:
            ad = jnp.where(fwd, p[:, 3 * G + 2 * LORA_W:3 * G + 2 * LORA_W + LORA_A],
                           p[:, 3 * G + 2 * LORA_W + LORA_A:3 * G + 2 * LORA_W + 2 * LORA_A])
            a = jax.nn.sigmoid(a0[ph] + _dot(ad, a2[ph]))
        else:
            o = 3 * G + 2 * LORA_W + d_static * LORA_A
            a = jax.nn.sigmoid(a0[d_static] + _dot(p[:, o:o + LORA_A], a2[d_static]))
        return a, k * (1.0 + (a - 1.0) * kaw[...])

    a_d, kd_d = rate()
    wd = jnp.where(fwd, p[:, 3 * G:3 * G + LORA_W], p[:, 3 * G + LORA_W:3 * G + 2 * LORA_W])
    w = w0[ph] + _dot(jnp.tanh(wd), w2[ph])
    lw_s[...] = -math.exp(-0.5) * jax.nn.sigmoid(w)
    r_s[...] = r
    v_s[...] = v
    kap_s[...] = kk
    alp_s[...] = a_d * kk
    kd_s[...] = kd_d

    C = CHUNK_D
    r_i = _iota((C, C), 0)
    c_i = _iota((C, C), 1)
    lag = jnp.where(fwd, r_i - c_i, c_i - r_i)
    tri = lag >= 0
    tri_f = tri.astype(F32)
    strict = lag > 0
    eye = (r_i == c_i).astype(F32)
    hd = HEAD_DIM_D

    n_q = SEQ_TILE // C
    eye_k = (_iota((hd, hd), 0) == _iota((hd, hd), 1)).astype(F32)

    units = [(q, h) for q in range(n_q) for h in range(HEADS_D)]
    rows_of = lambda q: slice(q * C, (q + 1) * C)
    lanes_of = lambda h: slice(h * hd, (h + 1) * hd)
    left, right, p_tot = {}, {}, {}
    for q in range(n_q):
        rows = rows_of(q)
        lw = lw_s[rows, :]
        incl = _dot_hi(tri_f, lw)
        tot = jnp.where(fwd, incl[C - 1:C, :], incl[0:1, :])
        p_inv = jnp.exp(-incl)
        kap_h = kap_s[rows, :] * jnp.exp(incl - lw)
        r_h = r_s[rows, :] * jnp.exp(incl)
        alp_b = alp_s[rows, :] * p_inv
        k_b = kd_s[rows, :] * p_inv
        ptq = jnp.exp(tot)
        for h in range(HEADS_D):
            sl = lanes_of(h)
            left[q, h] = jnp.concatenate([kap_h[:, sl], r_h[:, sl]], axis=0)
            right[q, h] = jnp.concatenate([alp_b[:, sl], k_b[:, sl]], axis=0)
            p_tot[q, h] = ptq[:, sl]
            left_s[q * HEADS_D + h] = left[q, h]
    m1 = {u: _dot(left[u], right[u], NT) for u in units}
    a_vk, pw, tinv = {}, {}, {}
    for u in units:
        qh = u[0] * HEADS_D + u[1]
        a_vk[u] = jnp.where(strict, m1[u][:C, C:], 0.0)
        amat_s[qh] = jnp.concatenate([a_vk[u], jnp.where(tri, m1[u][C:, C:], 0.0)], axis=0)
        ara_s[qh] = jnp.where(tri, m1[u][C:, :C], 0.0)
        pw[u] = jnp.where(strict, -m1[u][:C, :C], 0.0)
        tinv[u] = eye + pw[u]
    for _ in range(5):
        pw = {u: _dot(pw[u], pw[u]) for u in units}
        upd = {u: _dot(tinv[u], pw[u]) for u in units}
        tinv = {u: tinv[u] + upd[u] for u in units}
    wmat = {u: _dot(tinv[u], right[u][:C], TN) for u in units}
    kw = {u: _dot(left[u][:C], wmat[u], TN) for u in units}
    aw = {u: _dot(a_vk[u], wmat[u], TN) for u in units}
    vk = {u: _dot(v_s[rows_of(u[0]), lanes_of(u[1])], right[u][C:] - aw[u], TN) for u in units}
    for u in units:
        qh = u[0] * HEADS_D + u[1]
        phi_s[qh] = (eye_k - kw[u]) * p_tot[u]
        psi_s[qh] = vk[u] * p_tot[u]
        tinv_s[qh] = tinv[u]

    for it in range(n_q):
        q = jnp.where(fwd, it, n_q - 1 - it)
        for h in range(HEADS_D):
            qh = q * HEADS_D + h
            s_in = st[h]
            sin_s[qh] = s_in
            st[h] = _dot(s_in, phi_s[qh]) + psi_s[qh]

    qh_of = lambda u: u[0] * HEADS_D + u[1]
    x0 = {u: _dot(left_s[qh_of(u)], sin_s[qh_of(u)], NT) for u in units}
    x1 = {u: _dot(amat_s[qh_of(u)], v_s[rows_of(u[0]), lanes_of(u[1])]) for u in units}
    uu = {u: _dot(tinv_s[qh_of(u)], x0[u][:C] + x1[u][:C]) for u in units}
    au = {u: _dot(ara_s[qh_of(u)], uu[u]) for u in units}
    for q in range(n_q):
        y_s[rows_of(q), :] = jnp.concatenate(
            [x0[q, h][C:] + x1[q, h][C:] - au[q, h] for h in range(HEADS_D)], axis=1)

    @pl.when(fwd)
    def _():
        yf[slot] = y_s[...]

    @pl.when(ph == 1)
    def _():
        y = yf[slot] + y_s[...]
        inv = 1.0 / HEAD_DIM_D
        mu = _dot_hi(y, blk) * inv
        yc = y - mu
        var = _dot_hi(yc * yc, blk) * inv
        yn = yc * lax.rsqrt(var + GN_EPS_D) * gng[...] + gnb[...]
        _, kd_f = rate(0)
        bonus = _dot_hi(r * (kd_f + kd_d) * rkw[...], blk) * v
        gate = _dot(jax.nn.sigmoid(p[:, 3 * G + 2 * LORA_W + 2 * LORA_A:]), g2[...])
        out[...] = (yn + bonus) * gate


def _rwkv(pd, prm, n_b, n_l):
    rows = pd.shape[0]
    cur, prev, nxt = _scan_specs(n_b, n_l, rows, D_COLS)
    full = lambda a: pl.BlockSpec(a.shape, lambda b, ph, j: (0,) * a.ndim)
    tile = lambda: pltpu.VMEM((SEQ_TILE, D_GROUP), F32)
    per_qh = lambda r, c: pltpu.VMEM((SEQ_TILE // CHUNK_D * HEADS_D, r, c), F32)
    return pl.pallas_call(
        functools.partial(_rwkv_body, n_l),
        grid=(n_b, 2, n_l + 1),
        in_specs=[cur, prev, nxt] + [full(a) for a in prm],
        out_specs=_scan_out_spec(n_b, n_l, D_GROUP),
        out_shape=jax.ShapeDtypeStruct((rows, D_GROUP), F32),
        scratch_shapes=[pltpu.VMEM((n_l + 1, SEQ_TILE, D_GROUP), F32),
                        pltpu.VMEM((HEADS_D, HEAD_DIM_D, HEAD_DIM_D), F32)] + [tile() for _ in range(7)]
        + [per_qh(HEAD_DIM_D, HEAD_DIM_D) for _ in range(3)]
        + [per_qh(2 * CHUNK_D, HEAD_DIM_D), per_qh(2 * CHUNK_D, CHUNK_D),
           per_qh(CHUNK_D, CHUNK_D), per_qh(CHUNK_D, CHUNK_D)],
        compiler_params=_cparams(3), name="rwkv",
    )(pd, pd, pd, *prm)


def _outproj_body(yab, yc, yd, h_ref, gate, sh, sc, g, wab, wc, wd, oh, of):
    mix = (_dot(yab[...], wab[...]) + _dot(yc[...], wc[...]) + _dot(yd[...], wd[...]))
    h = h_ref[...] + gate[...] * mix
    oh[...] = h
    of[...] = _norm_mod(h, g[...], sh[...], sc[...]).astype(BF16)


def _outproj(yab, yc, yd, h, mods, g, ws, n_lat, per_seq):
    rows = h.shape[0]
    tile = lambda c: pl.BlockSpec((TOK_TILE, c), lambda i: (i, 0))
    full = lambda a: pl.BlockSpec(a.shape, lambda i: (0,) * a.ndim)
    return pl.pallas_call(
        _outproj_body, grid=(rows // TOK_TILE,),
        in_specs=[tile(512), tile(D_GROUP), tile(D_GROUP), tile(D_MODEL),
                  _mod_spec(n_lat, per_seq, 2), _mod_spec(n_lat, per_seq, 3),
                  _mod_spec(n_lat, per_seq, 4), full(g)] + [full(a) for a in ws],
        out_specs=[tile(D_MODEL), tile(D_MODEL)],
        out_shape=[jax.ShapeDtypeStruct((rows, D_MODEL), F32),
                   jax.ShapeDtypeStruct((rows, D_MODEL), BF16)],
        compiler_params=_cparams(1), name="outproj",
    )(yab, yc, yd, h, mods, mods, mods, g, *ws)


TOPK_TILE = 256


SUBLANES = 8


def _cmpx(lst, i, j):
    a, b = lst[i], lst[j]
    lst[i] = jnp.maximum(a, b)
    lst[j] = jnp.minimum(a, b)


def _bitonic_sort_desc(lst):
    n = len(lst)
    k = 2
    while k <= n:
        j = k // 2
        while j >= 1:
            for i in range(n):
                p = i ^ j
                if p > i:
                    if (i & k) == 0:
                        _cmpx(lst, i, p)
                    else:
                        _cmpx(lst, p, i)
            j //= 2
        k *= 2


def _bitonic_merge_desc(lst):
    n = len(lst)
    j = n // 2
    while j >= 1:
        for i in range(n):
            p = i ^ j
            if p > i:
                _cmpx(lst, i, p)
        j //= 2


def _merge_top(a, b):
    n = len(a)
    c = [jnp.maximum(a[i], b[n - 1 - i]) for i in range(n)]
    _bitonic_merge_desc(c)
    return c


def _merge_sublanes(lst):
    for shift in (4, 2, 1):
        lst = _merge_top(lst, [pltpu.roll(a, shift, axis=0) for a in lst])
    return lst


def _count_leading(pred, t):
    sel = jnp.where
    c1 = pred(t[7])
    c2 = pred(sel(c1, t[11], t[3]))
    c3 = pred(sel(c1, sel(c2, t[13], t[9]), sel(c2, t[5], t[1])))
    c4 = pred(sel(c1, sel(c2, sel(c3, t[14], t[12]), sel(c3, t[10], t[8])),
                  sel(c2, sel(c3, t[6], t[4]), sel(c3, t[2], t[0]))))
    c5 = pred(t[15])
    return (sel(c1, 8.0, 0.0) + sel(c2, 4.0, 0.0) + sel(c3, 2.0, 0.0) + sel(c4, 1.0, 0.0)
            + sel(c5, 1.0, 0.0))


def _top16_rows(s):
    lst = [s[SUBLANES * v:SUBLANES * (v + 1), :] for v in range(s.shape[0] // SUBLANES)]
    _bitonic_sort_desc(lst)
    return _merge_sublanes(lst)


def _topk_body(f_ref, wq_ref, k1_ref, k2_ref, e1_o, n_o, r2_o, e2_o, q_s):
    q_s[...] = lax.dot_general(wq_ref[...], f_ref[...], (NT, ((), ())),
                               preferred_element_type=F32).astype(BF16)
    half = PEER_DK // 2
    T = f_ref.shape[0]
    sub = _iota((SUBLANES, T), 0)

    def stack(rows):
        out = rows[SUBLANES - 1]
        for b in range(SUBLANES - 2, -1, -1):
            out = jnp.where(sub == b, rows[b], out)
        return out

    def rep(a):
        return jnp.concatenate([a] * (N_KEYS // SUBLANES), axis=0)

    def head(h, carry):
        base = pl.multiple_of(h * PEER_DK, PEER_DK)
        s1 = jnp.dot(k1_ref[...], q_s[pl.ds(base, half), :], preferred_element_type=F32)
        s2 = jnp.dot(k2_ref[...], q_s[pl.ds(base + half, half), :], preferred_element_type=F32)
        t1 = _top16_rows(s1)
        t2 = _top16_rows(s2)
        lo = stack(t2[:SUBLANES])
        hi = stack(t2[SUBLANES:])
        top = _merge_sublanes(_merge_top([t + lo for t in t1], [t + hi for t in t1]))
        z = jnp.exp(top[0] - top[0])
        for kk in range(1, PEER_TOPK):
            z = z + jnp.exp(top[kk] - top[0])
        thr = rep(top[PEER_TOPK - 1])
        t2r = [rep(t) for t in t2]
        r2 = _count_leading(lambda t: t > s2, t2r)
        n1 = _count_leading(lambda t: s1 + t >= thr, t2r)
        r2_o[h] = _pack_rows(r2)
        n_o[h] = _dup_bf16(n1)
        e1_o[h] = _dup_bf16(jnp.exp(s1 - rep(t1[0])) / rep(z))
        e2_o[h] = _pack_rows(jnp.exp(s2 - rep(t2[0])))
        return carry

    lax.fori_loop(0, PEER_HEADS, head, 0)


def _topk(f, wq_t, k1, k2):
    rows = f.shape[0]
    T = TOPK_TILE
    full = lambda a: pl.BlockSpec(a.shape, lambda i: (0,) * a.ndim)
    big = pl.BlockSpec((PEER_HEADS, N_KEYS, T), lambda i: (0, 0, i))
    big_shape = lambda dt: jax.ShapeDtypeStruct((PEER_HEADS, N_KEYS, rows), dt)
    packed = pl.BlockSpec((PEER_HEADS, N_KEYS // 2, T), lambda i: (0, 0, i))
    packed_shape = jax.ShapeDtypeStruct((PEER_HEADS, N_KEYS // 2, rows), U32)
    return pl.pallas_call(
        _topk_body, grid=(rows // T,),
        in_specs=[pl.BlockSpec((T, D_MODEL), lambda i: (i, 0)), full(wq_t), full(k1), full(k2)],
        out_specs=[big, big, packed, packed],
        out_shape=[big_shape(U32), big_shape(U32), packed_shape, packed_shape],
        scratch_shapes=[pltpu.VMEM((PEER_HEADS * PEER_DK, T), BF16)],
        compiler_params=_cparams(1), name="peer_topk",
    )(f, wq_t, k1, k2)


def _peer_body(final, n_e, f_ref, u_ref, v_ref, e1_ref, n_ref, r2_ref, e2_ref, h_ref,
               gate_ref, fg_ref, out, st_0, st_1, at_s, acc):
    s = pl.program_id(0)
    j_up = jnp.maximum(s - 1, 0) % n_e

    @pl.when(s == 0)
    def _():
        st_1[...] = jnp.zeros_like(st_1)

    @pl.when(j_up == 0)
    def _():
        acc[...] = jnp.zeros_like(acc)

    def stage(st_w, st_r):
        st_w[...] = lax.dot_general(u_ref[...], f_ref[...], (NT, ((), ())),
                                    preferred_element_type=F32)
        blk = (N_KEYS // 2, LANES)
        rows_per_blk = 2
        for e in range(EXP_TILE // N_KEYS // rows_per_blk):
            for ii in range(e * rows_per_blk, (e + 1) * rows_per_blk):
                rows = slice(ii * N_KEYS, (ii + 1) * N_KEYS)
                for lg in range(TOK_TILE // LANES):
                    ln = slice(lg * LANES, (lg + 1) * LANES)
                    g = None
                    for h in range(PEER_HEADS):
                        n_row = _unpack_rows(jnp.broadcast_to(n_ref[h, ii:ii + 1, ln], blk))
                        e1_row = _unpack_rows(jnp.broadcast_to(e1_ref[h, ii:ii + 1, ln], blk))
                        w = jnp.where(_unpack_rows(r2_ref[h, :, ln]) < n_row,
                                      _unpack_rows(e2_ref[h, :, ln]), 0.0) * e1_row
                        g = w if g is None else g + w
                    a_t = (_gelu_tanh(st_r[rows, ln]).astype(BF16) * g).T
                    at_s[lg * LANES // 2:(lg + 1) * LANES // 2, rows] = _pack_rows(a_t)
            ecols = slice(e * rows_per_blk * N_KEYS, (e + 1) * rows_per_blk * N_KEYS)
            acc[...] += jnp.dot(_unpack_rows(at_s[:, ecols]), v_ref[ecols, :],
                                preferred_element_type=F32)

    @pl.when(s % 2 == 0)
    def _():
        stage(st_0, st_1)

    @pl.when(s % 2 == 1)
    def _():
        stage(st_1, st_0)

    @pl.when(jnp.logical_and(s >= 1, j_up == n_e - 1))
    def _():
        h = h_ref[...] + gate_ref[...] * acc[...]
        if final:
            h = h * lax.rsqrt(jnp.mean(h * h, axis=-1, keepdims=True) + EPS) * fg_ref[...]
        out[...] = h


def _peer(f, u_b, v_b, e1, n1, r2, e2, h, mods, fg, n_lat, per_seq, n_tok_tiles, final):
    n_e = N_EXPERTS // EXP_TILE
    rpt = EXP_TILE // N_KEYS
    n_steps = n_tok_tiles * n_e

    def tile(s, lag):
        t = jnp.clip(s - lag, 0, n_steps - 1)
        return t // n_e, t % n_e

    tok = lambda lag: pl.BlockSpec((TOK_TILE, D_MODEL), lambda s: (tile(s, lag)[0], 0))
    exp = lambda lag: pl.BlockSpec((EXP_TILE, D_MODEL), lambda s: (tile(s, lag)[1], 0))
    rowblk = pl.BlockSpec((PEER_HEADS, rpt, TOK_TILE), lambda s: (0, tile(s, 1)[1], tile(s, 1)[0]))
    allkeys = pl.BlockSpec((PEER_HEADS, N_KEYS // 2, TOK_TILE), lambda s: (0, 0, tile(s, 1)[0]))

    def gate_map(s):
        i = tile(s, 1)[0]
        row = jnp.where(i < n_lat, i // per_seq, 2)
        return (row * 6 + 5, 0, 0)

    return pl.pallas_call(
        functools.partial(_peer_body, final, n_e),
        grid=(n_steps + 1,),
        in_specs=[tok(0), exp(0), exp(1), rowblk, rowblk, allkeys, allkeys, tok(1),
                  pl.BlockSpec((None, 1, D_MODEL), gate_map),
                  pl.BlockSpec((1, D_MODEL), lambda s: (0, 0))],
        out_specs=tok(1),
        out_shape=jax.ShapeDtypeStruct((n_tok_tiles * TOK_TILE, D_MODEL), F32),
        scratch_shapes=[pltpu.VMEM((EXP_TILE, TOK_TILE), F32),
                        pltpu.VMEM((EXP_TILE, TOK_TILE), F32),
                        pltpu.VMEM((TOK_TILE // 2, EXP_TILE), U32),
                        pltpu.VMEM((TOK_TILE, D_MODEL), F32)],
        compiler_params=_cparams(1), name="peer",
    )(f, u_b, v_b, e1, n1, r2, e2, h, mods, fg)


def _sincos_2d(t_len):
    rows = t_len // GRID_W
    row = jnp.repeat(jnp.arange(rows), GRID_W).astype(F32)
    col = jnp.tile(jnp.arange(GRID_W), rows).astype(F32)
    q = D_MODEL // 4
    freq = 10000.0 ** (-jnp.arange(q, dtype=F32) / q)
    ar = row[:, None] * freq
    ac = col[:, None] * freq
    return jnp.concatenate([jnp.sin(ar), jnp.cos(ar), jnp.sin(ac), jnp.cos(ac)], -1)


def _row(a):
    return a.reshape(1, -1).astype(F32)


def _pad_lanes(a, width=LANES):
    return jnp.pad(a, ((0, 0), (0, width - a.shape[-1])))


def kernel(x, c, ctx, c_ctx, ada_w, ada_b, norm_mix_g, norm_ffn_g, w_in, w_out, a_conv_w, a_conv_b, a_ln_g, a_ln_b, b_ln_g, b_ln_b, b_ws, b_bs, c_conv_w, c_conv_b, c_dt_bias, c_a_log, c_d, c_norm_g, d_mu_prev, d_mu_next, d_w0, d_w2, d_a0, d_a2, d_g2, d_k_k, d_k_a, d_r_k, d_gn_g, d_gn_b, peer_wq, peer_k1, peer_k2, peer_u, peer_v, final_g):
    n_b, t_lat, _ = x.shape
    t_ctx = ctx.shape[1]
    assert n_b == 2 and t_ctx == SEQ_TILE and t_lat % TOK_TILE == 0
    n_l = t_lat // SEQ_TILE
    per_seq = t_lat // TOK_TILE
    n_lat = n_b * per_seq
    n_lat256 = n_b * n_l

    cs = jnp.zeros((8, D_MODEL), F32).at[:n_b].set(c).at[n_b].set(c_ctx)
    mods_all = _ada(cs, ada_w, ada_b)
    pos = _sincos_2d(t_lat)
    x2 = x.reshape(n_b * t_lat, D_MODEL)
    ctx2 = ctx.reshape(n_b * t_ctx, D_MODEL)

    h = None
    for i in range(DEPTH):
        last_layer = i == DEPTH - 1
        mods = mods_all[i].reshape(8 * 6, 1, D_MODEL)
        w = w_in[i]
        dtc = OFF_C + D_GROUP + C_XBC
        w_c = jnp.concatenate([w[:, OFF_C:dtc], _pad_lanes(w[:, dtc:dtc + HEADS_C]),
                               _pad_lanes(w[:, dtc + HEADS_C:dtc + 2 * HEADS_C])], axis=1)
        ws_in = tuple(a.astype(BF16) for a in (w[:, :OFF_B], w[:, OFF_B:OFF_C], w_c, w[:, OFF_D:]))
        g_mix = _row(norm_mix_g[i])
        if i == 0:
            h, pa, pb, pc, pd = _inproj((x2, ctx2, pos), mods, g_mix, ws_in, n_lat, per_seq, True)
        else:
            pa, pb, pc, pd = _inproj(h, mods, g_mix, ws_in, n_lat, per_seq, False)

        conv_a = jnp.pad(a_conv_w[i], ((0, 32 - CONV_A), (0, 0)))
        bsm = jnp.repeat(b_bs[i].T, D_GROUP // HEADS_B, axis=1)
        yab = _mixab(pa, pb, (conv_a, _row(a_conv_b[i]), _row(a_ln_g[i]), _row(a_ln_b[i]),
                              _row(b_ln_g[i]), _row(b_ln_b[i]), b_ws[i].astype(BF16), bsm),
                     n_lat256, n_l)

        conv_c = jnp.pad(c_conv_w[i], ((0, 8 - CONV_C), (0, 0)))
        dtb = _pad_lanes(c_dt_bias[i]).reshape(2, 1, LANES)
        alog = _pad_lanes(c_a_log[i]).reshape(2, 1, LANES)
        dsk = _row(jnp.repeat(c_d[i], HEAD_DIM_C))
        yc = _mamba(pc, (conv_c, _row(c_conv_b[i]), dtb, alog, dsk, _row(c_norm_g[i])), n_b, n_l)

        yd = _rwkv(pd, (_row(d_mu_prev[i]), _row(d_mu_next[i]), d_w0[i].reshape(2, 1, D_GROUP),
                        d_w2[i].astype(BF16), d_a0[i].reshape(2, 1, D_GROUP), d_a2[i].astype(BF16),
                        d_g2[i].astype(BF16), _row(d_k_k[i]), _row(d_k_a[i]), _row(d_r_k[i]),
                        _row(d_gn_g[i]), _row(d_gn_b[i])), n_b, n_l)

        wo = w_out[i].astype(BF16)
        h, f = _outproj(yab, yc, yd, h, mods, _row(norm_ffn_g[i]),
                        (wo[:512], wo[512:768], wo[768:]), n_lat, per_seq)

        e1, n1, r2, e2 = _topk(f, peer_wq[i].T.astype(BF16), peer_k1[i].astype(BF16),
                               peer_k2[i].astype(BF16))
        n_tok = n_lat if last_layer else n_lat + 1
        h = _peer(f, peer_u[i].astype(BF16), peer_v[i].astype(BF16), e1, n1, r2, e2, h,
                  mods, _row(final_g), n_lat, per_seq, n_tok, last_layer)
    return h.reshape(n_b, t_lat, D_MODEL)
```

```python
import functools
import math

import jax
import jax.numpy as jnp
from jax import lax
from jax.experimental import pallas as pl
from jax.experimental.pallas import tpu as pltpu

F32 = jnp.float32
BF16 = jnp.bfloat16
U32 = jnp.uint32
HIGHEST = lax.Precision.HIGHEST

D_MODEL = 1024
DEPTH = 2
GRID_W = 64
EPS = 1e-6
D_GROUP = 256
CONV_A = 31
CHUNK_B = 128
HEADS_B = 4
HEADS_C = 4
HEAD_DIM_C = 64
STATE_C = 128
CONV_C = 5
CHUNK_C = 128
HEADS_D = 4
HEAD_DIM_D = 64
LORA_W = 64
LORA_A = 64
LORA_G = 128
GN_EPS_D = 64e-5
N_KEYS = 128
N_EXPERTS = N_KEYS * N_KEYS
PEER_HEADS = 8
PEER_DK = 256
PEER_TOPK = 16

C_XBC = D_GROUP + 2 * 2 * STATE_C
OFF_B = 2 * D_GROUP
OFF_C = OFF_B + 2 * D_GROUP
OFF_D = OFF_C + D_GROUP + C_XBC + 2 * HEADS_C
D_COLS = 3 * D_GROUP + 2 * LORA_W + 2 * LORA_A + LORA_G
PC_COLS = D_GROUP + C_XBC + 2 * 128

LANES = 128
SEQ_TILE = 256
TOK_TILE = 512
CHUNK_D = 64
EXP_TILE = 2048
VMEM_LIMIT = 56 * 1024 * 1024


def _cparams(n_axes):
    return pltpu.CompilerParams(dimension_semantics=("arbitrary",) * n_axes,
                                vmem_limit_bytes=VMEM_LIMIT)


def _silu(x):
    return x * jax.nn.sigmoid(x)


def _dot(a, b, dims=None):
    a = a.astype(BF16)
    b = b.astype(BF16)
    if dims is None:
        return jnp.dot(a, b, preferred_element_type=F32)
    return lax.dot_general(a, b, (dims, ((), ())), preferred_element_type=F32)


def _dot_hi(a, b):
    return jnp.dot(a, b, precision=HIGHEST, preferred_element_type=F32)


NT = ((1,), (1,))
TN = ((0,), (0,))


def _gelu_tanh(x):
    k1 = math.sqrt(2.0 / math.pi)
    k3 = 0.044715 * k1
    hx = 0.5 * x
    return hx + hx * jnp.tanh(x * (k1 + k3 * (x * x)))


def _pack_rows(x):
    return pltpu.bitcast(x.astype(BF16), U32)


def _unpack_rows(x):
    return pltpu.bitcast(x, BF16)


def _dup_bf16(x):
    hi = pltpu.bitcast(x.astype(BF16).astype(F32), U32)
    return hi | (hi >> 16)


def _iota(shape, axis):
    return lax.broadcasted_iota(jnp.int32, shape, axis)


def _head_block_ones(n, width):
    return (_iota((n, n), 0) // width == _iota((n, n), 1) // width).astype(F32)


def _ada_body(cs_ref, w_ref, b_ref, o_ref):
    o_ref[...] = _dot_hi(_silu(cs_ref[...]), w_ref[...]) + b_ref[...]


def _ada(cs, ada_w, ada_b):
    L = ada_w.shape[0]
    nb = 1536
    return pl.pallas_call(
        _ada_body,
        grid=(L, 6 * D_MODEL // nb),
        in_specs=[pl.BlockSpec((8, D_MODEL), lambda l, n: (0, 0)),
                  pl.BlockSpec((None, D_MODEL, nb), lambda l, n: (l, 0, n)),
                  pl.BlockSpec((None, 1, nb), lambda l, n: (l, 0, n))],
        out_specs=pl.BlockSpec((None, 8, nb), lambda l, n: (l, 0, n)),
        out_shape=jax.ShapeDtypeStruct((L, 8, 6 * D_MODEL), F32),
        compiler_params=_cparams(2),
        name="ada",
    )(cs, ada_w, ada_b.reshape(L, 1, 6 * D_MODEL))


def _norm_mod(h, g, shift, scale):
    xn = h * lax.rsqrt(jnp.mean(h * h, axis=-1, keepdims=True) + EPS) * g
    return xn * (1.0 + scale) + shift


def _inproj_tail(h, sh_ref, sc_ref, g_ref, wa, wb, wc, wd, oa, ob, oc, od):
    xm = _norm_mod(h, g_ref[...], sh_ref[...], sc_ref[...]).astype(BF16)
    oa[...] = jnp.dot(xm, wa[...], preferred_element_type=F32)
    ob[...] = jnp.dot(xm, wb[...], preferred_element_type=F32)
    oc[...] = jnp.dot(xm, wc[...], preferred_element_type=F32)
    od[...] = jnp.dot(xm, wd[...], preferred_element_type=F32)


def _inproj_first_body(n_lat, x_ref, ctx_ref, pos_ref, sh_ref, sc_ref, g_ref, wa, wb, wc, wd,
                       oh, oa, ob, oc, od):
    i = pl.program_id(0)
    h = jnp.where(i < n_lat, x_ref[...] + pos_ref[...], ctx_ref[...])
    oh[...] = h
    _inproj_tail(h, sh_ref, sc_ref, g_ref, wa, wb, wc, wd, oa, ob, oc, od)


def _inproj_body(h_ref, sh_ref, sc_ref, g_ref, wa, wb, wc, wd, oa, ob, oc, od):
    _inproj_tail(h_ref[...], sh_ref, sc_ref, g_ref, wa, wb, wc, wd, oa, ob, oc, od)


def _mod_spec(n_lat, per_seq, k):
    def imap(i):
        row = jnp.where(i < n_lat, i // per_seq, 2)
        return (row * 6 + k, 0, 0)
    return pl.BlockSpec((None, 1, D_MODEL), imap)


def _inproj(h_or_parts, mods, g, ws, n_lat, per_seq, first):
    wa, wb, wc, wd = ws
    n_tiles = n_lat + 1
    rows = n_tiles * TOK_TILE
    tile = lambda c: pl.BlockSpec((TOK_TILE, c), lambda i: (i, 0))
    full = lambda a: pl.BlockSpec(a.shape, lambda i: (0,) * a.ndim)
    common_specs = [_mod_spec(n_lat, per_seq, 0), _mod_spec(n_lat, per_seq, 1), full(g),
                    full(wa), full(wb), full(wc), full(wd)]
    out_specs = [tile(512), tile(512), tile(PC_COLS), tile(D_COLS)]
    out_shape = [jax.ShapeDtypeStruct((rows, c), F32) for c in (512, 512, PC_COLS, D_COLS)]
    if first:
        x2, ctx2, pos = h_or_parts
        in_specs = [pl.BlockSpec((TOK_TILE, D_MODEL), lambda i: (jnp.minimum(i, n_lat - 1), 0)),
                    pl.BlockSpec((TOK_TILE, D_MODEL), lambda i: (0, 0)),
                    pl.BlockSpec((TOK_TILE, D_MODEL), lambda i: (i % per_seq, 0))] + common_specs
        return pl.pallas_call(
            functools.partial(_inproj_first_body, n_lat),
            grid=(n_tiles,), in_specs=in_specs,
            out_specs=[tile(D_MODEL)] + out_specs,
            out_shape=[jax.ShapeDtypeStruct((rows, D_MODEL), F32)] + out_shape,
            compiler_params=_cparams(1), name="inproj_first",
        )(x2, ctx2, pos, mods, mods, g, wa, wb, wc, wd)
    return pl.pallas_call(
        _inproj_body, grid=(n_tiles,), in_specs=[tile(D_MODEL)] + common_specs,
        out_specs=out_specs, out_shape=out_shape,
        compiler_params=_cparams(1), name="inproj",
    )(h_or_parts, mods, mods, g, wa, wb, wc, wd)


def _layernorm(x, g, b, eps=1e-5):
    mu = jnp.mean(x, axis=-1, keepdims=True)
    xc = x - mu
    var = jnp.mean(xc * xc, axis=-1, keepdims=True)
    return xc * lax.rsqrt(var + eps) * g + b


def _mixab_body(n_lat, n_l, pa, pa_prev, pa_next, pb, cw, cb, alg, alb, blg, blb, ws, bsm,
                out, ext):
    i = pl.program_id(0)
    is_ctx = i >= n_lat
    tpos = i % n_l
    first = jnp.logical_or(is_ctx, tpos == 0)
    last = jnp.logical_or(is_ctx, tpos == n_l - 1)

    def glu(x):
        return x[:, :D_GROUP] * jax.nn.sigmoid(x[:, D_GROUP:])

    halo = 16
    ext[0:halo, :] = jnp.where(first, 0.0, glu(pa_prev[...]))
    ext[halo:halo + SEQ_TILE, :] = glu(pa[...])
    ext[halo + SEQ_TILE:, :] = jnp.where(last, 0.0, glu(pa_next[...]))
    acc = jnp.zeros((SEQ_TILE, D_GROUP), F32) + cb[...]
    for j in range(CONV_A):
        acc = acc + cw[j:j + 1, :] * ext[pl.ds(halo - (CONV_A - 1) // 2 + j, SEQ_TILE), :]
    out[:, :D_GROUP] = _silu(_layernorm(acc, alg[...], alb[...]))

    x = pb[...]
    u = x[:, :D_GROUP]
    v = _layernorm(x[:, D_GROUP:], blg[...], blb[...]).astype(BF16)
    hd = D_GROUP // HEADS_B
    rows = []
    for c in range(SEQ_TILE // CHUNK_B):
        vc = v[c * CHUNK_B:(c + 1) * CHUNK_B, :]
        heads = [jnp.dot(ws[hh], vc[:, hh * hd:(hh + 1) * hd], preferred_element_type=F32)
                 for hh in range(HEADS_B)]
        rows.append(jnp.concatenate(heads, axis=1) + bsm[...])
    out[:, D_GROUP:] = u * jnp.concatenate(rows, axis=0)


def _mixab(pa, pb, prm, n_lat, n_l):
    rows = pa.shape[0]
    n_tiles = rows // SEQ_TILE
    halo = 16
    per = SEQ_TILE // halo
    full = lambda a: pl.BlockSpec(a.shape, lambda i: (0,) * a.ndim)
    in_specs = [pl.BlockSpec((SEQ_TILE, 512), lambda i: (i, 0)),
                pl.BlockSpec((halo, 512), lambda i: (jnp.maximum(i * per - 1, 0), 0)),
                pl.BlockSpec((halo, 512), lambda i: (jnp.minimum((i + 1) * per, rows // halo - 1), 0)),
                pl.BlockSpec((SEQ_TILE, 512), lambda i: (i, 0))] + [full(a) for a in prm]
    return pl.pallas_call(
        functools.partial(_mixab_body, n_lat, n_l),
        grid=(n_tiles,), in_specs=in_specs,
        out_specs=pl.BlockSpec((SEQ_TILE, 512), lambda i: (i, 0)),
        out_shape=jax.ShapeDtypeStruct((rows, 512), F32),
        scratch_shapes=[pltpu.VMEM((SEQ_TILE + 2 * halo, D_GROUP), F32)],
        compiler_params=_cparams(1), name="mixab",
    )(pa, pa, pa, pb, *prm)


def _scan_tile(n_b, n_l, b, ph, j):
    lat = b * n_l + jnp.where(ph == 0, j - 1, n_l - j)
    return jnp.where(j == 0, n_b * n_l + b, lat)


def _scan_specs(n_b, n_l, rows, cols):
    tid = functools.partial(_scan_tile, n_b, n_l)
    per = SEQ_TILE // 8
    cur = pl.BlockSpec((SEQ_TILE, cols), lambda b, ph, j: (tid(b, ph, j), 0))
    prev = pl.BlockSpec((8, cols), lambda b, ph, j: (jnp.maximum(tid(b, ph, j) * per - 1, 0), 0))
    nxt = pl.BlockSpec((8, cols),
                       lambda b, ph, j: (jnp.minimum((tid(b, ph, j) + 1) * per, rows // 8 - 1), 0))
    return cur, prev, nxt


def _scan_out_spec(n_b, n_l, cols):
    def imap(b, ph, j):
        return (jnp.where(ph == 0, n_b * n_l + b, _scan_tile(n_b, n_l, b, 1, j)), 0)
    return pl.BlockSpec((SEQ_TILE, cols), imap)


def _scan_flags(n_l):
    ph = pl.program_id(1)
    j = pl.program_id(2)
    is_ctx = j == 0
    tpos = jnp.where(ph == 0, j - 1, n_l - j)
    first = jnp.logical_or(is_ctx, tpos == 0)
    last = jnp.logical_or(is_ctx, tpos == n_l - 1)
    slot = jnp.where(is_ctx, 0, tpos + 1)
    return ph, j, first, last, slot


def _mamba_body(n_l, cur_ref, prev_ref, next_ref, dt_ref, cw, cb, dtb, alog, dsk, ng,
                out, ext, yf, st, xbc_s, a_s, dt_s, y_s):
    ph, j, first, last, slot = _scan_flags(n_l)
    fwd = ph == 0

    @pl.when(j == 0)
    def _():
        st[...] = jnp.zeros_like(st)

    zx = D_GROUP
    ext[0:8, :] = jnp.where(first, 0.0, prev_ref[:, zx:])
    ext[8:8 + SEQ_TILE, :] = cur_ref[:, zx:]
    ext[8 + SEQ_TILE:, :] = jnp.where(last, 0.0, next_ref[:, zx:])
    acc = jnp.zeros((SEQ_TILE, C_XBC), F32) + cb[...]
    for jj in range(CONV_C):
        acc = acc + cw[jj:jj + 1, :] * ext[pl.ds(8 - (CONV_C - 1) // 2 + jj, SEQ_TILE), :]
    xbc_s[...] = _silu(acc)
    dt = jax.nn.softplus(dt_ref[...] + dtb[ph])
    dt_s[...] = dt
    a_s[...] = dt * (-jnp.exp(alog[ph]))

    L = CHUNK_C
    r_i = _iota((L, L), 0)
    c_i = _iota((L, L), 1)
    ltri = (r_i >= c_i).astype(F32)
    mask = jnp.where(fwd, r_i - c_i, c_i - r_i) >= 0
    sgn = jnp.where(fwd, 1.0, -1.0)
    hd = HEAD_DIM_C

    def chunk(it, carry):
        q = jnp.where(fwd, it, SEQ_TILE // L - 1 - it)
        rows = pl.ds(pl.multiple_of(q * L, L), L)
        a_q = a_s[rows, :]
        cs = _dot_hi(ltri, a_q)
        tot = cs[L - 1:L, :]
        e = jnp.where(fwd, cs, cs - a_q)
        e_t = e.T
        dt_q = dt_s[rows, :]
        xbc = xbc_s[rows, :]
        ys = []
        for g in range(2):
            bm = xbc[:, D_GROUP + g * STATE_C:D_GROUP + (g + 1) * STATE_C]
            cm = xbc[:, D_GROUP + 2 * STATE_C + g * STATE_C:D_GROUP + 2 * STATE_C + (g + 1) * STATE_C]
            bm_t = bm.T.astype(BF16)
            cm_b = cm.astype(BF16)
            gmat = jnp.dot(cm_b, bm_t, preferred_element_type=F32)
            for h in (2 * g, 2 * g + 1):
                ecol = e[:, h:h + 1]
                erow = e_t[h:h + 1, :]
                totc = tot[:, h:h + 1]
                xdt = xbc[:, h * hd:(h + 1) * hd] * dt_q[:, h:h + 1]
                lmat = jnp.exp(jnp.where(mask, sgn * (ecol - erow), -1e30))
                offs = jnp.exp(jnp.where(fwd, ecol, totc - ecol))
                stw = jnp.exp(jnp.where(fwd, totc - ecol, ecol))
                s_prev = st[h]
                y = _dot(gmat * lmat, xdt) + offs * _dot(cm_b, s_prev)
                st[h] = jnp.exp(totc) * s_prev + _dot(bm_t, xdt * stw)
                ys.append(y)
        y_s[rows, :] = jnp.concatenate(ys, axis=1)
        return carry

    lax.fori_loop(0, SEQ_TILE // L, chunk, 0)

    @pl.when(fwd)
    def _():
        yf[slot] = y_s[...]

    @pl.when(ph == 1)
    def _():
        y = yf[slot] + y_s[...] + dsk[...] * xbc_s[:, :D_GROUP]
        t = y * _silu(cur_ref[:, :D_GROUP])
        out[...] = t * lax.rsqrt(jnp.mean(t * t, axis=-1, keepdims=True) + EPS) * ng[...]


def _mamba(pc, prm, n_b, n_l):
    rows = pc.shape[0]
    cur, prev, nxt = _scan_specs(n_b, n_l, rows, D_GROUP + C_XBC)
    tid = functools.partial(_scan_tile, n_b, n_l)
    dt_spec = pl.BlockSpec((SEQ_TILE, LANES),
                           lambda b, ph, j: (tid(b, ph, j), (D_GROUP + C_XBC) // LANES + ph))
    full = lambda a: pl.BlockSpec(a.shape, lambda b, ph, j: (0,) * a.ndim)
    return pl.pallas_call(
        functools.partial(_mamba_body, n_l),
        grid=(n_b, 2, n_l + 1),
        in_specs=[cur, prev, nxt, dt_spec] + [full(a) for a in prm],
        out_specs=_scan_out_spec(n_b, n_l, D_GROUP),
        out_shape=jax.ShapeDtypeStruct((rows, D_GROUP), F32),
        scratch_shapes=[pltpu.VMEM((SEQ_TILE + 16, C_XBC), F32),
                        pltpu.VMEM((n_l + 1, SEQ_TILE, D_GROUP), F32),
                        pltpu.VMEM((HEADS_C, STATE_C, HEAD_DIM_C), F32),
                        pltpu.VMEM((SEQ_TILE, C_XBC), F32),
                        pltpu.VMEM((SEQ_TILE, LANES), F32),
                        pltpu.VMEM((SEQ_TILE, LANES), F32),
                        pltpu.VMEM((SEQ_TILE, D_GROUP), F32)],
        compiler_params=_cparams(3), name="mamba",
    )(pc, pc, pc, pc, *prm)


def _rwkv_body(n_l, cur_ref, prev_ref, next_ref, mup, mun, w0, w2, a0, a2, g2, kkw, kaw, rkw,
               gng, gnb, out, yf, st, r_s, v_s, kap_s, alp_s, kd_s, lw_s, y_s,
               phi_s, psi_s, sin_s, left_s, amat_s, ara_s, tinv_s):
    ph, j, first, last, slot = _scan_flags(n_l)
    fwd = ph == 0

    @pl.when(j == 0)
    def _():
        st[...] = jnp.zeros_like(st)

    cur = cur_ref[...]
    prv = jnp.concatenate([jnp.where(first, 0.0, prev_ref[7:8, :]), cur[:SEQ_TILE - 1, :]], axis=0)
    nxt = jnp.concatenate([cur[1:, :], jnp.where(last, 0.0, next_ref[0:1, :])], axis=0)
    p = cur + mup[...] * (prv - cur) + mun[...] * (nxt - cur)
    G = D_GROUP
    r = p[:, :G]
    k = p[:, G:2 * G]
    v = p[:, 2 * G:3 * G]
    blk = _head_block_ones(G, HEAD_DIM_D)
    kkr = k * kkw[...]
    kk = kkr * lax.rsqrt(_dot_hi(kkr * kkr, blk) + 1e-12)

    def rate(d_static=None):
        if d_static is None:
            ad = jnp.where(fwd, p[:, 3 * G + 2 * LORA_W:3 * G + 2 * LORA_W + LORA_A],
                           p[:, 3 * G + 2 * LORA_W + LORA_A:3 * G + 2 * LORA_W + 2 * LORA_A])
            a = jax.nn.sigmoid(a0[ph] + _dot(ad, a2[ph]))
        else:
            o = 3 * G + 2 * LORA_W + d_static * LORA_A
            a = jax.nn.sigmoid(a0[d_static] + _dot(p[:, o:o + LORA_A], a2[d_static]))
        return a, k * (1.0 + (a - 1.0) * kaw[...])

    a_d, kd_d = rate()
    wd = jnp.where(fwd, p[:, 3 * G:3 * G + LORA_W], p[:, 3 * G + LORA_W:3 * G + 2 * LORA_W])
    w = w0[ph] + _dot(jnp.tanh(wd), w2[ph])
    lw_s[...] = -math.exp(-0.5) * jax.nn.sigmoid(w)
    r_s[...] = r
    v_s[...] = v
    kap_s[...] = kk
    alp_s[...] = a_d * kk
    kd_s[...] = kd_d

    C = CHUNK_D
    r_i = _iota((C, C), 0)
    c_i = _iota((C, C), 1)
    lag = jnp.where(fwd, r_i - c_i, c_i - r_i)
    tri = lag >= 0
    tri_f = tri.astype(F32)
    strict = lag > 0
    eye = (r_i == c_i).astype(F32)
    hd = HEAD_DIM_D

    n_q = SEQ_TILE // C
    eye_k = (_iota((hd, hd), 0) == _iota((hd, hd), 1)).astype(F32)

    units = [(q, h) for q in range(n_q) for h in range(HEADS_D)]
    rows_of = lambda q: slice(q * C, (q + 1) * C)
    lanes_of = lambda h: slice(h * hd, (h + 1) * hd)
    left, right, p_tot = {}, {}, {}
    for q in range(n_q):
        rows = rows_of(q)
        lw = lw_s[rows, :]
        incl = _dot_hi(tri_f, lw)
        tot = jnp.where(fwd, incl[C - 1:C, :], incl[0:1, :])
        p_inv = jnp.exp(-incl)
        kap_h = kap_s[rows, :] * jnp.exp(incl - lw)
        r_h = r_s[rows, :] * jnp.exp(incl)
        alp_b = alp_s[rows, :] * p_inv
        k_b = kd_s[rows, :] * p_inv
        ptq = jnp.exp(tot)
        for h in range(HEADS_D):
            sl = lanes_of(h)
            left[q, h] = jnp.concatenate([kap_h[:, sl], r_h[:, sl]], axis=0)
            right[q, h] = jnp.concatenate([alp_b[:, sl], k_b[:, sl]], axis=0)
            p_tot[q, h] = ptq[:, sl]
            left_s[q * HEADS_D + h] = left[q, h]
    m1 = {u: _dot(left[u], right[u], NT) for u in units}
    a_vk, pw, tinv = {}, {}, {}
    for u in units:
        qh = u[0] * HEADS_D + u[1]
        a_vk[u] = jnp.where(strict, m1[u][:C, C:], 0.0)
        amat_s[qh] = jnp.concatenate([a_vk[u], jnp.where(tri, m1[u][C:, C:], 0.0)], axis=0)
        ara_s[qh] = jnp.where(tri, m1[u][C:, :C], 0.0)
        pw[u] = jnp.where(strict, -m1[u][:C, :C], 0.0)
        tinv[u] = eye + pw[u]
    for _ in range(5):
        pw = {u: _dot(pw[u], pw[u]) for u in units}
        upd = {u: _dot(tinv[u], pw[u]) for u in units}
        tinv = {u: tinv[u] + upd[u] for u in units}
    wmat = {u: _dot(tinv[u], right[u][:C], TN) for u in units}
    kw = {u: _dot(left[u][:C], wmat[u], TN) for u in units}
    aw = {u: _dot(a_vk[u], wmat[u], TN) for u in units}
    vk = {u: _dot(v_s[rows_of(u[0]), lanes_of(u[1])], right[u][C:] - aw[u], TN) for u in units}
    for u in units:
        qh = u[0] * HEADS_D + u[1]
        phi_s[qh] = (eye_k - kw[u]) * p_tot[u]
        psi_s[qh] = vk[u] * p_tot[u]
        tinv_s[qh] = tinv[u]

    for it in range(n_q):
        q = jnp.where(fwd, it, n_q - 1 - it)
        for h in range(HEADS_D):
            qh = q * HEADS_D + h
            s_in = st[h]
            sin_s[qh] = s_in
            st[h] = _dot(s_in, phi_s[qh]) + psi_s[qh]

    qh_of = lambda u: u[0] * HEADS_D + u[1]
    x0 = {u: _dot(left_s[qh_of(u)], sin_s[qh_of(u)], NT) for u in units}
    x1 = {u: _dot(amat_s[qh_of(u)], v_s[rows_of(u[0]), lanes_of(u[1])]) for u in units}
    uu = {u: _dot(tinv_s[qh_of(u)], x0[u][:C] + x1[u][:C]) for u in units}
    au = {u: _dot(ara_s[qh_of(u)], uu[u]) for u in units}
    for q in range(n_q):
        y_s[rows_of(q), :] = jnp.concatenate(
            [x0[q, h][C:] + x1[q, h][C:] - au[q, h] for h in range(HEADS_D)], axis=1)

    @pl.when(fwd)
    def _():
        yf[slot] = y_s[...]

    @pl.when(ph == 1)
    def _():
        y = yf[slot] + y_s[...]
        inv = 1.0 / HEAD_DIM_D
        mu = _dot_hi(y, blk) * inv
        yc = y - mu
        var = _dot_hi(yc * yc, blk) * inv
        yn = yc * lax.rsqrt(var + GN_EPS_D) * gng[...] + gnb[...]
        _, kd_f = rate(0)
        bonus = _dot_hi(r * (kd_f + kd_d) * rkw[...], blk) * v
        gate = _dot(jax.nn.sigmoid(p[:, 3 * G + 2 * LORA_W + 2 * LORA_A:]), g2[...])
        out[...] = (yn + bonus) * gate


def _rwkv(pd, prm, n_b, n_l):
    rows = pd.shape[0]
    cur, prev, nxt = _scan_specs(n_b, n_l, rows, D_COLS)
    full = lambda a: pl.BlockSpec(a.shape, lambda b, ph, j: (0,) * a.ndim)
    tile = lambda: pltpu.VMEM((SEQ_TILE, D_GROUP), F32)
    per_qh = lambda r, c: pltpu.VMEM((SEQ_TILE // CHUNK_D * HEADS_D, r, c), F32)
    return pl.pallas_call(
        functools.partial(_rwkv_body, n_l),
        grid=(n_b, 2, n_l + 1),
        in_specs=[cur, prev, nxt] + [full(a) for a in prm],
        out_specs=_scan_out_spec(n_b, n_l, D_GROUP),
        out_shape=jax.ShapeDtypeStruct((rows, D_GROUP), F32),
        scratch_shapes=[pltpu.VMEM((n_l + 1, SEQ_TILE, D_GROUP), F32),
                        pltpu.VMEM((HEADS_D, HEAD_DIM_D, HEAD_DIM_D), F32)] + [tile() for _ in range(7)]
        + [per_qh(HEAD_DIM_D, HEAD_DIM_D) for _ in range(3)]
        + [per_qh(2 * CHUNK_D, HEAD_DIM_D), per_qh(2 * CHUNK_D, CHUNK_D),
           per_qh(CHUNK_D, CHUNK_D), per_qh(CHUNK_D, CHUNK_D)],
        compiler_params=_cparams(3), name="rwkv",
    )(pd, pd, pd, *prm)


def _outproj_body(yab, yc, yd, h_ref, gate, sh, sc, g, wab, wc, wd, oh, of):
    mix = (_dot(yab[...], wab[...]) + _dot(yc[...], wc[...]) + _dot(yd[...], wd[...]))
    h = h_ref[...] + gate[...] * mix
    oh[...] = h
    of[...] = _norm_mod(h, g[...], sh[...], sc[...]).astype(BF16)


def _outproj(yab, yc, yd, h, mods, g, ws, n_lat, per_seq):
    rows = h.shape[0]
    tile = lambda c: pl.BlockSpec((TOK_TILE, c), lambda i: (i, 0))
    full = lambda a: pl.BlockSpec(a.shape, lambda i: (0,) * a.ndim)
    return pl.pallas_call(
        _outproj_body, grid=(rows // TOK_TILE,),
        in_specs=[tile(512), tile(D_GROUP), tile(D_GROUP), tile(D_MODEL),
                  _mod_spec(n_lat, per_seq, 2), _mod_spec(n_lat, per_seq, 3),
                  _mod_spec(n_lat, per_seq, 4), full(g)] + [full(a) for a in ws],
        out_specs=[tile(D_MODEL), tile(D_MODEL)],
        out_shape=[jax.ShapeDtypeStruct((rows, D_MODEL), F32),
                   jax.ShapeDtypeStruct((rows, D_MODEL), BF16)],
        compiler_params=_cparams(1), name="outproj",
    )(yab, yc, yd, h, mods, mods, mods, g, *ws)


TOPK_TILE = 256


SUBLANES = 8


def _cmpx(lst, i, j):
    a, b = lst[i], lst[j]
    lst[i] = jnp.maximum(a, b)
    lst[j] = jnp.minimum(a, b)


def _bitonic_sort_desc(lst):
    n = len(lst)
    k = 2
    while k <= n:
        j = k // 2
        while j >= 1:
            for i in range(n):
                p = i ^ j
                if p > i:
                    if (i & k) == 0:
                        _cmpx(lst, i, p)
                    else:
                        _cmpx(lst, p, i)
            j //= 2
        k *= 2


def _bitonic_merge_desc(lst):
    n = len(lst)
    j = n // 2
    while j >= 1:
        for i in range(n):
            p = i ^ j
            if p > i:
                _cmpx(lst, i, p)
        j //= 2


def _merge_top(a, b):
    n = len(a)
    c = [jnp.maximum(a[i], b[n - 1 - i]) for i in range(n)]
    _bitonic_merge_desc(c)
    return c


def _merge_sublanes(lst):
    for shift in (4, 2, 1):
        lst = _merge_top(lst, [pltpu.roll(a, shift, axis=0) for a in lst])
    return lst


def _count_leading(pred, t):
    sel = jnp.where
    c1 = pred(t[7])
    c2 = pred(sel(c1, t[11], t[3]))
    c3 = pred(sel(c1, sel(c2, t[13], t[9]), sel(c2, t[5], t[1])))
    c4 = pred(sel(c1, sel(c2, sel(c3, t[14], t[12]), sel(c3, t[10], t[8])),
                  sel(c2, sel(c3, t[6], t[4]), sel(c3, t[2], t[0]))))
    c5 = pred(t[15])
    return (sel(c1, 8.0, 0.0) + sel(c2, 4.0, 0.0) + sel(c3, 2.0, 0.0) + sel(c4, 1.0, 0.0)
            + sel(c5, 1.0, 0.0))


def _top16_rows(s):
    lst = [s[SUBLANES * v:SUBLANES * (v + 1), :] for v in range(s.shape[0] // SUBLANES)]
    _bitonic_sort_desc(lst)
    return _merge_sublanes(lst)


def _topk_body(f_ref, wq_ref, k1_ref, k2_ref, e1_o, n_o, r2_o, e2_o, q_s):
    q_s[...] = lax.dot_general(wq_ref[...], f_ref[...], (NT, ((), ())),
                               preferred_element_type=F32).astype(BF16)
    half = PEER_DK // 2
    T = f_ref.shape[0]
    sub = _iota((SUBLANES, T), 0)

    def stack(rows):
        out = rows[SUBLANES - 1]
        for b in range(SUBLANES - 2, -1, -1):
            out = jnp.where(sub == b, rows[b], out)
        return out

    def rep(a):
        return jnp.concatenate([a] * (N_KEYS // SUBLANES), axis=0)

    def head(h, carry):
        base = pl.multiple_of(h * PEER_DK, PEER_DK)
        s1 = jnp.dot(k1_ref[...], q_s[pl.ds(base, half), :], preferred_element_type=F32)
        s2 = jnp.dot(k2_ref[...], q_s[pl.ds(base + half, half), :], preferred_element_type=F32)
        t1 = _top16_rows(s1)
        t2 = _top16_rows(s2)
        lo = stack(t2[:SUBLANES])
        hi = stack(t2[SUBLANES:])
        top = _merge_sublanes(_merge_top([t + lo for t in t1], [t + hi for t in t1]))
        z = jnp.exp(top[0] - top[0])
        for kk in range(1, PEER_TOPK):
            z = z + jnp.exp(top[kk] - top[0])
        thr = rep(top[PEER_TOPK - 1])
        t2r = [rep(t) for t in t2]
        r2 = _count_leading(lambda t: t > s2, t2r)
        n1 = _count_leading(lambda t: s1 + t >= thr, t2r)
        r2_o[h] = _pack_rows(r2)
        n_o[h] = _dup_bf16(n1)
        e1_o[h] = _dup_bf16(jnp.exp(s1 - rep(t1[0])) / rep(z))
        e2_o[h] = _pack_rows(jnp.exp(s2 - rep(t2[0])))
        return carry

    lax.fori_loop(0, PEER_HEADS, head, 0)


def _topk(f, wq_t, k1, k2):
    rows = f.shape[0]
    T = TOPK_TILE
    full = lambda a: pl.BlockSpec(a.shape, lambda i: (0,) * a.ndim)
    big = pl.BlockSpec((PEER_HEADS, N_KEYS, T), lambda i: (0, 0, i))
    big_shape = lambda dt: jax.ShapeDtypeStruct((PEER_HEADS, N_KEYS, rows), dt)
    packed = pl.BlockSpec((PEER_HEADS, N_KEYS // 2, T), lambda i: (0, 0, i))
    packed_shape = jax.ShapeDtypeStruct((PEER_HEADS, N_KEYS // 2, rows), U32)
    return pl.pallas_call(
        _topk_body, grid=(rows // T,),
        in_specs=[pl.BlockSpec((T, D_MODEL), lambda i: (i, 0)), full(wq_t), full(k1), full(k2)],
        out_specs=[big, big, packed, packed],
        out_shape=[big_shape(U32), big_shape(U32), packed_shape, packed_shape],
        scratch_shapes=[pltpu.VMEM((PEER_HEADS * PEER_DK, T), BF16)],
        compiler_params=_cparams(1), name="peer_topk",
    )(f, wq_t, k1, k2)


def _peer_body(final, n_e, f_ref, u_ref, v_ref, e1_ref, n_ref, r2_ref, e2_ref, h_ref,
               gate_ref, fg_ref, out, st_s, at_s, acc):
    s = pl.program_id(0)
    j_up = jnp.maximum(s - 1, 0) % n_e
    cur = s % 2

    @pl.when(s == 0)
    def _():
        at_s[...] = jnp.zeros_like(at_s)

    @pl.when(j_up == 0)
    def _():
        acc[...] = jnp.zeros_like(acc)

    st_s[...] = lax.dot_general(u_ref[...], f_ref[...], (NT, ((), ())),
                                preferred_element_type=F32)
    acc[...] += lax.dot_general(_unpack_rows(at_s[1 - cur]), v_ref[...], (TN, ((), ())),
                                preferred_element_type=F32)

    blk = (N_KEYS // 2, LANES)
    for lg in range(TOK_TILE // LANES):
        ln = slice(lg * LANES, (lg + 1) * LANES)
        for ii in range(EXP_TILE // N_KEYS):
            rows = slice(ii * N_KEYS, (ii + 1) * N_KEYS)
            g = None
            for h in range(PEER_HEADS):
                n_row = _unpack_rows(jnp.broadcast_to(n_ref[h, ii:ii + 1, ln], blk))
                e1_row = _unpack_rows(jnp.broadcast_to(e1_ref[h, ii:ii + 1, ln], blk))
                w = jnp.where(_unpack_rows(r2_ref[h, :, ln]) < n_row,
                              _unpack_rows(e2_ref[h, :, ln]), 0.0) * e1_row
                g = w if g is None else g + w
            at_s[cur, ii * N_KEYS // 2:(ii + 1) * N_KEYS // 2, ln] = _pack_rows(
                _gelu_tanh(st_s[rows, ln]).astype(BF16) * g)

    @pl.when(jnp.logical_and(s >= 1, j_up == n_e - 1))
    def _():
        h = h_ref[...] + gate_ref[...] * acc[...]
        if final:
            h = h * lax.rsqrt(jnp.mean(h * h, axis=-1, keepdims=True) + EPS) * fg_ref[...]
        out[...] = h


def _peer(f, u_b, v_b, e1, n1, r2, e2, h, mods, fg, n_lat, per_seq, n_tok_tiles, final):
    n_e = N_EXPERTS // EXP_TILE
    rpt = EXP_TILE // N_KEYS
    n_steps = n_tok_tiles * n_e

    def tile(s, lag):
        t = jnp.clip(s - lag, 0, n_steps - 1)
        return t // n_e, t % n_e

    tok = lambda lag: pl.BlockSpec((TOK_TILE, D_MODEL), lambda s: (tile(s, lag)[0], 0))
    exp = lambda lag: pl.BlockSpec((EXP_TILE, D_MODEL), lambda s: (tile(s, lag)[1], 0))
    rowblk = pl.BlockSpec((PEER_HEADS, rpt, TOK_TILE), lambda s: (0, tile(s, 0)[1], tile(s, 0)[0]))
    allkeys = pl.BlockSpec((PEER_HEADS, N_KEYS // 2, TOK_TILE), lambda s: (0, 0, tile(s, 0)[0]))

    def gate_map(s):
        i = tile(s, 1)[0]
        row = jnp.where(i < n_lat, i // per_seq, 2)
        return (row * 6 + 5, 0, 0)

    return pl.pallas_call(
        functools.partial(_peer_body, final, n_e),
        grid=(n_steps + 1,),
        in_specs=[tok(0), exp(0), exp(1), rowblk, rowblk, allkeys, allkeys, tok(1),
                  pl.BlockSpec((None, 1, D_MODEL), gate_map),
                  pl.BlockSpec((1, D_MODEL), lambda s: (0, 0))],
        out_specs=tok(1),
        out_shape=jax.ShapeDtypeStruct((n_tok_tiles * TOK_TILE, D_MODEL), F32),
        scratch_shapes=[pltpu.VMEM((EXP_TILE, TOK_TILE), F32),
                        pltpu.VMEM((2, EXP_TILE // 2, TOK_TILE), U32),
                        pltpu.VMEM((TOK_TILE, D_MODEL), F32)],
        compiler_params=_cparams(1), name="peer",
    )(f, u_b, v_b, e1, n1, r2, e2, h, mods, fg)


def _sincos_2d(t_len):
    rows = t_len // GRID_W
    row = jnp.repeat(jnp.arange(rows), GRID_W).astype(F32)
    col = jnp.tile(jnp.arange(GRID_W), rows).astype(F32)
    q = D_MODEL // 4
    freq = 10000.0 ** (-jnp.arange(q, dtype=F32) / q)
    ar = row[:, None] * freq
    ac = col[:, None] * freq
    return jnp.concatenate([jnp.sin(ar), jnp.cos(ar), jnp.sin(ac), jnp.cos(ac)], -1)


def _row(a):
    return a.reshape(1, -1).astype(F32)


def _pad_lanes(a, width=LANES):
    return jnp.pad(a, ((0, 0), (0, width - a.shape[-1])))


def kernel(x, c, ctx, c_ctx, ada_w, ada_b, norm_mix_g, norm_ffn_g, w_in, w_out, a_conv_w, a_conv_b, a_ln_g, a_ln_b, b_ln_g, b_ln_b, b_ws, b_bs, c_conv_w, c_conv_b, c_dt_bias, c_a_log, c_d, c_norm_g, d_mu_prev, d_mu_next, d_w0, d_w2, d_a0, d_a2, d_g2, d_k_k, d_k_a, d_r_k, d_gn_g, d_gn_b, peer_wq, peer_k1, peer_k2, peer_u, peer_v, final_g):
    n_b, t_lat, _ = x.shape
    t_ctx = ctx.shape[1]
    assert n_b == 2 and t_ctx == SEQ_TILE and t_lat % TOK_TILE == 0
    n_l = t_lat // SEQ_TILE
    per_seq = t_lat // TOK_TILE
    n_lat = n_b * per_seq
    n_lat256 = n_b * n_l

    cs = jnp.zeros((8, D_MODEL), F32).at[:n_b].set(c).at[n_b].set(c_ctx)
    mods_all = _ada(cs, ada_w, ada_b)
    pos = _sincos_2d(t_lat)
    x2 = x.reshape(n_b * t_lat, D_MODEL)
    ctx2 = ctx.reshape(n_b * t_ctx, D_MODEL)

    h = None
    for i in range(DEPTH):
        last_layer = i == DEPTH - 1
        mods = mods_all[i].reshape(8 * 6, 1, D_MODEL)
        w = w_in[i]
        dtc = OFF_C + D_GROUP + C_XBC
        w_c = jnp.concatenate([w[:, OFF_C:dtc], _pad_lanes(w[:, dtc:dtc + HEADS_C]),
                               _pad_lanes(w[:, dtc + HEADS_C:dtc + 2 * HEADS_C])], axis=1)
        ws_in = tuple(a.astype(BF16) for a in (w[:, :OFF_B], w[:, OFF_B:OFF_C], w_c, w[:, OFF_D:]))
        g_mix = _row(norm_mix_g[i])
        if i == 0:
            h, pa, pb, pc, pd = _inproj((x2, ctx2, pos), mods, g_mix, ws_in, n_lat, per_seq, True)
        else:
            pa, pb, pc, pd = _inproj(h, mods, g_mix, ws_in, n_lat, per_seq, False)

        conv_a = jnp.pad(a_conv_w[i], ((0, 32 - CONV_A), (0, 0)))
        bsm = jnp.repeat(b_bs[i].T, D_GROUP // HEADS_B, axis=1)
        yab = _mixab(pa, pb, (conv_a, _row(a_conv_b[i]), _row(a_ln_g[i]), _row(a_ln_b[i]),
                              _row(b_ln_g[i]), _row(b_ln_b[i]), b_ws[i].astype(BF16), bsm),
                     n_lat256, n_l)

        conv_c = jnp.pad(c_conv_w[i], ((0, 8 - CONV_C), (0, 0)))
        dtb = _pad_lanes(c_dt_bias[i]).reshape(2, 1, LANES)
        alog = _pad_lanes(c_a_log[i]).reshape(2, 1, LANES)
        dsk = _row(jnp.repeat(c_d[i], HEAD_DIM_C))
        yc = _mamba(pc, (conv_c, _row(c_conv_b[i]), dtb, alog, dsk, _row(c_norm_g[i])), n_b, n_l)

        yd = _rwkv(pd, (_row(d_mu_prev[i]), _row(d_mu_next[i]), d_w0[i].reshape(2, 1, D_GROUP),
                        d_w2[i].astype(BF16), d_a0[i].reshape(2, 1, D_GROUP), d_a2[i].astype(BF16),
                        d_g2[i].astype(BF16), _row(d_k_k[i]), _row(d_k_a[i]), _row(d_r_k[i]),
                        _row(d_gn_g[i]), _row(d_gn_b[i])), n_b, n_l)

        wo = w_out[i].astype(BF16)
        h, f = _outproj(yab, yc, yd, h, mods, _row(norm_ffn_g[i]),
                        (wo[:512], wo[512:768], wo[768:]), n_lat, per_seq)

        e1, n1, r2, e2 = _topk(f, peer_wq[i].T.astype(BF16), peer_k1[i].astype(BF16),
                               peer_k2[i].astype(BF16))
        n_tok = n_lat if last_layer else n_lat + 1
        h = _peer(f, peer_u[i].astype(BF16), peer_v[i].astype(BF16), e1, n1, r2, e2, h,
                  mods, _row(final_g), n_lat, per_seq, n_tok, last_layer)
    return h.reshape(n_b, t_lat, D_MODEL)
```

```python
import functools
import math

import jax
import jax.numpy as jnp
from jax import lax
from jax.experimental import pallas as pl
from jax.experimental.pallas import tpu as pltpu

F32 = jnp.float32
BF16 = jnp.bfloat16
U32 = jnp.uint32
HIGHEST = lax.Precision.HIGHEST

D_MODEL = 1024
DEPTH = 2
GRID_W = 64
EPS = 1e-6
D_GROUP = 256
CONV_A = 31
CHUNK_B = 128
HEADS_B = 4
HEADS_C = 4
HEAD_DIM_C = 64
STATE_C = 128
CONV_C = 5
CHUNK_C = 128
HEADS_D = 4
HEAD_DIM_D = 64
LORA_W = 64
LORA_A = 64
LORA_G = 128
GN_EPS_D = 64e-5
N_KEYS = 128
N_EXPERTS = N_KEYS * N_KEYS
PEER_HEADS = 8
PEER_DK = 256
PEER_TOPK = 16

C_XBC = D_GROUP + 2 * 2 * STATE_C
OFF_B = 2 * D_GROUP
OFF_C = OFF_B + 2 * D_GROUP
OFF_D = OFF_C + D_GROUP + C_XBC + 2 * HEADS_C
D_COLS = 3 * D_GROUP + 2 * LORA_W + 2 * LORA_A + LORA_G
PC_COLS = D_GROUP + C_XBC + 2 * 128

LANES = 128
SEQ_TILE = 256
TOK_TILE = 512
CHUNK_D = 64
EXP_TILE = 1024
VMEM_LIMIT = 56 * 1024 * 1024


def _cparams(n_axes):
    return pltpu.CompilerParams(dimension_semantics=("arbitrary",) * n_axes,
                                vmem_limit_bytes=VMEM_LIMIT)


def _silu(x):
    return x * jax.nn.sigmoid(x)


def _dot(a, b, dims=None):
    a = a.astype(BF16)
    b = b.astype(BF16)
    if dims is None:
        return jnp.dot(a, b, preferred_element_type=F32)
    return lax.dot_general(a, b, (dims, ((), ())), preferred_element_type=F32)


def _dot_hi(a, b):
    return jnp.dot(a, b, precision=HIGHEST, preferred_element_type=F32)


NT = ((1,), (1,))
TN = ((0,), (0,))


def _gelu_tanh(x):
    k1 = math.sqrt(2.0 / math.pi)
    k3 = 0.044715 * k1
    hx = 0.5 * x
    return hx + hx * jnp.tanh(x * (k1 + k3 * (x * x)))


def _pack_rows(x):
    return pltpu.bitcast(x.astype(BF16), U32)


def _unpack_rows(x):
    return pltpu.bitcast(x, BF16)


def _dup_bf16(x):
    hi = pltpu.bitcast(x.astype(BF16).astype(F32), U32)
    return hi | (hi >> 16)


def _iota(shape, axis):
    return lax.broadcasted_iota(jnp.int32, shape, axis)


def _head_block_ones(n, width):
    return (_iota((n, n), 0) // width == _iota((n, n), 1) // width).astype(F32)


def _ada_body(cs_ref, w_ref, b_ref, o_ref):
    o_ref[...] = _dot_hi(_silu(cs_ref[...]), w_ref[...]) + b_ref[...]


def _ada(cs, ada_w, ada_b):
    L = ada_w.shape[0]
    nb = 1536
    return pl.pallas_call(
        _ada_body,
        grid=(L, 6 * D_MODEL // nb),
        in_specs=[pl.BlockSpec((8, D_MODEL), lambda l, n: (0, 0)),
                  pl.BlockSpec((None, D_MODEL, nb), lambda l, n: (l, 0, n)),
                  pl.BlockSpec((None, 1, nb), lambda l, n: (l, 0, n))],
        out_specs=pl.BlockSpec((None, 8, nb), lambda l, n: (l, 0, n)),
        out_shape=jax.ShapeDtypeStruct((L, 8, 6 * D_MODEL), F32),
        compiler_params=_cparams(2),
        name="ada",
    )(cs, ada_w, ada_b.reshape(L, 1, 6 * D_MODEL))


def _norm_mod(h, g, shift, scale):
    xn = h * lax.rsqrt(jnp.mean(h * h, axis=-1, keepdims=True) + EPS) * g
    return xn * (1.0 + scale) + shift


def _inproj_tail(h, sh_ref, sc_ref, g_ref, wa, wb, wc, wd, oa, ob, oc, od):
    xm = _norm_mod(h, g_ref[...], sh_ref[...], sc_ref[...]).astype(BF16)
    oa[...] = jnp.dot(xm, wa[...], preferred_element_type=F32)
    ob[...] = jnp.dot(xm, wb[...], preferred_element_type=F32)
    oc[...] = jnp.dot(xm, wc[...], preferred_element_type=F32)
    od[...] = jnp.dot(xm, wd[...], preferred_element_type=F32)


def _inproj_first_body(n_lat, x_ref, ctx_ref, pos_ref, sh_ref, sc_ref, g_ref, wa, wb, wc, wd,
                       oh, oa, ob, oc, od):
    i = pl.program_id(0)
    h = jnp.where(i < n_lat, x_ref[...] + pos_ref[...], ctx_ref[...])
    oh[...] = h
    _inproj_tail(h, sh_ref, sc_ref, g_ref, wa, wb, wc, wd, oa, ob, oc, od)


def _inproj_body(h_ref, sh_ref, sc_ref, g_ref, wa, wb, wc, wd, oa, ob, oc, od):
    _inproj_tail(h_ref[...], sh_ref, sc_ref, g_ref, wa, wb, wc, wd, oa, ob, oc, od)


def _mod_spec(n_lat, per_seq, k):
    def imap(i):
        row = jnp.where(i < n_lat, i // per_seq, 2)
        return (row * 6 + k, 0, 0)
    return pl.BlockSpec((None, 1, D_MODEL), imap)


def _inproj(h_or_parts, mods, g, ws, n_lat, per_seq, first):
    wa, wb, wc, wd = ws
    n_tiles = n_lat + 1
    rows = n_tiles * TOK_TILE
    tile = lambda c: pl.BlockSpec((TOK_TILE, c), lambda i: (i, 0))
    full = lambda a: pl.BlockSpec(a.shape, lambda i: (0,) * a.ndim)
    common_specs = [_mod_spec(n_lat, per_seq, 0), _mod_spec(n_lat, per_seq, 1), full(g),
                    full(wa), full(wb), full(wc), full(wd)]
    out_specs = [tile(512), tile(512), tile(PC_COLS), tile(D_COLS)]
    out_shape = [jax.ShapeDtypeStruct((rows, c), F32) for c in (512, 512, PC_COLS, D_COLS)]
    if first:
        x2, ctx2, pos = h_or_parts
        in_specs = [pl.BlockSpec((TOK_TILE, D_MODEL), lambda i: (jnp.minimum(i, n_lat - 1), 0)),
                    pl.BlockSpec((TOK_TILE, D_MODEL), lambda i: (0, 0)),
                    pl.BlockSpec((TOK_TILE, D_MODEL), lambda i: (i % per_seq, 0))] + common_specs
        return pl.pallas_call(
            functools.partial(_inproj_first_body, n_lat),
            grid=(n_tiles,), in_specs=in_specs,
            out_specs=[tile(D_MODEL)] + out_specs,
            out_shape=[jax.ShapeDtypeStruct((rows, D_MODEL), F32)] + out_shape,
            compiler_params=_cparams(1), name="inproj_first",
        )(x2, ctx2, pos, mods, mods, g, wa, wb, wc, wd)
    return pl.pallas_call(
        _inproj_body, grid=(n_tiles,), in_specs=[tile(D_MODEL)] + common_specs,
        out_specs=out_specs, out_shape=out_shape,
        compiler_params=_cparams(1), name="inproj",
    )(h_or_parts, mods, mods, g, wa, wb, wc, wd)


def _layernorm(x, g, b, eps=1e-5):
    mu = jnp.mean(x, axis=-1, keepdims=True)
    xc = x - mu
    var = jnp.mean(xc * xc, axis=-1, keepdims=True)
    return xc * lax.rsqrt(var + eps) * g + b


def _mixab_body(n_lat, n_l, pa, pa_prev, pa_next, pb, cw, cb, alg, alb, blg, blb, ws, bsm,
                out, ext):
    i = pl.program_id(0)
    is_ctx = i >= n_lat
    tpos = i % n_l
    first = jnp.logical_or(is_ctx, tpos == 0)
    last = jnp.logical_or(is_ctx, tpos == n_l - 1)

    def glu(x):
        return x[:, :D_GROUP] * jax.nn.sigmoid(x[:, D_GROUP:])

    halo = 16
    ext[0:halo, :] = jnp.where(first, 0.0, glu(pa_prev[...]))
    ext[halo:halo + SEQ_TILE, :] = glu(pa[...])
    ext[halo + SEQ_TILE:, :] = jnp.where(last, 0.0, glu(pa_next[...]))
    acc = jnp.zeros((SEQ_TILE, D_GROUP), F32) + cb[...]
    for j in range(CONV_A):
        acc = acc + cw[j:j + 1, :] * ext[pl.ds(halo - (CONV_A - 1) // 2 + j, SEQ_TILE), :]
    out[:, :D_GROUP] = _silu(_layernorm(acc, alg[...], alb[...]))

    x = pb[...]
    u = x[:, :D_GROUP]
    v = _layernorm(x[:, D_GROUP:], blg[...], blb[...]).astype(BF16)
    hd = D_GROUP // HEADS_B
    rows = []
    for c in range(SEQ_TILE // CHUNK_B):
        vc = v[c * CHUNK_B:(c + 1) * CHUNK_B, :]
        heads = [jnp.dot(ws[hh], vc[:, hh * hd:(hh + 1) * hd], preferred_element_type=F32)
                 for hh in range(HEADS_B)]
        rows.append(jnp.concatenate(heads, axis=1) + bsm[...])
    out[:, D_GROUP:] = u * jnp.concatenate(rows, axis=0)


def _mixab(pa, pb, prm, n_lat, n_l):
    rows = pa.shape[0]
    n_tiles = rows // SEQ_TILE
    halo = 16
    per = SEQ_TILE // halo
    full = lambda a: pl.BlockSpec(a.shape, lambda i: (0,) * a.ndim)
    in_specs = [pl.BlockSpec((SEQ_TILE, 512), lambda i: (i, 0)),
                pl.BlockSpec((halo, 512), lambda i: (jnp.maximum(i * per - 1, 0), 0)),
                pl.BlockSpec((halo, 512), lambda i: (jnp.minimum((i + 1) * per, rows // halo - 1), 0)),
                pl.BlockSpec((SEQ_TILE, 512), lambda i: (i, 0))] + [full(a) for a in prm]
    return pl.pallas_call(
        functools.partial(_mixab_body, n_lat, n_l),
        grid=(n_tiles,), in_specs=in_specs,
        out_specs=pl.BlockSpec((SEQ_TILE, 512), lambda i: (i, 0)),
        out_shape=jax.ShapeDtypeStruct((rows, 512), F32),
        scratch_shapes=[pltpu.VMEM((SEQ_TILE + 2 * halo, D_GROUP), F32)],
        compiler_params=_cparams(1), name="mixab",
    )(pa, pa, pa, pb, *prm)


def _scan_tile(n_b, n_l, b, ph, j):
    lat = b * n_l + jnp.where(ph == 0, j - 1, n_l - j)
    return jnp.where(j == 0, n_b * n_l + b, lat)


def _scan_specs(n_b, n_l, rows, cols):
    tid = functools.partial(_scan_tile, n_b, n_l)
    per = SEQ_TILE // 8
    cur = pl.BlockSpec((SEQ_TILE, cols), lambda b, ph, j: (tid(b, ph, j), 0))
    prev = pl.BlockSpec((8, cols), lambda b, ph, j: (jnp.maximum(tid(b, ph, j) * per - 1, 0), 0))
    nxt = pl.BlockSpec((8, cols),
                       lambda b, ph, j: (jnp.minimum((tid(b, ph, j) + 1) * per, rows // 8 - 1), 0))
    return cur, prev, nxt


def _scan_out_spec(n_b, n_l, cols):
    def imap(b, ph, j):
        return (jnp.where(ph == 0, n_b * n_l + b, _scan_tile(n_b, n_l, b, 1, j)), 0)
    return pl.BlockSpec((SEQ_TILE, cols), imap)


def _scan_flags(n_l):
    ph = pl.program_id(1)
    j = pl.program_id(2)
    is_ctx = j == 0
    tpos = jnp.where(ph == 0, j - 1, n_l - j)
    first = jnp.logical_or(is_ctx, tpos == 0)
    last = jnp.logical_or(is_ctx, tpos == n_l - 1)
    slot = jnp.where(is_ctx, 0, tpos + 1)
    return ph, j, first, last, slot


def _mamba_body(n_l, cur_ref, prev_ref, next_ref, dt_ref, cw, cb, dtb, alog, dsk, ng,
                out, ext, yf, st, xbc_s, a_s, dt_s, y_s):
    ph, j, first, last, slot = _scan_flags(n_l)
    fwd = ph == 0

    @pl.when(j == 0)
    def _():
        st[...] = jnp.zeros_like(st)

    zx = D_GROUP
    ext[0:8, :] = jnp.where(first, 0.0, prev_ref[:, zx:])
    ext[8:8 + SEQ_TILE, :] = cur_ref[:, zx:]
    ext[8 + SEQ_TILE:, :] = jnp.where(last, 0.0, next_ref[:, zx:])
    acc = jnp.zeros((SEQ_TILE, C_XBC), F32) + cb[...]
    for jj in range(CONV_C):
        acc = acc + cw[jj:jj + 1, :] * ext[pl.ds(8 - (CONV_C - 1) // 2 + jj, SEQ_TILE), :]
    xbc_s[...] = _silu(acc)
    dt = jax.nn.softplus(dt_ref[...] + dtb[ph])
    dt_s[...] = dt
    a_s[...] = dt * (-jnp.exp(alog[ph]))

    L = CHUNK_C
    r_i = _iota((L, L), 0)
    c_i = _iota((L, L), 1)
    ltri = (r_i >= c_i).astype(F32)
    mask = jnp.where(fwd, r_i - c_i, c_i - r_i) >= 0
    sgn = jnp.where(fwd, 1.0, -1.0)
    hd = HEAD_DIM_C

    def chunk(it, carry):
        q = jnp.where(fwd, it, SEQ_TILE // L - 1 - it)
        rows = pl.ds(pl.multiple_of(q * L, L), L)
        a_q = a_s[rows, :]
        cs = _dot_hi(ltri, a_q)
        tot = cs[L - 1:L, :]
        e = jnp.where(fwd, cs, cs - a_q)
        e_t = e.T
        dt_q = dt_s[rows, :]
        xbc = xbc_s[rows, :]
        ys = []
        for g in range(2):
            bm = xbc[:, D_GROUP + g * STATE_C:D_GROUP + (g + 1) * STATE_C]
            cm = xbc[:, D_GROUP + 2 * STATE_C + g * STATE_C:D_GROUP + 2 * STATE_C + (g + 1) * STATE_C]
            bm_t = bm.T.astype(BF16)
            cm_b = cm.astype(BF16)
            gmat = jnp.dot(cm_b, bm_t, preferred_element_type=F32)
            for h in (2 * g, 2 * g + 1):
                ecol = e[:, h:h + 1]
                erow = e_t[h:h + 1, :]
                totc = tot[:, h:h + 1]
                xdt = xbc[:, h * hd:(h + 1) * hd] * dt_q[:, h:h + 1]
                lmat = jnp.exp(jnp.where(mask, sgn * (ecol - erow), -1e30))
                offs = jnp.exp(jnp.where(fwd, ecol, totc - ecol))
                stw = jnp.exp(jnp.where(fwd, totc - ecol, ecol))
                s_prev = st[h]
                y = _dot(gmat * lmat, xdt) + offs * _dot(cm_b, s_prev)
                st[h] = jnp.exp(totc) * s_prev + _dot(bm_t, xdt * stw)
                ys.append(y)
        y_s[rows, :] = jnp.concatenate(ys, axis=1)
        return carry

    lax.fori_loop(0, SEQ_TILE // L, chunk, 0)

    @pl.when(fwd)
    def _():
        yf[slot] = y_s[...]

    @pl.when(ph == 1)
    def _():
        y = yf[slot] + y_s[...] + dsk[...] * xbc_s[:, :D_GROUP]
        t = y * _silu(cur_ref[:, :D_GROUP])
        out[...] = t * lax.rsqrt(jnp.mean(t * t, axis=-1, keepdims=True) + EPS) * ng[...]


def _mamba(pc, prm, n_b, n_l):
    rows = pc.shape[0]
    cur, prev, nxt = _scan_specs(n_b, n_l, rows, D_GROUP + C_XBC)
    tid = functools.partial(_scan_tile, n_b, n_l)
    dt_spec = pl.BlockSpec((SEQ_TILE, LANES),
                           lambda b, ph, j: (tid(b, ph, j), (D_GROUP + C_XBC) // LANES + ph))
    full = lambda a: pl.BlockSpec(a.shape, lambda b, ph, j: (0,) * a.ndim)
    return pl.pallas_call(
        functools.partial(_mamba_body, n_l),
        grid=(n_b, 2, n_l + 1),
        in_specs=[cur, prev, nxt, dt_spec] + [full(a) for a in prm],
        out_specs=_scan_out_spec(n_b, n_l, D_GROUP),
        out_shape=jax.ShapeDtypeStruct((rows, D_GROUP), F32),
        scratch_shapes=[pltpu.VMEM((SEQ_TILE + 16, C_XBC), F32),
                        pltpu.VMEM((n_l + 1, SEQ_TILE, D_GROUP), F32),
                        pltpu.VMEM((HEADS_C, STATE_C, HEAD_DIM_C), F32),
                        pltpu.VMEM((SEQ_TILE, C_XBC), F32),
                        pltpu.VMEM((SEQ_TILE, LANES), F32),
                        pltpu.VMEM((SEQ_TILE, LANES), F32),
                        pltpu.VMEM((SEQ_TILE, D_GROUP), F32)],
        compiler_params=_cparams(3), name="mamba",
    )(pc, pc, pc, pc, *prm)


def _rwkv_body(n_l, cur_ref, prev_ref, next_ref, mup, mun, w0, w2, a0, a2, g2, kkw, kaw, rkw,
               gng, gnb, out, yf, st, r_s, v_s, kap_s, alp_s, kd_s, lw_s, y_s,
               phi_s, psi_s, sin_s, left_s, amat_s, ara_s, tinv_s):
    ph, j, first, last, slot = _scan_flags(n_l)
    fwd = ph == 0

    @pl.when(j == 0)
    def _():
        st[...] = jnp.zeros_like(st)

    cur = cur_ref[...]
    prv = jnp.concatenate([jnp.where(first, 0.0, prev_ref[7:8, :]), cur[:SEQ_TILE - 1, :]], axis=0)
    nxt = jnp.concatenate([cur[1:, :], jnp.where(last, 0.0, next_ref[0:1, :])], axis=0)
    p = cur + mup[...] * (prv - cur) + mun[...] * (nxt - cur)
    G = D_GROUP
    r = p[:, :G]
    k = p[:, G:2 * G]
    v = p[:, 2 * G:3 * G]
    blk = _head_block_ones(G, HEAD_DIM_D)
    kkr = k * kkw[...]
    kk = kkr * lax.rsqrt(_dot_hi(kkr * kkr, blk) + 1e-12)

    def rate(d_static=None):
        if d_static is None:
            ad = jnp.where(fwd, p[:, 3 * G + 2 * LORA_W:3 * G + 2 * LORA_W + LORA_A],
                           p[:, 3 * G + 2 * LORA_W + LORA_A:3 * G + 2 * LORA_W + 2 * LORA_A])
            a = jax.nn.sigmoid(a0[ph] + _dot(ad, a2[ph]))
        else:
            o = 3 * G + 2 * LORA_W + d_static * LORA_A
            a = jax.nn.sigmoid(a0[d_static] + _dot(p[:, o:o + LORA_A], a2[d_static]))
        return a, k * (1.0 + (a - 1.0) * kaw[...])

    a_d, kd_d = rate()
    wd = jnp.where(fwd, p[:, 3 * G:3 * G + LORA_W], p[:, 3 * G + LORA_W:3 * G + 2 * LORA_W])
    w = w0[ph] + _dot(jnp.tanh(wd), w2[ph])
    lw_s[...] = -math.exp(-0.5) * jax.nn.sigmoid(w)
    r_s[...] = r
    v_s[...] = v
    kap_s[...] = kk
    alp_s[...] = a_d * kk
    kd_s[...] = kd_d

    C = CHUNK_D
    r_i = _iota((C, C), 0)
    c_i = _iota((C, C), 1)
    lag = jnp.where(fwd, r_i - c_i, c_i - r_i)
    tri = lag >= 0
    tri_f = tri.astype(F32)
    strict = lag > 0
    eye = (r_i == c_i).astype(F32)
    hd = HEAD_DIM_D

    n_q = SEQ_TILE // C
    eye_k = (_iota((hd, hd), 0) == _iota((hd, hd), 1)).astype(F32)

    units = [(q, h) for q in range(n_q) for h in range(HEADS_D)]
    rows_of = lambda q: slice(q * C, (q + 1) * C)
    lanes_of = lambda h: slice(h * hd, (h + 1) * hd)
    left, right, p_tot = {}, {}, {}
    for q in range(n_q):
        rows = rows_of(q)
        lw = lw_s[rows, :]
        incl = _dot_hi(tri_f, lw)
        tot = jnp.where(fwd, incl[C - 1:C, :], incl[0:1, :])
        p_inv = jnp.exp(-incl)
        kap_h = kap_s[rows, :] * jnp.exp(incl - lw)
        r_h = r_s[rows, :] * jnp.exp(incl)
        alp_b = alp_s[rows, :] * p_inv
        k_b = kd_s[rows, :] * p_inv
        ptq = jnp.exp(tot)
        for h in range(HEADS_D):
            sl = lanes_of(h)
            left[q, h] = jnp.concatenate([kap_h[:, sl], r_h[:, sl]], axis=0)
            right[q, h] = jnp.concatenate([alp_b[:, sl], k_b[:, sl]], axis=0)
            p_tot[q, h] = ptq[:, sl]
            left_s[q * HEADS_D + h] = left[q, h]
    m1 = {u: _dot(left[u], right[u], NT) for u in units}
    a_vk, pw, tinv = {}, {}, {}
    for u in units:
        qh = u[0] * HEADS_D + u[1]
        a_vk[u] = jnp.where(strict, m1[u][:C, C:], 0.0)
        amat_s[qh] = jnp.concatenate([a_vk[u], jnp.where(tri, m1[u][C:, C:], 0.0)], axis=0)
        ara_s[qh] = jnp.where(tri, m1[u][C:, :C], 0.0)
        pw[u] = jnp.where(strict, -m1[u][:C, :C], 0.0)
        tinv[u] = eye + pw[u]
    for _ in range(5):
        pw = {u: _dot(pw[u], pw[u]) for u in units}
        upd = {u: _dot(tinv[u], pw[u]) for u in units}
        tinv = {u: tinv[u] + upd[u] for u in units}
    wmat = {u: _dot(tinv[u], right[u][:C], TN) for u in units}
    kw = {u: _dot(left[u][:C], wmat[u], TN) for u in units}
    aw = {u: _dot(a_vk[u], wmat[u], TN) for u in units}
    vk = {u: _dot(v_s[rows_of(u[0]), lanes_of(u[1])], right[u][C:] - aw[u], TN) for u in units}
    for u in units:
        qh = u[0] * HEADS_D + u[1]
        phi_s[qh] = (eye_k - kw[u]) * p_tot[u]
        psi_s[qh] = vk[u] * p_tot[u]
        tinv_s[qh] = tinv[u]

    for it in range(n_q):
        q = jnp.where(fwd, it, n_q - 1 - it)
        for h in range(HEADS_D):
            qh = q * HEADS_D + h
            s_in = st[h]
            sin_s[qh] = s_in
            st[h] = _dot(s_in, phi_s[qh]) + psi_s[qh]

    qh_of = lambda u: u[0] * HEADS_D + u[1]
    x0 = {u: _dot(left_s[qh_of(u)], sin_s[qh_of(u)], NT) for u in units}
    x1 = {u: _dot(amat_s[qh_of(u)], v_s[rows_of(u[0]), lanes_of(u[1])]) for u in units}
    uu = {u: _dot(tinv_s[qh_of(u)], x0[u][:C] + x1[u][:C]) for u in units}
    au = {u: _dot(ara_s[qh_of(u)], uu[u]) for u in units}
    for q in range(n_q):
        y_s[rows_of(q), :] = jnp.concatenate(
            [x0[q, h][C:] + x1[q, h][C:] - au[q, h] for h in range(HEADS_D)], axis=1)

    @pl.when(fwd)
    def _():
        yf[slot] = y_s[...]

    @pl.when(ph == 1)
    def _():
        y = yf[slot] + y_s[...]
        inv = 1.0 / HEAD_DIM_D
        mu = _dot_hi(y, blk) * inv
        yc = y - mu
        var = _dot_hi(yc * yc, blk) * inv
        yn = yc * lax.rsqrt(var + GN_EPS_D) * gng[...] + gnb[...]
        _, kd_f = rate(0)
        bonus = _dot_hi(r * (kd_f + kd_d) * rkw[...], blk) * v
        gate = _dot(jax.nn.sigmoid(p[:, 3 * G + 2 * LORA_W + 2 * LORA_A:]), g2[...])
        out[...] = (yn + bonus) * gate


def _rwkv(pd, prm, n_b, n_l):
    rows = pd.shape[0]
    cur, prev, nxt = _scan_specs(n_b, n_l, rows, D_COLS)
    full = lambda a: pl.BlockSpec(a.shape, lambda b, ph, j: (0,) * a.ndim)
    tile = lambda: pltpu.VMEM((SEQ_TILE, D_GROUP), F32)
    per_qh = lambda r, c: pltpu.VMEM((SEQ_TILE // CHUNK_D * HEADS_D, r, c), F32)
    return pl.pallas_call(
        functools.partial(_rwkv_body, n_l),
        grid=(n_b, 2, n_l + 1),
        in_specs=[cur, prev, nxt] + [full(a) for a in prm],
        out_specs=_scan_out_spec(n_b, n_l, D_GROUP),
        out_shape=jax.ShapeDtypeStruct((rows, D_GROUP), F32),
        scratch_shapes=[pltpu.VMEM((n_l + 1, SEQ_TILE, D_GROUP), F32),
                        pltpu.VMEM((HEADS_D, HEAD_DIM_D, HEAD_DIM_D), F32)] + [tile() for _ in range(7)]
        + [per_qh(HEAD_DIM_D, HEAD_DIM_D) for _ in range(3)]
        + [per_qh(2 * CHUNK_D, HEAD_DIM_D), per_qh(2 * CHUNK_D, CHUNK_D),
           per_qh(CHUNK_D, CHUNK_D), per_qh(CHUNK_D, CHUNK_D)],
        compiler_params=_cparams(3), name="rwkv",
    )(pd, pd, pd, *prm)


def _outproj_body(yab, yc, yd, h_ref, gate, sh, sc, g, wab, wc, wd, oh, of, oft):
    mix = (_dot(yab[...], wab[...]) + _dot(yc[...], wc[...]) + _dot(yd[...], wd[...]))
    h = h_ref[...] + gate[...] * mix
    oh[...] = h
    f = _norm_mod(h, g[...], sh[...], sc[...]).astype(BF16)
    of[...] = f
    oft[...] = f.T


def _outproj(yab, yc, yd, h, mods, g, ws, n_lat, per_seq):
    rows = h.shape[0]
    tile = lambda c: pl.BlockSpec((TOK_TILE, c), lambda i: (i, 0))
    full = lambda a: pl.BlockSpec(a.shape, lambda i: (0,) * a.ndim)
    return pl.pallas_call(
        _outproj_body, grid=(rows // TOK_TILE,),
        in_specs=[tile(512), tile(D_GROUP), tile(D_GROUP), tile(D_MODEL),
                  _mod_spec(n_lat, per_seq, 2), _mod_spec(n_lat, per_seq, 3),
                  _mod_spec(n_lat, per_seq, 4), full(g)] + [full(a) for a in ws],
        out_specs=[tile(D_MODEL), tile(D_MODEL),
                   pl.BlockSpec((D_MODEL, TOK_TILE), lambda i: (0, i))],
        out_shape=[jax.ShapeDtypeStruct((rows, D_MODEL), F32),
                   jax.ShapeDtypeStruct((rows, D_MODEL), BF16),
                   jax.ShapeDtypeStruct((D_MODEL, rows), BF16)],
        compiler_params=_cparams(1), name="outproj",
    )(yab, yc, yd, h, mods, mods, mods, g, *ws)


TOPK_TILE = 256


SUBLANES = 8


def _cmpx(lst, i, j):
    a, b = lst[i], lst[j]
    lst[i] = jnp.maximum(a, b)
    lst[j] = jnp.minimum(a, b)


def _bitonic_sort_desc(lst):
    n = len(lst)
    k = 2
    while k <= n:
        j = k // 2
        while j >= 1:
            for i in range(n):
                p = i ^ j
                if p > i:
                    if (i & k) == 0:
                        _cmpx(lst, i, p)
                    else:
                        _cmpx(lst, p, i)
            j //= 2
        k *= 2


def _bitonic_merge_desc(lst):
    n = len(lst)
    j = n // 2
    while j >= 1:
        for i in range(n):
            p = i ^ j
            if p > i:
                _cmpx(lst, i, p)
        j //= 2


def _merge_top(a, b):
    n = len(a)
    c = [jnp.maximum(a[i], b[n - 1 - i]) for i in range(n)]
    _bitonic_merge_desc(c)
    return c


def _merge_sublanes(lst):
    for shift in (4, 2, 1):
        lst = _merge_top(lst, [pltpu.roll(a, shift, axis=0) for a in lst])
    return lst


def _count_leading(pred, t):
    sel = jnp.where
    c1 = pred(t[7])
    c2 = pred(sel(c1, t[11], t[3]))
    c3 = pred(sel(c1, sel(c2, t[13], t[9]), sel(c2, t[5], t[1])))
    c4 = pred(sel(c1, sel(c2, sel(c3, t[14], t[12]), sel(c3, t[10], t[8])),
                  sel(c2, sel(c3, t[6], t[4]), sel(c3, t[2], t[0]))))
    c5 = pred(t[15])
    return (sel(c1, 8.0, 0.0) + sel(c2, 4.0, 0.0) + sel(c3, 2.0, 0.0) + sel(c4, 1.0, 0.0)
            + sel(c5, 1.0, 0.0))


def _top16_rows(s):
    lst = [s[SUBLANES * v:SUBLANES * (v + 1), :] for v in range(s.shape[0] // SUBLANES)]
    _bitonic_sort_desc(lst)
    return _merge_sublanes(lst)


def _topk_body(f_ref, wq_ref, k1_ref, k2_ref, e1_o, n_o, r2_o, e2_o, q_s):
    q_s[...] = lax.dot_general(wq_ref[...], f_ref[...], (NT, ((), ())),
                               preferred_element_type=F32).astype(BF16)
    half = PEER_DK // 2
    T = f_ref.shape[0]
    sub = _iota((SUBLANES, T), 0)

    def stack(rows):
        out = rows[SUBLANES - 1]
        for b in range(SUBLANES - 2, -1, -1):
            out = jnp.where(sub == b, rows[b], out)
        return out

    def rep(a):
        return jnp.concatenate([a] * (N_KEYS // SUBLANES), axis=0)

    def head(h, carry):
        base = pl.multiple_of(h * PEER_DK, PEER_DK)
        s1 = jnp.dot(k1_ref[...], q_s[pl.ds(base, half), :], preferred_element_type=F32)
        s2 = jnp.dot(k2_ref[...], q_s[pl.ds(base + half, half), :], preferred_element_type=F32)
        t1 = _top16_rows(s1)
        t2 = _top16_rows(s2)
        lo = stack(t2[:SUBLANES])
        hi = stack(t2[SUBLANES:])
        top = _merge_sublanes(_merge_top([t + lo for t in t1], [t + hi for t in t1]))
        z = jnp.exp(top[0] - top[0])
        for kk in range(1, PEER_TOPK):
            z = z + jnp.exp(top[kk] - top[0])
        thr = rep(top[PEER_TOPK - 1])
        t2r = [rep(t) for t in t2]
        r2 = _count_leading(lambda t: t > s2, t2r)
        n1 = _count_leading(lambda t: s1 + t >= thr, t2r)
        r2_o[h] = _pack_rows(r2)
        n_o[h] = _dup_bf16(n1)
        e1_o[h] = _dup_bf16(jnp.exp(s1 - rep(t1[0])) / rep(z))
        e2_o[h] = _pack_rows(jnp.exp(s2 - rep(t2[0])))
        return carry

    lax.fori_loop(0, PEER_HEADS, head, 0)


def _topk(f, wq_t, k1, k2):
    rows = f.shape[0]
    T = TOPK_TILE
    full = lambda a: pl.BlockSpec(a.shape, lambda i: (0,) * a.ndim)
    big = pl.BlockSpec((PEER_HEADS, N_KEYS, T), lambda i: (0, 0, i))
    big_shape = lambda dt: jax.ShapeDtypeStruct((PEER_HEADS, N_KEYS, rows), dt)
    packed = pl.BlockSpec((PEER_HEADS, N_KEYS // 2, T), lambda i: (0, 0, i))
    packed_shape = jax.ShapeDtypeStruct((PEER_HEADS, N_KEYS // 2, rows), U32)
    return pl.pallas_call(
        _topk_body, grid=(rows // T,),
        in_specs=[pl.BlockSpec((T, D_MODEL), lambda i: (i, 0)), full(wq_t), full(k1), full(k2)],
        out_specs=[big, big, packed, packed],
        out_shape=[big_shape(U32), big_shape(U32), packed_shape, packed_shape],
        scratch_shapes=[pltpu.VMEM((PEER_HEADS * PEER_DK, T), BF16)],
        compiler_params=_cparams(1), name="peer_topk",
    )(f, wq_t, k1, k2)


MXU_TILE = 256
ACC_SLOTS = (0, 128)
PEER_CHAIN_LAG = 4


def _mxu_unit(mxu, slot, lhs, rhs, zero, reg):
    if zero is not None:
        head = pltpu.bitcast(pltpu.bitcast(rhs[:16, :], U32) | zero, BF16)
        rhs = jnp.concatenate([head, rhs[16:, :]], axis=0)
    pltpu.matmul_push_rhs(rhs, staging_register=reg, mxu_index=mxu)
    pltpu.matmul_acc_lhs(ACC_SLOTS[slot], lhs, mxu_index=mxu, load_staged_rhs=reg)


def _mxu_pop(mxu, slot):
    return pltpu.matmul_pop(ACC_SLOTS[slot], (MXU_TILE, MXU_TILE), F32, mxu_index=mxu)


def _peer_body(final, n_e, ft_ref, u_ref, v_ref, e1_ref, n_ref, r2_ref, e2_ref, h_ref,
               gate_ref, fg_ref, out, st_0, st_1, at_0, at_1, acc):
    s = pl.program_id(0)
    j_up = jnp.maximum(s - 2, 0) % n_e

    @pl.when(s == 0)
    def _():
        for ref in (st_0, st_1, at_0, at_1):
            ref[...] = jnp.zeros_like(ref)

    @pl.when(j_up == 0)
    def _():
        acc[...] = jnp.zeros_like(acc)

    T = MXU_TILE
    n_k = D_MODEL // T

    def stage(st_w, st_r, at_w, at_r):
        blk = (N_KEYS // 2, LANES)
        slabs = [(lg, ii) for lg in range(TOK_TILE // LANES) for ii in range(EXP_TILE // N_KEYS)]
        down = [(tt, et, k) for tt in range(TOK_TILE // T) for et in range(EXP_TILE // T)
                for k in range(n_k)]
        up = [(tt, dt, k) for tt in range(TOK_TILE // T) for dt in range(D_MODEL // T)
              for k in range(EXP_TILE // T)]
        assert len(down) == len(up) == len(slabs)
        zero = None
        zeros = []
        n_uk = EXP_TILE // T
        for i, ((tt, et, k), (ut, dt, uk), (lg, ii)) in enumerate(zip(down, up, slabs)):
            slot = (i // n_k) % 2
            _mxu_unit(0, slot, u_ref[et * T:(et + 1) * T, k * T:(k + 1) * T],
                      ft_ref[k * T:(k + 1) * T, tt * T:(tt + 1) * T], zero, k % 2)
            if k == n_k - 1:
                st_w[et * T:(et + 1) * T, tt * T:(tt + 1) * T] = _mxu_pop(0, slot)
            slot = (i // n_uk) % 2
            _mxu_unit(1, slot, _unpack_rows(at_r[ut * T // 2:(ut + 1) * T // 2, uk * T:(uk + 1) * T]),
                      v_ref[uk * T:(uk + 1) * T, dt * T:(dt + 1) * T], zero, uk % 2)
            if uk == n_uk - 1:
                acc[ut * T:(ut + 1) * T, dt * T:(dt + 1) * T] += _mxu_pop(1, slot)
            ln = slice(lg * LANES, (lg + 1) * LANES)
            rows = slice(ii * N_KEYS, (ii + 1) * N_KEYS)
            g = None
            for h in range(PEER_HEADS):
                n_row = _unpack_rows(jnp.broadcast_to(n_ref[h, ii:ii + 1, ln], blk))
                e1_row = _unpack_rows(jnp.broadcast_to(e1_ref[h, ii:ii + 1, ln], blk))
                w = jnp.where(_unpack_rows(r2_ref[h, :, ln]) < n_row,
                              _unpack_rows(e2_ref[h, :, ln]), 0.0) * e1_row
                g = w if g is None else g + w
            a_t = _pack_rows((_gelu_tanh(st_r[rows, ln]).astype(BF16) * g).T)
            at_w[lg * LANES // 2:(lg + 1) * LANES // 2, rows] = a_t
            last = a_t[-SUBLANES:, :]
            zeros.append(jnp.concatenate([(last >> 16) >> 16] * (T // LANES), axis=1))
            zero = zeros[-PEER_CHAIN_LAG] if len(zeros) >= PEER_CHAIN_LAG else None

    @pl.when(s % 2 == 0)
    def _():
        stage(st_0, st_1, at_1, at_0)

    @pl.when(s % 2 == 1)
    def _():
        stage(st_1, st_0, at_0, at_1)

    @pl.when(jnp.logical_and(s >= 2, j_up == n_e - 1))
    def _():
        h = h_ref[...] + gate_ref[...] * acc[...]
        if final:
            h = h * lax.rsqrt(jnp.mean(h * h, axis=-1, keepdims=True) + EPS) * fg_ref[...]
        out[...] = h


def _peer(f_t, u_b, v_b, e1, n1, r2, e2, h, mods, fg, n_lat, per_seq, n_tok_tiles, final):
    n_e = N_EXPERTS // EXP_TILE
    rpt = EXP_TILE // N_KEYS
    n_steps = n_tok_tiles * n_e

    def tile(s, lag):
        t = jnp.clip(s - lag, 0, n_steps - 1)
        return t // n_e, t % n_e

    tok = lambda lag: pl.BlockSpec((TOK_TILE, D_MODEL), lambda s: (tile(s, lag)[0], 0))
    exp = lambda lag: pl.BlockSpec((EXP_TILE, D_MODEL), lambda s: (tile(s, lag)[1], 0))
    rowblk = pl.BlockSpec((PEER_HEADS, rpt, TOK_TILE), lambda s: (0, tile(s, 1)[1], tile(s, 1)[0]))
    allkeys = pl.BlockSpec((PEER_HEADS, N_KEYS // 2, TOK_TILE), lambda s: (0, 0, tile(s, 1)[0]))

    def gate_map(s):
        i = tile(s, 2)[0]
        row = jnp.where(i < n_lat, i // per_seq, 2)
        return (row * 6 + 5, 0, 0)

    return pl.pallas_call(
        functools.partial(_peer_body, final, n_e),
        grid=(n_steps + 2,),
        in_specs=[pl.BlockSpec((D_MODEL, TOK_TILE), lambda s: (0, tile(s, 0)[0])),
                  exp(0), exp(2), rowblk, rowblk, allkeys, allkeys, tok(2),
                  pl.BlockSpec((None, 1, D_MODEL), gate_map),
                  pl.BlockSpec((1, D_MODEL), lambda s: (0, 0))],
        out_specs=tok(2),
        out_shape=jax.ShapeDtypeStruct((n_tok_tiles * TOK_TILE, D_MODEL), F32),
        scratch_shapes=[pltpu.VMEM((EXP_TILE, TOK_TILE), F32),
                        pltpu.VMEM((EXP_TILE, TOK_TILE), F32),
                        pltpu.VMEM((TOK_TILE // 2, EXP_TILE), U32),
                        pltpu.VMEM((TOK_TILE // 2, EXP_TILE), U32),
                        pltpu.VMEM((TOK_TILE, D_MODEL), F32)],
        compiler_params=_cparams(1), name="peer",
    )(f_t, u_b, v_b, e1, n1, r2, e2, h, mods, fg)


def _sincos_2d(t_len):
    rows = t_len // GRID_W
    row = jnp.repeat(jnp.arange(rows), GRID_W).astype(F32)
    col = jnp.tile(jnp.arange(GRID_W), rows).astype(F32)
    q = D_MODEL // 4
    freq = 10000.0 ** (-jnp.arange(q, dtype=F32) / q)
    ar = row[:, None] * freq
    ac = col[:, None] * freq
    return jnp.concatenate([jnp.sin(ar), jnp.cos(ar), jnp.sin(ac), jnp.cos(ac)], -1)


def _row(a):
    return a.reshape(1, -1).astype(F32)


def _pad_lanes(a, width=LANES):
    return jnp.pad(a, ((0, 0), (0, width - a.shape[-1])))


def kernel(x, c, ctx, c_ctx, ada_w, ada_b, norm_mix_g, norm_ffn_g, w_in, w_out, a_conv_w, a_conv_b, a_ln_g, a_ln_b, b_ln_g, b_ln_b, b_ws, b_bs, c_conv_w, c_conv_b, c_dt_bias, c_a_log, c_d, c_norm_g, d_mu_prev, d_mu_next, d_w0, d_w2, d_a0, d_a2, d_g2, d_k_k, d_k_a, d_r_k, d_gn_g, d_gn_b, peer_wq, peer_k1, peer_k2, peer_u, peer_v, final_g):
    n_b, t_lat, _ = x.shape
    t_ctx = ctx.shape[1]
    assert n_b == 2 and t_ctx == SEQ_TILE and t_lat % TOK_TILE == 0
    n_l = t_lat // SEQ_TILE
    per_seq = t_lat // TOK_TILE
    n_lat = n_b * per_seq
    n_lat256 = n_b * n_l

    cs = jnp.zeros((8, D_MODEL), F32).at[:n_b].set(c).at[n_b].set(c_ctx)
    mods_all = _ada(cs, ada_w, ada_b)
    pos = _sincos_2d(t_lat)
    x2 = x.reshape(n_b * t_lat, D_MODEL)
    ctx2 = ctx.reshape(n_b * t_ctx, D_MODEL)

    h = None
    for i in range(DEPTH):
        last_layer = i == DEPTH - 1
        mods = mods_all[i].reshape(8 * 6, 1, D_MODEL)
        w = w_in[i]
        dtc = OFF_C + D_GROUP + C_XBC
        w_c = jnp.concatenate([w[:, OFF_C:dtc], _pad_lanes(w[:, dtc:dtc + HEADS_C]),
                               _pad_lanes(w[:, dtc + HEADS_C:dtc + 2 * HEADS_C])], axis=1)
        ws_in = tuple(a.astype(BF16) for a in (w[:, :OFF_B], w[:, OFF_B:OFF_C], w_c, w[:, OFF_D:]))
        g_mix = _row(norm_mix_g[i])
        if i == 0:
            h, pa, pb, pc, pd = _inproj((x2, ctx2, pos), mods, g_mix, ws_in, n_lat, per_seq, True)
        else:
            pa, pb, pc, pd = _inproj(h, mods, g_mix, ws_in, n_lat, per_seq, False)

        conv_a = jnp.pad(a_conv_w[i], ((0, 32 - CONV_A), (0, 0)))
        bsm = jnp.repeat(b_bs[i].T, D_GROUP // HEADS_B, axis=1)
        yab = _mixab(pa, pb, (conv_a, _row(a_conv_b[i]), _row(a_ln_g[i]), _row(a_ln_b[i]),
                              _row(b_ln_g[i]), _row(b_ln_b[i]), b_ws[i].astype(BF16), bsm),
                     n_lat256, n_l)

        conv_c = jnp.pad(c_conv_w[i], ((0, 8 - CONV_C), (0, 0)))
        dtb = _pad_lanes(c_dt_bias[i]).reshape(2, 1, LANES)
        alog = _pad_lanes(c_a_log[i]).reshape(2, 1, LANES)
        dsk = _row(jnp.repeat(c_d[i], HEAD_DIM_C))
        yc = _mamba(pc, (conv_c, _row(c_conv_b[i]), dtb, alog, dsk, _row(c_norm_g[i])), n_b, n_l)

        yd = _rwkv(pd, (_row(d_mu_prev[i]), _row(d_mu_next[i]), d_w0[i].reshape(2, 1, D_GROUP),
                        d_w2[i].astype(BF16), d_a0[i].reshape(2, 1, D_GROUP), d_a2[i].astype(BF16),
                        d_g2[i].astype(BF16), _row(d_k_k[i]), _row(d_k_a[i]), _row(d_r_k[i]),
                        _row(d_gn_g[i]), _row(d_gn_b[i])), n_b, n_l)

        wo = w_out[i].astype(BF16)
        h, f, f_t = _outproj(yab, yc, yd, h, mods, _row(norm_ffn_g[i]),
                        (wo[:512], wo[512:768], wo[768:]), n_lat, per_seq)

        e1, n1, r2, e2 = _topk(f, peer_wq[i].T.astype(BF16), peer_k1[i].astype(BF16),
                               peer_k2[i].astype(BF16))
        n_tok = n_lat if last_layer else n_lat + 1
        h = _peer(f_t, peer_u[i].astype(BF16), peer_v[i].astype(BF16), e1, n1, r2, e2, h,
                  mods, _row(final_g), n_lat, per_seq, n_tok, last_layer)
    return h.reshape(n_b, t_lat, D_MODEL)
```

```python
import functools
import math

import jax
import jax.numpy as jnp
from jax import lax
from jax.experimental import pallas as pl
from jax.experimental.pallas import tpu as pltpu

F32 = jnp.float32
BF16 = jnp.bfloat16
U32 = jnp.uint32
HIGHEST = lax.Precision.HIGHEST

D_MODEL = 1024
DEPTH = 2
GRID_W = 64
EPS = 1e-6
D_GROUP = 256
CONV_A = 31
CHUNK_B = 128
HEADS_B = 4
HEADS_C = 4
HEAD_DIM_C = 64
STATE_C = 128
CONV_C = 5
CHUNK_C = 128
HEADS_D = 4
HEAD_DIM_D = 64
LORA_W = 64
LORA_A = 64
LORA_G = 128
GN_EPS_D = 64e-5
N_KEYS = 128
N_EXPERTS = N_KEYS * N_KEYS
PEER_HEADS = 8
PEER_DK = 256
PEER_TOPK = 16

C_XBC = D_GROUP + 2 * 2 * STATE_C
OFF_B = 2 * D_GROUP
OFF_C = OFF_B + 2 * D_GROUP
OFF_D = OFF_C + D_GROUP + C_XBC + 2 * HEADS_C
D_COLS = 3 * D_GROUP + 2 * LORA_W + 2 * LORA_A + LORA_G
PC_COLS = D_GROUP + C_XBC + 2 * 128

LANES = 128
SEQ_TILE = 256
TOK_TILE = 512
CHUNK_D = 64
EXP_TILE = 2048
VMEM_LIMIT = 56 * 1024 * 1024


def _cparams(n_axes):
    return pltpu.CompilerParams(dimension_semantics=("arbitrary",) * n_axes,
                                vmem_limit_bytes=VMEM_LIMIT)


def _silu(x):
    return x * jax.nn.sigmoid(x)


def _dot(a, b, dims=None):
    a = a.astype(BF16)
    b = b.astype(BF16)
    if dims is None:
        return jnp.dot(a, b, preferred_element_type=F32)
    return lax.dot_general(a, b, (dims, ((), ())), preferred_element_type=F32)


def _dot_hi(a, b):
    return jnp.dot(a, b, precision=HIGHEST, preferred_element_type=F32)


def _split_bf16(x, terms):
    parts = []
    for _ in range(terms):
        part = x.astype(BF16)
        parts.append(part)
        x = x - part.astype(F32)
    return parts


def _dot_mask_rhs(a, mask, terms=3):
    m = mask.astype(BF16)
    return sum(jnp.dot(p, m, preferred_element_type=F32) for p in _split_bf16(a, terms))


def _dot_mask_lhs(mask, b, terms=3):
    m = mask.astype(BF16)
    return sum(jnp.dot(m, p, preferred_element_type=F32) for p in _split_bf16(b, terms))


NT = ((1,), (1,))
TN = ((0,), (0,))


def _gelu_tanh(x):
    k1 = math.sqrt(2.0 / math.pi)
    k3 = 0.044715 * k1
    hx = 0.5 * x
    return hx + hx * jnp.tanh(x * (k1 + k3 * (x * x)))


def _pack_rows(x):
    return pltpu.bitcast(x.astype(BF16), U32)


def _unpack_rows(x):
    return pltpu.bitcast(x, BF16)


def _dup_bf16(x):
    hi = pltpu.bitcast(x.astype(BF16).astype(F32), U32)
    return hi | (hi >> 16)


def _iota(shape, axis):
    return lax.broadcasted_iota(jnp.int32, shape, axis)


def _head_block_ones(n, width):
    return (_iota((n, n), 0) // width == _iota((n, n), 1) // width).astype(F32)


def _ada_body(cs_ref, w_ref, b_ref, o_ref):
    o_ref[...] = _dot_hi(_silu(cs_ref[...]), w_ref[...]) + b_ref[...]


def _ada(cs, ada_w, ada_b):
    L = ada_w.shape[0]
    nb = 1536
    return pl.pallas_call(
        _ada_body,
        grid=(L, 6 * D_MODEL // nb),
        in_specs=[pl.BlockSpec((8, D_MODEL), lambda l, n: (0, 0)),
                  pl.BlockSpec((None, D_MODEL, nb), lambda l, n: (l, 0, n)),
                  pl.BlockSpec((None, 1, nb), lambda l, n: (l, 0, n))],
        out_specs=pl.BlockSpec((None, 8, nb), lambda l, n: (l, 0, n)),
        out_shape=jax.ShapeDtypeStruct((L, 8, 6 * D_MODEL), F32),
        compiler_params=_cparams(2),
        name="ada",
    )(cs, ada_w, ada_b.reshape(L, 1, 6 * D_MODEL))


def _norm_mod(h, g, shift, scale):
    xn = h * lax.rsqrt(jnp.mean(h * h, axis=-1, keepdims=True) + EPS) * g
    return xn * (1.0 + scale) + shift


def _inproj_tail(h, sh_ref, sc_ref, g_ref, wa, wb, wc, wd, oa, ob, oc, od):
    xm = _norm_mod(h, g_ref[...], sh_ref[...], sc_ref[...]).astype(BF16)
    oa[...] = jnp.dot(xm, wa[...], preferred_element_type=F32)
    ob[...] = jnp.dot(xm, wb[...], preferred_element_type=F32)
    oc[...] = jnp.dot(xm, wc[...], preferred_element_type=F32)
    od[...] = jnp.dot(xm, wd[...], preferred_element_type=F32)


def _inproj_first_body(n_lat, x_ref, ctx_ref, pos_ref, sh_ref, sc_ref, g_ref, wa, wb, wc, wd,
                       oh, oa, ob, oc, od):
    i = pl.program_id(0)
    h = jnp.where(i < n_lat, x_ref[...] + pos_ref[...], ctx_ref[...])
    oh[...] = h
    _inproj_tail(h, sh_ref, sc_ref, g_ref, wa, wb, wc, wd, oa, ob, oc, od)


def _inproj_body(h_ref, sh_ref, sc_ref, g_ref, wa, wb, wc, wd, oa, ob, oc, od):
    _inproj_tail(h_ref[...], sh_ref, sc_ref, g_ref, wa, wb, wc, wd, oa, ob, oc, od)


def _mod_spec(n_lat, per_seq, k):
    def imap(i):
        row = jnp.where(i < n_lat, i // per_seq, 2)
        return (row * 6 + k, 0, 0)
    return pl.BlockSpec((None, 1, D_MODEL), imap)


def _inproj(h_or_parts, mods, g, ws, n_lat, per_seq, first):
    wa, wb, wc, wd = ws
    n_tiles = n_lat + 1
    rows = n_tiles * TOK_TILE
    tile = lambda c: pl.BlockSpec((TOK_TILE, c), lambda i: (i, 0))
    full = lambda a: pl.BlockSpec(a.shape, lambda i: (0,) * a.ndim)
    common_specs = [_mod_spec(n_lat, per_seq, 0), _mod_spec(n_lat, per_seq, 1), full(g),
                    full(wa), full(wb), full(wc), full(wd)]
    out_specs = [tile(512), tile(512), tile(PC_COLS), tile(D_COLS)]
    out_shape = [jax.ShapeDtypeStruct((rows, c), F32) for c in (512, 512, PC_COLS, D_COLS)]
    if first:
        x2, ctx2, pos = h_or_parts
        in_specs = [pl.BlockSpec((TOK_TILE, D_MODEL), lambda i: (jnp.minimum(i, n_lat - 1), 0)),
                    pl.BlockSpec((TOK_TILE, D_MODEL), lambda i: (0, 0)),
                    pl.BlockSpec((TOK_TILE, D_MODEL), lambda i: (i % per_seq, 0))] + common_specs
        return pl.pallas_call(
            functools.partial(_inproj_first_body, n_lat),
            grid=(n_tiles,), in_specs=in_specs,
            out_specs=[tile(D_MODEL)] + out_specs,
            out_shape=[jax.ShapeDtypeStruct((rows, D_MODEL), F32)] + out_shape,
            compiler_params=_cparams(1), name="inproj_first",
        )(x2, ctx2, pos, mods, mods, g, wa, wb, wc, wd)
    return pl.pallas_call(
        _inproj_body, grid=(n_tiles,), in_specs=[tile(D_MODEL)] + common_specs,
        out_specs=out_specs, out_shape=out_shape,
        compiler_params=_cparams(1), name="inproj",
    )(h_or_parts, mods, mods, g, wa, wb, wc, wd)


def _layernorm(x, g, b, eps=1e-5):
    mu = jnp.mean(x, axis=-1, keepdims=True)
    xc = x - mu
    var = jnp.mean(xc * xc, axis=-1, keepdims=True)
    return xc * lax.rsqrt(var + eps) * g + b


def _mixab_body(n_lat, n_l, pa, pa_prev, pa_next, pb, cw, cb, alg, alb, blg, blb, ws, bsm,
                out, ext):
    i = pl.program_id(0)
    is_ctx = i >= n_lat
    tpos = i % n_l
    first = jnp.logical_or(is_ctx, tpos == 0)
    last = jnp.logical_or(is_ctx, tpos == n_l - 1)

    def glu(x):
        return x[:, :D_GROUP] * jax.nn.sigmoid(x[:, D_GROUP:])

    halo = 16
    ext[0:halo, :] = jnp.where(first, 0.0, glu(pa_prev[...]))
    ext[halo:halo + SEQ_TILE, :] = glu(pa[...])
    ext[halo + SEQ_TILE:, :] = jnp.where(last, 0.0, glu(pa_next[...]))
    acc = jnp.zeros((SEQ_TILE, D_GROUP), F32) + cb[...]
    for j in range(CONV_A):
        acc = acc + cw[j:j + 1, :] * ext[pl.ds(halo - (CONV_A - 1) // 2 + j, SEQ_TILE), :]
    out[:, :D_GROUP] = _silu(_layernorm(acc, alg[...], alb[...]))

    x = pb[...]
    u = x[:, :D_GROUP]
    v = _layernorm(x[:, D_GROUP:], blg[...], blb[...]).astype(BF16)
    hd = D_GROUP // HEADS_B
    rows = []
    for c in range(SEQ_TILE // CHUNK_B):
        vc = v[c * CHUNK_B:(c + 1) * CHUNK_B, :]
        heads = [jnp.dot(ws[hh], vc[:, hh * hd:(hh + 1) * hd], preferred_element_type=F32)
                 for hh in range(HEADS_B)]
        rows.append(jnp.concatenate(heads, axis=1) + bsm[...])
    out[:, D_GROUP:] = u * jnp.concatenate(rows, axis=0)


def _mixab(pa, pb, prm, n_lat, n_l):
    rows = pa.shape[0]
    n_tiles = rows // SEQ_TILE
    halo = 16
    per = SEQ_TILE // halo
    full = lambda a: pl.BlockSpec(a.shape, lambda i: (0,) * a.ndim)
    in_specs = [pl.BlockSpec((SEQ_TILE, 512), lambda i: (i, 0)),
                pl.BlockSpec((halo, 512), lambda i: (jnp.maximum(i * per - 1, 0), 0)),
                pl.BlockSpec((halo, 512), lambda i: (jnp.minimum((i + 1) * per, rows // halo - 1), 0)),
                pl.BlockSpec((SEQ_TILE, 512), lambda i: (i, 0))] + [full(a) for a in prm]
    return pl.pallas_call(
        functools.partial(_mixab_body, n_lat, n_l),
        grid=(n_tiles,), in_specs=in_specs,
        out_specs=pl.BlockSpec((SEQ_TILE, 512), lambda i: (i, 0)),
        out_shape=jax.ShapeDtypeStruct((rows, 512), F32),
        scratch_shapes=[pltpu.VMEM((SEQ_TILE + 2 * halo, D_GROUP), F32)],
        compiler_params=_cparams(1), name="mixab",
    )(pa, pa, pa, pb, *prm)


def _scan_tile(n_b, n_l, b, ph, j):
    lat = b * n_l + jnp.where(ph == 0, j - 1, n_l - j)
    return jnp.where(j == 0, n_b * n_l + b, lat)


def _scan_specs(n_b, n_l, rows, cols):
    tid = functools.partial(_scan_tile, n_b, n_l)
    per = SEQ_TILE // 8
    cur = pl.BlockSpec((SEQ_TILE, cols), lambda b, ph, j: (tid(b, ph, j), 0))
    prev = pl.BlockSpec((8, cols), lambda b, ph, j: (jnp.maximum(tid(b, ph, j) * per - 1, 0), 0))
    nxt = pl.BlockSpec((8, cols),
                       lambda b, ph, j: (jnp.minimum((tid(b, ph, j) + 1) * per, rows // 8 - 1), 0))
    return cur, prev, nxt


def _scan_out_spec(n_b, n_l, cols):
    def imap(b, ph, j):
        return (jnp.where(ph == 0, n_b * n_l + b, _scan_tile(n_b, n_l, b, 1, j)), 0)
    return pl.BlockSpec((SEQ_TILE, cols), imap)


def _scan_flags(n_l):
    ph = pl.program_id(1)
    j = pl.program_id(2)
    is_ctx = j == 0
    tpos = jnp.where(ph == 0, j - 1, n_l - j)
    first = jnp.logical_or(is_ctx, tpos == 0)
    last = jnp.logical_or(is_ctx, tpos == n_l - 1)
    slot = jnp.where(is_ctx, 0, tpos + 1)
    return ph, j, first, last, slot


def _mamba_body(n_l, cur_ref, prev_ref, next_ref, dt_ref, cw, cb, dtb, alog, dsk, ng,
                out, ext, yf, st, xbc_s, a_s, dt_s, y_s):
    ph, j, first, last, slot = _scan_flags(n_l)
    fwd = ph == 0

    @pl.when(j == 0)
    def _():
        st[...] = jnp.zeros_like(st)

    zx = D_GROUP
    ext[0:8, :] = jnp.where(first, 0.0, prev_ref[:, zx:])
    ext[8:8 + SEQ_TILE, :] = cur_ref[:, zx:]
    ext[8 + SEQ_TILE:, :] = jnp.where(last, 0.0, next_ref[:, zx:])
    acc = jnp.zeros((SEQ_TILE, C_XBC), F32) + cb[...]
    for jj in range(CONV_C):
        acc = acc + cw[jj:jj + 1, :] * ext[pl.ds(8 - (CONV_C - 1) // 2 + jj, SEQ_TILE), :]
    xbc_s[...] = _silu(acc)
    dt = jax.nn.softplus(dt_ref[...] + dtb[ph])
    dt_s[...] = dt
    a_s[...] = dt * (-jnp.exp(alog[ph]))

    L = CHUNK_C
    r_i = _iota((L, L), 0)
    c_i = _iota((L, L), 1)
    ltri = (r_i >= c_i).astype(F32)
    mask = jnp.where(fwd, r_i - c_i, c_i - r_i) >= 0
    sgn = jnp.where(fwd, 1.0, -1.0)
    hd = HEAD_DIM_C

    def chunk(it, carry):
        q = jnp.where(fwd, it, SEQ_TILE // L - 1 - it)
        rows = pl.ds(pl.multiple_of(q * L, L), L)
        a_q = a_s[rows, :]
        cs = _dot_mask_lhs(ltri, a_q)
        tot = cs[L - 1:L, :]
        e = jnp.where(fwd, cs, cs - a_q)
        e_t = e.T
        dt_q = dt_s[rows, :]
        xbc = xbc_s[rows, :]
        ys = []
        for g in range(2):
            bm = xbc[:, D_GROUP + g * STATE_C:D_GROUP + (g + 1) * STATE_C]
            cm = xbc[:, D_GROUP + 2 * STATE_C + g * STATE_C:D_GROUP + 2 * STATE_C + (g + 1) * STATE_C]
            bm_t = bm.T.astype(BF16)
            cm_b = cm.astype(BF16)
            gmat = jnp.dot(cm_b, bm_t, preferred_element_type=F32)
            for h in (2 * g, 2 * g + 1):
                ecol = e[:, h:h + 1]
                erow = e_t[h:h + 1, :]
                totc = tot[:, h:h + 1]
                xdt = xbc[:, h * hd:(h + 1) * hd] * dt_q[:, h:h + 1]
                lmat = jnp.exp(jnp.where(mask, sgn * (ecol - erow), -1e30))
                offs = jnp.exp(jnp.where(fwd, ecol, totc - ecol))
                stw = jnp.exp(jnp.where(fwd, totc - ecol, ecol))
                s_prev = st[h]
                y = _dot(gmat * lmat, xdt) + offs * _dot(cm_b, s_prev)
                st[h] = jnp.exp(totc) * s_prev + _dot(bm_t, xdt * stw)
                ys.append(y)
        y_s[rows, :] = jnp.concatenate(ys, axis=1)
        return carry

    lax.fori_loop(0, SEQ_TILE // L, chunk, 0)

    @pl.when(fwd)
    def _():
        yf[slot] = y_s[...]

    @pl.when(ph == 1)
    def _():
        y = yf[slot] + y_s[...] + dsk[...] * xbc_s[:, :D_GROUP]
        t = y * _silu(cur_ref[:, :D_GROUP])
        out[...] = t * lax.rsqrt(jnp.mean(t * t, axis=-1, keepdims=True) + EPS) * ng[...]


def _mamba(pc, prm, n_b, n_l):
    rows = pc.shape[0]
    cur, prev, nxt = _scan_specs(n_b, n_l, rows, D_GROUP + C_XBC)
    tid = functools.partial(_scan_tile, n_b, n_l)
    dt_spec = pl.BlockSpec((SEQ_TILE, LANES),
                           lambda b, ph, j: (tid(b, ph, j), (D_GROUP + C_XBC) // LANES + ph))
    full = lambda a: pl.BlockSpec(a.shape, lambda b, ph, j: (0,) * a.ndim)
    return pl.pallas_call(
        functools.partial(_mamba_body, n_l),
        grid=(n_b, 2, n_l + 1),
        in_specs=[cur, prev, nxt, dt_spec] + [full(a) for a in prm],
        out_specs=_scan_out_spec(n_b, n_l, D_GROUP),
        out_shape=jax.ShapeDtypeStruct((rows, D_GROUP), F32),
        scratch_shapes=[pltpu.VMEM((SEQ_TILE + 16, C_XBC), F32),
                        pltpu.VMEM((n_l + 1, SEQ_TILE, D_GROUP), F32),
                        pltpu.VMEM((HEADS_C, STATE_C, HEAD_DIM_C), F32),
                        pltpu.VMEM((SEQ_TILE, C_XBC), F32),
                        pltpu.VMEM((SEQ_TILE, LANES), F32),
                        pltpu.VMEM((SEQ_TILE, LANES), F32),
                        pltpu.VMEM((SEQ_TILE, D_GROUP), F32)],
        compiler_params=_cparams(3), name="mamba",
    )(pc, pc, pc, pc, *prm)


def _rwkv_body(n_l, cur_ref, prev_ref, next_ref, mup, mun, w0, w2, a0, a2, g2, kkw, kaw, rkw,
               gng, gnb, out, yf, st, r_s, v_s, kap_s, alp_s, kd_s, lw_s, y_s,
               phi_s, psi_s, sin_s, left_s, amat_s, ara_s, tinv_s):
    ph, j, first, last, slot = _scan_flags(n_l)
    fwd = ph == 0

    @pl.when(j == 0)
    def _():
        st[...] = jnp.zeros_like(st)

    cur = cur_ref[...]
    prv = jnp.concatenate([jnp.where(first, 0.0, prev_ref[7:8, :]), cur[:SEQ_TILE - 1, :]], axis=0)
    nxt = jnp.concatenate([cur[1:, :], jnp.where(last, 0.0, next_ref[0:1, :])], axis=0)
    p = cur + mup[...] * (prv - cur) + mun[...] * (nxt - cur)
    G = D_GROUP
    r = p[:, :G]
    k = p[:, G:2 * G]
    v = p[:, 2 * G:3 * G]
    blk = _head_block_ones(G, HEAD_DIM_D)
    kkr = k * kkw[...]
    kk = kkr * lax.rsqrt(_dot_mask_rhs(kkr * kkr, blk) + 1e-12)

    def rate(d_static=None):
        if d_static is None:
            ad = jnp.where(fwd, p[:, 3 * G + 2 * LORA_W:3 * G + 2 * LORA_W + LORA_A],
                           p[:, 3 * G + 2 * LORA_W + LORA_A:3 * G + 2 * LORA_W + 2 * LORA_A])
            a = jax.nn.sigmoid(a0[ph] + _dot(ad, a2[ph]))
        else:
            o = 3 * G + 2 * LORA_W + d_static * LORA_A
            a = jax.nn.sigmoid(a0[d_static] + _dot(p[:, o:o + LORA_A], a2[d_static]))
        return a, k * (1.0 + (a - 1.0) * kaw[...])

    a_d, kd_d = rate()
    wd = jnp.where(fwd, p[:, 3 * G:3 * G + LORA_W], p[:, 3 * G + LORA_W:3 * G + 2 * LORA_W])
    w = w0[ph] + _dot(jnp.tanh(wd), w2[ph])
    lw_s[...] = -math.exp(-0.5) * jax.nn.sigmoid(w)
    r_s[...] = r
    v_s[...] = v
    kap_s[...] = kk
    alp_s[...] = a_d * kk
    kd_s[...] = kd_d

    C = CHUNK_D
    r_i = _iota((C, C), 0)
    c_i = _iota((C, C), 1)
    lag = jnp.where(fwd, r_i - c_i, c_i - r_i)
    tri = lag >= 0
    tri_f = tri.astype(F32)
    strict = lag > 0
    eye = (r_i == c_i).astype(F32)
    hd = HEAD_DIM_D

    n_q = SEQ_TILE // C
    eye_k = (_iota((hd, hd), 0) == _iota((hd, hd), 1)).astype(F32)

    units = [(q, h) for q in range(n_q) for h in range(HEADS_D)]
    rows_of = lambda q: slice(q * C, (q + 1) * C)
    lanes_of = lambda h: slice(h * hd, (h + 1) * hd)
    left, right, p_tot = {}, {}, {}
    for q in range(n_q):
        rows = rows_of(q)
        lw = lw_s[rows, :]
        incl = _dot_mask_lhs(tri_f, lw)
        tot = jnp.where(fwd, incl[C - 1:C, :], incl[0:1, :])
        p_inv = jnp.exp(-incl)
        kap_h = kap_s[rows, :] * jnp.exp(incl - lw)
        r_h = r_s[rows, :] * jnp.exp(incl)
        alp_b = alp_s[rows, :] * p_inv
        k_b = kd_s[rows, :] * p_inv
        ptq = jnp.exp(tot)
        for h in range(HEADS_D):
            sl = lanes_of(h)
            left[q, h] = jnp.concatenate([kap_h[:, sl], r_h[:, sl]], axis=0)
            right[q, h] = jnp.concatenate([alp_b[:, sl], k_b[:, sl]], axis=0)
            p_tot[q, h] = ptq[:, sl]
            left_s[q * HEADS_D + h] = left[q, h]
    m1 = {u: _dot(left[u], right[u], NT) for u in units}
    a_vk, pw, tinv = {}, {}, {}
    for u in units:
        qh = u[0] * HEADS_D + u[1]
        a_vk[u] = jnp.where(strict, m1[u][:C, C:], 0.0)
        amat_s[qh] = jnp.concatenate([a_vk[u], jnp.where(tri, m1[u][C:, C:], 0.0)], axis=0)
        ara_s[qh] = jnp.where(tri, m1[u][C:, :C], 0.0)
        pw[u] = jnp.where(strict, -m1[u][:C, :C], 0.0)
        tinv[u] = eye + pw[u]
    for _ in range(5):
        pw = {u: _dot(pw[u], pw[u]) for u in units}
        upd = {u: _dot(tinv[u], pw[u]) for u in units}
        tinv = {u: tinv[u] + upd[u] for u in units}
    wmat = {u: _dot(tinv[u], right[u][:C], TN) for u in units}
    kw = {u: _dot(left[u][:C], wmat[u], TN) for u in units}
    aw = {u: _dot(a_vk[u], wmat[u], TN) for u in units}
    vk = {u: _dot(v_s[rows_of(u[0]), lanes_of(u[1])], right[u][C:] - aw[u], TN) for u in units}
    for u in units:
        qh = u[0] * HEADS_D + u[1]
        phi_s[qh] = (eye_k - kw[u]) * p_tot[u]
        psi_s[qh] = vk[u] * p_tot[u]
        tinv_s[qh] = tinv[u]

    for it in range(n_q):
        q = jnp.where(fwd, it, n_q - 1 - it)
        for h in range(HEADS_D):
            qh = q * HEADS_D + h
            s_in = st[h]
            sin_s[qh] = s_in
            st[h] = _dot(s_in, phi_s[qh]) + psi_s[qh]

    qh_of = lambda u: u[0] * HEADS_D + u[1]
    x0 = {u: _dot(left_s[qh_of(u)], sin_s[qh_of(u)], NT) for u in units}
    x1 = {u: _dot(amat_s[qh_of(u)], v_s[rows_of(u[0]), lanes_of(u[1])]) for u in units}
    uu = {u: _dot(tinv_s[qh_of(u)], x0[u][:C] + x1[u][:C]) for u in units}
    au = {u: _dot(ara_s[qh_of(u)], uu[u]) for u in units}
    for q in range(n_q):
        y_s[rows_of(q), :] = jnp.concatenate(
            [x0[q, h][C:] + x1[q, h][C:] - au[q, h] for h in range(HEADS_D)], axis=1)

    @pl.when(fwd)
    def _():
        yf[slot] = y_s[...]

    @pl.when(ph == 1)
    def _():
        y = yf[slot] + y_s[...]
        inv = 1.0 / HEAD_DIM_D
        mu = _dot_mask_rhs(y, blk) * inv
        yc = y - mu
        var = _dot_mask_rhs(yc * yc, blk) * inv
        yn = yc * lax.rsqrt(var + GN_EPS_D) * gng[...] + gnb[...]
        _, kd_f = rate(0)
        bonus = _dot_mask_rhs(r * (kd_f + kd_d) * rkw[...], blk) * v
        gate = _dot(jax.nn.sigmoid(p[:, 3 * G + 2 * LORA_W + 2 * LORA_A:]), g2[...])
        out[...] = (yn + bonus) * gate


def _rwkv(pd, prm, n_b, n_l):
    rows = pd.shape[0]
    cur, prev, nxt = _scan_specs(n_b, n_l, rows, D_COLS)
    full = lambda a: pl.BlockSpec(a.shape, lambda b, ph, j: (0,) * a.ndim)
    tile = lambda: pltpu.VMEM((SEQ_TILE, D_GROUP), F32)
    per_qh = lambda r, c: pltpu.VMEM((SEQ_TILE // CHUNK_D * HEADS_D, r, c), F32)
    return pl.pallas_call(
        functools.partial(_rwkv_body, n_l),
        grid=(n_b, 2, n_l + 1),
        in_specs=[cur, prev, nxt] + [full(a) for a in prm],
        out_specs=_scan_out_spec(n_b, n_l, D_GROUP),
        out_shape=jax.ShapeDtypeStruct((rows, D_GROUP), F32),
        scratch_shapes=[pltpu.VMEM((n_l + 1, SEQ_TILE, D_GROUP), F32),
                        pltpu.VMEM((HEADS_D, HEAD_DIM_D, HEAD_DIM_D), F32)] + [tile() for _ in range(7)]
        + [per_qh(HEAD_DIM_D, HEAD_DIM_D) for _ in range(3)]
        + [per_qh(2 * CHUNK_D, HEAD_DIM_D), per_qh(2 * CHUNK_D, CHUNK_D),
           per_qh(CHUNK_D, CHUNK_D), per_qh(CHUNK_D, CHUNK_D)],
        compiler_params=_cparams(3), name="rwkv",
    )(pd, pd, pd, *prm)


def _outproj_body(yab, yc, yd, h_ref, gate, sh, sc, g, wab, wc, wd, oh, of):
    mix = (_dot(yab[...], wab[...]) + _dot(yc[...], wc[...]) + _dot(yd[...], wd[...]))
    h = h_ref[...] + gate[...] * mix
    oh[...] = h
    of[...] = _norm_mod(h, g[...], sh[...], sc[...]).astype(BF16)


def _outproj(yab, yc, yd, h, mods, g, ws, n_lat, per_seq):
    rows = h.shape[0]
    tile = lambda c: pl.BlockSpec((TOK_TILE, c), lambda i: (i, 0))
    full = lambda a: pl.BlockSpec(a.shape, lambda i: (0,) * a.ndim)
    return pl.pallas_call(
        _outproj_body, grid=(rows // TOK_TILE,),
        in_specs=[tile(512), tile(D_GROUP), tile(D_GROUP), tile(D_MODEL),
                  _mod_spec(n_lat, per_seq, 2), _mod_spec(n_lat, per_seq, 3),
                  _mod_spec(n_lat, per_seq, 4), full(g)] + [full(a) for a in ws],
        out_specs=[tile(D_MODEL), tile(D_MODEL)],
        out_shape=[jax.ShapeDtypeStruct((rows, D_MODEL), F32),
                   jax.ShapeDtypeStruct((rows, D_MODEL), BF16)],
        compiler_params=_cparams(1), name="outproj",
    )(yab, yc, yd, h, mods, mods, mods, g, *ws)


TOPK_TILE = 256


SUBLANES = 8


def _cmpx(lst, i, j):
    a, b = lst[i], lst[j]
    lst[i] = jnp.maximum(a, b)
    lst[j] = jnp.minimum(a, b)


def _bitonic_sort_desc(lst):
    n = len(lst)
    k = 2
    while k <= n:
        j = k // 2
        while j >= 1:
            for i in range(n):
                p = i ^ j
                if p > i:
                    if (i & k) == 0:
                        _cmpx(lst, i, p)
                    else:
                        _cmpx(lst, p, i)
            j //= 2
        k *= 2


def _bitonic_merge_desc(lst):
    n = len(lst)
    j = n // 2
    while j >= 1:
        for i in range(n):
            p = i ^ j
            if p > i:
                _cmpx(lst, i, p)
        j //= 2


def _merge_top(a, b):
    n = len(a)
    c = [jnp.maximum(a[i], b[n - 1 - i]) for i in range(n)]
    _bitonic_merge_desc(c)
    return c


def _merge_sublanes(lst):
    for shift in (4, 2, 1):
        lst = _merge_top(lst, [pltpu.roll(a, shift, axis=0) for a in lst])
    return lst


def _count_leading(pred, t):
    sel = jnp.where
    c1 = pred(t[7])
    c2 = pred(sel(c1, t[11], t[3]))
    c3 = pred(sel(c1, sel(c2, t[13], t[9]), sel(c2, t[5], t[1])))
    c4 = pred(sel(c1, sel(c2, sel(c3, t[14], t[12]), sel(c3, t[10], t[8])),
                  sel(c2, sel(c3, t[6], t[4]), sel(c3, t[2], t[0]))))
    c5 = pred(t[15])
    return (sel(c1, 8.0, 0.0) + sel(c2, 4.0, 0.0) + sel(c3, 2.0, 0.0) + sel(c4, 1.0, 0.0)
            + sel(c5, 1.0, 0.0))


def _top16_rows(s):
    lst = [s[SUBLANES * v:SUBLANES * (v + 1), :] for v in range(s.shape[0] // SUBLANES)]
    _bitonic_sort_desc(lst)
    return _merge_sublanes(lst)


def _topk_body(f_ref, wq_ref, k1_ref, k2_ref, e1_o, n_o, r2_o, e2_o, q_s):
    q_s[...] = lax.dot_general(wq_ref[...], f_ref[...], (NT, ((), ())),
                               preferred_element_type=F32).astype(BF16)
    half = PEER_DK // 2
    T = f_ref.shape[0]
    sub = _iota((SUBLANES, T), 0)

    def stack(rows):
        out = rows[SUBLANES - 1]
        for b in range(SUBLANES - 2, -1, -1):
            out = jnp.where(sub == b, rows[b], out)
        return out

    def rep(a):
        return jnp.concatenate([a] * (N_KEYS // SUBLANES), axis=0)

    def head(h, carry):
        base = pl.multiple_of(h * PEER_DK, PEER_DK)
        s1 = jnp.dot(k1_ref[...], q_s[pl.ds(base, half), :], preferred_element_type=F32)
        s2 = jnp.dot(k2_ref[...], q_s[pl.ds(base + half, half), :], preferred_element_type=F32)
        t1 = _top16_rows(s1)
        t2 = _top16_rows(s2)
        lo = stack(t2[:SUBLANES])
        hi = stack(t2[SUBLANES:])
        top = _merge_sublanes(_merge_top([t + lo for t in t1], [t + hi for t in t1]))
        z = jnp.exp(top[0] - top[0])
        for kk in range(1, PEER_TOPK):
            z = z + jnp.exp(top[kk] - top[0])
        thr = rep(top[PEER_TOPK - 1])
        t2r = [rep(t) for t in t2]
        r2 = _count_leading(lambda t: t > s2, t2r)
        n1 = _count_leading(lambda t: s1 + t >= thr, t2r)
        r2_o[h] = _pack_rows(r2)
        n_o[h] = _dup_bf16(n1)
        e1_o[h] = _dup_bf16(jnp.exp(s1 - rep(t1[0])) / rep(z))
        e2_o[h] = _pack_rows(jnp.exp(s2 - rep(t2[0])))
        return carry

    lax.fori_loop(0, PEER_HEADS, head, 0)


def _topk(f, wq_t, k1, k2):
    rows = f.shape[0]
    T = TOPK_TILE
    full = lambda a: pl.BlockSpec(a.shape, lambda i: (0,) * a.ndim)
    big = pl.BlockSpec((PEER_HEADS, N_KEYS, T), lambda i: (0, 0, i))
    big_shape = lambda dt: jax.ShapeDtypeStruct((PEER_HEADS, N_KEYS, rows), dt)
    packed = pl.BlockSpec((PEER_HEADS, N_KEYS // 2, T), lambda i: (0, 0, i))
    packed_shape = jax.ShapeDtypeStruct((PEER_HEADS, N_KEYS // 2, rows), U32)
    return pl.pallas_call(
        _topk_body, grid=(rows // T,),
        in_specs=[pl.BlockSpec((T, D_MODEL), lambda i: (i, 0)), full(wq_t), full(k1), full(k2)],
        out_specs=[big, big, packed, packed],
        out_shape=[big_shape(U32), big_shape(U32), packed_shape, packed_shape],
        scratch_shapes=[pltpu.VMEM((PEER_HEADS * PEER_DK, T), BF16)],
        compiler_params=_cparams(1), name="peer_topk",
    )(f, wq_t, k1, k2)


def _peer_body(final, n_e, f_ref, u_ref, v_ref, e1_ref, n_ref, r2_ref, e2_ref, h_ref,
               gate_ref, fg_ref, out, st_s, at_s, acc):
    s = pl.program_id(0)
    j_up = jnp.maximum(s - 1, 0) % n_e
    cur = s % 2

    @pl.when(s == 0)
    def _():
        at_s[...] = jnp.zeros_like(at_s)

    @pl.when(j_up == 0)
    def _():
        acc[...] = jnp.zeros_like(acc)

    st_s[...] = lax.dot_general(u_ref[...], f_ref[...], (NT, ((), ())),
                                preferred_element_type=F32)
    acc[...] += lax.dot_general(_unpack_rows(at_s[1 - cur]), v_ref[...], (TN, ((), ())),
                                preferred_element_type=F32)

    blk = (N_KEYS // 2, LANES)
    for lg in range(TOK_TILE // LANES):
        ln = slice(lg * LANES, (lg + 1) * LANES)
        for ii in range(EXP_TILE // N_KEYS):
            rows = slice(ii * N_KEYS, (ii + 1) * N_KEYS)
            g = None
            for h in range(PEER_HEADS):
                n_row = _unpack_rows(jnp.broadcast_to(n_ref[h, ii:ii + 1, ln], blk))
                e1_row = _unpack_rows(jnp.broadcast_to(e1_ref[h, ii:ii + 1, ln], blk))
                w = jnp.where(_unpack_rows(r2_ref[h, :, ln]) < n_row,
                              _unpack_rows(e2_ref[h, :, ln]), 0.0) * e1_row
                g = w if g is None else g + w
            at_s[cur, ii * N_KEYS // 2:(ii + 1) * N_KEYS // 2, ln] = _pack_rows(
                _gelu_tanh(st_s[rows, ln]).astype(BF16) * g)

    @pl.when(jnp.logical_and(s >= 1, j_up == n_e - 1))
    def _():
        h = h_ref[...] + gate_ref[...] * acc[...]
        if final:
            h = h * lax.rsqrt(jnp.mean(h * h, axis=-1, keepdims=True) + EPS) * fg_ref[...]
        out[...] = h


def _peer(f, u_b, v_b, e1, n1, r2, e2, h, mods, fg, n_lat, per_seq, n_tok_tiles, final):
    n_e = N_EXPERTS // EXP_TILE
    rpt = EXP_TILE // N_KEYS
    n_steps = n_tok_tiles * n_e

    def tile(s, lag):
        t = jnp.clip(s - lag, 0, n_steps - 1)
        return t // n_e, t % n_e

    tok = lambda lag: pl.BlockSpec((TOK_TILE, D_MODEL), lambda s: (tile(s, lag)[0], 0))
    exp = lambda lag: pl.BlockSpec((EXP_TILE, D_MODEL), lambda s: (tile(s, lag)[1], 0))
    rowblk = pl.BlockSpec((PEER_HEADS, rpt, TOK_TILE), lambda s: (0, tile(s, 0)[1], tile(s, 0)[0]))
    allkeys = pl.BlockSpec((PEER_HEADS, N_KEYS // 2, TOK_TILE), lambda s: (0, 0, tile(s, 0)[0]))

    def gate_map(s):
        i = tile(s, 1)[0]
        row = jnp.where(i < n_lat, i // per_seq, 2)
        return (row * 6 + 5, 0, 0)

    return pl.pallas_call(
        functools.partial(_peer_body, final, n_e),
        grid=(n_steps + 1,),
        in_specs=[tok(0), exp(0), exp(1), rowblk, rowblk, allkeys, allkeys, tok(1),
                  pl.BlockSpec((None, 1, D_MODEL), gate_map),
                  pl.BlockSpec((1, D_MODEL), lambda s: (0, 0))],
        out_specs=tok(1),
        out_shape=jax.ShapeDtypeStruct((n_tok_tiles * TOK_TILE, D_MODEL), F32),
        scratch_shapes=[pltpu.VMEM((EXP_TILE, TOK_TILE), F32),
                        pltpu.VMEM((2, EXP_TILE // 2, TOK_TILE), U32),
                        pltpu.VMEM((TOK_TILE, D_MODEL), F32)],
        compiler_params=_cparams(1), name="peer",
    )(f, u_b, v_b, e1, n1, r2, e2, h, mods, fg)


def _sincos_2d(t_len):
    rows = t_len // GRID_W
    q = D_MODEL // 4
    freq = 10000.0 ** (-jnp.arange(q, dtype=F32) / q)
    ar = jnp.arange(rows, dtype=F32)[:, None] * freq
    ac = jnp.arange(GRID_W, dtype=F32)[:, None] * freq
    per_row = lambda a: jnp.repeat(a, GRID_W, axis=0)
    per_col = lambda a: jnp.tile(a, (rows, 1))
    return jnp.concatenate([per_row(jnp.sin(ar)), per_row(jnp.cos(ar)),
                            per_col(jnp.sin(ac)), per_col(jnp.cos(ac))], -1)


def _row(a):
    return a.reshape(1, -1).astype(F32)


def _pad_lanes(a, width=LANES):
    return jnp.pad(a, ((0, 0), (0, width - a.shape[-1])))


def kernel(x, c, ctx, c_ctx, ada_w, ada_b, norm_mix_g, norm_ffn_g, w_in, w_out, a_conv_w, a_conv_b, a_ln_g, a_ln_b, b_ln_g, b_ln_b, b_ws, b_bs, c_conv_w, c_conv_b, c_dt_bias, c_a_log, c_d, c_norm_g, d_mu_prev, d_mu_next, d_w0, d_w2, d_a0, d_a2, d_g2, d_k_k, d_k_a, d_r_k, d_gn_g, d_gn_b, peer_wq, peer_k1, peer_k2, peer_u, peer_v, final_g):
    n_b, t_lat, _ = x.shape
    t_ctx = ctx.shape[1]
    assert n_b == 2 and t_ctx == SEQ_TILE and t_lat % TOK_TILE == 0
    n_l = t_lat // SEQ_TILE
    per_seq = t_lat // TOK_TILE
    n_lat = n_b * per_seq
    n_lat256 = n_b * n_l

    cs = jnp.zeros((8, D_MODEL), F32).at[:n_b].set(c).at[n_b].set(c_ctx)
    mods_all = _ada(cs, ada_w, ada_b)
    pos = _sincos_2d(t_lat)
    x2 = x.reshape(n_b * t_lat, D_MODEL)
    ctx2 = ctx.reshape(n_b * t_ctx, D_MODEL)

    h = None
    for i in range(DEPTH):
        last_layer = i == DEPTH - 1
        mods = mods_all[i].reshape(8 * 6, 1, D_MODEL)
        w = w_in[i]
        dtc = OFF_C + D_GROUP + C_XBC
        w_c = jnp.concatenate([w[:, OFF_C:dtc], _pad_lanes(w[:, dtc:dtc + HEADS_C]),
                               _pad_lanes(w[:, dtc + HEADS_C:dtc + 2 * HEADS_C])], axis=1)
        ws_in = tuple(a.astype(BF16) for a in (w[:, :OFF_B], w[:, OFF_B:OFF_C], w_c, w[:, OFF_D:]))
        g_mix = _row(norm_mix_g[i])
        if i == 0:
            h, pa, pb, pc, pd = _inproj((x2, ctx2, pos), mods, g_mix, ws_in, n_lat, per_seq, True)
        else:
            pa, pb, pc, pd = _inproj(h, mods, g_mix, ws_in, n_lat, per_seq, False)

        conv_a = jnp.pad(a_conv_w[i], ((0, 32 - CONV_A), (0, 0)))
        bsm = jnp.repeat(b_bs[i].T, D_GROUP // HEADS_B, axis=1)
        yab = _mixab(pa, pb, (conv_a, _row(a_conv_b[i]), _row(a_ln_g[i]), _row(a_ln_b[i]),
                              _row(b_ln_g[i]), _row(b_ln_b[i]), b_ws[i].astype(BF16), bsm),
                     n_lat256, n_l)

        conv_c = jnp.pad(c_conv_w[i], ((0, 8 - CONV_C), (0, 0)))
        dtb = _pad_lanes(c_dt_bias[i]).reshape(2, 1, LANES)
        alog = _pad_lanes(c_a_log[i]).reshape(2, 1, LANES)
        dsk = _row(jnp.repeat(c_d[i], HEAD_DIM_C))
        yc = _mamba(pc, (conv_c, _row(c_conv_b[i]), dtb, alog, dsk, _row(c_norm_g[i])), n_b, n_l)

        yd = _rwkv(pd, (_row(d_mu_prev[i]), _row(d_mu_next[i]), d_w0[i].reshape(2, 1, D_GROUP),
                        d_w2[i].astype(BF16), d_a0[i].reshape(2, 1, D_GROUP), d_a2[i].astype(BF16),
                        d_g2[i].astype(BF16), _row(d_k_k[i]), _row(d_k_a[i]), _row(d_r_k[i]),
                        _row(d_gn_g[i]), _row(d_gn_b[i])), n_b, n_l)

        wo = w_out[i].astype(BF16)
        h, f = _outproj(yab, yc, yd, h, mods, _row(norm_ffn_g[i]),
                        (wo[:512], wo[512:768], wo[768:]), n_lat, per_seq)

        e1, n1, r2, e2 = _topk(f, peer_wq[i].T.astype(BF16), peer_k1[i].astype(BF16),
                               peer_k2[i].astype(BF16))
        n_tok = n_lat if last_layer else n_lat + 1
        h = _peer(f, peer_u[i].astype(BF16), peer_v[i].astype(BF16), e1, n1, r2, e2, h,
                  mods, _row(final_g), n_lat, per_seq, n_tok, last_layer)
    return h.reshape(n_b, t_lat, D_MODEL)
```

```python
import functools
import math

import jax
import jax.numpy as jnp
from jax import lax
from jax.experimental import pallas as pl
from jax.experimental.pallas import tpu as pltpu

F32 = jnp.float32
BF16 = jnp.bfloat16
U32 = jnp.uint32
HIGHEST = lax.Precision.HIGHEST

D_MODEL = 1024
DEPTH = 2
GRID_W = 64
EPS = 1e-6
D_GROUP = 256
CONV_A = 31
CHUNK_B = 128
HEADS_B = 4
HEADS_C = 4
HEAD_DIM_C = 64
STATE_C = 128
CONV_C = 5
CHUNK_C = 128
HEADS_D = 4
HEAD_DIM_D = 64
LORA_W = 64
LORA_A = 64
LORA_G = 128
GN_EPS_D = 64e-5
N_KEYS = 128
N_EXPERTS = N_KEYS * N_KEYS
PEER_HEADS = 8
PEER_DK = 256
PEER_TOPK = 16

C_XBC = D_GROUP + 2 * 2 * STATE_C
OFF_B = 2 * D_GROUP
OFF_C = OFF_B + 2 * D_GROUP
OFF_D = OFF_C + D_GROUP + C_XBC + 2 * HEADS_C
D_COLS = 3 * D_GROUP + 2 * LORA_W + 2 * LORA_A + LORA_G
PC_COLS = D_GROUP + C_XBC + 2 * 128

LANES = 128
SEQ_TILE = 256
TOK_TILE = 512
CHUNK_D = 64
EXP_TILE = 2048
VMEM_LIMIT = 56 * 1024 * 1024


def _cparams(n_axes):
    return pltpu.CompilerParams(dimension_semantics=("arbitrary",) * n_axes,
                                vmem_limit_bytes=VMEM_LIMIT)


def _silu(x):
    return x * jax.nn.sigmoid(x)


def _dot(a, b, dims=None):
    a = a.astype(BF16)
    b = b.astype(BF16)
    if dims is None:
        return jnp.dot(a, b, preferred_element_type=F32)
    return lax.dot_general(a, b, (dims, ((), ())), preferred_element_type=F32)


def _dot_hi(a, b):
    return jnp.dot(a, b, precision=HIGHEST, preferred_element_type=F32)


def _split_bf16(x, terms):
    parts = []
    for _ in range(terms):
        part = x.astype(BF16)
        parts.append(part)
        x = x - part.astype(F32)
    return parts


def _dot_mask_rhs(a, mask, terms=3):
    m = mask.astype(BF16)
    return sum(jnp.dot(p, m, preferred_element_type=F32) for p in _split_bf16(a, terms))


def _dot_mask_lhs(mask, b, terms=3):
    m = mask.astype(BF16)
    return sum(jnp.dot(m, p, preferred_element_type=F32) for p in _split_bf16(b, terms))


NT = ((1,), (1,))
TN = ((0,), (0,))


def _gelu_tanh(x):
    k1 = math.sqrt(2.0 / math.pi)
    k3 = 0.044715 * k1
    hx = 0.5 * x
    return hx + hx * jnp.tanh(x * (k1 + k3 * (x * x)))


def _pack_rows(x):
    return pltpu.bitcast(x.astype(BF16), U32)


def _unpack_rows(x):
    return pltpu.bitcast(x, BF16)


def _dup_bf16(x):
    hi = pltpu.bitcast(x.astype(BF16).astype(F32), U32)
    return hi | (hi >> 16)


def _iota(shape, axis):
    return lax.broadcasted_iota(jnp.int32, shape, axis)


def _head_block_ones(n, width):
    return (_iota((n, n), 0) // width == _iota((n, n), 1) // width).astype(F32)


def _ada_body(cs_ref, w_ref, b_ref, o_ref):
    o_ref[...] = _dot_hi(_silu(cs_ref[...]), w_ref[...]) + b_ref[...]


def _ada(cs, ada_w, ada_b):
    L = ada_w.shape[0]
    nb = 1536
    return pl.pallas_call(
        _ada_body,
        grid=(L, 6 * D_MODEL // nb),
        in_specs=[pl.BlockSpec((8, D_MODEL), lambda l, n: (0, 0)),
                  pl.BlockSpec((None, D_MODEL, nb), lambda l, n: (l, 0, n)),
                  pl.BlockSpec((None, 1, nb), lambda l, n: (l, 0, n))],
        out_specs=pl.BlockSpec((None, 8, nb), lambda l, n: (l, 0, n)),
        out_shape=jax.ShapeDtypeStruct((L, 8, 6 * D_MODEL), F32),
        compiler_params=_cparams(2),
        name="ada",
    )(cs, ada_w, ada_b.reshape(L, 1, 6 * D_MODEL))


def _norm_mod(h, g, shift, scale):
    xn = h * lax.rsqrt(jnp.mean(h * h, axis=-1, keepdims=True) + EPS) * g
    return xn * (1.0 + scale) + shift


def _inproj_tail(h, sh_ref, sc_ref, g_ref, wa, wb, wc, wd, oa, ob, oc, od):
    xm = _norm_mod(h, g_ref[...], sh_ref[...], sc_ref[...]).astype(BF16)
    oa[...] = jnp.dot(xm, wa[...], preferred_element_type=F32)
    ob[...] = jnp.dot(xm, wb[...], preferred_element_type=F32)
    oc[...] = jnp.dot(xm, wc[...], preferred_element_type=F32)
    od[...] = jnp.dot(xm, wd[...], preferred_element_type=F32)


def _inproj_first_body(n_lat, x_ref, ctx_ref, pos_ref, sh_ref, sc_ref, g_ref, wa, wb, wc, wd,
                       oh, oa, ob, oc, od):
    i = pl.program_id(0)
    h = jnp.where(i < n_lat, x_ref[...] + pos_ref[...], ctx_ref[...])
    oh[...] = h
    _inproj_tail(h, sh_ref, sc_ref, g_ref, wa, wb, wc, wd, oa, ob, oc, od)


def _inproj_body(h_ref, sh_ref, sc_ref, g_ref, wa, wb, wc, wd, oa, ob, oc, od):
    _inproj_tail(h_ref[...], sh_ref, sc_ref, g_ref, wa, wb, wc, wd, oa, ob, oc, od)


def _mod_spec(n_lat, per_seq, k):
    def imap(i):
        row = jnp.where(i < n_lat, i // per_seq, 2)
        return (row * 6 + k, 0, 0)
    return pl.BlockSpec((None, 1, D_MODEL), imap)


def _inproj(h_or_parts, mods, g, ws, n_lat, per_seq, first):
    wa, wb, wc, wd = ws
    n_tiles = n_lat + 1
    rows = n_tiles * TOK_TILE
    tile = lambda c: pl.BlockSpec((TOK_TILE, c), lambda i: (i, 0))
    full = lambda a: pl.BlockSpec(a.shape, lambda i: (0,) * a.ndim)
    common_specs = [_mod_spec(n_lat, per_seq, 0), _mod_spec(n_lat, per_seq, 1), full(g),
                    full(wa), full(wb), full(wc), full(wd)]
    out_specs = [tile(512), tile(512), tile(PC_COLS), tile(D_COLS)]
    out_shape = [jax.ShapeDtypeStruct((rows, c), F32) for c in (512, 512, PC_COLS, D_COLS)]
    if first:
        x2, ctx2, pos = h_or_parts
        in_specs = [pl.BlockSpec((TOK_TILE, D_MODEL), lambda i: (jnp.minimum(i, n_lat - 1), 0)),
                    pl.BlockSpec((TOK_TILE, D_MODEL), lambda i: (0, 0)),
                    pl.BlockSpec((TOK_TILE, D_MODEL), lambda i: (i % per_seq, 0))] + common_specs
        return pl.pallas_call(
            functools.partial(_inproj_first_body, n_lat),
            grid=(n_tiles,), in_specs=in_specs,
            out_specs=[tile(D_MODEL)] + out_specs,
            out_shape=[jax.ShapeDtypeStruct((rows, D_MODEL), F32)] + out_shape,
            compiler_params=_cparams(1), name="inproj_first",
        )(x2, ctx2, pos, mods, mods, g, wa, wb, wc, wd)
    return pl.pallas_call(
        _inproj_body, grid=(n_tiles,), in_specs=[tile(D_MODEL)] + common_specs,
        out_specs=out_specs, out_shape=out_shape,
        compiler_params=_cparams(1), name="inproj",
    )(h_or_parts, mods, mods, g, wa, wb, wc, wd)


def _layernorm(x, g, b, eps=1e-5):
    mu = jnp.mean(x, axis=-1, keepdims=True)
    xc = x - mu
    var = jnp.mean(xc * xc, axis=-1, keepdims=True)
    return xc * lax.rsqrt(var + eps) * g + b


def _mixab_body(n_lat, n_l, pa, pa_prev, pa_next, pb, cw, cb, alg, alb, blg, blb, ws, bsm,
                out, ext):
    i = pl.program_id(0)
    is_ctx = i >= n_lat
    tpos = i % n_l
    first = jnp.logical_or(is_ctx, tpos == 0)
    last = jnp.logical_or(is_ctx, tpos == n_l - 1)

    def glu(x):
        return x[:, :D_GROUP] * jax.nn.sigmoid(x[:, D_GROUP:])

    halo = 16
    ext[0:halo, :] = jnp.where(first, 0.0, glu(pa_prev[...]))
    ext[halo:halo + SEQ_TILE, :] = glu(pa[...])
    ext[halo + SEQ_TILE:, :] = jnp.where(last, 0.0, glu(pa_next[...]))
    acc = jnp.zeros((SEQ_TILE, D_GROUP), F32) + cb[...]
    for j in range(CONV_A):
        acc = acc + cw[j:j + 1, :] * ext[pl.ds(halo - (CONV_A - 1) // 2 + j, SEQ_TILE), :]
    out[:, :D_GROUP] = _silu(_layernorm(acc, alg[...], alb[...]))

    x = pb[...]
    u = x[:, :D_GROUP]
    v = _layernorm(x[:, D_GROUP:], blg[...], blb[...]).astype(BF16)
    hd = D_GROUP // HEADS_B
    rows = []
    for c in range(SEQ_TILE // CHUNK_B):
        vc = v[c * CHUNK_B:(c + 1) * CHUNK_B, :]
        heads = [jnp.dot(ws[hh], vc[:, hh * hd:(hh + 1) * hd], preferred_element_type=F32)
                 for hh in range(HEADS_B)]
        rows.append(jnp.concatenate(heads, axis=1) + bsm[...])
    out[:, D_GROUP:] = u * jnp.concatenate(rows, axis=0)


def _mixab(pa, pb, prm, n_lat, n_l):
    rows = pa.shape[0]
    n_tiles = rows // SEQ_TILE
    halo = 16
    per = SEQ_TILE // halo
    full = lambda a: pl.BlockSpec(a.shape, lambda i: (0,) * a.ndim)
    in_specs = [pl.BlockSpec((SEQ_TILE, 512), lambda i: (i, 0)),
                pl.BlockSpec((halo, 512), lambda i: (jnp.maximum(i * per - 1, 0), 0)),
                pl.BlockSpec((halo, 512), lambda i: (jnp.minimum((i + 1) * per, rows // halo - 1), 0)),
                pl.BlockSpec((SEQ_TILE, 512), lambda i: (i, 0))] + [full(a) for a in prm]
    return pl.pallas_call(
        functools.partial(_mixab_body, n_lat, n_l),
        grid=(n_tiles,), in_specs=in_specs,
        out_specs=pl.BlockSpec((SEQ_TILE, 512), lambda i: (i, 0)),
        out_shape=jax.ShapeDtypeStruct((rows, 512), F32),
        scratch_shapes=[pltpu.VMEM((SEQ_TILE + 2 * halo, D_GROUP), F32)],
        compiler_params=_cparams(1), name="mixab",
    )(pa, pa, pa, pb, *prm)


def _scan_tile(n_b, n_l, b, ph, j):
    lat = b * n_l + jnp.where(ph == 0, j - 1, n_l - j)
    return jnp.where(j == 0, n_b * n_l + b, lat)


def _scan_specs(n_b, n_l, rows, cols):
    tid = functools.partial(_scan_tile, n_b, n_l)
    per = SEQ_TILE // 8
    cur = pl.BlockSpec((SEQ_TILE, cols), lambda b, ph, j: (tid(b, ph, j), 0))
    prev = pl.BlockSpec((8, cols), lambda b, ph, j: (jnp.maximum(tid(b, ph, j) * per - 1, 0), 0))
    nxt = pl.BlockSpec((8, cols),
                       lambda b, ph, j: (jnp.minimum((tid(b, ph, j) + 1) * per, rows // 8 - 1), 0))
    return cur, prev, nxt


def _scan_out_spec(n_b, n_l, cols):
    def imap(b, ph, j):
        return (jnp.where(ph == 0, n_b * n_l + b, _scan_tile(n_b, n_l, b, 1, j)), 0)
    return pl.BlockSpec((SEQ_TILE, cols), imap)


def _scan_flags(n_l):
    ph = pl.program_id(1)
    j = pl.program_id(2)
    is_ctx = j == 0
    tpos = jnp.where(ph == 0, j - 1, n_l - j)
    first = jnp.logical_or(is_ctx, tpos == 0)
    last = jnp.logical_or(is_ctx, tpos == n_l - 1)
    slot = jnp.where(is_ctx, 0, tpos + 1)
    return ph, j, first, last, slot


def _mamba_body(n_l, cur_ref, prev_ref, next_ref, dt_ref, cw, cb, dtb, alog, dsk, ng,
                out, ext, yf, st, xbc_s, a_s, dt_s, y_s):
    ph, j, first, last, slot = _scan_flags(n_l)
    fwd = ph == 0

    @pl.when(j == 0)
    def _():
        st[...] = jnp.zeros_like(st)

    zx = D_GROUP
    ext[0:8, :] = jnp.where(first, 0.0, prev_ref[:, zx:])
    ext[8:8 + SEQ_TILE, :] = cur_ref[:, zx:]
    ext[8 + SEQ_TILE:, :] = jnp.where(last, 0.0, next_ref[:, zx:])
    acc = jnp.zeros((SEQ_TILE, C_XBC), F32) + cb[...]
    for jj in range(CONV_C):
        acc = acc + cw[jj:jj + 1, :] * ext[pl.ds(8 - (CONV_C - 1) // 2 + jj, SEQ_TILE), :]
    xbc_s[...] = _silu(acc)
    dt = jax.nn.softplus(dt_ref[...] + dtb[ph])
    dt_s[...] = dt
    a_s[...] = dt * (-jnp.exp(alog[ph]))

    L = CHUNK_C
    r_i = _iota((L, L), 0)
    c_i = _iota((L, L), 1)
    ltri = (r_i >= c_i).astype(F32)
    mask = jnp.where(fwd, r_i - c_i, c_i - r_i) >= 0
    sgn = jnp.where(fwd, 1.0, -1.0)
    hd = HEAD_DIM_C

    def chunk(it, carry):
        q = jnp.where(fwd, it, SEQ_TILE // L - 1 - it)
        rows = pl.ds(pl.multiple_of(q * L, L), L)
        a_q = a_s[rows, :]
        cs = _dot_mask_lhs(ltri, a_q)
        tot = cs[L - 1:L, :]
        e = jnp.where(fwd, cs, cs - a_q)
        e_t = e.T
        dt_q = dt_s[rows, :]
        xbc = xbc_s[rows, :]
        ys = []
        for g in range(2):
            bm = xbc[:, D_GROUP + g * STATE_C:D_GROUP + (g + 1) * STATE_C]
            cm = xbc[:, D_GROUP + 2 * STATE_C + g * STATE_C:D_GROUP + 2 * STATE_C + (g + 1) * STATE_C]
            bm_t = bm.T.astype(BF16)
            cm_b = cm.astype(BF16)
            gmat = jnp.dot(cm_b, bm_t, preferred_element_type=F32)
            for h in (2 * g, 2 * g + 1):
                ecol = e[:, h:h + 1]
                erow = e_t[h:h + 1, :]
                totc = tot[:, h:h + 1]
                xdt = xbc[:, h * hd:(h + 1) * hd] * dt_q[:, h:h + 1]
                lmat = jnp.exp(jnp.where(mask, sgn * (ecol - erow), -1e30))
                offs = jnp.exp(jnp.where(fwd, ecol, totc - ecol))
                stw = jnp.exp(jnp.where(fwd, totc - ecol, ecol))
                s_prev = st[h]
                y = _dot(gmat * lmat, xdt) + offs * _dot(cm_b, s_prev)
                st[h] = jnp.exp(totc) * s_prev + _dot(bm_t, xdt * stw)
                ys.append(y)
        y_s[rows, :] = jnp.concatenate(ys, axis=1)
        return carry

    lax.fori_loop(0, SEQ_TILE // L, chunk, 0)

    @pl.when(fwd)
    def _():
        yf[slot] = y_s[...]

    @pl.when(ph == 1)
    def _():
        y = yf[slot] + y_s[...] + dsk[...] * xbc_s[:, :D_GROUP]
        t = y * _silu(cur_ref[:, :D_GROUP])
        out[...] = t * lax.rsqrt(jnp.mean(t * t, axis=-1, keepdims=True) + EPS) * ng[...]


def _mamba(pc, prm, n_b, n_l):
    rows = pc.shape[0]
    cur, prev, nxt = _scan_specs(n_b, n_l, rows, D_GROUP + C_XBC)
    tid = functools.partial(_scan_tile, n_b, n_l)
    dt_spec = pl.BlockSpec((SEQ_TILE, LANES),
                           lambda b, ph, j: (tid(b, ph, j), (D_GROUP + C_XBC) // LANES + ph))
    full = lambda a: pl.BlockSpec(a.shape, lambda b, ph, j: (0,) * a.ndim)
    return pl.pallas_call(
        functools.partial(_mamba_body, n_l),
        grid=(n_b, 2, n_l + 1),
        in_specs=[cur, prev, nxt, dt_spec] + [full(a) for a in prm],
        out_specs=_scan_out_spec(n_b, n_l, D_GROUP),
        out_shape=jax.ShapeDtypeStruct((rows, D_GROUP), F32),
        scratch_shapes=[pltpu.VMEM((SEQ_TILE + 16, C_XBC), F32),
                        pltpu.VMEM((n_l + 1, SEQ_TILE, D_GROUP), F32),
                        pltpu.VMEM((HEADS_C, STATE_C, HEAD_DIM_C), F32),
                        pltpu.VMEM((SEQ_TILE, C_XBC), F32),
                        pltpu.VMEM((SEQ_TILE, LANES), F32),
                        pltpu.VMEM((SEQ_TILE, LANES), F32),
                        pltpu.VMEM((SEQ_TILE, D_GROUP), F32)],
        compiler_params=_cparams(3), name="mamba",
    )(pc, pc, pc, pc, *prm)


def _rwkv_body(n_l, cur_ref, prev_ref, next_ref, mup, mun, w0, w2, a0, a2, g2, kkw, kaw, rkw,
               gng, gnb, out, yf, st, r_s, v_s, kap_s, alp_s, kd_s, lw_s, y_s,
               phi_s, psi_s, sin_s, left_s, amat_s, ara_s, tinv_s):
    ph, j, first, last, slot = _scan_flags(n_l)
    fwd = ph == 0

    @pl.when(j == 0)
    def _():
        st[...] = jnp.zeros_like(st)

    cur = cur_ref[...]
    prv = jnp.concatenate([jnp.where(first, 0.0, prev_ref[7:8, :]), cur[:SEQ_TILE - 1, :]], axis=0)
    nxt = jnp.concatenate([cur[1:, :], jnp.where(last, 0.0, next_ref[0:1, :])], axis=0)
    p = cur + mup[...] * (prv - cur) + mun[...] * (nxt - cur)
    G = D_GROUP
    r = p[:, :G]
    k = p[:, G:2 * G]
    v = p[:, 2 * G:3 * G]
    blk = _head_block_ones(G, HEAD_DIM_D)
    kkr = k * kkw[...]
    kk = kkr * lax.rsqrt(_dot_mask_rhs(kkr * kkr, blk) + 1e-12)

    def rate(d_static=None):
        if d_static is None:
            ad = jnp.where(fwd, p[:, 3 * G + 2 * LORA_W:3 * G + 2 * LORA_W + LORA_A],
                           p[:, 3 * G + 2 * LORA_W + LORA_A:3 * G + 2 * LORA_W + 2 * LORA_A])
            a = jax.nn.sigmoid(a0[ph] + _dot(ad, a2[ph]))
        else:
            o = 3 * G + 2 * LORA_W + d_static * LORA_A
            a = jax.nn.sigmoid(a0[d_static] + _dot(p[:, o:o + LORA_A], a2[d_static]))
        return a, k * (1.0 + (a - 1.0) * kaw[...])

    a_d, kd_d = rate()
    wd = jnp.where(fwd, p[:, 3 * G:3 * G + LORA_W], p[:, 3 * G + LORA_W:3 * G + 2 * LORA_W])
    w = w0[ph] + _dot(jnp.tanh(wd), w2[ph])
    lw_s[...] = -math.exp(-0.5) * jax.nn.sigmoid(w)
    r_s[...] = r
    v_s[...] = v
    kap_s[...] = kk
    alp_s[...] = a_d * kk
    kd_s[...] = kd_d

    C = CHUNK_D
    r_i = _iota((C, C), 0)
    c_i = _iota((C, C), 1)
    lag = jnp.where(fwd, r_i - c_i, c_i - r_i)
    tri = lag >= 0
    tri_f = tri.astype(F32)
    strict = lag > 0
    eye = (r_i == c_i).astype(F32)
    hd = HEAD_DIM_D

    n_q = SEQ_TILE // C
    eye_k = (_iota((hd, hd), 0) == _iota((hd, hd), 1)).astype(F32)

    units = [(q, h) for q in range(n_q) for h in range(HEADS_D)]
    rows_of = lambda q: slice(q * C, (q + 1) * C)
    lanes_of = lambda h: slice(h * hd, (h + 1) * hd)
    left, right, p_tot = {}, {}, {}
    for q in range(n_q):
        rows = rows_of(q)
        lw = lw_s[rows, :]
        incl = _dot_mask_lhs(tri_f, lw)
        tot = jnp.where(fwd, incl[C - 1:C, :], incl[0:1, :])
        p_inv = jnp.exp(-incl)
        kap_h = kap_s[rows, :] * jnp.exp(incl - lw)
        r_h = r_s[rows, :] * jnp.exp(incl)
        alp_b = alp_s[rows, :] * p_inv
        k_b = kd_s[rows, :] * p_inv
        ptq = jnp.exp(tot)
        for h in range(HEADS_D):
            sl = lanes_of(h)
            left[q, h] = jnp.concatenate([kap_h[:, sl], r_h[:, sl]], axis=0)
            right[q, h] = jnp.concatenate([alp_b[:, sl], k_b[:, sl]], axis=0)
            p_tot[q, h] = ptq[:, sl]
            left_s[q * HEADS_D + h] = left[q, h]
    m1 = {u: _dot(left[u], right[u], NT) for u in units}
    a_vk, pw, tinv = {}, {}, {}
    for u in units:
        qh = u[0] * HEADS_D + u[1]
        a_vk[u] = jnp.where(strict, m1[u][:C, C:], 0.0)
        amat_s[qh] = jnp.concatenate([a_vk[u], jnp.where(tri, m1[u][C:, C:], 0.0)], axis=0)
        ara_s[qh] = jnp.where(tri, m1[u][C:, :C], 0.0)
        pw[u] = jnp.where(strict, -m1[u][:C, :C], 0.0)
        tinv[u] = eye + pw[u]
    for _ in range(5):
        pw = {u: _dot(pw[u], pw[u]) for u in units}
        upd = {u: _dot(tinv[u], pw[u]) for u in units}
        tinv = {u: tinv[u] + upd[u] for u in units}
    wmat = {u: _dot(tinv[u], right[u][:C], TN) for u in units}
    kw = {u: _dot(left[u][:C], wmat[u], TN) for u in units}
    aw = {u: _dot(a_vk[u], wmat[u], TN) for u in units}
    vk = {u: _dot(v_s[rows_of(u[0]), lanes_of(u[1])], right[u][C:] - aw[u], TN) for u in units}
    for u in units:
        qh = u[0] * HEADS_D + u[1]
        phi_s[qh] = (eye_k - kw[u]) * p_tot[u]
        psi_s[qh] = vk[u] * p_tot[u]
        tinv_s[qh] = tinv[u]

    for it in range(n_q):
        q = jnp.where(fwd, it, n_q - 1 - it)
        for h in range(HEADS_D):
            qh = q * HEADS_D + h
            s_in = st[h]
            sin_s[qh] = s_in
            st[h] = _dot(s_in, phi_s[qh]) + psi_s[qh]

    qh_of = lambda u: u[0] * HEADS_D + u[1]
    x0 = {u: _dot(left_s[qh_of(u)], sin_s[qh_of(u)], NT) for u in units}
    x1 = {u: _dot(amat_s[qh_of(u)], v_s[rows_of(u[0]), lanes_of(u[1])]) for u in units}
    uu = {u: _dot(tinv_s[qh_of(u)], x0[u][:C] + x1[u][:C]) for u in units}
    au = {u: _dot(ara_s[qh_of(u)], uu[u]) for u in units}
    for q in range(n_q):
        y_s[rows_of(q), :] = jnp.concatenate(
            [x0[q, h][C:] + x1[q, h][C:] - au[q, h] for h in range(HEADS_D)], axis=1)

    @pl.when(fwd)
    def _():
        yf[slot] = y_s[...]

    @pl.when(ph == 1)
    def _():
        y = yf[slot] + y_s[...]
        inv = 1.0 / HEAD_DIM_D
        mu = _dot_mask_rhs(y, blk) * inv
        yc = y - mu
        var = _dot_mask_rhs(yc * yc, blk) * inv
        yn = yc * lax.rsqrt(var + GN_EPS_D) * gng[...] + gnb[...]
        _, kd_f = rate(0)
        bonus = _dot_mask_rhs(r * (kd_f + kd_d) * rkw[...], blk) * v
        gate = _dot(jax.nn.sigmoid(p[:, 3 * G + 2 * LORA_W + 2 * LORA_A:]), g2[...])
        out[...] = (yn + bonus) * gate


def _rwkv(pd, prm, n_b, n_l):
    rows = pd.shape[0]
    cur, prev, nxt = _scan_specs(n_b, n_l, rows, D_COLS)
    full = lambda a: pl.BlockSpec(a.shape, lambda b, ph, j: (0,) * a.ndim)
    tile = lambda: pltpu.VMEM((SEQ_TILE, D_GROUP), F32)
    per_qh = lambda r, c: pltpu.VMEM((SEQ_TILE // CHUNK_D * HEADS_D, r, c), F32)
    return pl.pallas_call(
        functools.partial(_rwkv_body, n_l),
        grid=(n_b, 2, n_l + 1),
        in_specs=[cur, prev, nxt] + [full(a) for a in prm],
        out_specs=_scan_out_spec(n_b, n_l, D_GROUP),
        out_shape=jax.ShapeDtypeStruct((rows, D_GROUP), F32),
        scratch_shapes=[pltpu.VMEM((n_l + 1, SEQ_TILE, D_GROUP), F32),
                        pltpu.VMEM((HEADS_D, HEAD_DIM_D, HEAD_DIM_D), F32)] + [tile() for _ in range(7)]
        + [per_qh(HEAD_DIM_D, HEAD_DIM_D) for _ in range(3)]
        + [per_qh(2 * CHUNK_D, HEAD_DIM_D), per_qh(2 * CHUNK_D, CHUNK_D),
           per_qh(CHUNK_D, CHUNK_D), per_qh(CHUNK_D, CHUNK_D)],
        compiler_params=_cparams(3), name="rwkv",
    )(pd, pd, pd, *prm)


def _outproj_body(yab, yc, yd, h_ref, gate, sh, sc, g, wab, wc, wd, oh, of):
    mix = (_dot(yab[...], wab[...]) + _dot(yc[...], wc[...]) + _dot(yd[...], wd[...]))
    h = h_ref[...] + gate[...] * mix
    oh[...] = h
    of[...] = _norm_mod(h, g[...], sh[...], sc[...]).astype(BF16)


def _outproj(yab, yc, yd, h, mods, g, ws, n_lat, per_seq):
    rows = h.shape[0]
    tile = lambda c: pl.BlockSpec((TOK_TILE, c), lambda i: (i, 0))
    full = lambda a: pl.BlockSpec(a.shape, lambda i: (0,) * a.ndim)
    return pl.pallas_call(
        _outproj_body, grid=(rows // TOK_TILE,),
        in_specs=[tile(512), tile(D_GROUP), tile(D_GROUP), tile(D_MODEL),
                  _mod_spec(n_lat, per_seq, 2), _mod_spec(n_lat, per_seq, 3),
                  _mod_spec(n_lat, per_seq, 4), full(g)] + [full(a) for a in ws],
        out_specs=[tile(D_MODEL), tile(D_MODEL)],
        out_shape=[jax.ShapeDtypeStruct((rows, D_MODEL), F32),
                   jax.ShapeDtypeStruct((rows, D_MODEL), BF16)],
        compiler_params=_cparams(1), name="outproj",
    )(yab, yc, yd, h, mods, mods, mods, g, *ws)


TOPK_TILE = 512


SUBLANES = 8


def _cmpx(lst, i, j):
    a, b = lst[i], lst[j]
    lst[i] = jnp.maximum(a, b)
    lst[j] = jnp.minimum(a, b)


def _bitonic_sort_desc(lst):
    n = len(lst)
    k = 2
    while k <= n:
        j = k // 2
        while j >= 1:
            for i in range(n):
                p = i ^ j
                if p > i:
                    if (i & k) == 0:
                        _cmpx(lst, i, p)
                    else:
                        _cmpx(lst, p, i)
            j //= 2
        k *= 2


def _bitonic_merge_desc(lst):
    n = len(lst)
    j = n // 2
    while j >= 1:
        for i in range(n):
            p = i ^ j
            if p > i:
                _cmpx(lst, i, p)
        j //= 2


def _merge_top(a, b):
    n = len(a)
    c = [jnp.maximum(a[i], b[n - 1 - i]) for i in range(n)]
    _bitonic_merge_desc(c)
    return c


def _merge_sublanes(lst):
    for shift in (4, 2, 1):
        lst = _merge_top(lst, [pltpu.roll(a, shift, axis=0) for a in lst])
    return lst


def _count_leading(pred, t):
    sel = jnp.where
    c1 = pred(t[7])
    c2 = pred(sel(c1, t[11], t[3]))
    c3 = pred(sel(c1, sel(c2, t[13], t[9]), sel(c2, t[5], t[1])))
    c4 = pred(sel(c1, sel(c2, sel(c3, t[14], t[12]), sel(c3, t[10], t[8])),
                  sel(c2, sel(c3, t[6], t[4]), sel(c3, t[2], t[0]))))
    c5 = pred(t[15])
    return (sel(c1, 8.0, 0.0) + sel(c2, 4.0, 0.0) + sel(c3, 2.0, 0.0) + sel(c4, 1.0, 0.0)
            + sel(c5, 1.0, 0.0))


def _top16_rows(s):
    lst = [s[SUBLANES * v:SUBLANES * (v + 1), :] for v in range(s.shape[0] // SUBLANES)]
    _bitonic_sort_desc(lst)
    return _merge_sublanes(lst)


def _topk_body(f_ref, wq_ref, k1_ref, k2_ref, e1_o, n_o, r2_o, e2_o, q_s):
    q_s[...] = lax.dot_general(wq_ref[...], f_ref[...], (NT, ((), ())),
                               preferred_element_type=F32).astype(BF16)
    half = PEER_DK // 2
    T = f_ref.shape[0]
    sub = _iota((SUBLANES, T), 0)

    def stack(rows):
        out = rows[SUBLANES - 1]
        for b in range(SUBLANES - 2, -1, -1):
            out = jnp.where(sub == b, rows[b], out)
        return out

    def rep(a):
        return jnp.concatenate([a] * (N_KEYS // SUBLANES), axis=0)

    def head(h, carry):
        base = pl.multiple_of(h * PEER_DK, PEER_DK)
        s1 = jnp.dot(k1_ref[...], q_s[pl.ds(base, half), :], preferred_element_type=F32)
        s2 = jnp.dot(k2_ref[...], q_s[pl.ds(base + half, half), :], preferred_element_type=F32)
        t1 = _top16_rows(s1)
        t2 = _top16_rows(s2)
        lo = stack(t2[:SUBLANES])
        hi = stack(t2[SUBLANES:])
        top = _merge_sublanes(_merge_top([t + lo for t in t1], [t + hi for t in t1]))
        z = jnp.exp(top[0] - top[0])
        for kk in range(1, PEER_TOPK):
            z = z + jnp.exp(top[kk] - top[0])
        thr = rep(top[PEER_TOPK - 1])
        t2r = [rep(t) for t in t2]
        r2 = _count_leading(lambda t: t > s2, t2r)
        n1 = _count_leading(lambda t: s1 + t >= thr, t2r)
        r2_o[h] = _pack_rows(r2)
        n_o[h] = _dup_bf16(n1)
        e1_o[h] = _dup_bf16(jnp.exp(s1 - rep(t1[0])) / rep(z))
        e2_o[h] = _pack_rows(jnp.exp(s2 - rep(t2[0])))
        return carry

    lax.fori_loop(0, PEER_HEADS, head, 0)


def _topk(f, wq_t, k1, k2):
    rows = f.shape[0]
    T = TOPK_TILE
    full = lambda a: pl.BlockSpec(a.shape, lambda i: (0,) * a.ndim)
    big = pl.BlockSpec((PEER_HEADS, N_KEYS, T), lambda i: (0, 0, i))
    big_shape = lambda dt: jax.ShapeDtypeStruct((PEER_HEADS, N_KEYS, rows), dt)
    packed = pl.BlockSpec((PEER_HEADS, N_KEYS // 2, T), lambda i: (0, 0, i))
    packed_shape = jax.ShapeDtypeStruct((PEER_HEADS, N_KEYS // 2, rows), U32)
    return pl.pallas_call(
        _topk_body, grid=(rows // T,),
        in_specs=[pl.BlockSpec((T, D_MODEL), lambda i: (i, 0)), full(wq_t), full(k1), full(k2)],
        out_specs=[big, big, packed, packed],
        out_shape=[big_shape(U32), big_shape(U32), packed_shape, packed_shape],
        scratch_shapes=[pltpu.VMEM((PEER_HEADS * PEER_DK, T), BF16)],
        compiler_params=_cparams(1), name="peer_topk",
    )(f, wq_t, k1, k2)


def _peer_body(final, n_e, f_ref, u_ref, v_ref, e1_ref, n_ref, r2_ref, e2_ref, h_ref,
               gate_ref, fg_ref, out, st_s, at_s, acc):
    s = pl.program_id(0)
    j_up = jnp.maximum(s - 1, 0) % n_e
    cur = s % 2

    @pl.when(s == 0)
    def _():
        at_s[...] = jnp.zeros_like(at_s)

    @pl.when(j_up == 0)
    def _():
        acc[...] = jnp.zeros_like(acc)

    st_s[...] = lax.dot_general(u_ref[...], f_ref[...], (NT, ((), ())),
                                preferred_element_type=F32)
    acc[...] += lax.dot_general(_unpack_rows(at_s[1 - cur]), v_ref[...], (TN, ((), ())),
                                preferred_element_type=F32)

    blk = (N_KEYS // 2, LANES)
    group = 2
    for lg in range(TOK_TILE // LANES):
        ln = slice(lg * LANES, (lg + 1) * LANES)
        for i0 in range(0, EXP_TILE // N_KEYS, group):
            g = [None] * group
            for h in range(PEER_HEADS):
                r2_blk = _unpack_rows(r2_ref[h, :, ln])
                e2_blk = _unpack_rows(e2_ref[h, :, ln])
                for r in range(group):
                    ii = i0 + r
                    n_row = _unpack_rows(jnp.broadcast_to(n_ref[h, ii:ii + 1, ln], blk))
                    e1_row = _unpack_rows(jnp.broadcast_to(e1_ref[h, ii:ii + 1, ln], blk))
                    w = jnp.where(r2_blk < n_row, e2_blk, 0.0) * e1_row
                    g[r] = w if g[r] is None else g[r] + w
            for r in range(group):
                ii = i0 + r
                rows = slice(ii * N_KEYS, (ii + 1) * N_KEYS)
                at_s[cur, ii * N_KEYS // 2:(ii + 1) * N_KEYS // 2, ln] = _pack_rows(
                    _gelu_tanh(st_s[rows, ln]).astype(BF16) * g[r])

    @pl.when(jnp.logical_and(s >= 1, j_up == n_e - 1))
    def _():
        h = h_ref[...] + gate_ref[...] * acc[...]
        if final:
            h = h * lax.rsqrt(jnp.mean(h * h, axis=-1, keepdims=True) + EPS) * fg_ref[...]
        out[...] = h


def _peer(f, u_b, v_b, e1, n1, r2, e2, h, mods, fg, n_lat, per_seq, n_tok_tiles, final):
    n_e = N_EXPERTS // EXP_TILE
    rpt = EXP_TILE // N_KEYS
    n_steps = n_tok_tiles * n_e

    def tile(s, lag):
        t = jnp.clip(s - lag, 0, n_steps - 1)
        return t // n_e, t % n_e

    tok = lambda lag: pl.BlockSpec((TOK_TILE, D_MODEL), lambda s: (tile(s, lag)[0], 0))
    exp = lambda lag: pl.BlockSpec((EXP_TILE, D_MODEL), lambda s: (tile(s, lag)[1], 0))
    rowblk = pl.BlockSpec((PEER_HEADS, rpt, TOK_TILE), lambda s: (0, tile(s, 0)[1], tile(s, 0)[0]))
    allkeys = pl.BlockSpec((PEER_HEADS, N_KEYS // 2, TOK_TILE), lambda s: (0, 0, tile(s, 0)[0]))

    def gate_map(s):
        i = tile(s, 1)[0]
        row = jnp.where(i < n_lat, i // per_seq, 2)
        return (row * 6 + 5, 0, 0)

    return pl.pallas_call(
        functools.partial(_peer_body, final, n_e),
        grid=(n_steps + 1,),
        in_specs=[tok(0), exp(0), exp(1), rowblk, rowblk, allkeys, allkeys, tok(1),
                  pl.BlockSpec((None, 1, D_MODEL), gate_map),
                  pl.BlockSpec((1, D_MODEL), lambda s: (0, 0))],
        out_specs=tok(1),
        out_shape=jax.ShapeDtypeStruct((n_tok_tiles * TOK_TILE, D_MODEL), F32),
        scratch_shapes=[pltpu.VMEM((EXP_TILE, TOK_TILE), F32),
                        pltpu.VMEM((2, EXP_TILE // 2, TOK_TILE), U32),
                        pltpu.VMEM((TOK_TILE, D_MODEL), F32)],
        compiler_params=_cparams(1), name="peer",
    )(f, u_b, v_b, e1, n1, r2, e2, h, mods, fg)


def _sincos_2d(t_len):
    rows = t_len // GRID_W
    q = D_MODEL // 4
    freq = 10000.0 ** (-jnp.arange(q, dtype=F32) / q)
    ar = jnp.arange(rows, dtype=F32)[:, None] * freq
    ac = jnp.arange(GRID_W, dtype=F32)[:, None] * freq
    per_row = lambda a: jnp.repeat(a, GRID_W, axis=0)
    per_col = lambda a: jnp.tile(a, (rows, 1))
    return jnp.concatenate([per_row(jnp.sin(ar)), per_row(jnp.cos(ar)),
                            per_col(jnp.sin(ac)), per_col(jnp.cos(ac))], -1)


def _row(a):
    return a.reshape(1, -1).astype(F32)


def _pad_lanes(a, width=LANES):
    return jnp.pad(a, ((0, 0), (0, width - a.shape[-1])))


def kernel(x, c, ctx, c_ctx, ada_w, ada_b, norm_mix_g, norm_ffn_g, w_in, w_out, a_conv_w, a_conv_b, a_ln_g, a_ln_b, b_ln_g, b_ln_b, b_ws, b_bs, c_conv_w, c_conv_b, c_dt_bias, c_a_log, c_d, c_norm_g, d_mu_prev, d_mu_next, d_w0, d_w2, d_a0, d_a2, d_g2, d_k_k, d_k_a, d_r_k, d_gn_g, d_gn_b, peer_wq, peer_k1, peer_k2, peer_u, peer_v, final_g):
    n_b, t_lat, _ = x.shape
    t_ctx = ctx.shape[1]
    assert n_b == 2 and t_ctx == SEQ_TILE and t_lat % TOK_TILE == 0
    n_l = t_lat // SEQ_TILE
    per_seq = t_lat // TOK_TILE
    n_lat = n_b * per_seq
    n_lat256 = n_b * n_l

    cs = jnp.zeros((8, D_MODEL), F32).at[:n_b].set(c).at[n_b].set(c_ctx)
    mods_all = _ada(cs, ada_w, ada_b)
    pos = _sincos_2d(t_lat)
    x2 = x.reshape(n_b * t_lat, D_MODEL)
    ctx2 = ctx.reshape(n_b * t_ctx, D_MODEL)

    h = None
    for i in range(DEPTH):
        last_layer = i == DEPTH - 1
        mods = mods_all[i].reshape(8 * 6, 1, D_MODEL)
        w = w_in[i]
        dtc = OFF_C + D_GROUP + C_XBC
        w_c = jnp.concatenate([w[:, OFF_C:dtc], _pad_lanes(w[:, dtc:dtc + HEADS_C]),
                               _pad_lanes(w[:, dtc + HEADS_C:dtc + 2 * HEADS_C])], axis=1)
        ws_in = tuple(a.astype(BF16) for a in (w[:, :OFF_B], w[:, OFF_B:OFF_C], w_c, w[:, OFF_D:]))
        g_mix = _row(norm_mix_g[i])
        if i == 0:
            h, pa, pb, pc, pd = _inproj((x2, ctx2, pos), mods, g_mix, ws_in, n_lat, per_seq, True)
        else:
            pa, pb, pc, pd = _inproj(h, mods, g_mix, ws_in, n_lat, per_seq, False)

        conv_a = jnp.pad(a_conv_w[i], ((0, 32 - CONV_A), (0, 0)))
        bsm = jnp.repeat(b_bs[i].T, D_GROUP // HEADS_B, axis=1)
        yab = _mixab(pa, pb, (conv_a, _row(a_conv_b[i]), _row(a_ln_g[i]), _row(a_ln_b[i]),
                              _row(b_ln_g[i]), _row(b_ln_b[i]), b_ws[i].astype(BF16), bsm),
                     n_lat256, n_l)

        conv_c = jnp.pad(c_conv_w[i], ((0, 8 - CONV_C), (0, 0)))
        dtb = _pad_lanes(c_dt_bias[i]).reshape(2, 1, LANES)
        alog = _pad_lanes(c_a_log[i]).reshape(2, 1, LANES)
        dsk = _row(jnp.repeat(c_d[i], HEAD_DIM_C))
        yc = _mamba(pc, (conv_c, _row(c_conv_b[i]), dtb, alog, dsk, _row(c_norm_g[i])), n_b, n_l)

        yd = _rwkv(pd, (_row(d_mu_prev[i]), _row(d_mu_next[i]), d_w0[i].reshape(2, 1, D_GROUP),
                        d_w2[i].astype(BF16), d_a0[i].reshape(2, 1, D_GROUP), d_a2[i].astype(BF16),
                        d_g2[i].astype(BF16), _row(d_k_k[i]), _row(d_k_a[i]), _row(d_r_k[i]),
                        _row(d_gn_g[i]), _row(d_gn_b[i])), n_b, n_l)

        wo = w_out[i].astype(BF16)
        h, f = _outproj(yab, yc, yd, h, mods, _row(norm_ffn_g[i]),
                        (wo[:512], wo[512:768], wo[768:]), n_lat, per_seq)

        e1, n1, r2, e2 = _topk(f, peer_wq[i].T.astype(BF16), peer_k1[i].astype(BF16),
                               peer_k2[i].astype(BF16))
        n_tok = n_lat if last_layer else n_lat + 1
        h = _peer(f, peer_u[i].astype(BF16), peer_v[i].astype(BF16), e1, n1, r2, e2, h,
                  mods, _row(final_g), n_lat, per_seq, n_tok, last_layer)
    return h.reshape(n_b, t_lat, D_MODEL)
```

```python
import functools
import math

import jax
import jax.numpy as jnp
from jax import lax
from jax.experimental import pallas as pl
from jax.experimental.pallas import tpu as pltpu

F32 = jnp.float32
BF16 = jnp.bfloat16
U32 = jnp.uint32
HIGHEST = lax.Precision.HIGHEST

D_MODEL = 1024
DEPTH = 2
GRID_W = 64
EPS = 1e-6
D_GROUP = 256
CONV_A = 31
CHUNK_B = 128
HEADS_B = 4
HEADS_C = 4
HEAD_DIM_C = 64
STATE_C = 128
CONV_C = 5
CHUNK_C = 128
HEADS_D = 4
HEAD_DIM_D = 64
LORA_W = 64
LORA_A = 64
LORA_G = 128
GN_EPS_D = 64e-5
N_KEYS = 128
N_EXPERTS = N_KEYS * N_KEYS
PEER_HEADS = 8
PEER_DK = 256
PEER_TOPK = 16

C_XBC = D_GROUP + 2 * 2 * STATE_C
OFF_B = 2 * D_GROUP
OFF_C = OFF_B + 2 * D_GROUP
OFF_D = OFF_C + D_GROUP + C_XBC + 2 * HEADS_C
D_COLS = 3 * D_GROUP + 2 * LORA_W + 2 * LORA_A + LORA_G
PC_COLS = D_GROUP + C_XBC + 2 * 128

LANES = 128
SEQ_TILE = 256
TOK_TILE = 512
CHUNK_D = 64
EXP_TILE = 2048
VMEM_LIMIT = 56 * 1024 * 1024


def _cparams(n_axes):
    return pltpu.CompilerParams(dimension_semantics=("arbitrary",) * n_axes,
                                vmem_limit_bytes=VMEM_LIMIT)


def _silu(x):
    return x * jax.nn.sigmoid(x)


def _dot(a, b, dims=None):
    a = a.astype(BF16)
    b = b.astype(BF16)
    if dims is None:
        return jnp.dot(a, b, preferred_element_type=F32)
    return lax.dot_general(a, b, (dims, ((), ())), preferred_element_type=F32)


def _dot_hi(a, b):
    return jnp.dot(a, b, precision=HIGHEST, preferred_element_type=F32)


def _split_bf16(x, terms):
    parts = []
    for _ in range(terms):
        part = x.astype(BF16)
        parts.append(part)
        x = x - part.astype(F32)
    return parts


def _dot_mask_rhs(a, mask, terms=3):
    m = mask.astype(BF16)
    return sum(jnp.dot(p, m, preferred_element_type=F32) for p in _split_bf16(a, terms))


def _dot_mask_lhs(mask, b, terms=3):
    m = mask.astype(BF16)
    return sum(jnp.dot(m, p, preferred_element_type=F32) for p in _split_bf16(b, terms))


NT = ((1,), (1,))
TN = ((0,), (0,))


def _gelu_tanh(x):
    k1 = math.sqrt(2.0 / math.pi)
    k3 = 0.044715 * k1
    hx = 0.5 * x
    return hx + hx * jnp.tanh(x * (k1 + k3 * (x * x)))


def _pack_rows(x):
    return pltpu.bitcast(x.astype(BF16), U32)


def _unpack_rows(x):
    return pltpu.bitcast(x, BF16)


def _dup_bf16(x):
    hi = pltpu.bitcast(x.astype(BF16).astype(F32), U32)
    return hi | (hi >> 16)


def _iota(shape, axis):
    return lax.broadcasted_iota(jnp.int32, shape, axis)


def _head_block_ones(n, width):
    return (_iota((n, n), 0) // width == _iota((n, n), 1) // width).astype(F32)


def _ada_body(cs_ref, w_ref, b_ref, o_ref):
    o_ref[...] = _dot_hi(_silu(cs_ref[...]), w_ref[...]) + b_ref[...]


def _ada(cs, ada_w, ada_b):
    L = ada_w.shape[0]
    nb = 1536
    return pl.pallas_call(
        _ada_body,
        grid=(L, 6 * D_MODEL // nb),
        in_specs=[pl.BlockSpec((8, D_MODEL), lambda l, n: (0, 0)),
                  pl.BlockSpec((None, D_MODEL, nb), lambda l, n: (l, 0, n)),
                  pl.BlockSpec((None, 1, nb), lambda l, n: (l, 0, n))],
        out_specs=pl.BlockSpec((None, 8, nb), lambda l, n: (l, 0, n)),
        out_shape=jax.ShapeDtypeStruct((L, 8, 6 * D_MODEL), F32),
        compiler_params=_cparams(2),
        name="ada",
    )(cs, ada_w, ada_b.reshape(L, 1, 6 * D_MODEL))


def _norm_mod(h, g, shift, scale):
    xn = h * lax.rsqrt(jnp.mean(h * h, axis=-1, keepdims=True) + EPS) * g
    return xn * (1.0 + scale) + shift


def _inproj_tail(h, sh_ref, sc_ref, g_ref, wa, wb, wc, wd, oa, ob, oc, od):
    xm = _norm_mod(h, g_ref[...], sh_ref[...], sc_ref[...]).astype(BF16)
    oa[...] = jnp.dot(xm, wa[...], preferred_element_type=F32)
    ob[...] = jnp.dot(xm, wb[...], preferred_element_type=F32)
    oc[...] = jnp.dot(xm, wc[...], preferred_element_type=F32)
    od[...] = jnp.dot(xm, wd[...], preferred_element_type=F32)


def _inproj_first_body(n_lat, x_ref, ctx_ref, pos_ref, sh_ref, sc_ref, g_ref, wa, wb, wc, wd,
                       oh, oa, ob, oc, od):
    i = pl.program_id(0)
    h = jnp.where(i < n_lat, x_ref[...] + pos_ref[...], ctx_ref[...])
    oh[...] = h
    _inproj_tail(h, sh_ref, sc_ref, g_ref, wa, wb, wc, wd, oa, ob, oc, od)


def _inproj_body(h_ref, sh_ref, sc_ref, g_ref, wa, wb, wc, wd, oa, ob, oc, od):
    _inproj_tail(h_ref[...], sh_ref, sc_ref, g_ref, wa, wb, wc, wd, oa, ob, oc, od)


def _mod_spec(n_lat, per_seq, k):
    def imap(i):
        row = jnp.where(i < n_lat, i // per_seq, 2)
        return (row * 6 + k, 0, 0)
    return pl.BlockSpec((None, 1, D_MODEL), imap)


def _inproj(h_or_parts, mods, g, ws, n_lat, per_seq, first):
    wa, wb, wc, wd = ws
    n_tiles = n_lat + 1
    rows = n_tiles * TOK_TILE
    tile = lambda c: pl.BlockSpec((TOK_TILE, c), lambda i: (i, 0))
    full = lambda a: pl.BlockSpec(a.shape, lambda i: (0,) * a.ndim)
    common_specs = [_mod_spec(n_lat, per_seq, 0), _mod_spec(n_lat, per_seq, 1), full(g),
                    full(wa), full(wb), full(wc), full(wd)]
    out_specs = [tile(512), tile(512), tile(PC_COLS), tile(D_COLS)]
    out_shape = [jax.ShapeDtypeStruct((rows, c), F32) for c in (512, 512, PC_COLS, D_COLS)]
    if first:
        x2, ctx2, pos = h_or_parts
        in_specs = [pl.BlockSpec((TOK_TILE, D_MODEL), lambda i: (jnp.minimum(i, n_lat - 1), 0)),
                    pl.BlockSpec((TOK_TILE, D_MODEL), lambda i: (0, 0)),
                    pl.BlockSpec((TOK_TILE, D_MODEL), lambda i: (i % per_seq, 0))] + common_specs
        return pl.pallas_call(
            functools.partial(_inproj_first_body, n_lat),
            grid=(n_tiles,), in_specs=in_specs,
            out_specs=[tile(D_MODEL)] + out_specs,
            out_shape=[jax.ShapeDtypeStruct((rows, D_MODEL), F32)] + out_shape,
            compiler_params=_cparams(1), name="inproj_first",
        )(x2, ctx2, pos, mods, mods, g, wa, wb, wc, wd)
    return pl.pallas_call(
        _inproj_body, grid=(n_tiles,), in_specs=[tile(D_MODEL)] + common_specs,
        out_specs=out_specs, out_shape=out_shape,
        compiler_params=_cparams(1), name="inproj",
    )(h_or_parts, mods, mods, g, wa, wb, wc, wd)


def _layernorm(x, g, b, eps=1e-5):
    mu = jnp.mean(x, axis=-1, keepdims=True)
    xc = x - mu
    var = jnp.mean(xc * xc, axis=-1, keepdims=True)
    return xc * lax.rsqrt(var + eps) * g + b


def _mixab_body(n_lat, n_l, pa, pa_prev, pa_next, pb, cw, cb, alg, alb, blg, blb, ws, bsm,
                out, ext):
    i = pl.program_id(0)
    is_ctx = i >= n_lat
    tpos = i % n_l
    first = jnp.logical_or(is_ctx, tpos == 0)
    last = jnp.logical_or(is_ctx, tpos == n_l - 1)

    def glu(x):
        return x[:, :D_GROUP] * jax.nn.sigmoid(x[:, D_GROUP:])

    halo = 16
    ext[0:halo, :] = jnp.where(first, 0.0, glu(pa_prev[...]))
    ext[halo:halo + SEQ_TILE, :] = glu(pa[...])
    ext[halo + SEQ_TILE:, :] = jnp.where(last, 0.0, glu(pa_next[...]))
    acc = jnp.zeros((SEQ_TILE, D_GROUP), F32) + cb[...]
    for j in range(CONV_A):
        acc = acc + cw[j:j + 1, :] * ext[pl.ds(halo - (CONV_A - 1) // 2 + j, SEQ_TILE), :]
    out[:, :D_GROUP] = _silu(_layernorm(acc, alg[...], alb[...]))

    x = pb[...]
    u = x[:, :D_GROUP]
    v = _layernorm(x[:, D_GROUP:], blg[...], blb[...]).astype(BF16)
    hd = D_GROUP // HEADS_B
    rows = []
    for c in range(SEQ_TILE // CHUNK_B):
        vc = v[c * CHUNK_B:(c + 1) * CHUNK_B, :]
        heads = [jnp.dot(ws[hh], vc[:, hh * hd:(hh + 1) * hd], preferred_element_type=F32)
                 for hh in range(HEADS_B)]
        rows.append(jnp.concatenate(heads, axis=1) + bsm[...])
    out[:, D_GROUP:] = u * jnp.concatenate(rows, axis=0)


def _mixab(pa, pb, prm, n_lat, n_l):
    rows = pa.shape[0]
    n_tiles = rows // SEQ_TILE
    halo = 16
    per = SEQ_TILE // halo
    full = lambda a: pl.BlockSpec(a.shape, lambda i: (0,) * a.ndim)
    in_specs = [pl.BlockSpec((SEQ_TILE, 512), lambda i: (i, 0)),
                pl.BlockSpec((halo, 512), lambda i: (jnp.maximum(i * per - 1, 0), 0)),
                pl.BlockSpec((halo, 512), lambda i: (jnp.minimum((i + 1) * per, rows // halo - 1), 0)),
                pl.BlockSpec((SEQ_TILE, 512), lambda i: (i, 0))] + [full(a) for a in prm]
    return pl.pallas_call(
        functools.partial(_mixab_body, n_lat, n_l),
        grid=(n_tiles,), in_specs=in_specs,
        out_specs=pl.BlockSpec((SEQ_TILE, 512), lambda i: (i, 0)),
        out_shape=jax.ShapeDtypeStruct((rows, 512), F32),
        scratch_shapes=[pltpu.VMEM((SEQ_TILE + 2 * halo, D_GROUP), F32)],
        compiler_params=_cparams(1), name="mixab",
    )(pa, pa, pa, pb, *prm)


def _scan_tile(n_b, n_l, b, ph, j):
    lat = b * n_l + jnp.where(ph == 0, j - 1, n_l - j)
    return jnp.where(j == 0, n_b * n_l + b, lat)


def _scan_specs(n_b, n_l, rows, cols):
    tid = functools.partial(_scan_tile, n_b, n_l)
    per = SEQ_TILE // 8
    cur = pl.BlockSpec((SEQ_TILE, cols), lambda b, ph, j: (tid(b, ph, j), 0))
    prev = pl.BlockSpec((8, cols), lambda b, ph, j: (jnp.maximum(tid(b, ph, j) * per - 1, 0), 0))
    nxt = pl.BlockSpec((8, cols),
                       lambda b, ph, j: (jnp.minimum((tid(b, ph, j) + 1) * per, rows // 8 - 1), 0))
    return cur, prev, nxt


def _scan_out_spec(n_b, n_l, cols):
    def imap(b, ph, j):
        return (jnp.where(ph == 0, n_b * n_l + b, _scan_tile(n_b, n_l, b, 1, j)), 0)
    return pl.BlockSpec((SEQ_TILE, cols), imap)


def _scan_flags(n_l):
    ph = pl.program_id(1)
    j = pl.program_id(2)
    is_ctx = j == 0
    tpos = jnp.where(ph == 0, j - 1, n_l - j)
    first = jnp.logical_or(is_ctx, tpos == 0)
    last = jnp.logical_or(is_ctx, tpos == n_l - 1)
    slot = jnp.where(is_ctx, 0, tpos + 1)
    return ph, j, first, last, slot


def _mamba_body(n_l, cur_ref, prev_ref, next_ref, dt_ref, cw, cb, dtb, alog, dsk, ng,
                out, ext, yf, st, xbc_s, a_s, dt_s, y_s):
    ph, j, first, last, slot = _scan_flags(n_l)
    fwd = ph == 0

    @pl.when(j == 0)
    def _():
        st[...] = jnp.zeros_like(st)

    zx = D_GROUP
    ext[0:8, :] = jnp.where(first, 0.0, prev_ref[:, zx:])
    ext[8:8 + SEQ_TILE, :] = cur_ref[:, zx:]
    ext[8 + SEQ_TILE:, :] = jnp.where(last, 0.0, next_ref[:, zx:])
    acc = jnp.zeros((SEQ_TILE, C_XBC), F32) + cb[...]
    for jj in range(CONV_C):
        acc = acc + cw[jj:jj + 1, :] * ext[pl.ds(8 - (CONV_C - 1) // 2 + jj, SEQ_TILE), :]
    xbc_s[...] = _silu(acc)
    dt = jax.nn.softplus(dt_ref[...] + dtb[ph])
    dt_s[...] = dt
    a_s[...] = dt * (-jnp.exp(alog[ph]))

    L = CHUNK_C
    r_i = _iota((L, L), 0)
    c_i = _iota((L, L), 1)
    ltri = (r_i >= c_i).astype(F32)
    mask = jnp.where(fwd, r_i - c_i, c_i - r_i) >= 0
    sgn = jnp.where(fwd, 1.0, -1.0)
    hd = HEAD_DIM_C

    def chunk(it, carry):
        q = jnp.where(fwd, it, SEQ_TILE // L - 1 - it)
        rows = pl.ds(pl.multiple_of(q * L, L), L)
        a_q = a_s[rows, :]
        cs = _dot_mask_lhs(ltri, a_q)
        tot = cs[L - 1:L, :]
        e = jnp.where(fwd, cs, cs - a_q)
        e_t = e.T
        dt_q = dt_s[rows, :]
        xbc = xbc_s[rows, :]
        ys = []
        for g in range(2):
            bm = xbc[:, D_GROUP + g * STATE_C:D_GROUP + (g + 1) * STATE_C]
            cm = xbc[:, D_GROUP + 2 * STATE_C + g * STATE_C:D_GROUP + 2 * STATE_C + (g + 1) * STATE_C]
            bm_t = bm.T.astype(BF16)
            cm_b = cm.astype(BF16)
            gmat = jnp.dot(cm_b, bm_t, preferred_element_type=F32)
            for h in (2 * g, 2 * g + 1):
                ecol = e[:, h:h + 1]
                erow = e_t[h:h + 1, :]
                totc = tot[:, h:h + 1]
                xdt = xbc[:, h * hd:(h + 1) * hd] * dt_q[:, h:h + 1]
                lmat = jnp.exp(jnp.where(mask, sgn * (ecol - erow), -1e30))
                offs = jnp.exp(jnp.where(fwd, ecol, totc - ecol))
                stw = jnp.exp(jnp.where(fwd, totc - ecol, ecol))
                s_prev = st[h]
                y = _dot(gmat * lmat, xdt) + offs * _dot(cm_b, s_prev)
                st[h] = jnp.exp(totc) * s_prev + _dot(bm_t, xdt * stw)
                ys.append(y)
        y_s[rows, :] = jnp.concatenate(ys, axis=1)
        return carry

    lax.fori_loop(0, SEQ_TILE // L, chunk, 0)

    @pl.when(fwd)
    def _():
        yf[slot] = y_s[...]

    @pl.when(ph == 1)
    def _():
        y = yf[slot] + y_s[...] + dsk[...] * xbc_s[:, :D_GROUP]
        t = y * _silu(cur_ref[:, :D_GROUP])
        out[...] = t * lax.rsqrt(jnp.mean(t * t, axis=-1, keepdims=True) + EPS) * ng[...]


def _mamba(pc, prm, n_b, n_l):
    rows = pc.shape[0]
    cur, prev, nxt = _scan_specs(n_b, n_l, rows, D_GROUP + C_XBC)
    tid = functools.partial(_scan_tile, n_b, n_l)
    dt_spec = pl.BlockSpec((SEQ_TILE, LANES),
                           lambda b, ph, j: (tid(b, ph, j), (D_GROUP + C_XBC) // LANES + ph))
    full = lambda a: pl.BlockSpec(a.shape, lambda b, ph, j: (0,) * a.ndim)
    return pl.pallas_call(
        functools.partial(_mamba_body, n_l),
        grid=(n_b, 2, n_l + 1),
        in_specs=[cur, prev, nxt, dt_spec] + [full(a) for a in prm],
        out_specs=_scan_out_spec(n_b, n_l, D_GROUP),
        out_shape=jax.ShapeDtypeStruct((rows, D_GROUP), F32),
        scratch_shapes=[pltpu.VMEM((SEQ_TILE + 16, C_XBC), F32),
                        pltpu.VMEM((n_l + 1, SEQ_TILE, D_GROUP), F32),
                        pltpu.VMEM((HEADS_C, STATE_C, HEAD_DIM_C), F32),
                        pltpu.VMEM((SEQ_TILE, C_XBC), F32),
                        pltpu.VMEM((SEQ_TILE, LANES), F32),
                        pltpu.VMEM((SEQ_TILE, LANES), F32),
                        pltpu.VMEM((SEQ_TILE, D_GROUP), F32)],
        compiler_params=_cparams(3), name="mamba",
    )(pc, pc, pc, pc, *prm)


def _rwkv_body(n_l, cur_ref, prev_ref, next_ref, mup, mun, w0, w2, a0, a2, g2, kkw, kaw, rkw,
               gng, gnb, out, yf, st, r_s, v_s, kap_s, alp_s, kd_s, lw_s, y_s,
               phi_s, psi_s, sin_s, left_s, amat_s, ara_s, tinv_s):
    ph, j, first, last, slot = _scan_flags(n_l)
    fwd = ph == 0

    @pl.when(j == 0)
    def _():
        st[...] = jnp.zeros_like(st)

    cur = cur_ref[...]
    prv = jnp.concatenate([jnp.where(first, 0.0, prev_ref[7:8, :]), cur[:SEQ_TILE - 1, :]], axis=0)
    nxt = jnp.concatenate([cur[1:, :], jnp.where(last, 0.0, next_ref[0:1, :])], axis=0)
    p = cur + mup[...] * (prv - cur) + mun[...] * (nxt - cur)
    G = D_GROUP
    r = p[:, :G]
    k = p[:, G:2 * G]
    v = p[:, 2 * G:3 * G]
    blk = _head_block_ones(G, HEAD_DIM_D)
    kkr = k * kkw[...]
    kk = kkr * lax.rsqrt(_dot_mask_rhs(kkr * kkr, blk) + 1e-12)

    def rate(d_static=None):
        if d_static is None:
            ad = jnp.where(fwd, p[:, 3 * G + 2 * LORA_W:3 * G + 2 * LORA_W + LORA_A],
                           p[:, 3 * G + 2 * LORA_W + LORA_A:3 * G + 2 * LORA_W + 2 * LORA_A])
            a = jax.nn.sigmoid(a0[ph] + _dot(ad, a2[ph]))
        else:
            o = 3 * G + 2 * LORA_W + d_static * LORA_A
            a = jax.nn.sigmoid(a0[d_static] + _dot(p[:, o:o + LORA_A], a2[d_static]))
        return a, k * (1.0 + (a - 1.0) * kaw[...])

    a_d, kd_d = rate()
    wd = jnp.where(fwd, p[:, 3 * G:3 * G + LORA_W], p[:, 3 * G + LORA_W:3 * G + 2 * LORA_W])
    w = w0[ph] + _dot(jnp.tanh(wd), w2[ph])
    lw_s[...] = -math.exp(-0.5) * jax.nn.sigmoid(w)
    r_s[...] = r
    v_s[...] = v
    kap_s[...] = kk
    alp_s[...] = a_d * kk
    kd_s[...] = kd_d

    C = CHUNK_D
    r_i = _iota((C, C), 0)
    c_i = _iota((C, C), 1)
    lag = jnp.where(fwd, r_i - c_i, c_i - r_i)
    tri = lag >= 0
    tri_f = tri.astype(F32)
    strict = lag > 0
    eye = (r_i == c_i).astype(F32)
    hd = HEAD_DIM_D

    n_q = SEQ_TILE // C
    eye_k = (_iota((hd, hd), 0) == _iota((hd, hd), 1)).astype(F32)

    units = [(q, h) for q in range(n_q) for h in range(HEADS_D)]
    rows_of = lambda q: slice(q * C, (q + 1) * C)
    lanes_of = lambda h: slice(h * hd, (h + 1) * hd)
    left, right, p_tot = {}, {}, {}
    for q in range(n_q):
        rows = rows_of(q)
        lw = lw_s[rows, :]
        incl = _dot_mask_lhs(tri_f, lw)
        tot = jnp.where(fwd, incl[C - 1:C, :], incl[0:1, :])
        p_inv = jnp.exp(-incl)
        kap_h = kap_s[rows, :] * jnp.exp(incl - lw)
        r_h = r_s[rows, :] * jnp.exp(incl)
        alp_b = alp_s[rows, :] * p_inv
        k_b = kd_s[rows, :] * p_inv
        ptq = jnp.exp(tot)
        for h in range(HEADS_D):
            sl = lanes_of(h)
            left[q, h] = jnp.concatenate([kap_h[:, sl], r_h[:, sl]], axis=0)
            right[q, h] = jnp.concatenate([alp_b[:, sl], k_b[:, sl]], axis=0)
            p_tot[q, h] = ptq[:, sl]
            left_s[q * HEADS_D + h] = left[q, h]
    m1 = {u: _dot(left[u], right[u], NT) for u in units}
    a_vk, pw, tinv = {}, {}, {}
    for u in units:
        qh = u[0] * HEADS_D + u[1]
        a_vk[u] = jnp.where(strict, m1[u][:C, C:], 0.0)
        amat_s[qh] = jnp.concatenate([a_vk[u], jnp.where(tri, m1[u][C:, C:], 0.0)], axis=0)
        ara_s[qh] = jnp.where(tri, m1[u][C:, :C], 0.0)
        pw[u] = jnp.where(strict, -m1[u][:C, :C], 0.0)
        tinv[u] = eye + pw[u]
    for _ in range(5):
        pw = {u: _dot(pw[u], pw[u]) for u in units}
        upd = {u: _dot(tinv[u], pw[u]) for u in units}
        tinv = {u: tinv[u] + upd[u] for u in units}
    wmat = {u: _dot(tinv[u], right[u][:C], TN) for u in units}
    kw = {u: _dot(left[u][:C], wmat[u], TN) for u in units}
    aw = {u: _dot(a_vk[u], wmat[u], TN) for u in units}
    vk = {u: _dot(v_s[rows_of(u[0]), lanes_of(u[1])], right[u][C:] - aw[u], TN) for u in units}
    for u in units:
        qh = u[0] * HEADS_D + u[1]
        phi_s[qh] = (eye_k - kw[u]) * p_tot[u]
        psi_s[qh] = vk[u] * p_tot[u]
        tinv_s[qh] = tinv[u]

    for it in range(n_q):
        q = jnp.where(fwd, it, n_q - 1 - it)
        for h in range(HEADS_D):
            qh = q * HEADS_D + h
            s_in = st[h]
            sin_s[qh] = s_in
            st[h] = _dot(s_in, phi_s[qh]) + psi_s[qh]

    qh_of = lambda u: u[0] * HEADS_D + u[1]
    x0 = {u: _dot(left_s[qh_of(u)], sin_s[qh_of(u)], NT) for u in units}
    x1 = {u: _dot(amat_s[qh_of(u)], v_s[rows_of(u[0]), lanes_of(u[1])]) for u in units}
    uu = {u: _dot(tinv_s[qh_of(u)], x0[u][:C] + x1[u][:C]) for u in units}
    au = {u: _dot(ara_s[qh_of(u)], uu[u]) for u in units}
    for q in range(n_q):
        y_s[rows_of(q), :] = jnp.concatenate(
            [x0[q, h][C:] + x1[q, h][C:] - au[q, h] for h in range(HEADS_D)], axis=1)

    @pl.when(fwd)
    def _():
        yf[slot] = y_s[...]

    @pl.when(ph == 1)
    def _():
        y = yf[slot] + y_s[...]
        inv = 1.0 / HEAD_DIM_D
        mu = _dot_mask_rhs(y, blk) * inv
        yc = y - mu
        var = _dot_mask_rhs(yc * yc, blk) * inv
        yn = yc * lax.rsqrt(var + GN_EPS_D) * gng[...] + gnb[...]
        _, kd_f = rate(0)
        bonus = _dot_mask_rhs(r * (kd_f + kd_d) * rkw[...], blk) * v
        gate = _dot(jax.nn.sigmoid(p[:, 3 * G + 2 * LORA_W + 2 * LORA_A:]), g2[...])
        out[...] = (yn + bonus) * gate


def _rwkv(pd, prm, n_b, n_l):
    rows = pd.shape[0]
    cur, prev, nxt = _scan_specs(n_b, n_l, rows, D_COLS)
    full = lambda a: pl.BlockSpec(a.shape, lambda b, ph, j: (0,) * a.ndim)
    tile = lambda: pltpu.VMEM((SEQ_TILE, D_GROUP), F32)
    per_qh = lambda r, c: pltpu.VMEM((SEQ_TILE // CHUNK_D * HEADS_D, r, c), F32)
    return pl.pallas_call(
        functools.partial(_rwkv_body, n_l),
        grid=(n_b, 2, n_l + 1),
        in_specs=[cur, prev, nxt] + [full(a) for a in prm],
        out_specs=_scan_out_spec(n_b, n_l, D_GROUP),
        out_shape=jax.ShapeDtypeStruct((rows, D_GROUP), F32),
        scratch_shapes=[pltpu.VMEM((n_l + 1, SEQ_TILE, D_GROUP), F32),
                        pltpu.VMEM((HEADS_D, HEAD_DIM_D, HEAD_DIM_D), F32)] + [tile() for _ in range(7)]
        + [per_qh(HEAD_DIM_D, HEAD_DIM_D) for _ in range(3)]
        + [per_qh(2 * CHUNK_D, HEAD_DIM_D), per_qh(2 * CHUNK_D, CHUNK_D),
           per_qh(CHUNK_D, CHUNK_D), per_qh(CHUNK_D, CHUNK_D)],
        compiler_params=_cparams(3), name="rwkv",
    )(pd, pd, pd, *prm)


def _outproj_body(yab, yc, yd, h_ref, gate, sh, sc, g, wab, wc, wd, oh, of):
    mix = (_dot(yab[...], wab[...]) + _dot(yc[...], wc[...]) + _dot(yd[...], wd[...]))
    h = h_ref[...] + gate[...] * mix
    oh[...] = h
    of[...] = _norm_mod(h, g[...], sh[...], sc[...]).astype(BF16)


def _outproj(yab, yc, yd, h, mods, g, ws, n_lat, per_seq):
    rows = h.shape[0]
    tile = lambda c: pl.BlockSpec((TOK_TILE, c), lambda i: (i, 0))
    full = lambda a: pl.BlockSpec(a.shape, lambda i: (0,) * a.ndim)
    return pl.pallas_call(
        _outproj_body, grid=(rows // TOK_TILE,),
        in_specs=[tile(512), tile(D_GROUP), tile(D_GROUP), tile(D_MODEL),
                  _mod_spec(n_lat, per_seq, 2), _mod_spec(n_lat, per_seq, 3),
                  _mod_spec(n_lat, per_seq, 4), full(g)] + [full(a) for a in ws],
        out_specs=[tile(D_MODEL), tile(D_MODEL)],
        out_shape=[jax.ShapeDtypeStruct((rows, D_MODEL), F32),
                   jax.ShapeDtypeStruct((rows, D_MODEL), BF16)],
        compiler_params=_cparams(1), name="outproj",
    )(yab, yc, yd, h, mods, mods, mods, g, *ws)


TOPK_TILE = 512


SUBLANES = 8


def _cmpx(lst, i, j):
    a, b = lst[i], lst[j]
    lst[i] = jnp.maximum(a, b)
    lst[j] = jnp.minimum(a, b)


def _bitonic_sort_desc(lst):
    n = len(lst)
    k = 2
    while k <= n:
        j = k // 2
        while j >= 1:
            for i in range(n):
                p = i ^ j
                if p > i:
                    if (i & k) == 0:
                        _cmpx(lst, i, p)
                    else:
                        _cmpx(lst, p, i)
            j //= 2
        k *= 2


def _bitonic_merge_desc(lst):
    n = len(lst)
    j = n // 2
    while j >= 1:
        for i in range(n):
            p = i ^ j
            if p > i:
                _cmpx(lst, i, p)
        j //= 2


def _merge_top(a, b):
    n = len(a)
    c = [jnp.maximum(a[i], b[n - 1 - i]) for i in range(n)]
    _bitonic_merge_desc(c)
    return c


def _merge_sublanes(lst):
    for shift in (4, 2, 1):
        lst = _merge_top(lst, [pltpu.roll(a, shift, axis=0) for a in lst])
    return lst


def _count_leading(pred, t):
    sel = jnp.where
    c1 = pred(t[7])
    c2 = pred(sel(c1, t[11], t[3]))
    c3 = pred(sel(c1, sel(c2, t[13], t[9]), sel(c2, t[5], t[1])))
    c4 = pred(sel(c1, sel(c2, sel(c3, t[14], t[12]), sel(c3, t[10], t[8])),
                  sel(c2, sel(c3, t[6], t[4]), sel(c3, t[2], t[0]))))
    c5 = pred(t[15])
    return (sel(c1, 8.0, 0.0) + sel(c2, 4.0, 0.0) + sel(c3, 2.0, 0.0) + sel(c4, 1.0, 0.0)
            + sel(c5, 1.0, 0.0))


def _top16_rows(s):
    lst = [s[SUBLANES * v:SUBLANES * (v + 1), :] for v in range(s.shape[0] // SUBLANES)]
    _bitonic_sort_desc(lst)
    return _merge_sublanes(lst)


def _topk_body(f_ref, wq_ref, k1_ref, k2_ref, e1_o, n_o, r2_o, e2_o, q_s):
    q_s[...] = lax.dot_general(wq_ref[...], f_ref[...], (NT, ((), ())),
                               preferred_element_type=F32).astype(BF16)
    half = PEER_DK // 2
    T = f_ref.shape[0]
    sub = _iota((SUBLANES, T), 0)

    def stack(rows):
        out = rows[SUBLANES - 1]
        for b in range(SUBLANES - 2, -1, -1):
            out = jnp.where(sub == b, rows[b], out)
        return out

    def rep(a):
        return jnp.concatenate([a] * (N_KEYS // SUBLANES), axis=0)

    def head(h, carry):
        base = pl.multiple_of(h * PEER_DK, PEER_DK)
        s1 = jnp.dot(k1_ref[...], q_s[pl.ds(base, half), :], preferred_element_type=F32)
        s2 = jnp.dot(k2_ref[...], q_s[pl.ds(base + half, half), :], preferred_element_type=F32)
        t1 = _top16_rows(s1)
        t2 = _top16_rows(s2)
        lo = stack(t2[:SUBLANES])
        hi = stack(t2[SUBLANES:])
        top = _merge_sublanes(_merge_top([t + lo for t in t1], [t + hi for t in t1]))
        z = jnp.exp(top[0] - top[0])
        for kk in range(1, PEER_TOPK):
            z = z + jnp.exp(top[kk] - top[0])
        thr = rep(top[PEER_TOPK - 1])
        t2r = [rep(t) for t in t2]
        r2 = _count_leading(lambda t: t > s2, t2r)
        n1 = _count_leading(lambda t: s1 + t >= thr, t2r)
        r2_o[h] = _pack_rows(r2)
        n_o[h] = _dup_bf16(n1)
        e1_o[h] = _dup_bf16(jnp.exp(s1 - rep(t1[0])) / rep(z))
        e2_o[h] = _pack_rows(jnp.exp(s2 - rep(t2[0])))
        return carry

    lax.fori_loop(0, PEER_HEADS, head, 0)


def _topk(f, wq_t, k1, k2):
    rows = f.shape[0]
    T = TOPK_TILE
    full = lambda a: pl.BlockSpec(a.shape, lambda i: (0,) * a.ndim)
    big = pl.BlockSpec((PEER_HEADS, N_KEYS, T), lambda i: (0, 0, i))
    big_shape = lambda dt: jax.ShapeDtypeStruct((PEER_HEADS, N_KEYS, rows), dt)
    packed = pl.BlockSpec((PEER_HEADS, N_KEYS // 2, T), lambda i: (0, 0, i))
    packed_shape = jax.ShapeDtypeStruct((PEER_HEADS, N_KEYS // 2, rows), U32)
    return pl.pallas_call(
        _topk_body, grid=(rows // T,),
        in_specs=[pl.BlockSpec((T, D_MODEL), lambda i: (i, 0)), full(wq_t), full(k1), full(k2)],
        out_specs=[big, big, packed, packed],
        out_shape=[big_shape(U32), big_shape(U32), packed_shape, packed_shape],
        scratch_shapes=[pltpu.VMEM((PEER_HEADS * PEER_DK, T), BF16)],
        compiler_params=_cparams(1), name="peer_topk",
    )(f, wq_t, k1, k2)


def _peer_body(final, n_e, f_ref, u_ref, v_ref, e1_ref, n_ref, r2_ref, e2_ref, h_ref,
               gate_ref, fg_ref, out, st_s, at_s, acc):
    s = pl.program_id(0)
    j_up = jnp.maximum(s - 1, 0) % n_e
    cur = s % 2

    @pl.when(s == 0)
    def _():
        at_s[...] = jnp.zeros_like(at_s)

    @pl.when(j_up == 0)
    def _():
        acc[...] = jnp.zeros_like(acc)

    st_s[...] = lax.dot_general(u_ref[...], f_ref[...], (NT, ((), ())),
                                preferred_element_type=F32)
    acc[...] += lax.dot_general(_unpack_rows(at_s[1 - cur]), v_ref[...], (TN, ((), ())),
                                preferred_element_type=F32)

    blk = (N_KEYS // 2, LANES)
    group = 2
    for lg in range(TOK_TILE // LANES):
        ln = slice(lg * LANES, (lg + 1) * LANES)
        for i0 in range(0, EXP_TILE // N_KEYS, group):
            g = [None] * group
            for h in range(PEER_HEADS):
                r2_blk = _unpack_rows(r2_ref[h, :, ln])
                e2_blk = _unpack_rows(e2_ref[h, :, ln])
                for r in range(group):
                    ii = i0 + r
                    n_row = _unpack_rows(jnp.broadcast_to(n_ref[h, ii:ii + 1, ln], blk))
                    e1_row = _unpack_rows(jnp.broadcast_to(e1_ref[h, ii:ii + 1, ln], blk))
                    w = jnp.where(r2_blk < n_row, e2_blk, 0.0) * e1_row
                    g[r] = w if g[r] is None else g[r] + w
            for r in range(group):
                ii = i0 + r
                rows = slice(ii * N_KEYS, (ii + 1) * N_KEYS)
                at_s[cur, ii * N_KEYS // 2:(ii + 1) * N_KEYS // 2, ln] = _pack_rows(
                    _gelu_tanh(st_s[rows, ln]).astype(BF16) * g[r])

    @pl.when(jnp.logical_and(s >= 1, j_up == n_e - 1))
    def _():
        h = h_ref[...] + gate_ref[...] * acc[...]
        if final:
            h = h * lax.rsqrt(jnp.mean(h * h, axis=-1, keepdims=True) + EPS) * fg_ref[...]
        out[...] = h


def _peer(f, u_b, v_b, layer, e1, n1, r2, e2, h, mods, fg, n_lat, per_seq, n_tok_tiles, final):
    n_e = N_EXPERTS // EXP_TILE
    rpt = EXP_TILE // N_KEYS
    n_steps = n_tok_tiles * n_e

    def tile(s, lag):
        t = jnp.clip(s - lag, 0, n_steps - 1)
        return t // n_e, t % n_e

    tok = lambda lag: pl.BlockSpec((TOK_TILE, D_MODEL), lambda s: (tile(s, lag)[0], 0))
    exp = lambda lag: pl.BlockSpec((None, EXP_TILE, D_MODEL), lambda s: (layer, tile(s, lag)[1], 0))
    rowblk = pl.BlockSpec((PEER_HEADS, rpt, TOK_TILE), lambda s: (0, tile(s, 0)[1], tile(s, 0)[0]))
    allkeys = pl.BlockSpec((PEER_HEADS, N_KEYS // 2, TOK_TILE), lambda s: (0, 0, tile(s, 0)[0]))

    def gate_map(s):
        i = tile(s, 1)[0]
        row = jnp.where(i < n_lat, i // per_seq, 2)
        return (row * 6 + 5, 0, 0)

    return pl.pallas_call(
        functools.partial(_peer_body, final, n_e),
        grid=(n_steps + 1,),
        in_specs=[tok(0), exp(0), exp(1), rowblk, rowblk, allkeys, allkeys, tok(1),
                  pl.BlockSpec((None, 1, D_MODEL), gate_map),
                  pl.BlockSpec((1, D_MODEL), lambda s: (0, 0))],
        out_specs=tok(1),
        out_shape=jax.ShapeDtypeStruct((n_tok_tiles * TOK_TILE, D_MODEL), F32),
        scratch_shapes=[pltpu.VMEM((EXP_TILE, TOK_TILE), F32),
                        pltpu.VMEM((2, EXP_TILE // 2, TOK_TILE), U32),
                        pltpu.VMEM((TOK_TILE, D_MODEL), F32)],
        compiler_params=_cparams(1), name="peer",
    )(f, u_b, v_b, e1, n1, r2, e2, h, mods, fg)


def _cast_body(u_ref, v_ref, uo, vo):
    uo[...] = u_ref[...].astype(BF16)
    vo[...] = v_ref[...].astype(BF16)


def _cast_tables(u, v):
    L, n, d = u.shape
    rows = 1024
    spec = pl.BlockSpec((None, rows, d), lambda l, i: (l, i, 0))
    shape = jax.ShapeDtypeStruct((L, n, d), BF16)
    return pl.pallas_call(
        _cast_body, grid=(L, n // rows), in_specs=[spec, spec], out_specs=[spec, spec],
        out_shape=[shape, shape], compiler_params=_cparams(2), name="cast_tables",
    )(u, v)


def _sincos_2d(t_len):
    rows = t_len // GRID_W
    q = D_MODEL // 4
    freq = 10000.0 ** (-jnp.arange(q, dtype=F32) / q)
    ar = jnp.arange(rows, dtype=F32)[:, None] * freq
    ac = jnp.arange(GRID_W, dtype=F32)[:, None] * freq
    per_row = lambda a: jnp.repeat(a, GRID_W, axis=0)
    per_col = lambda a: jnp.tile(a, (rows, 1))
    return jnp.concatenate([per_row(jnp.sin(ar)), per_row(jnp.cos(ar)),
                            per_col(jnp.sin(ac)), per_col(jnp.cos(ac))], -1)


def _row(a):
    return a.reshape(1, -1).astype(F32)


def _pad_lanes(a, width=LANES):
    return jnp.pad(a, ((0, 0), (0, width - a.shape[-1])))


def kernel(x, c, ctx, c_ctx, ada_w, ada_b, norm_mix_g, norm_ffn_g, w_in, w_out, a_conv_w, a_conv_b, a_ln_g, a_ln_b, b_ln_g, b_ln_b, b_ws, b_bs, c_conv_w, c_conv_b, c_dt_bias, c_a_log, c_d, c_norm_g, d_mu_prev, d_mu_next, d_w0, d_w2, d_a0, d_a2, d_g2, d_k_k, d_k_a, d_r_k, d_gn_g, d_gn_b, peer_wq, peer_k1, peer_k2, peer_u, peer_v, final_g):
    n_b, t_lat, _ = x.shape
    t_ctx = ctx.shape[1]
    assert n_b == 2 and t_ctx == SEQ_TILE and t_lat % TOK_TILE == 0
    n_l = t_lat // SEQ_TILE
    per_seq = t_lat // TOK_TILE
    n_lat = n_b * per_seq
    n_lat256 = n_b * n_l

    cs = jnp.zeros((8, D_MODEL), F32).at[:n_b].set(c).at[n_b].set(c_ctx)
    mods_all = _ada(cs, ada_w, ada_b)
    pos = _sincos_2d(t_lat)
    u_bf, v_bf = _cast_tables(peer_u, peer_v)
    x2 = x.reshape(n_b * t_lat, D_MODEL)
    ctx2 = ctx.reshape(n_b * t_ctx, D_MODEL)

    h = None
    for i in range(DEPTH):
        last_layer = i == DEPTH - 1
        mods = mods_all[i].reshape(8 * 6, 1, D_MODEL)
        w = w_in[i]
        dtc = OFF_C + D_GROUP + C_XBC
        w_c = jnp.concatenate([w[:, OFF_C:dtc], _pad_lanes(w[:, dtc:dtc + HEADS_C]),
                               _pad_lanes(w[:, dtc + HEADS_C:dtc + 2 * HEADS_C])], axis=1)
        ws_in = tuple(a.astype(BF16) for a in (w[:, :OFF_B], w[:, OFF_B:OFF_C], w_c, w[:, OFF_D:]))
        g_mix = _row(norm_mix_g[i])
        if i == 0:
            h, pa, pb, pc, pd = _inproj((x2, ctx2, pos), mods, g_mix, ws_in, n_lat, per_seq, True)
        else:
            pa, pb, pc, pd = _inproj(h, mods, g_mix, ws_in, n_lat, per_seq, False)

        conv_a = jnp.pad(a_conv_w[i], ((0, 32 - CONV_A), (0, 0)))
        bsm = jnp.repeat(b_bs[i].T, D_GROUP // HEADS_B, axis=1)
        yab = _mixab(pa, pb, (conv_a, _row(a_conv_b[i]), _row(a_ln_g[i]), _row(a_ln_b[i]),
                              _row(b_ln_g[i]), _row(b_ln_b[i]), b_ws[i].astype(BF16), bsm),
                     n_lat256, n_l)

        conv_c = jnp.pad(c_conv_w[i], ((0, 8 - CONV_C), (0, 0)))
        dtb = _pad_lanes(c_dt_bias[i]).reshape(2, 1, LANES)
        alog = _pad_lanes(c_a_log[i]).reshape(2, 1, LANES)
        dsk = _row(jnp.repeat(c_d[i], HEAD_DIM_C))
        yc = _mamba(pc, (conv_c, _row(c_conv_b[i]), dtb, alog, dsk, _row(c_norm_g[i])), n_b, n_l)

        yd = _rwkv(pd, (_row(d_mu_prev[i]), _row(d_mu_next[i]), d_w0[i].reshape(2, 1, D_GROUP),
                        d_w2[i].astype(BF16), d_a0[i].reshape(2, 1, D_GROUP), d_a2[i].astype(BF16),
                        d_g2[i].astype(BF16), _row(d_k_k[i]), _row(d_k_a[i]), _row(d_r_k[i]),
                        _row(d_gn_g[i]), _row(d_gn_b[i])), n_b, n_l)

        wo = w_out[i].astype(BF16)
        h, f = _outproj(yab, yc, yd, h, mods, _row(norm_ffn_g[i]),
                        (wo[:512], wo[512:768], wo[768:]), n_lat, per_seq)

        e1, n1, r2, e2 = _topk(f, peer_wq[i].T.astype(BF16), peer_k1[i].astype(BF16),
                               peer_k2[i].astype(BF16))
        n_tok = n_lat if last_layer else n_lat + 1
        h = _peer(f, u_bf, v_bf, i, e1, n1, r2, e2, h,
                  mods, _row(final_g), n_lat, per_seq, n_tok, last_layer)
    return h.reshape(n_b, t_lat, D_MODEL)
```

```python
import functools
import math

import jax
import jax.numpy as jnp
from jax import lax
from jax.experimental import pallas as pl
from jax.experimental.pallas import tpu as pltpu

F32 = jnp.float32
BF16 = jnp.bfloat16
U32 = jnp.uint32
HIGHEST = lax.Precision.HIGHEST

D_MODEL = 1024
DEPTH = 2
GRID_W = 64
EPS = 1e-6
D_GROUP = 256
CONV_A = 31
CHUNK_B = 128
HEADS_B = 4
HEADS_C = 4
HEAD_DIM_C = 64
STATE_C = 128
CONV_C = 5
CHUNK_C = 128
HEADS_D = 4
HEAD_DIM_D = 64
LORA_W = 64
LORA_A = 64
LORA_G = 128
GN_EPS_D = 64e-5
N_KEYS = 128
N_EXPERTS = N_KEYS * N_KEYS
PEER_HEADS = 8
PEER_DK = 256
PEER_TOPK = 16

C_XBC = D_GROUP + 2 * 2 * STATE_C
OFF_B = 2 * D_GROUP
OFF_C = OFF_B + 2 * D_GROUP
OFF_D = OFF_C + D_GROUP + C_XBC + 2 * HEADS_C
D_COLS = 3 * D_GROUP + 2 * LORA_W + 2 * LORA_A + LORA_G
PC_COLS = D_GROUP + C_XBC + 2 * 128

LANES = 128
SEQ_TILE = 256
TOK_TILE = 512
CHUNK_D = 64
EXP_TILE = 2048
VMEM_LIMIT = 56 * 1024 * 1024


def _cparams(n_axes):
    return pltpu.CompilerParams(dimension_semantics=("arbitrary",) * n_axes,
                                vmem_limit_bytes=VMEM_LIMIT)


def _silu(x):
    return x * jax.nn.sigmoid(x)


def _dot(a, b, dims=None):
    a = a.astype(BF16)
    b = b.astype(BF16)
    if dims is None:
        return jnp.dot(a, b, preferred_element_type=F32)
    return lax.dot_general(a, b, (dims, ((), ())), preferred_element_type=F32)


def _dot_hi(a, b):
    return jnp.dot(a, b, precision=HIGHEST, preferred_element_type=F32)


def _split_bf16(x, terms):
    parts = []
    for _ in range(terms):
        part = x.astype(BF16)
        parts.append(part)
        x = x - part.astype(F32)
    return parts


def _dot_mask_rhs(a, mask, terms=3):
    m = mask.astype(BF16)
    return sum(jnp.dot(p, m, preferred_element_type=F32) for p in _split_bf16(a, terms))


def _dot_mask_lhs(mask, b, terms=3):
    m = mask.astype(BF16)
    return sum(jnp.dot(m, p, preferred_element_type=F32) for p in _split_bf16(b, terms))


NT = ((1,), (1,))
TN = ((0,), (0,))


def _gelu_tanh(x):
    k1 = math.sqrt(2.0 / math.pi)
    k3 = 0.044715 * k1
    hx = 0.5 * x
    return hx + hx * jnp.tanh(x * (k1 + k3 * (x * x)))


def _pack_rows(x):
    return pltpu.bitcast(x.astype(BF16), U32)


def _unpack_rows(x):
    return pltpu.bitcast(x, BF16)


def _dup_bf16(x):
    hi = pltpu.bitcast(x.astype(BF16).astype(F32), U32)
    return hi | (hi >> 16)


def _iota(shape, axis):
    return lax.broadcasted_iota(jnp.int32, shape, axis)


def _head_block_ones(n, width):
    return (_iota((n, n), 0) // width == _iota((n, n), 1) // width).astype(F32)


def _ada_body(cs_ref, w_ref, b_ref, o_ref):
    o_ref[...] = _dot_hi(_silu(cs_ref[...]), w_ref[...]) + b_ref[...]


def _ada(cs, ada_w, ada_b):
    L = ada_w.shape[0]
    nb = 1536
    return pl.pallas_call(
        _ada_body,
        grid=(L, 6 * D_MODEL // nb),
        in_specs=[pl.BlockSpec((8, D_MODEL), lambda l, n: (0, 0)),
                  pl.BlockSpec((None, D_MODEL, nb), lambda l, n: (l, 0, n)),
                  pl.BlockSpec((None, 1, nb), lambda l, n: (l, 0, n))],
        out_specs=pl.BlockSpec((None, 8, nb), lambda l, n: (l, 0, n)),
        out_shape=jax.ShapeDtypeStruct((L, 8, 6 * D_MODEL), F32),
        compiler_params=_cparams(2),
        name="ada",
    )(cs, ada_w, ada_b.reshape(L, 1, 6 * D_MODEL))


def _norm_mod(h, g, shift, scale):
    xn = h * lax.rsqrt(jnp.mean(h * h, axis=-1, keepdims=True) + EPS) * g
    return xn * (1.0 + scale) + shift


def _inproj_tail(h, sh_ref, sc_ref, g_ref, wa, wb, wc, wd, oa, ob, oc, od):
    xm = _norm_mod(h, g_ref[...], sh_ref[...], sc_ref[...]).astype(BF16)
    oa[...] = jnp.dot(xm, wa[...], preferred_element_type=F32)
    ob[...] = jnp.dot(xm, wb[...], preferred_element_type=F32)
    oc[...] = jnp.dot(xm, wc[...], preferred_element_type=F32)
    od[...] = jnp.dot(xm, wd[...], preferred_element_type=F32)


def _inproj_first_body(n_lat, x_ref, ctx_ref, pos_ref, sh_ref, sc_ref, g_ref, wa, wb, wc, wd,
                       oh, oa, ob, oc, od):
    i = pl.program_id(0)
    h = jnp.where(i < n_lat, x_ref[...] + pos_ref[...], ctx_ref[...])
    oh[...] = h
    _inproj_tail(h, sh_ref, sc_ref, g_ref, wa, wb, wc, wd, oa, ob, oc, od)


def _inproj_body(h_ref, sh_ref, sc_ref, g_ref, wa, wb, wc, wd, oa, ob, oc, od):
    _inproj_tail(h_ref[...], sh_ref, sc_ref, g_ref, wa, wb, wc, wd, oa, ob, oc, od)


def _mod_spec(n_lat, per_seq, k):
    def imap(i):
        row = jnp.where(i < n_lat, i // per_seq, 2)
        return (row * 6 + k, 0, 0)
    return pl.BlockSpec((None, 1, D_MODEL), imap)


def _inproj(h_or_parts, mods, g, ws, n_lat, per_seq, first):
    wa, wb, wc, wd = ws
    n_tiles = n_lat + 1
    rows = n_tiles * TOK_TILE
    tile = lambda c: pl.BlockSpec((TOK_TILE, c), lambda i: (i, 0))
    full = lambda a: pl.BlockSpec(a.shape, lambda i: (0,) * a.ndim)
    common_specs = [_mod_spec(n_lat, per_seq, 0), _mod_spec(n_lat, per_seq, 1), full(g),
                    full(wa), full(wb), full(wc), full(wd)]
    out_specs = [tile(512), tile(512), tile(PC_COLS), tile(D_COLS)]
    out_shape = [jax.ShapeDtypeStruct((rows, c), F32) for c in (512, 512, PC_COLS, D_COLS)]
    if first:
        x2, ctx2, pos = h_or_parts
        in_specs = [pl.BlockSpec((TOK_TILE, D_MODEL), lambda i: (jnp.minimum(i, n_lat - 1), 0)),
                    pl.BlockSpec((TOK_TILE, D_MODEL), lambda i: (0, 0)),
                    pl.BlockSpec((TOK_TILE, D_MODEL), lambda i: (i % per_seq, 0))] + common_specs
        return pl.pallas_call(
            functools.partial(_inproj_first_body, n_lat),
            grid=(n_tiles,), in_specs=in_specs,
            out_specs=[tile(D_MODEL)] + out_specs,
            out_shape=[jax.ShapeDtypeStruct((rows, D_MODEL), F32)] + out_shape,
            compiler_params=_cparams(1), name="inproj_first",
        )(x2, ctx2, pos, mods, mods, g, wa, wb, wc, wd)
    return pl.pallas_call(
        _inproj_body, grid=(n_tiles,), in_specs=[tile(D_MODEL)] + common_specs,
        out_specs=out_specs, out_shape=out_shape,
        compiler_params=_cparams(1), name="inproj",
    )(h_or_parts, mods, mods, g, wa, wb, wc, wd)


def _layernorm(x, g, b, eps=1e-5):
    mu = jnp.mean(x, axis=-1, keepdims=True)
    xc = x - mu
    var = jnp.mean(xc * xc, axis=-1, keepdims=True)
    return xc * lax.rsqrt(var + eps) * g + b


def _mixab_body(n_lat, n_l, pa, pa_prev, pa_next, pb, cw, cb, alg, alb, blg, blb, ws, bsm,
                out, ext):
    i = pl.program_id(0)
    is_ctx = i >= n_lat
    tpos = i % n_l
    first = jnp.logical_or(is_ctx, tpos == 0)
    last = jnp.logical_or(is_ctx, tpos == n_l - 1)

    def glu(x):
        return x[:, :D_GROUP] * jax.nn.sigmoid(x[:, D_GROUP:])

    halo = 16
    ext[0:halo, :] = jnp.where(first, 0.0, glu(pa_prev[...]))
    ext[halo:halo + SEQ_TILE, :] = glu(pa[...])
    ext[halo + SEQ_TILE:, :] = jnp.where(last, 0.0, glu(pa_next[...]))
    acc = jnp.zeros((SEQ_TILE, D_GROUP), F32) + cb[...]
    for j in range(CONV_A):
        acc = acc + cw[j:j + 1, :] * ext[pl.ds(halo - (CONV_A - 1) // 2 + j, SEQ_TILE), :]
    out[:, :D_GROUP] = _silu(_layernorm(acc, alg[...], alb[...]))

    x = pb[...]
    u = x[:, :D_GROUP]
    v = _layernorm(x[:, D_GROUP:], blg[...], blb[...]).astype(BF16)
    hd = D_GROUP // HEADS_B
    rows = []
    for c in range(SEQ_TILE // CHUNK_B):
        vc = v[c * CHUNK_B:(c + 1) * CHUNK_B, :]
        heads = [jnp.dot(ws[hh], vc[:, hh * hd:(hh + 1) * hd], preferred_element_type=F32)
                 for hh in range(HEADS_B)]
        rows.append(jnp.concatenate(heads, axis=1) + bsm[...])
    out[:, D_GROUP:] = u * jnp.concatenate(rows, axis=0)


def _mixab(pa, pb, prm, n_lat, n_l):
    rows = pa.shape[0]
    n_tiles = rows // SEQ_TILE
    halo = 16
    per = SEQ_TILE // halo
    full = lambda a: pl.BlockSpec(a.shape, lambda i: (0,) * a.ndim)
    in_specs = [pl.BlockSpec((SEQ_TILE, 512), lambda i: (i, 0)),
                pl.BlockSpec((halo, 512), lambda i: (jnp.maximum(i * per - 1, 0), 0)),
                pl.BlockSpec((halo, 512), lambda i: (jnp.minimum((i + 1) * per, rows // halo - 1), 0)),
                pl.BlockSpec((SEQ_TILE, 512), lambda i: (i, 0))] + [full(a) for a in prm]
    return pl.pallas_call(
        functools.partial(_mixab_body, n_lat, n_l),
        grid=(n_tiles,), in_specs=in_specs,
        out_specs=pl.BlockSpec((SEQ_TILE, 512), lambda i: (i, 0)),
        out_shape=jax.ShapeDtypeStruct((rows, 512), F32),
        scratch_shapes=[pltpu.VMEM((SEQ_TILE + 2 * halo, D_GROUP), F32)],
        compiler_params=_cparams(1), name="mixab",
    )(pa, pa, pa, pb, *prm)


def _scan_tile(n_b, n_l, b, ph, j):
    lat = b * n_l + jnp.where(ph == 0, j - 1, n_l - j)
    return jnp.where(j == 0, n_b * n_l + b, lat)


def _scan_specs(n_b, n_l, rows, cols):
    tid = functools.partial(_scan_tile, n_b, n_l)
    per = SEQ_TILE // 8
    cur = pl.BlockSpec((SEQ_TILE, cols), lambda b, ph, j: (tid(b, ph, j), 0))
    prev = pl.BlockSpec((8, cols), lambda b, ph, j: (jnp.maximum(tid(b, ph, j) * per - 1, 0), 0))
    nxt = pl.BlockSpec((8, cols),
                       lambda b, ph, j: (jnp.minimum((tid(b, ph, j) + 1) * per, rows // 8 - 1), 0))
    return cur, prev, nxt


def _scan_out_spec(n_b, n_l, cols):
    def imap(b, ph, j):
        return (jnp.where(ph == 0, n_b * n_l + b, _scan_tile(n_b, n_l, b, 1, j)), 0)
    return pl.BlockSpec((SEQ_TILE, cols), imap)


def _scan_flags(n_l):
    ph = pl.program_id(1)
    j = pl.program_id(2)
    is_ctx = j == 0
    tpos = jnp.where(ph == 0, j - 1, n_l - j)
    first = jnp.logical_or(is_ctx, tpos == 0)
    last = jnp.logical_or(is_ctx, tpos == n_l - 1)
    slot = jnp.where(is_ctx, 0, tpos + 1)
    return ph, j, first, last, slot


def _mamba_body(n_l, cur_ref, prev_ref, next_ref, dt_ref, cw, cb, dtb, alog, dsk, ng,
                out, ext, yf, st, xbc_s, a_s, dt_s, y_s):
    ph, j, first, last, slot = _scan_flags(n_l)
    fwd = ph == 0

    @pl.when(j == 0)
    def _():
        st[...] = jnp.zeros_like(st)

    zx = D_GROUP
    ext[0:8, :] = jnp.where(first, 0.0, prev_ref[:, zx:])
    ext[8:8 + SEQ_TILE, :] = cur_ref[:, zx:]
    ext[8 + SEQ_TILE:, :] = jnp.where(last, 0.0, next_ref[:, zx:])
    acc = jnp.zeros((SEQ_TILE, C_XBC), F32) + cb[...]
    for jj in range(CONV_C):
        acc = acc + cw[jj:jj + 1, :] * ext[pl.ds(8 - (CONV_C - 1) // 2 + jj, SEQ_TILE), :]
    xbc_s[...] = _silu(acc)
    dt = jax.nn.softplus(dt_ref[...] + dtb[ph])
    dt_s[...] = dt
    a_s[...] = dt * (-jnp.exp(alog[ph]))

    L = CHUNK_C
    r_i = _iota((L, L), 0)
    c_i = _iota((L, L), 1)
    ltri = (r_i >= c_i).astype(F32)
    mask = jnp.where(fwd, r_i - c_i, c_i - r_i) >= 0
    sgn = jnp.where(fwd, 1.0, -1.0)
    hd = HEAD_DIM_C

    def chunk(it, carry):
        q = jnp.where(fwd, it, SEQ_TILE // L - 1 - it)
        rows = pl.ds(pl.multiple_of(q * L, L), L)
        a_q = a_s[rows, :]
        cs = _dot_mask_lhs(ltri, a_q)
        tot = cs[L - 1:L, :]
        e = jnp.where(fwd, cs, cs - a_q)
        e_t = e.T
        dt_q = dt_s[rows, :]
        xbc = xbc_s[rows, :]
        ys = []
        for g in range(2):
            bm = xbc[:, D_GROUP + g * STATE_C:D_GROUP + (g + 1) * STATE_C]
            cm = xbc[:, D_GROUP + 2 * STATE_C + g * STATE_C:D_GROUP + 2 * STATE_C + (g + 1) * STATE_C]
            bm_t = bm.T.astype(BF16)
            cm_b = cm.astype(BF16)
            gmat = jnp.dot(cm_b, bm_t, preferred_element_type=F32)
            for h in (2 * g, 2 * g + 1):
                ecol = e[:, h:h + 1]
                erow = e_t[h:h + 1, :]
                totc = tot[:, h:h + 1]
                xdt = xbc[:, h * hd:(h + 1) * hd] * dt_q[:, h:h + 1]
                lmat = jnp.exp(jnp.where(mask, sgn * (ecol - erow), -1e30))
                offs = jnp.exp(jnp.where(fwd, ecol, totc - ecol))
                stw = jnp.exp(jnp.where(fwd, totc - ecol, ecol))
                s_prev = st[h]
                y = _dot(gmat * lmat, xdt) + offs * _dot(cm_b, s_prev)
                st[h] = jnp.exp(totc) * s_prev + _dot(bm_t, xdt * stw)
                ys.append(y)
        y_s[rows, :] = jnp.concatenate(ys, axis=1)
        return carry

    lax.fori_loop(0, SEQ_TILE // L, chunk, 0)

    @pl.when(fwd)
    def _():
        yf[slot] = y_s[...]

    @pl.when(ph == 1)
    def _():
        y = yf[slot] + y_s[...] + dsk[...] * xbc_s[:, :D_GROUP]
        t = y * _silu(cur_ref[:, :D_GROUP])
        out[...] = t * lax.rsqrt(jnp.mean(t * t, axis=-1, keepdims=True) + EPS) * ng[...]


def _mamba(pc, prm, n_b, n_l):
    rows = pc.shape[0]
    cur, prev, nxt = _scan_specs(n_b, n_l, rows, D_GROUP + C_XBC)
    tid = functools.partial(_scan_tile, n_b, n_l)
    dt_spec = pl.BlockSpec((SEQ_TILE, LANES),
                           lambda b, ph, j: (tid(b, ph, j), (D_GROUP + C_XBC) // LANES + ph))
    full = lambda a: pl.BlockSpec(a.shape, lambda b, ph, j: (0,) * a.ndim)
    return pl.pallas_call(
        functools.partial(_mamba_body, n_l),
        grid=(n_b, 2, n_l + 1),
        in_specs=[cur, prev, nxt, dt_spec] + [full(a) for a in prm],
        out_specs=_scan_out_spec(n_b, n_l, D_GROUP),
        out_shape=jax.ShapeDtypeStruct((rows, D_GROUP), F32),
        scratch_shapes=[pltpu.VMEM((SEQ_TILE + 16, C_XBC), F32),
                        pltpu.VMEM((n_l + 1, SEQ_TILE, D_GROUP), F32),
                        pltpu.VMEM((HEADS_C, STATE_C, HEAD_DIM_C), F32),
                        pltpu.VMEM((SEQ_TILE, C_XBC), F32),
                        pltpu.VMEM((SEQ_TILE, LANES), F32),
                        pltpu.VMEM((SEQ_TILE, LANES), F32),
                        pltpu.VMEM((SEQ_TILE, D_GROUP), F32)],
        compiler_params=_cparams(3), name="mamba",
    )(pc, pc, pc, pc, *prm)


def _rwkv_body(n_l, cur_ref, prev_ref, next_ref, mup, mun, w0, w2, a0, a2, g2, kkw, kaw, rkw,
               gng, gnb, out, yf, st, r_s, v_s, kap_s, alp_s, kd_s, lw_s, y_s,
               phi_s, psi_s, sin_s, left_s, amat_s, ara_s, tinv_s):
    ph, j, first, last, slot = _scan_flags(n_l)
    fwd = ph == 0

    @pl.when(j == 0)
    def _():
        st[...] = jnp.zeros_like(st)

    cur = cur_ref[...]
    prv = jnp.concatenate([jnp.where(first, 0.0, prev_ref[7:8, :]), cur[:SEQ_TILE - 1, :]], axis=0)
    nxt = jnp.concatenate([cur[1:, :], jnp.where(last, 0.0, next_ref[0:1, :])], axis=0)
    p = cur + mup[...] * (prv - cur) + mun[...] * (nxt - cur)
    G = D_GROUP
    r = p[:, :G]
    k = p[:, G:2 * G]
    v = p[:, 2 * G:3 * G]
    blk = _head_block_ones(G, HEAD_DIM_D)
    kkr = k * kkw[...]
    kk = kkr * lax.rsqrt(_dot_mask_rhs(kkr * kkr, blk) + 1e-12)

    def rate(d_static=None):
        if d_static is None:
            ad = jnp.where(fwd, p[:, 3 * G + 2 * LORA_W:3 * G + 2 * LORA_W + LORA_A],
                           p[:, 3 * G + 2 * LORA_W + LORA_A:3 * G + 2 * LORA_W + 2 * LORA_A])
            a = jax.nn.sigmoid(a0[ph] + _dot(ad, a2[ph]))
        else:
            o = 3 * G + 2 * LORA_W + d_static * LORA_A
            a = jax.nn.sigmoid(a0[d_static] + _dot(p[:, o:o + LORA_A], a2[d_static]))
        return a, k * (1.0 + (a - 1.0) * kaw[...])

    a_d, kd_d = rate()
    wd = jnp.where(fwd, p[:, 3 * G:3 * G + LORA_W], p[:, 3 * G + LORA_W:3 * G + 2 * LORA_W])
    w = w0[ph] + _dot(jnp.tanh(wd), w2[ph])
    lw_s[...] = -math.exp(-0.5) * jax.nn.sigmoid(w)
    r_s[...] = r
    v_s[...] = v
    kap_s[...] = kk
    alp_s[...] = a_d * kk
    kd_s[...] = kd_d

    C = CHUNK_D
    r_i = _iota((C, C), 0)
    c_i = _iota((C, C), 1)
    lag = jnp.where(fwd, r_i - c_i, c_i - r_i)
    tri = lag >= 0
    tri_f = tri.astype(F32)
    strict = lag > 0
    eye = (r_i == c_i).astype(F32)
    hd = HEAD_DIM_D

    n_q = SEQ_TILE // C
    eye_k = (_iota((hd, hd), 0) == _iota((hd, hd), 1)).astype(F32)

    units = [(q, h) for q in range(n_q) for h in range(HEADS_D)]
    rows_of = lambda q: slice(q * C, (q + 1) * C)
    lanes_of = lambda h: slice(h * hd, (h + 1) * hd)
    left, right, p_tot = {}, {}, {}
    for q in range(n_q):
        rows = rows_of(q)
        lw = lw_s[rows, :]
        incl = _dot_mask_lhs(tri_f, lw)
        tot = jnp.where(fwd, incl[C - 1:C, :], incl[0:1, :])
        p_inv = jnp.exp(-incl)
        kap_h = kap_s[rows, :] * jnp.exp(incl - lw)
        r_h = r_s[rows, :] * jnp.exp(incl)
        alp_b = alp_s[rows, :] * p_inv
        k_b = kd_s[rows, :] * p_inv
        ptq = jnp.exp(tot)
        for h in range(HEADS_D):
            sl = lanes_of(h)
            left[q, h] = jnp.concatenate([kap_h[:, sl], r_h[:, sl]], axis=0)
            right[q, h] = jnp.concatenate([alp_b[:, sl], k_b[:, sl]], axis=0)
            p_tot[q, h] = ptq[:, sl]
            left_s[q * HEADS_D + h] = left[q, h]
    m1 = {u: _dot(left[u], right[u], NT) for u in units}
    a_vk, pw, tinv = {}, {}, {}
    for u in units:
        qh = u[0] * HEADS_D + u[1]
        a_vk[u] = jnp.where(strict, m1[u][:C, C:], 0.0)
        amat_s[qh] = jnp.concatenate([a_vk[u], jnp.where(tri, m1[u][C:, C:], 0.0)], axis=0)
        ara_s[qh] = jnp.where(tri, m1[u][C:, :C], 0.0)
        pw[u] = jnp.where(strict, -m1[u][:C, :C], 0.0)
        tinv[u] = eye + pw[u]
    for _ in range(5):
        pw = {u: _dot(pw[u], pw[u]) for u in units}
        upd = {u: _dot(tinv[u], pw[u]) for u in units}
        tinv = {u: tinv[u] + upd[u] for u in units}
    wmat = {u: _dot(tinv[u], right[u][:C], TN) for u in units}
    kw = {u: _dot(left[u][:C], wmat[u], TN) for u in units}
    aw = {u: _dot(a_vk[u], wmat[u], TN) for u in units}
    vk = {u: _dot(v_s[rows_of(u[0]), lanes_of(u[1])], right[u][C:] - aw[u], TN) for u in units}
    for u in units:
        qh = u[0] * HEADS_D + u[1]
        phi_s[qh] = (eye_k - kw[u]) * p_tot[u]
        psi_s[qh] = vk[u] * p_tot[u]
        tinv_s[qh] = tinv[u]

    for it in range(n_q):
        q = jnp.where(fwd, it, n_q - 1 - it)
        for h in range(HEADS_D):
            qh = q * HEADS_D + h
            s_in = st[h]
            sin_s[qh] = s_in
            st[h] = _dot(s_in, phi_s[qh]) + psi_s[qh]

    qh_of = lambda u: u[0] * HEADS_D + u[1]
    x0 = {u: _dot(left_s[qh_of(u)], sin_s[qh_of(u)], NT) for u in units}
    x1 = {u: _dot(amat_s[qh_of(u)], v_s[rows_of(u[0]), lanes_of(u[1])]) for u in units}
    uu = {u: _dot(tinv_s[qh_of(u)], x0[u][:C] + x1[u][:C]) for u in units}
    au = {u: _dot(ara_s[qh_of(u)], uu[u]) for u in units}
    for q in range(n_q):
        y_s[rows_of(q), :] = jnp.concatenate(
            [x0[q, h][C:] + x1[q, h][C:] - au[q, h] for h in range(HEADS_D)], axis=1)

    @pl.when(fwd)
    def _():
        yf[slot] = y_s[...]

    @pl.when(ph == 1)
    def _():
        y = yf[slot] + y_s[...]
        inv = 1.0 / HEAD_DIM_D
        mu = _dot_mask_rhs(y, blk) * inv
        yc = y - mu
        var = _dot_mask_rhs(yc * yc, blk) * inv
        yn = yc * lax.rsqrt(var + GN_EPS_D) * gng[...] + gnb[...]
        _, kd_f = rate(0)
        bonus = _dot_mask_rhs(r * (kd_f + kd_d) * rkw[...], blk) * v
        gate = _dot(jax.nn.sigmoid(p[:, 3 * G + 2 * LORA_W + 2 * LORA_A:]), g2[...])
        out[...] = (yn + bonus) * gate


def _rwkv(pd, prm, n_b, n_l):
    rows = pd.shape[0]
    cur, prev, nxt = _scan_specs(n_b, n_l, rows, D_COLS)
    full = lambda a: pl.BlockSpec(a.shape, lambda b, ph, j: (0,) * a.ndim)
    tile = lambda: pltpu.VMEM((SEQ_TILE, D_GROUP), F32)
    per_qh = lambda r, c: pltpu.VMEM((SEQ_TILE // CHUNK_D * HEADS_D, r, c), F32)
    return pl.pallas_call(
        functools.partial(_rwkv_body, n_l),
        grid=(n_b, 2, n_l + 1),
        in_specs=[cur, prev, nxt] + [full(a) for a in prm],
        out_specs=_scan_out_spec(n_b, n_l, D_GROUP),
        out_shape=jax.ShapeDtypeStruct((rows, D_GROUP), F32),
        scratch_shapes=[pltpu.VMEM((n_l + 1, SEQ_TILE, D_GROUP), F32),
                        pltpu.VMEM((HEADS_D, HEAD_DIM_D, HEAD_DIM_D), F32)] + [tile() for _ in range(7)]
        + [per_qh(HEAD_DIM_D, HEAD_DIM_D) for _ in range(3)]
        + [per_qh(2 * CHUNK_D, HEAD_DIM_D), per_qh(2 * CHUNK_D, CHUNK_D),
           per_qh(CHUNK_D, CHUNK_D), per_qh(CHUNK_D, CHUNK_D)],
        compiler_params=_cparams(3), name="rwkv",
    )(pd, pd, pd, *prm)


def _outproj_body(yab, yc, yd, h_ref, gate, sh, sc, g, wab, wc, wd, oh, of):
    mix = (_dot(yab[...], wab[...]) + _dot(yc[...], wc[...]) + _dot(yd[...], wd[...]))
    h = h_ref[...] + gate[...] * mix
    oh[...] = h
    of[...] = _norm_mod(h, g[...], sh[...], sc[...]).astype(BF16)


def _outproj(yab, yc, yd, h, mods, g, ws, n_lat, per_seq):
    rows = h.shape[0]
    tile = lambda c: pl.BlockSpec((TOK_TILE, c), lambda i: (i, 0))
    full = lambda a: pl.BlockSpec(a.shape, lambda i: (0,) * a.ndim)
    return pl.pallas_call(
        _outproj_body, grid=(rows // TOK_TILE,),
        in_specs=[tile(512), tile(D_GROUP), tile(D_GROUP), tile(D_MODEL),
                  _mod_spec(n_lat, per_seq, 2), _mod_spec(n_lat, per_seq, 3),
                  _mod_spec(n_lat, per_seq, 4), full(g)] + [full(a) for a in ws],
        out_specs=[tile(D_MODEL), tile(D_MODEL)],
        out_shape=[jax.ShapeDtypeStruct((rows, D_MODEL), F32),
                   jax.ShapeDtypeStruct((rows, D_MODEL), BF16)],
        compiler_params=_cparams(1), name="outproj",
    )(yab, yc, yd, h, mods, mods, mods, g, *ws)


TOPK_TILE = 512


SUBLANES = 8


def _cmpx(lst, i, j):
    a, b = lst[i], lst[j]
    lst[i] = jnp.maximum(a, b)
    lst[j] = jnp.minimum(a, b)


def _bitonic_sort_desc(lst):
    n = len(lst)
    k = 2
    while k <= n:
        j = k // 2
        while j >= 1:
            for i in range(n):
                p = i ^ j
                if p > i:
                    if (i & k) == 0:
                        _cmpx(lst, i, p)
                    else:
                        _cmpx(lst, p, i)
            j //= 2
        k *= 2


def _bitonic_merge_desc(lst):
    n = len(lst)
    j = n // 2
    while j >= 1:
        for i in range(n):
            p = i ^ j
            if p > i:
                _cmpx(lst, i, p)
        j //= 2


def _merge_top(a, b):
    n = len(a)
    c = [jnp.maximum(a[i], b[n - 1 - i]) for i in range(n)]
    _bitonic_merge_desc(c)
    return c


def _merge_sublanes(lst):
    for shift in (4, 2, 1):
        lst = _merge_top(lst, [pltpu.roll(a, shift, axis=0) for a in lst])
    return lst


def _count_leading(pred, t):
    sel = jnp.where
    c1 = pred(t[7])
    c2 = pred(sel(c1, t[11], t[3]))
    c3 = pred(sel(c1, sel(c2, t[13], t[9]), sel(c2, t[5], t[1])))
    c4 = pred(sel(c1, sel(c2, sel(c3, t[14], t[12]), sel(c3, t[10], t[8])),
                  sel(c2, sel(c3, t[6], t[4]), sel(c3, t[2], t[0]))))
    c5 = pred(t[15])
    return (sel(c1, 8.0, 0.0) + sel(c2, 4.0, 0.0) + sel(c3, 2.0, 0.0) + sel(c4, 1.0, 0.0)
            + sel(c5, 1.0, 0.0))


def _top16_rows(s):
    lst = [s[SUBLANES * v:SUBLANES * (v + 1), :] for v in range(s.shape[0] // SUBLANES)]
    _bitonic_sort_desc(lst)
    return _merge_sublanes(lst)


def _topk_body(f_ref, wq_ref, k1_ref, k2_ref, e1_o, n_o, r2_o, e2_o, q_s):
    q_s[...] = lax.dot_general(wq_ref[...], f_ref[...], (NT, ((), ())),
                               preferred_element_type=F32).astype(BF16)
    half = PEER_DK // 2
    T = f_ref.shape[0]
    sub = _iota((SUBLANES, T), 0)

    def stack(rows):
        out = rows[SUBLANES - 1]
        for b in range(SUBLANES - 2, -1, -1):
            out = jnp.where(sub == b, rows[b], out)
        return out

    def rep(a):
        return jnp.concatenate([a] * (N_KEYS // SUBLANES), axis=0)

    def head(h, carry):
        base = pl.multiple_of(h * PEER_DK, PEER_DK)
        s1 = jnp.dot(k1_ref[...], q_s[pl.ds(base, half), :], preferred_element_type=F32)
        s2 = jnp.dot(k2_ref[...], q_s[pl.ds(base + half, half), :], preferred_element_type=F32)
        t1 = _top16_rows(s1)
        t2 = _top16_rows(s2)
        lo = stack(t2[:SUBLANES])
        hi = stack(t2[SUBLANES:])
        top = _merge_sublanes(_merge_top([t + lo for t in t1], [t + hi for t in t1]))
        z = jnp.exp(top[0] - top[0])
        for kk in range(1, PEER_TOPK):
            z = z + jnp.exp(top[kk] - top[0])
        thr = rep(top[PEER_TOPK - 1])
        t2r = [rep(t) for t in t2]
        r2 = _count_leading(lambda t: t > s2, t2r)
        n1 = _count_leading(lambda t: s1 + t >= thr, t2r)
        r2_o[h] = _pack_rows(r2)
        n_o[h] = _dup_bf16(n1)
        e1_o[h] = _dup_bf16(jnp.exp(s1 - rep(t1[0])) / rep(z))
        e2_o[h] = _pack_rows(jnp.exp(s2 - rep(t2[0])))
        return carry

    lax.fori_loop(0, PEER_HEADS, head, 0)


def _topk(f, wq_t, k1, k2):
    rows = f.shape[0]
    T = TOPK_TILE
    full = lambda a: pl.BlockSpec(a.shape, lambda i: (0,) * a.ndim)
    big = pl.BlockSpec((PEER_HEADS, N_KEYS, T), lambda i: (0, 0, i))
    big_shape = lambda dt: jax.ShapeDtypeStruct((PEER_HEADS, N_KEYS, rows), dt)
    packed = pl.BlockSpec((PEER_HEADS, N_KEYS // 2, T), lambda i: (0, 0, i))
    packed_shape = jax.ShapeDtypeStruct((PEER_HEADS, N_KEYS // 2, rows), U32)
    return pl.pallas_call(
        _topk_body, grid=(rows // T,),
        in_specs=[pl.BlockSpec((T, D_MODEL), lambda i: (i, 0)), full(wq_t), full(k1), full(k2)],
        out_specs=[big, big, packed, packed],
        out_shape=[big_shape(U32), big_shape(U32), packed_shape, packed_shape],
        scratch_shapes=[pltpu.VMEM((PEER_HEADS * PEER_DK, T), BF16)],
        compiler_params=_cparams(1), name="peer_topk",
    )(f, wq_t, k1, k2)


def _peer_body(final, n_e, f_ref, u_ref, v_ref, e1_ref, n_ref, r2_ref, e2_ref, h_ref,
               gate_ref, fg_ref, out, st_s, at_s, acc):
    s = pl.program_id(0)
    j_up = jnp.maximum(s - 1, 0) % n_e
    cur = s % 2

    @pl.when(s == 0)
    def _():
        at_s[...] = jnp.zeros_like(at_s)

    @pl.when(j_up == 0)
    def _():
        acc[...] = jnp.zeros_like(acc)

    st_s[...] = _pack_rows(lax.dot_general(u_ref[...], f_ref[...], (NT, ((), ())),
                                           preferred_element_type=F32))
    acc[...] += lax.dot_general(_unpack_rows(at_s[1 - cur]), v_ref[...], (TN, ((), ())),
                                preferred_element_type=F32)

    blk = (N_KEYS // 2, LANES)
    group = 2
    for lg in range(TOK_TILE // LANES):
        ln = slice(lg * LANES, (lg + 1) * LANES)
        for i0 in range(0, EXP_TILE // N_KEYS, group):
            g = [None] * group
            for h in range(PEER_HEADS):
                r2_blk = _unpack_rows(r2_ref[h, :, ln])
                e2_blk = _unpack_rows(e2_ref[h, :, ln])
                for r in range(group):
                    ii = i0 + r
                    n_row = _unpack_rows(jnp.broadcast_to(n_ref[h, ii:ii + 1, ln], blk))
                    e1_row = _unpack_rows(jnp.broadcast_to(e1_ref[h, ii:ii + 1, ln], blk))
                    w = jnp.where(r2_blk < n_row, e2_blk, 0.0) * e1_row
                    g[r] = w if g[r] is None else g[r] + w
            for r in range(group):
                rows = slice((i0 + r) * N_KEYS // 2, (i0 + r + 1) * N_KEYS // 2)
                at_s[cur, rows, ln] = _pack_rows(
                    _gelu_tanh(_unpack_rows(st_s[rows, ln])) * g[r])

    @pl.when(jnp.logical_and(s >= 1, j_up == n_e - 1))
    def _():
        h = h_ref[...] + gate_ref[...] * acc[...]
        if final:
            h = h * lax.rsqrt(jnp.mean(h * h, axis=-1, keepdims=True) + EPS) * fg_ref[...]
        out[...] = h


def _peer(f, u_b, v_b, layer, e1, n1, r2, e2, h, mods, fg, n_lat, per_seq, n_tok_tiles, final):
    n_e = N_EXPERTS // EXP_TILE
    rpt = EXP_TILE // N_KEYS
    n_steps = n_tok_tiles * n_e

    def tile(s, lag):
        t = jnp.clip(s - lag, 0, n_steps - 1)
        return t // n_e, t % n_e

    tok = lambda lag: pl.BlockSpec((TOK_TILE, D_MODEL), lambda s: (tile(s, lag)[0], 0))
    exp = lambda lag: pl.BlockSpec((None, EXP_TILE, D_MODEL), lambda s: (layer, tile(s, lag)[1], 0))
    rowblk = pl.BlockSpec((PEER_HEADS, rpt, TOK_TILE), lambda s: (0, tile(s, 0)[1], tile(s, 0)[0]))
    allkeys = pl.BlockSpec((PEER_HEADS, N_KEYS // 2, TOK_TILE), lambda s: (0, 0, tile(s, 0)[0]))

    def gate_map(s):
        i = tile(s, 1)[0]
        row = jnp.where(i < n_lat, i // per_seq, 2)
        return (row * 6 + 5, 0, 0)

    return pl.pallas_call(
        functools.partial(_peer_body, final, n_e),
        grid=(n_steps + 1,),
        in_specs=[tok(0), exp(0), exp(1), rowblk, rowblk, allkeys, allkeys, tok(1),
                  pl.BlockSpec((None, 1, D_MODEL), gate_map),
                  pl.BlockSpec((1, D_MODEL), lambda s: (0, 0))],
        out_specs=tok(1),
        out_shape=jax.ShapeDtypeStruct((n_tok_tiles * TOK_TILE, D_MODEL), F32),
        scratch_shapes=[pltpu.VMEM((EXP_TILE // 2, TOK_TILE), U32),
                        pltpu.VMEM((2, EXP_TILE // 2, TOK_TILE), U32),
                        pltpu.VMEM((TOK_TILE, D_MODEL), F32)],
        compiler_params=_cparams(1), name="peer",
    )(f, u_b, v_b, e1, n1, r2, e2, h, mods, fg)


def _cast_body(u_ref, v_ref, uo, vo):
    uo[...] = u_ref[...].astype(BF16)
    vo[...] = v_ref[...].astype(BF16)


def _cast_tables(u, v):
    L, n, d = u.shape
    rows = 1024
    spec = pl.BlockSpec((None, rows, d), lambda l, i: (l, i, 0))
    shape = jax.ShapeDtypeStruct((L, n, d), BF16)
    return pl.pallas_call(
        _cast_body, grid=(L, n // rows), in_specs=[spec, spec], out_specs=[spec, spec],
        out_shape=[shape, shape], compiler_params=_cparams(2), name="cast_tables",
    )(u, v)


def _sincos_2d(t_len):
    rows = t_len // GRID_W
    q = D_MODEL // 4
    freq = 10000.0 ** (-jnp.arange(q, dtype=F32) / q)
    ar = jnp.arange(rows, dtype=F32)[:, None] * freq
    ac = jnp.arange(GRID_W, dtype=F32)[:, None] * freq
    per_row = lambda a: jnp.repeat(a, GRID_W, axis=0)
    per_col = lambda a: jnp.tile(a, (rows, 1))
    return jnp.concatenate([per_row(jnp.sin(ar)), per_row(jnp.cos(ar)),
                            per_col(jnp.sin(ac)), per_col(jnp.cos(ac))], -1)


def _row(a):
    return a.reshape(1, -1).astype(F32)


def _pad_lanes(a, width=LANES):
    return jnp.pad(a, ((0, 0), (0, width - a.shape[-1])))


def kernel(x, c, ctx, c_ctx, ada_w, ada_b, norm_mix_g, norm_ffn_g, w_in, w_out, a_conv_w, a_conv_b, a_ln_g, a_ln_b, b_ln_g, b_ln_b, b_ws, b_bs, c_conv_w, c_conv_b, c_dt_bias, c_a_log, c_d, c_norm_g, d_mu_prev, d_mu_next, d_w0, d_w2, d_a0, d_a2, d_g2, d_k_k, d_k_a, d_r_k, d_gn_g, d_gn_b, peer_wq, peer_k1, peer_k2, peer_u, peer_v, final_g):
    n_b, t_lat, _ = x.shape
    t_ctx = ctx.shape[1]
    assert n_b == 2 and t_ctx == SEQ_TILE and t_lat % TOK_TILE == 0
    n_l = t_lat // SEQ_TILE
    per_seq = t_lat // TOK_TILE
    n_lat = n_b * per_seq
    n_lat256 = n_b * n_l

    cs = jnp.zeros((8, D_MODEL), F32).at[:n_b].set(c).at[n_b].set(c_ctx)
    mods_all = _ada(cs, ada_w, ada_b)
    pos = _sincos_2d(t_lat)
    u_bf, v_bf = _cast_tables(peer_u, peer_v)
    x2 = x.reshape(n_b * t_lat, D_MODEL)
    ctx2 = ctx.reshape(n_b * t_ctx, D_MODEL)

    h = None
    for i in range(DEPTH):
        last_layer = i == DEPTH - 1
        mods = mods_all[i].reshape(8 * 6, 1, D_MODEL)
        w = w_in[i]
        dtc = OFF_C + D_GROUP + C_XBC
        w_c = jnp.concatenate([w[:, OFF_C:dtc], _pad_lanes(w[:, dtc:dtc + HEADS_C]),
                               _pad_lanes(w[:, dtc + HEADS_C:dtc + 2 * HEADS_C])], axis=1)
        ws_in = tuple(a.astype(BF16) for a in (w[:, :OFF_B], w[:, OFF_B:OFF_C], w_c, w[:, OFF_D:]))
        g_mix = _row(norm_mix_g[i])
        if i == 0:
            h, pa, pb, pc, pd = _inproj((x2, ctx2, pos), mods, g_mix, ws_in, n_lat, per_seq, True)
        else:
            pa, pb, pc, pd = _inproj(h, mods, g_mix, ws_in, n_lat, per_seq, False)

        conv_a = jnp.pad(a_conv_w[i], ((0, 32 - CONV_A), (0, 0)))
        bsm = jnp.repeat(b_bs[i].T, D_GROUP // HEADS_B, axis=1)
        yab = _mixab(pa, pb, (conv_a, _row(a_conv_b[i]), _row(a_ln_g[i]), _row(a_ln_b[i]),
                              _row(b_ln_g[i]), _row(b_ln_b[i]), b_ws[i].astype(BF16), bsm),
                     n_lat256, n_l)

        conv_c = jnp.pad(c_conv_w[i], ((0, 8 - CONV_C), (0, 0)))
        dtb = _pad_lanes(c_dt_bias[i]).reshape(2, 1, LANES)
        alog = _pad_lanes(c_a_log[i]).reshape(2, 1, LANES)
        dsk = _row(jnp.repeat(c_d[i], HEAD_DIM_C))
        yc = _mamba(pc, (conv_c, _row(c_conv_b[i]), dtb, alog, dsk, _row(c_norm_g[i])), n_b, n_l)

        yd = _rwkv(pd, (_row(d_mu_prev[i]), _row(d_mu_next[i]), d_w0[i].reshape(2, 1, D_GROUP),
                        d_w2[i].astype(BF16), d_a0[i].reshape(2, 1, D_GROUP), d_a2[i].astype(BF16),
                        d_g2[i].astype(BF16), _row(d_k_k[i]), _row(d_k_a[i]), _row(d_r_k[i]),
                        _row(d_gn_g[i]), _row(d_gn_b[i])), n_b, n_l)

        wo = w_out[i].astype(BF16)
        h, f = _outproj(yab, yc, yd, h, mods, _row(norm_ffn_g[i]),
                        (wo[:512], wo[512:768], wo[768:]), n_lat, per_seq)

        e1, n1, r2, e2 = _topk(f, peer_wq[i].T.astype(BF16), peer_k1[i].astype(BF16),
                               peer_k2[i].astype(BF16))
        n_tok = n_lat if last_layer else n_lat + 1
        h = _peer(f, u_bf, v_bf, i, e1, n1, r2, e2, h,
                  mods, _row(final_g), n_lat, per_seq, n_tok, last_layer)
    return h.reshape(n_b, t_lat, D_MODEL)
```

```python
import functools
import math

import jax
import jax.numpy as jnp
from jax import lax
from jax.experimental import pallas as pl
from jax.experimental.pallas import tpu as pltpu

F32 = jnp.float32
BF16 = jnp.bfloat16
U32 = jnp.uint32
HIGHEST = lax.Precision.HIGHEST

D_MODEL = 1024
DEPTH = 2
GRID_W = 64
EPS = 1e-6
D_GROUP = 256
CONV_A = 31
CHUNK_B = 128
HEADS_B = 4
HEADS_C = 4
HEAD_DIM_C = 64
STATE_C = 128
CONV_C = 5
CHUNK_C = 128
HEADS_D = 4
HEAD_DIM_D = 64
LORA_W = 64
LORA_A = 64
LORA_G = 128
GN_EPS_D = 64e-5
N_KEYS = 128
N_EXPERTS = N_KEYS * N_KEYS
PEER_HEADS = 8
PEER_DK = 256
PEER_TOPK = 16

C_XBC = D_GROUP + 2 * 2 * STATE_C
OFF_B = 2 * D_GROUP
OFF_C = OFF_B + 2 * D_GROUP
OFF_D = OFF_C + D_GROUP + C_XBC + 2 * HEADS_C
D_COLS = 3 * D_GROUP + 2 * LORA_W + 2 * LORA_A + LORA_G
PC_COLS = D_GROUP + C_XBC + 2 * 128

LANES = 128
SEQ_TILE = 256
TOK_TILE = 512
CHUNK_D = 64
EXP_TILE = 2048
VMEM_LIMIT = 56 * 1024 * 1024


def _cparams(n_axes):
    return pltpu.CompilerParams(dimension_semantics=("arbitrary",) * n_axes,
                                vmem_limit_bytes=VMEM_LIMIT)


def _silu(x):
    return x * jax.nn.sigmoid(x)


def _dot(a, b, dims=None):
    a = a.astype(BF16)
    b = b.astype(BF16)
    if dims is None:
        return jnp.dot(a, b, preferred_element_type=F32)
    return lax.dot_general(a, b, (dims, ((), ())), preferred_element_type=F32)


def _dot_hi(a, b):
    return jnp.dot(a, b, precision=HIGHEST, preferred_element_type=F32)


def _split_bf16(x, terms):
    parts = []
    for _ in range(terms):
        part = x.astype(BF16)
        parts.append(part)
        x = x - part.astype(F32)
    return parts


def _dot_mask_rhs(a, mask, terms=3):
    m = mask.astype(BF16)
    return sum(jnp.dot(p, m, preferred_element_type=F32) for p in _split_bf16(a, terms))


def _dot_mask_lhs(mask, b, terms=3):
    m = mask.astype(BF16)
    return sum(jnp.dot(m, p, preferred_element_type=F32) for p in _split_bf16(b, terms))


NT = ((1,), (1,))
TN = ((0,), (0,))


def _gelu_tanh(x):
    k1 = math.sqrt(2.0 / math.pi)
    k3 = 0.044715 * k1
    hx = 0.5 * x
    return hx + hx * jnp.tanh(x * (k1 + k3 * (x * x)))


def _pack_rows(x):
    return pltpu.bitcast(x.astype(BF16), U32)


def _unpack_rows(x):
    return pltpu.bitcast(x, BF16)


def _dup_bf16(x):
    hi = pltpu.bitcast(x.astype(BF16).astype(F32), U32)
    return hi | (hi >> 16)


def _iota(shape, axis):
    return lax.broadcasted_iota(jnp.int32, shape, axis)


def _head_block_ones(n, width):
    return (_iota((n, n), 0) // width == _iota((n, n), 1) // width).astype(F32)


def _ada_body(cs_ref, w_ref, b_ref, o_ref):
    o_ref[...] = _dot_hi(_silu(cs_ref[...]), w_ref[...]) + b_ref[...]


def _ada(cs, ada_w, ada_b):
    L = ada_w.shape[0]
    nb = 1536
    return pl.pallas_call(
        _ada_body,
        grid=(L, 6 * D_MODEL // nb),
        in_specs=[pl.BlockSpec((8, D_MODEL), lambda l, n: (0, 0)),
                  pl.BlockSpec((None, D_MODEL, nb), lambda l, n: (l, 0, n)),
                  pl.BlockSpec((None, 1, nb), lambda l, n: (l, 0, n))],
        out_specs=pl.BlockSpec((None, 8, nb), lambda l, n: (l, 0, n)),
        out_shape=jax.ShapeDtypeStruct((L, 8, 6 * D_MODEL), F32),
        compiler_params=_cparams(2),
        name="ada",
    )(cs, ada_w, ada_b.reshape(L, 1, 6 * D_MODEL))


def _norm_mod(h, g, shift, scale):
    xn = h * lax.rsqrt(jnp.mean(h * h, axis=-1, keepdims=True) + EPS) * g
    return xn * (1.0 + scale) + shift


def _inproj_tail(h, sh_ref, sc_ref, g_ref, wa, wb, wc, wd, oa, ob, oc, od):
    xm = _norm_mod(h, g_ref[...], sh_ref[...], sc_ref[...]).astype(BF16)
    oa[...] = jnp.dot(xm, wa[...], preferred_element_type=F32)
    ob[...] = jnp.dot(xm, wb[...], preferred_element_type=F32)
    oc[...] = jnp.dot(xm, wc[...], preferred_element_type=F32)
    od[...] = jnp.dot(xm, wd[...], preferred_element_type=F32)


def _inproj_first_body(n_lat, x_ref, ctx_ref, pos_ref, sh_ref, sc_ref, g_ref, wa, wb, wc, wd,
                       oh, oa, ob, oc, od):
    i = pl.program_id(0)
    h = jnp.where(i < n_lat, x_ref[...] + pos_ref[...], ctx_ref[...])
    oh[...] = h
    _inproj_tail(h, sh_ref, sc_ref, g_ref, wa, wb, wc, wd, oa, ob, oc, od)


def _inproj_body(h_ref, sh_ref, sc_ref, g_ref, wa, wb, wc, wd, oa, ob, oc, od):
    _inproj_tail(h_ref[...], sh_ref, sc_ref, g_ref, wa, wb, wc, wd, oa, ob, oc, od)


def _mod_spec(n_lat, per_seq, k):
    def imap(i):
        row = jnp.where(i < n_lat, i // per_seq, 2)
        return (row * 6 + k, 0, 0)
    return pl.BlockSpec((None, 1, D_MODEL), imap)


def _inproj(h_or_parts, mods, g, ws, n_lat, per_seq, first):
    wa, wb, wc, wd = ws
    n_tiles = n_lat + 1
    rows = n_tiles * TOK_TILE
    tile = lambda c: pl.BlockSpec((TOK_TILE, c), lambda i: (i, 0))
    full = lambda a: pl.BlockSpec(a.shape, lambda i: (0,) * a.ndim)
    common_specs = [_mod_spec(n_lat, per_seq, 0), _mod_spec(n_lat, per_seq, 1), full(g),
                    full(wa), full(wb), full(wc), full(wd)]
    out_specs = [tile(512), tile(512), tile(PC_COLS), tile(D_COLS)]
    out_shape = [jax.ShapeDtypeStruct((rows, c), F32) for c in (512, 512, PC_COLS, D_COLS)]
    if first:
        x2, ctx2, pos = h_or_parts
        in_specs = [pl.BlockSpec((TOK_TILE, D_MODEL), lambda i: (jnp.minimum(i, n_lat - 1), 0)),
                    pl.BlockSpec((TOK_TILE, D_MODEL), lambda i: (0, 0)),
                    pl.BlockSpec((TOK_TILE, D_MODEL), lambda i: (i % per_seq, 0))] + common_specs
        return pl.pallas_call(
            functools.partial(_inproj_first_body, n_lat),
            grid=(n_tiles,), in_specs=in_specs,
            out_specs=[tile(D_MODEL)] + out_specs,
            out_shape=[jax.ShapeDtypeStruct((rows, D_MODEL), F32)] + out_shape,
            compiler_params=_cparams(1), name="inproj_first",
        )(x2, ctx2, pos, mods, mods, g, wa, wb, wc, wd)
    return pl.pallas_call(
        _inproj_body, grid=(n_tiles,), in_specs=[tile(D_MODEL)] + common_specs,
        out_specs=out_specs, out_shape=out_shape,
        compiler_params=_cparams(1), name="inproj",
    )(h_or_parts, mods, mods, g, wa, wb, wc, wd)


def _layernorm(x, g, b, eps=1e-5):
    mu = jnp.mean(x, axis=-1, keepdims=True)
    xc = x - mu
    var = jnp.mean(xc * xc, axis=-1, keepdims=True)
    return xc * lax.rsqrt(var + eps) * g + b


def _mixab_body(n_lat, n_l, pa, pa_prev, pa_next, pb, cw, cb, alg, alb, blg, blb, ws, bsm,
                out, ext):
    i = pl.program_id(0)
    is_ctx = i >= n_lat
    tpos = i % n_l
    first = jnp.logical_or(is_ctx, tpos == 0)
    last = jnp.logical_or(is_ctx, tpos == n_l - 1)

    def glu(x):
        return x[:, :D_GROUP] * jax.nn.sigmoid(x[:, D_GROUP:])

    halo = 16
    ext[0:halo, :] = jnp.where(first, 0.0, glu(pa_prev[...]))
    ext[halo:halo + SEQ_TILE, :] = glu(pa[...])
    ext[halo + SEQ_TILE:, :] = jnp.where(last, 0.0, glu(pa_next[...]))
    acc = jnp.zeros((SEQ_TILE, D_GROUP), F32) + cb[...]
    for j in range(CONV_A):
        acc = acc + cw[j:j + 1, :] * ext[pl.ds(halo - (CONV_A - 1) // 2 + j, SEQ_TILE), :]
    out[:, :D_GROUP] = _silu(_layernorm(acc, alg[...], alb[...]))

    x = pb[...]
    u = x[:, :D_GROUP]
    v = _layernorm(x[:, D_GROUP:], blg[...], blb[...]).astype(BF16)
    hd = D_GROUP // HEADS_B
    rows = []
    for c in range(SEQ_TILE // CHUNK_B):
        vc = v[c * CHUNK_B:(c + 1) * CHUNK_B, :]
        heads = [jnp.dot(ws[hh], vc[:, hh * hd:(hh + 1) * hd], preferred_element_type=F32)
                 for hh in range(HEADS_B)]
        rows.append(jnp.concatenate(heads, axis=1) + bsm[...])
    out[:, D_GROUP:] = u * jnp.concatenate(rows, axis=0)


def _mixab(pa, pb, prm, n_lat, n_l):
    rows = pa.shape[0]
    n_tiles = rows // SEQ_TILE
    halo = 16
    per = SEQ_TILE // halo
    full = lambda a: pl.BlockSpec(a.shape, lambda i: (0,) * a.ndim)
    in_specs = [pl.BlockSpec((SEQ_TILE, 512), lambda i: (i, 0)),
                pl.BlockSpec((halo, 512), lambda i: (jnp.maximum(i * per - 1, 0), 0)),
                pl.BlockSpec((halo, 512), lambda i: (jnp.minimum((i + 1) * per, rows // halo - 1), 0)),
                pl.BlockSpec((SEQ_TILE, 512), lambda i: (i, 0))] + [full(a) for a in prm]
    return pl.pallas_call(
        functools.partial(_mixab_body, n_lat, n_l),
        grid=(n_tiles,), in_specs=in_specs,
        out_specs=pl.BlockSpec((SEQ_TILE, 512), lambda i: (i, 0)),
        out_shape=jax.ShapeDtypeStruct((rows, 512), F32),
        scratch_shapes=[pltpu.VMEM((SEQ_TILE + 2 * halo, D_GROUP), F32)],
        compiler_params=_cparams(1), name="mixab",
    )(pa, pa, pa, pb, *prm)


def _scan_tile(n_b, n_l, b, ph, j):
    lat = b * n_l + jnp.where(ph == 0, j - 1, n_l - j)
    return jnp.where(j == 0, n_b * n_l + b, lat)


def _scan_specs(n_b, n_l, rows, cols):
    tid = functools.partial(_scan_tile, n_b, n_l)
    per = SEQ_TILE // 8
    cur = pl.BlockSpec((SEQ_TILE, cols), lambda b, ph, j: (tid(b, ph, j), 0))
    prev = pl.BlockSpec((8, cols), lambda b, ph, j: (jnp.maximum(tid(b, ph, j) * per - 1, 0), 0))
    nxt = pl.BlockSpec((8, cols),
                       lambda b, ph, j: (jnp.minimum((tid(b, ph, j) + 1) * per, rows // 8 - 1), 0))
    return cur, prev, nxt


def _scan_out_spec(n_b, n_l, cols):
    def imap(b, ph, j):
        return (jnp.where(ph == 0, n_b * n_l + b, _scan_tile(n_b, n_l, b, 1, j)), 0)
    return pl.BlockSpec((SEQ_TILE, cols), imap)


def _scan_flags(n_l):
    ph = pl.program_id(1)
    j = pl.program_id(2)
    is_ctx = j == 0
    tpos = jnp.where(ph == 0, j - 1, n_l - j)
    first = jnp.logical_or(is_ctx, tpos == 0)
    last = jnp.logical_or(is_ctx, tpos == n_l - 1)
    slot = jnp.where(is_ctx, 0, tpos + 1)
    return ph, j, first, last, slot


def _mamba_body(n_l, cur_ref, prev_ref, next_ref, dt_ref, cw, cb, dtb, alog, dsk, ng,
                out, ext, yf, st, xbc_s, a_s, dt_s, y_s):
    ph, j, first, last, slot = _scan_flags(n_l)
    fwd = ph == 0

    @pl.when(j == 0)
    def _():
        st[...] = jnp.zeros_like(st)

    zx = D_GROUP
    ext[0:8, :] = jnp.where(first, 0.0, prev_ref[:, zx:])
    ext[8:8 + SEQ_TILE, :] = cur_ref[:, zx:]
    ext[8 + SEQ_TILE:, :] = jnp.where(last, 0.0, next_ref[:, zx:])
    acc = jnp.zeros((SEQ_TILE, C_XBC), F32) + cb[...]
    for jj in range(CONV_C):
        acc = acc + cw[jj:jj + 1, :] * ext[pl.ds(8 - (CONV_C - 1) // 2 + jj, SEQ_TILE), :]
    xbc_s[...] = _silu(acc)
    dt = jax.nn.softplus(dt_ref[...] + dtb[ph])
    dt_s[...] = dt
    a_s[...] = dt * (-jnp.exp(alog[ph]))

    L = CHUNK_C
    r_i = _iota((L, L), 0)
    c_i = _iota((L, L), 1)
    ltri = (r_i >= c_i).astype(F32)
    mask = jnp.where(fwd, r_i - c_i, c_i - r_i) >= 0
    sgn = jnp.where(fwd, 1.0, -1.0)
    hd = HEAD_DIM_C

    def chunk(it, carry):
        q = jnp.where(fwd, it, SEQ_TILE // L - 1 - it)
        rows = pl.ds(pl.multiple_of(q * L, L), L)
        a_q = a_s[rows, :]
        cs = _dot_mask_lhs(ltri, a_q)
        tot = cs[L - 1:L, :]
        e = jnp.where(fwd, cs, cs - a_q)
        e_t = e.T
        dt_q = dt_s[rows, :]
        xbc = xbc_s[rows, :]
        ys = []
        for g in range(2):
            bm = xbc[:, D_GROUP + g * STATE_C:D_GROUP + (g + 1) * STATE_C]
            cm = xbc[:, D_GROUP + 2 * STATE_C + g * STATE_C:D_GROUP + 2 * STATE_C + (g + 1) * STATE_C]
            bm_t = bm.T.astype(BF16)
            cm_b = cm.astype(BF16)
            gmat = jnp.dot(cm_b, bm_t, preferred_element_type=F32)
            for h in (2 * g, 2 * g + 1):
                ecol = e[:, h:h + 1]
                erow = e_t[h:h + 1, :]
                totc = tot[:, h:h + 1]
                xdt = xbc[:, h * hd:(h + 1) * hd] * dt_q[:, h:h + 1]
                lmat = jnp.exp(jnp.where(mask, sgn * (ecol - erow), -1e30))
                offs = jnp.exp(jnp.where(fwd, ecol, totc - ecol))
                stw = jnp.exp(jnp.where(fwd, totc - ecol, ecol))
                s_prev = st[h]
                y = _dot(gmat * lmat, xdt) + offs * _dot(cm_b, s_prev)
                st[h] = jnp.exp(totc) * s_prev + _dot(bm_t, xdt * stw)
                ys.append(y)
        y_s[rows, :] = jnp.concatenate(ys, axis=1)
        return carry

    lax.fori_loop(0, SEQ_TILE // L, chunk, 0)

    @pl.when(fwd)
    def _():
        yf[slot] = y_s[...]

    @pl.when(ph == 1)
    def _():
        y = yf[slot] + y_s[...] + dsk[...] * xbc_s[:, :D_GROUP]
        t = y * _silu(cur_ref[:, :D_GROUP])
        out[...] = t * lax.rsqrt(jnp.mean(t * t, axis=-1, keepdims=True) + EPS) * ng[...]


def _mamba(pc, prm, n_b, n_l):
    rows = pc.shape[0]
    cur, prev, nxt = _scan_specs(n_b, n_l, rows, D_GROUP + C_XBC)
    tid = functools.partial(_scan_tile, n_b, n_l)
    dt_spec = pl.BlockSpec((SEQ_TILE, LANES),
                           lambda b, ph, j: (tid(b, ph, j), (D_GROUP + C_XBC) // LANES + ph))
    full = lambda a: pl.BlockSpec(a.shape, lambda b, ph, j: (0,) * a.ndim)
    return pl.pallas_call(
        functools.partial(_mamba_body, n_l),
        grid=(n_b, 2, n_l + 1),
        in_specs=[cur, prev, nxt, dt_spec] + [full(a) for a in prm],
        out_specs=_scan_out_spec(n_b, n_l, D_GROUP),
        out_shape=jax.ShapeDtypeStruct((rows, D_GROUP), F32),
        scratch_shapes=[pltpu.VMEM((SEQ_TILE + 16, C_XBC), F32),
                        pltpu.VMEM((n_l + 1, SEQ_TILE, D_GROUP), F32),
                        pltpu.VMEM((HEADS_C, STATE_C, HEAD_DIM_C), F32),
                        pltpu.VMEM((SEQ_TILE, C_XBC), F32),
                        pltpu.VMEM((SEQ_TILE, LANES), F32),
                        pltpu.VMEM((SEQ_TILE, LANES), F32),
                        pltpu.VMEM((SEQ_TILE, D_GROUP), F32)],
        compiler_params=_cparams(3), name="mamba",
    )(pc, pc, pc, pc, *prm)


def _rwkv_body(n_l, cur_ref, prev_ref, next_ref, mup, mun, w0, w2, a0, a2, g2, kkw, kaw, rkw,
               gng, gnb, out, yf, st, r_s, v_s, kap_s, alp_s, kd_s, lw_s, y_s,
               phi_s, psi_s, sin_s, left_s, amat_s, ara_s, tinv_s):
    ph, j, first, last, slot = _scan_flags(n_l)
    fwd = ph == 0

    @pl.when(j == 0)
    def _():
        st[...] = jnp.zeros_like(st)

    cur = cur_ref[...]
    prv = jnp.concatenate([jnp.where(first, 0.0, prev_ref[7:8, :]), cur[:SEQ_TILE - 1, :]], axis=0)
    nxt = jnp.concatenate([cur[1:, :], jnp.where(last, 0.0, next_ref[0:1, :])], axis=0)
    p = cur + mup[...] * (prv - cur) + mun[...] * (nxt - cur)
    G = D_GROUP
    r = p[:, :G]
    k = p[:, G:2 * G]
    v = p[:, 2 * G:3 * G]
    blk = _head_block_ones(G, HEAD_DIM_D)
    kkr = k * kkw[...]
    kk = kkr * lax.rsqrt(_dot_mask_rhs(kkr * kkr, blk) + 1e-12)

    def rate(d_static=None):
        if d_static is None:
            ad = jnp.where(fwd, p[:, 3 * G + 2 * LORA_W:3 * G + 2 * LORA_W + LORA_A],
                           p[:, 3 * G + 2 * LORA_W + LORA_A:3 * G + 2 * LORA_W + 2 * LORA_A])
            a = jax.nn.sigmoid(a0[ph] + _dot(ad, a2[ph]))
        else:
            o = 3 * G + 2 * LORA_W + d_static * LORA_A
            a = jax.nn.sigmoid(a0[d_static] + _dot(p[:, o:o + LORA_A], a2[d_static]))
        return a, k * (1.0 + (a - 1.0) * kaw[...])

    a_d, kd_d = rate()
    wd = jnp.where(fwd, p[:, 3 * G:3 * G + LORA_W], p[:, 3 * G + LORA_W:3 * G + 2 * LORA_W])
    w = w0[ph] + _dot(jnp.tanh(wd), w2[ph])
    lw_s[...] = -math.exp(-0.5) * jax.nn.sigmoid(w)
    r_s[...] = r
    v_s[...] = v
    kap_s[...] = kk
    alp_s[...] = a_d * kk
    kd_s[...] = kd_d

    C = CHUNK_D
    r_i = _iota((C, C), 0)
    c_i = _iota((C, C), 1)
    lag = jnp.where(fwd, r_i - c_i, c_i - r_i)
    tri = lag >= 0
    tri_f = tri.astype(F32)
    strict = lag > 0
    eye = (r_i == c_i).astype(F32)
    hd = HEAD_DIM_D

    n_q = SEQ_TILE // C
    eye_k = (_iota((hd, hd), 0) == _iota((hd, hd), 1)).astype(F32)

    units = [(q, h) for q in range(n_q) for h in range(HEADS_D)]
    rows_of = lambda q: slice(q * C, (q + 1) * C)
    lanes_of = lambda h: slice(h * hd, (h + 1) * hd)
    left, right, p_tot = {}, {}, {}
    for q in range(n_q):
        rows = rows_of(q)
        lw = lw_s[rows, :]
        incl = _dot_mask_lhs(tri_f, lw)
        tot = jnp.where(fwd, incl[C - 1:C, :], incl[0:1, :])
        p_inv = jnp.exp(-incl)
        kap_h = kap_s[rows, :] * jnp.exp(incl - lw)
        r_h = r_s[rows, :] * jnp.exp(incl)
        alp_b = alp_s[rows, :] * p_inv
        k_b = kd_s[rows, :] * p_inv
        ptq = jnp.exp(tot)
        for h in range(HEADS_D):
            sl = lanes_of(h)
            left[q, h] = jnp.concatenate([kap_h[:, sl], r_h[:, sl]], axis=0)
            right[q, h] = jnp.concatenate([alp_b[:, sl], k_b[:, sl]], axis=0)
            p_tot[q, h] = ptq[:, sl]
            left_s[q * HEADS_D + h] = left[q, h]
    m1 = {u: _dot(left[u], right[u], NT) for u in units}
    a_vk, pw, tinv = {}, {}, {}
    for u in units:
        qh = u[0] * HEADS_D + u[1]
        a_vk[u] = jnp.where(strict, m1[u][:C, C:], 0.0)
        amat_s[qh] = jnp.concatenate([a_vk[u], jnp.where(tri, m1[u][C:, C:], 0.0)], axis=0)
        ara_s[qh] = jnp.where(tri, m1[u][C:, :C], 0.0)
        pw[u] = jnp.where(strict, -m1[u][:C, :C], 0.0)
        tinv[u] = eye + pw[u]
    pw = {u: _dot(pw[u], pw[u]) for u in units}
    for _ in range(5):
        both = {u: _dot(jnp.concatenate([pw[u], tinv[u]], axis=0), pw[u]) for u in units}
        tinv = {u: tinv[u] + both[u][C:] for u in units}
        pw = {u: both[u][:C] for u in units}
    wmat = {u: _dot(tinv[u], right[u][:C], TN) for u in units}
    kw_aw = {u: _dot(jnp.concatenate([left[u][:C], a_vk[u]], axis=1), wmat[u], TN) for u in units}
    vk = {u: _dot(v_s[rows_of(u[0]), lanes_of(u[1])], right[u][C:] - kw_aw[u][hd:], TN)
          for u in units}
    for u in units:
        qh = u[0] * HEADS_D + u[1]
        phi_s[qh] = (eye_k - kw_aw[u][:hd]) * p_tot[u]
        psi_s[qh] = vk[u] * p_tot[u]
        tinv_s[qh] = tinv[u]

    for it in range(n_q):
        q = jnp.where(fwd, it, n_q - 1 - it)
        for h in range(HEADS_D):
            qh = q * HEADS_D + h
            s_in = st[h]
            sin_s[qh] = s_in
            st[h] = _dot(s_in, phi_s[qh]) + psi_s[qh]

    qh_of = lambda u: u[0] * HEADS_D + u[1]
    x0 = {u: _dot(left_s[qh_of(u)], sin_s[qh_of(u)], NT) for u in units}
    x1 = {u: _dot(amat_s[qh_of(u)], v_s[rows_of(u[0]), lanes_of(u[1])]) for u in units}
    uu = {u: _dot(tinv_s[qh_of(u)], x0[u][:C] + x1[u][:C]) for u in units}
    au = {u: _dot(ara_s[qh_of(u)], uu[u]) for u in units}
    for q in range(n_q):
        y_s[rows_of(q), :] = jnp.concatenate(
            [x0[q, h][C:] + x1[q, h][C:] - au[q, h] for h in range(HEADS_D)], axis=1)

    @pl.when(fwd)
    def _():
        yf[slot] = y_s[...]

    @pl.when(ph == 1)
    def _():
        y = yf[slot] + y_s[...]
        inv = 1.0 / HEAD_DIM_D
        mu = _dot_mask_rhs(y, blk) * inv
        yc = y - mu
        var = _dot_mask_rhs(yc * yc, blk) * inv
        yn = yc * lax.rsqrt(var + GN_EPS_D) * gng[...] + gnb[...]
        _, kd_f = rate(0)
        bonus = _dot_mask_rhs(r * (kd_f + kd_d) * rkw[...], blk) * v
        gate = _dot(jax.nn.sigmoid(p[:, 3 * G + 2 * LORA_W + 2 * LORA_A:]), g2[...])
        out[...] = (yn + bonus) * gate


def _rwkv(pd, prm, n_b, n_l):
    rows = pd.shape[0]
    cur, prev, nxt = _scan_specs(n_b, n_l, rows, D_COLS)
    full = lambda a: pl.BlockSpec(a.shape, lambda b, ph, j: (0,) * a.ndim)
    tile = lambda: pltpu.VMEM((SEQ_TILE, D_GROUP), F32)
    per_qh = lambda r, c: pltpu.VMEM((SEQ_TILE // CHUNK_D * HEADS_D, r, c), F32)
    return pl.pallas_call(
        functools.partial(_rwkv_body, n_l),
        grid=(n_b, 2, n_l + 1),
        in_specs=[cur, prev, nxt] + [full(a) for a in prm],
        out_specs=_scan_out_spec(n_b, n_l, D_GROUP),
        out_shape=jax.ShapeDtypeStruct((rows, D_GROUP), F32),
        scratch_shapes=[pltpu.VMEM((n_l + 1, SEQ_TILE, D_GROUP), F32),
                        pltpu.VMEM((HEADS_D, HEAD_DIM_D, HEAD_DIM_D), F32)] + [tile() for _ in range(7)]
        + [per_qh(HEAD_DIM_D, HEAD_DIM_D) for _ in range(3)]
        + [per_qh(2 * CHUNK_D, HEAD_DIM_D), per_qh(2 * CHUNK_D, CHUNK_D),
           per_qh(CHUNK_D, CHUNK_D), per_qh(CHUNK_D, CHUNK_D)],
        compiler_params=_cparams(3), name="rwkv",
    )(pd, pd, pd, *prm)


def _outproj_body(yab, yc, yd, h_ref, gate, sh, sc, g, wab, wc, wd, oh, of):
    mix = (_dot(yab[...], wab[...]) + _dot(yc[...], wc[...]) + _dot(yd[...], wd[...]))
    h = h_ref[...] + gate[...] * mix
    oh[...] = h
    of[...] = _norm_mod(h, g[...], sh[...], sc[...]).astype(BF16)


def _outproj(yab, yc, yd, h, mods, g, ws, n_lat, per_seq):
    rows = h.shape[0]
    tile = lambda c: pl.BlockSpec((TOK_TILE, c), lambda i: (i, 0))
    full = lambda a: pl.BlockSpec(a.shape, lambda i: (0,) * a.ndim)
    return pl.pallas_call(
        _outproj_body, grid=(rows // TOK_TILE,),
        in_specs=[tile(512), tile(D_GROUP), tile(D_GROUP), tile(D_MODEL),
                  _mod_spec(n_lat, per_seq, 2), _mod_spec(n_lat, per_seq, 3),
                  _mod_spec(n_lat, per_seq, 4), full(g)] + [full(a) for a in ws],
        out_specs=[tile(D_MODEL), tile(D_MODEL)],
        out_shape=[jax.ShapeDtypeStruct((rows, D_MODEL), F32),
                   jax.ShapeDtypeStruct((rows, D_MODEL), BF16)],
        compiler_params=_cparams(1), name="outproj",
    )(yab, yc, yd, h, mods, mods, mods, g, *ws)


TOPK_TILE = 512


SUBLANES = 8


def _cmpx(lst, i, j):
    a, b = lst[i], lst[j]
    lst[i] = jnp.maximum(a, b)
    lst[j] = jnp.minimum(a, b)


def _bitonic_sort_desc(lst):
    n = len(lst)
    k = 2
    while k <= n:
        j = k // 2
        while j >= 1:
            for i in range(n):
                p = i ^ j
                if p > i:
                    if (i & k) == 0:
                        _cmpx(lst, i, p)
                    else:
                        _cmpx(lst, p, i)
            j //= 2
        k *= 2


def _bitonic_merge_desc(lst):
    n = len(lst)
    j = n // 2
    while j >= 1:
        for i in range(n):
            p = i ^ j
            if p > i:
                _cmpx(lst, i, p)
        j //= 2


def _merge_top(a, b):
    n = len(a)
    c = [jnp.maximum(a[i], b[n - 1 - i]) for i in range(n)]
    _bitonic_merge_desc(c)
    return c


def _merge_sublanes(lst):
    for shift in (4, 2, 1):
        lst = _merge_top(lst, [pltpu.roll(a, shift, axis=0) for a in lst])
    return lst


def _count_leading(pred, t):
    sel = jnp.where
    c1 = pred(t[7])
    c2 = pred(sel(c1, t[11], t[3]))
    c3 = pred(sel(c1, sel(c2, t[13], t[9]), sel(c2, t[5], t[1])))
    c4 = pred(sel(c1, sel(c2, sel(c3, t[14], t[12]), sel(c3, t[10], t[8])),
                  sel(c2, sel(c3, t[6], t[4]), sel(c3, t[2], t[0]))))
    c5 = pred(t[15])
    return (sel(c1, 8.0, 0.0) + sel(c2, 4.0, 0.0) + sel(c3, 2.0, 0.0) + sel(c4, 1.0, 0.0)
            + sel(c5, 1.0, 0.0))


def _top16_rows(s):
    lst = [s[SUBLANES * v:SUBLANES * (v + 1), :] for v in range(s.shape[0] // SUBLANES)]
    _bitonic_sort_desc(lst)
    return _merge_sublanes(lst)


def _topk_body(f_ref, wq_ref, k1_ref, k2_ref, e1_o, n_o, r2_o, e2_o, q_s):
    q_s[...] = lax.dot_general(wq_ref[...], f_ref[...], (NT, ((), ())),
                               preferred_element_type=F32).astype(BF16)
    half = PEER_DK // 2
    T = f_ref.shape[0]
    sub = _iota((SUBLANES, T), 0)

    def stack(rows):
        out = rows[SUBLANES - 1]
        for b in range(SUBLANES - 2, -1, -1):
            out = jnp.where(sub == b, rows[b], out)
        return out

    def rep(a):
        return jnp.concatenate([a] * (N_KEYS // SUBLANES), axis=0)

    def head(h, carry):
        base = pl.multiple_of(h * PEER_DK, PEER_DK)
        s1 = jnp.dot(k1_ref[...], q_s[pl.ds(base, half), :], preferred_element_type=F32)
        s2 = jnp.dot(k2_ref[...], q_s[pl.ds(base + half, half), :], preferred_element_type=F32)
        t1 = _top16_rows(s1)
        t2 = _top16_rows(s2)
        lo = stack(t2[:SUBLANES])
        hi = stack(t2[SUBLANES:])
        top = _merge_sublanes(_merge_top([t + lo for t in t1], [t + hi for t in t1]))
        z = jnp.exp(top[0] - top[0])
        for kk in range(1, PEER_TOPK):
            z = z + jnp.exp(top[kk] - top[0])
        thr = rep(top[PEER_TOPK - 1])
        t2r = [rep(t) for t in t2]
        r2 = _count_leading(lambda t: t > s2, t2r)
        n1 = _count_leading(lambda t: s1 + t >= thr, t2r)
        r2_o[h] = _pack_rows(r2)
        n_o[h] = _dup_bf16(n1)
        e1_o[h] = _dup_bf16(jnp.exp(s1 - rep(t1[0])) / rep(z))
        e2_o[h] = _pack_rows(jnp.exp(s2 - rep(t2[0])))
        return carry

    lax.fori_loop(0, PEER_HEADS, head, 0)


def _topk(f, wq_t, k1, k2):
    rows = f.shape[0]
    T = TOPK_TILE
    full = lambda a: pl.BlockSpec(a.shape, lambda i: (0,) * a.ndim)
    big = pl.BlockSpec((PEER_HEADS, N_KEYS, T), lambda i: (0, 0, i))
    big_shape = lambda dt: jax.ShapeDtypeStruct((PEER_HEADS, N_KEYS, rows), dt)
    packed = pl.BlockSpec((PEER_HEADS, N_KEYS // 2, T), lambda i: (0, 0, i))
    packed_shape = jax.ShapeDtypeStruct((PEER_HEADS, N_KEYS // 2, rows), U32)
    return pl.pallas_call(
        _topk_body, grid=(rows // T,),
        in_specs=[pl.BlockSpec((T, D_MODEL), lambda i: (i, 0)), full(wq_t), full(k1), full(k2)],
        out_specs=[big, big, packed, packed],
        out_shape=[big_shape(U32), big_shape(U32), packed_shape, packed_shape],
        scratch_shapes=[pltpu.VMEM((PEER_HEADS * PEER_DK, T), BF16)],
        compiler_params=_cparams(1), name="peer_topk",
    )(f, wq_t, k1, k2)


def _peer_body(final, n_e, f_ref, u_ref, v_ref, e1_ref, n_ref, r2_ref, e2_ref, h_ref,
               gate_ref, fg_ref, out, st_s, at_s, acc):
    s = pl.program_id(0)
    j_up = jnp.maximum(s - 1, 0) % n_e
    cur = s % 2

    @pl.when(s == 0)
    def _():
        at_s[...] = jnp.zeros_like(at_s)

    @pl.when(j_up == 0)
    def _():
        acc[...] = jnp.zeros_like(acc)

    st_s[...] = _pack_rows(lax.dot_general(u_ref[...], f_ref[...], (NT, ((), ())),
                                           preferred_element_type=F32))
    acc[...] += lax.dot_general(_unpack_rows(at_s[1 - cur]), v_ref[...], (TN, ((), ())),
                                preferred_element_type=F32)

    blk = (N_KEYS // 2, LANES)
    group = 2
    for lg in range(TOK_TILE // LANES):
        ln = slice(lg * LANES, (lg + 1) * LANES)
        for i0 in range(0, EXP_TILE // N_KEYS, group):
            g = [None] * group
            for h in range(PEER_HEADS):
                r2_blk = _unpack_rows(r2_ref[h, :, ln])
                e2_blk = _unpack_rows(e2_ref[h, :, ln])
                for r in range(group):
                    ii = i0 + r
                    n_row = _unpack_rows(jnp.broadcast_to(n_ref[h, ii:ii + 1, ln], blk))
                    e1_row = _unpack_rows(jnp.broadcast_to(e1_ref[h, ii:ii + 1, ln], blk))
                    w = jnp.where(r2_blk < n_row, e2_blk, 0.0) * e1_row
                    g[r] = w if g[r] is None else g[r] + w
            for r in range(group):
                rows = slice((i0 + r) * N_KEYS // 2, (i0 + r + 1) * N_KEYS // 2)
                at_s[cur, rows, ln] = _pack_rows(
                    _gelu_tanh(_unpack_rows(st_s[rows, ln])) * g[r])

    @pl.when(jnp.logical_and(s >= 1, j_up == n_e - 1))
    def _():
        h = h_ref[...] + gate_ref[...] * acc[...]
        if final:
            h = h * lax.rsqrt(jnp.mean(h * h, axis=-1, keepdims=True) + EPS) * fg_ref[...]
        out[...] = h


def _peer(f, u_b, v_b, layer, e1, n1, r2, e2, h, mods, fg, n_lat, per_seq, n_tok_tiles, final):
    n_e = N_EXPERTS // EXP_TILE
    rpt = EXP_TILE // N_KEYS
    n_steps = n_tok_tiles * n_e

    def tile(s, lag):
        t = jnp.clip(s - lag, 0, n_steps - 1)
        return t // n_e, t % n_e

    tok = lambda lag: pl.BlockSpec((TOK_TILE, D_MODEL), lambda s: (tile(s, lag)[0], 0))
    exp = lambda lag: pl.BlockSpec((None, EXP_TILE, D_MODEL), lambda s: (layer, tile(s, lag)[1], 0))
    rowblk = pl.BlockSpec((PEER_HEADS, rpt, TOK_TILE), lambda s: (0, tile(s, 0)[1], tile(s, 0)[0]))
    allkeys = pl.BlockSpec((PEER_HEADS, N_KEYS // 2, TOK_TILE), lambda s: (0, 0, tile(s, 0)[0]))

    def gate_map(s):
        i = tile(s, 1)[0]
        row = jnp.where(i < n_lat, i // per_seq, 2)
        return (row * 6 + 5, 0, 0)

    return pl.pallas_call(
        functools.partial(_peer_body, final, n_e),
        grid=(n_steps + 1,),
        in_specs=[tok(0), exp(0), exp(1), rowblk, rowblk, allkeys, allkeys, tok(1),
                  pl.BlockSpec((None, 1, D_MODEL), gate_map),
                  pl.BlockSpec((1, D_MODEL), lambda s: (0, 0))],
        out_specs=tok(1),
        out_shape=jax.ShapeDtypeStruct((n_tok_tiles * TOK_TILE, D_MODEL), F32),
        scratch_shapes=[pltpu.VMEM((EXP_TILE // 2, TOK_TILE), U32),
                        pltpu.VMEM((2, EXP_TILE // 2, TOK_TILE), U32),
                        pltpu.VMEM((TOK_TILE, D_MODEL), F32)],
        compiler_params=_cparams(1), name="peer",
    )(f, u_b, v_b, e1, n1, r2, e2, h, mods, fg)


def _cast_body(u_ref, v_ref, uo, vo):
    uo[...] = u_ref[...].astype(BF16)
    vo[...] = v_ref[...].astype(BF16)


def _cast_tables(u, v):
    L, n, d = u.shape
    rows = 1024
    spec = pl.BlockSpec((None, rows, d), lambda l, i: (l, i, 0))
    shape = jax.ShapeDtypeStruct((L, n, d), BF16)
    return pl.pallas_call(
        _cast_body, grid=(L, n // rows), in_specs=[spec, spec], out_specs=[spec, spec],
        out_shape=[shape, shape], compiler_params=_cparams(2), name="cast_tables",
    )(u, v)


def _sincos_2d(t_len):
    rows = t_len // GRID_W
    q = D_MODEL // 4
    freq = 10000.0 ** (-jnp.arange(q, dtype=F32) / q)
    ar = jnp.arange(rows, dtype=F32)[:, None] * freq
    ac = jnp.arange(GRID_W, dtype=F32)[:, None] * freq
    per_row = lambda a: jnp.repeat(a, GRID_W, axis=0)
    per_col = lambda a: jnp.tile(a, (rows, 1))
    return jnp.concatenate([per_row(jnp.sin(ar)), per_row(jnp.cos(ar)),
                            per_col(jnp.sin(ac)), per_col(jnp.cos(ac))], -1)


def _row(a):
    return a.reshape(1, -1).astype(F32)


def _pad_lanes(a, width=LANES):
    return jnp.pad(a, ((0, 0), (0, width - a.shape[-1])))


def kernel(x, c, ctx, c_ctx, ada_w, ada_b, norm_mix_g, norm_ffn_g, w_in, w_out, a_conv_w, a_conv_b, a_ln_g, a_ln_b, b_ln_g, b_ln_b, b_ws, b_bs, c_conv_w, c_conv_b, c_dt_bias, c_a_log, c_d, c_norm_g, d_mu_prev, d_mu_next, d_w0, d_w2, d_a0, d_a2, d_g2, d_k_k, d_k_a, d_r_k, d_gn_g, d_gn_b, peer_wq, peer_k1, peer_k2, peer_u, peer_v, final_g):
    n_b, t_lat, _ = x.shape
    t_ctx = ctx.shape[1]
    assert n_b == 2 and t_ctx == SEQ_TILE and t_lat % TOK_TILE == 0
    n_l = t_lat // SEQ_TILE
    per_seq = t_lat // TOK_TILE
    n_lat = n_b * per_seq
    n_lat256 = n_b * n_l

    cs = jnp.zeros((8, D_MODEL), F32).at[:n_b].set(c).at[n_b].set(c_ctx)
    mods_all = _ada(cs, ada_w, ada_b)
    pos = _sincos_2d(t_lat)
    u_bf, v_bf = _cast_tables(peer_u, peer_v)
    x2 = x.reshape(n_b * t_lat, D_MODEL)
    ctx2 = ctx.reshape(n_b * t_ctx, D_MODEL)

    h = None
    for i in range(DEPTH):
        last_layer = i == DEPTH - 1
        mods = mods_all[i].reshape(8 * 6, 1, D_MODEL)
        w = w_in[i]
        dtc = OFF_C + D_GROUP + C_XBC
        w_c = jnp.concatenate([w[:, OFF_C:dtc], _pad_lanes(w[:, dtc:dtc + HEADS_C]),
                               _pad_lanes(w[:, dtc + HEADS_C:dtc + 2 * HEADS_C])], axis=1)
        ws_in = tuple(a.astype(BF16) for a in (w[:, :OFF_B], w[:, OFF_B:OFF_C], w_c, w[:, OFF_D:]))
        g_mix = _row(norm_mix_g[i])
        if i == 0:
            h, pa, pb, pc, pd = _inproj((x2, ctx2, pos), mods, g_mix, ws_in, n_lat, per_seq, True)
        else:
            pa, pb, pc, pd = _inproj(h, mods, g_mix, ws_in, n_lat, per_seq, False)

        conv_a = jnp.pad(a_conv_w[i], ((0, 32 - CONV_A), (0, 0)))
        bsm = jnp.repeat(b_bs[i].T, D_GROUP // HEADS_B, axis=1)
        yab = _mixab(pa, pb, (conv_a, _row(a_conv_b[i]), _row(a_ln_g[i]), _row(a_ln_b[i]),
                              _row(b_ln_g[i]), _row(b_ln_b[i]), b_ws[i].astype(BF16), bsm),
                     n_lat256, n_l)

        conv_c = jnp.pad(c_conv_w[i], ((0, 8 - CONV_C), (0, 0)))
        dtb = _pad_lanes(c_dt_bias[i]).reshape(2, 1, LANES)
        alog = _pad_lanes(c_a_log[i]).reshape(2, 1, LANES)
        dsk = _row(jnp.repeat(c_d[i], HEAD_DIM_C))
        yc = _mamba(pc, (conv_c, _row(c_conv_b[i]), dtb, alog, dsk, _row(c_norm_g[i])), n_b, n_l)

        yd = _rwkv(pd, (_row(d_mu_prev[i]), _row(d_mu_next[i]), d_w0[i].reshape(2, 1, D_GROUP),
                        d_w2[i].astype(BF16), d_a0[i].reshape(2, 1, D_GROUP), d_a2[i].astype(BF16),
                        d_g2[i].astype(BF16), _row(d_k_k[i]), _row(d_k_a[i]), _row(d_r_k[i]),
                        _row(d_gn_g[i]), _row(d_gn_b[i])), n_b, n_l)

        wo = w_out[i].astype(BF16)
        h, f = _outproj(yab, yc, yd, h, mods, _row(norm_ffn_g[i]),
                        (wo[:512], wo[512:768], wo[768:]), n_lat, per_seq)

        e1, n1, r2, e2 = _topk(f, peer_wq[i].T.astype(BF16), peer_k1[i].astype(BF16),
                               peer_k2[i].astype(BF16))
        n_tok = n_lat if last_layer else n_lat + 1
        h = _peer(f, u_bf, v_bf, i, e1, n1, r2, e2, h,
                  mods, _row(final_g), n_lat, per_seq, n_tok, last_layer)
    return h.reshape(n_b, t_lat, D_MODEL)
```

```python
import functools
import math

import jax
import jax.numpy as jnp
from jax import lax
from jax.experimental import pallas as pl
from jax.experimental.pallas import tpu as pltpu

F32 = jnp.float32
BF16 = jnp.bfloat16
U32 = jnp.uint32
HIGHEST = lax.Precision.HIGHEST

D_MODEL = 1024
DEPTH = 2
GRID_W = 64
EPS = 1e-6
D_GROUP = 256
CONV_A = 31
CHUNK_B = 128
HEADS_B = 4
HEADS_C = 4
HEAD_DIM_C = 64
STATE_C = 128
CONV_C = 5
CHUNK_C = 128
HEADS_D = 4
HEAD_DIM_D = 64
LORA_W = 64
LORA_A = 64
LORA_G = 128
GN_EPS_D = 64e-5
N_KEYS = 128
N_EXPERTS = N_KEYS * N_KEYS
PEER_HEADS = 8
PEER_DK = 256
PEER_TOPK = 16

C_XBC = D_GROUP + 2 * 2 * STATE_C
OFF_B = 2 * D_GROUP
OFF_C = OFF_B + 2 * D_GROUP
OFF_D = OFF_C + D_GROUP + C_XBC + 2 * HEADS_C
D_COLS = 3 * D_GROUP + 2 * LORA_W + 2 * LORA_A + LORA_G
PC_COLS = D_GROUP + C_XBC + 2 * 128

LANES = 128
SUBLANES = 8
SEQ_TILE = 256
TOK_TILE = 512
TOPK_TILE = 512
MOD_ROWS = SUBLANES
CHUNK_D = 64
EXP_TILE = 2048
VMEM_LIMIT = 56 * 1024 * 1024


def _cparams(n_axes):
    return pltpu.CompilerParams(dimension_semantics=("arbitrary",) * n_axes,
                                vmem_limit_bytes=VMEM_LIMIT)


def _silu(x):
    return x * jax.nn.sigmoid(x)


def _dot(a, b, dims=None):
    a = a.astype(BF16)
    b = b.astype(BF16)
    if dims is None:
        return jnp.dot(a, b, preferred_element_type=F32)
    return lax.dot_general(a, b, (dims, ((), ())), preferred_element_type=F32)


def _dot_hi(a, b):
    return jnp.dot(a, b, precision=HIGHEST, preferred_element_type=F32)


def _split_bf16(x, terms):
    parts = []
    for _ in range(terms):
        part = x.astype(BF16)
        parts.append(part)
        x = x - part.astype(F32)
    return parts


def _dot_mask_rhs(a, mask, terms=3):
    m = mask.astype(BF16)
    return sum(jnp.dot(p, m, preferred_element_type=F32) for p in _split_bf16(a, terms))


def _dot_mask_lhs(mask, b, terms=3):
    m = mask.astype(BF16)
    return sum(jnp.dot(m, p, preferred_element_type=F32) for p in _split_bf16(b, terms))


NT = ((1,), (1,))
TN = ((0,), (0,))


def _gelu_tanh(x):
    k1 = math.sqrt(2.0 / math.pi)
    k3 = 0.044715 * k1
    hx = 0.5 * x
    return hx + hx * jnp.tanh(x * (k1 + k3 * (x * x)))


def _pack_rows(x):
    return pltpu.bitcast(x.astype(BF16), U32)


def _unpack_rows(x):
    return pltpu.bitcast(x, BF16)


def _dup_bf16(x):
    hi = pltpu.bitcast(x.astype(BF16).astype(F32), U32)
    return hi | (hi >> 16)


def _iota(shape, axis):
    return lax.broadcasted_iota(jnp.int32, shape, axis)


def _head_block_ones(n, width):
    return (_iota((n, n), 0) // width == _iota((n, n), 1) // width).astype(F32)


def _ada_body(cs_ref, w_ref, b_ref, o_ref):
    o_ref[...] = _dot_hi(_silu(cs_ref[...]), w_ref[...]) + b_ref[...]


def _ada(cs, ada_w, ada_b):
    L = ada_w.shape[0]
    nb = 6 * D_MODEL // 4
    return pl.pallas_call(
        _ada_body,
        grid=(L, 6 * D_MODEL // nb),
        in_specs=[pl.BlockSpec((MOD_ROWS, D_MODEL), lambda l, n: (0, 0)),
                  pl.BlockSpec((None, D_MODEL, nb), lambda l, n: (l, 0, n)),
                  pl.BlockSpec((None, 1, nb), lambda l, n: (l, 0, n))],
        out_specs=pl.BlockSpec((None, MOD_ROWS, nb), lambda l, n: (l, 0, n)),
        out_shape=jax.ShapeDtypeStruct((L, MOD_ROWS, 6 * D_MODEL), F32),
        compiler_params=_cparams(2),
        name="ada",
    )(cs, ada_w, ada_b.reshape(L, 1, 6 * D_MODEL))


def _norm_mod(h, g, shift, scale):
    xn = h * lax.rsqrt(jnp.mean(h * h, axis=-1, keepdims=True) + EPS) * g
    return xn * (1.0 + scale) + shift


def _inproj_tail(h, sh_ref, sc_ref, g_ref, wa, wb, wc, wd, oa, ob, oc, od):
    xm = _norm_mod(h, g_ref[...], sh_ref[...], sc_ref[...]).astype(BF16)
    oa[...] = jnp.dot(xm, wa[...], preferred_element_type=F32)
    ob[...] = jnp.dot(xm, wb[...], preferred_element_type=F32)
    oc[...] = jnp.dot(xm, wc[...], preferred_element_type=F32)
    od[...] = jnp.dot(xm, wd[...], preferred_element_type=F32)


def _inproj_first_body(n_lat, x_ref, ctx_ref, pos_ref, sh_ref, sc_ref, g_ref, wa, wb, wc, wd,
                       oh, oa, ob, oc, od):
    i = pl.program_id(0)
    h = jnp.where(i < n_lat, x_ref[...] + pos_ref[...], ctx_ref[...])
    oh[...] = h
    _inproj_tail(h, sh_ref, sc_ref, g_ref, wa, wb, wc, wd, oa, ob, oc, od)


def _inproj_body(h_ref, sh_ref, sc_ref, g_ref, wa, wb, wc, wd, oa, ob, oc, od):
    _inproj_tail(h_ref[...], sh_ref, sc_ref, g_ref, wa, wb, wc, wd, oa, ob, oc, od)


def _mod_spec(n_lat, per_seq, k):
    def imap(i):
        row = jnp.where(i < n_lat, i // per_seq, 2)
        return (row * 6 + k, 0, 0)
    return pl.BlockSpec((None, 1, D_MODEL), imap)


def _inproj(h_or_parts, mods, g, ws, n_lat, per_seq, first):
    wa, wb, wc, wd = ws
    n_tiles = n_lat + 1
    rows = n_tiles * TOK_TILE
    tile = lambda c: pl.BlockSpec((TOK_TILE, c), lambda i: (i, 0))
    full = lambda a: pl.BlockSpec(a.shape, lambda i: (0,) * a.ndim)
    common_specs = [_mod_spec(n_lat, per_seq, 0), _mod_spec(n_lat, per_seq, 1), full(g),
                    full(wa), full(wb), full(wc), full(wd)]
    out_specs = [tile(512), tile(512), tile(PC_COLS), tile(D_COLS)]
    out_shape = [jax.ShapeDtypeStruct((rows, c), F32) for c in (512, 512, PC_COLS, D_COLS)]
    if first:
        x2, ctx2, pos = h_or_parts
        in_specs = [pl.BlockSpec((TOK_TILE, D_MODEL), lambda i: (jnp.minimum(i, n_lat - 1), 0)),
                    pl.BlockSpec((TOK_TILE, D_MODEL), lambda i: (0, 0)),
                    pl.BlockSpec((TOK_TILE, D_MODEL), lambda i: (i % per_seq, 0))] + common_specs
        return pl.pallas_call(
            functools.partial(_inproj_first_body, n_lat),
            grid=(n_tiles,), in_specs=in_specs,
            out_specs=[tile(D_MODEL)] + out_specs,
            out_shape=[jax.ShapeDtypeStruct((rows, D_MODEL), F32)] + out_shape,
            compiler_params=_cparams(1), name="inproj_first",
        )(x2, ctx2, pos, mods, mods, g, wa, wb, wc, wd)
    return pl.pallas_call(
        _inproj_body, grid=(n_tiles,), in_specs=[tile(D_MODEL)] + common_specs,
        out_specs=out_specs, out_shape=out_shape,
        compiler_params=_cparams(1), name="inproj",
    )(h_or_parts, mods, mods, g, wa, wb, wc, wd)


def _layernorm(x, g, b, eps=1e-5):
    mu = jnp.mean(x, axis=-1, keepdims=True)
    xc = x - mu
    var = jnp.mean(xc * xc, axis=-1, keepdims=True)
    return xc * lax.rsqrt(var + eps) * g + b


def _mixab_body(n_lat, n_l, pa, pa_prev, pa_next, pb, cw, cb, alg, alb, blg, blb, ws, bsm,
                out, ext):
    i = pl.program_id(0)
    is_ctx = i >= n_lat
    tpos = i % n_l
    first = jnp.logical_or(is_ctx, tpos == 0)
    last = jnp.logical_or(is_ctx, tpos == n_l - 1)

    def glu(x):
        return x[:, :D_GROUP] * jax.nn.sigmoid(x[:, D_GROUP:])

    halo = 16
    ext[0:halo, :] = jnp.where(first, 0.0, glu(pa_prev[...]))
    ext[halo:halo + SEQ_TILE, :] = glu(pa[...])
    ext[halo + SEQ_TILE:, :] = jnp.where(last, 0.0, glu(pa_next[...]))
    acc = jnp.zeros((SEQ_TILE, D_GROUP), F32) + cb[...]
    for j in range(CONV_A):
        acc = acc + cw[j:j + 1, :] * ext[pl.ds(halo - (CONV_A - 1) // 2 + j, SEQ_TILE), :]
    out[:, :D_GROUP] = _silu(_layernorm(acc, alg[...], alb[...]))

    x = pb[...]
    u = x[:, :D_GROUP]
    v = _layernorm(x[:, D_GROUP:], blg[...], blb[...]).astype(BF16)
    hd = D_GROUP // HEADS_B
    rows = []
    for c in range(SEQ_TILE // CHUNK_B):
        vc = v[c * CHUNK_B:(c + 1) * CHUNK_B, :]
        heads = [jnp.dot(ws[hh], vc[:, hh * hd:(hh + 1) * hd], preferred_element_type=F32)
                 for hh in range(HEADS_B)]
        rows.append(jnp.concatenate(heads, axis=1) + bsm[...])
    out[:, D_GROUP:] = u * jnp.concatenate(rows, axis=0)


def _mixab(pa, pb, prm, n_lat, n_l):
    rows = pa.shape[0]
    n_tiles = rows // SEQ_TILE
    halo = 16
    per = SEQ_TILE // halo
    full = lambda a: pl.BlockSpec(a.shape, lambda i: (0,) * a.ndim)
    in_specs = [pl.BlockSpec((SEQ_TILE, 512), lambda i: (i, 0)),
                pl.BlockSpec((halo, 512), lambda i: (jnp.maximum(i * per - 1, 0), 0)),
                pl.BlockSpec((halo, 512), lambda i: (jnp.minimum((i + 1) * per, rows // halo - 1), 0)),
                pl.BlockSpec((SEQ_TILE, 512), lambda i: (i, 0))] + [full(a) for a in prm]
    return pl.pallas_call(
        functools.partial(_mixab_body, n_lat, n_l),
        grid=(n_tiles,), in_specs=in_specs,
        out_specs=pl.BlockSpec((SEQ_TILE, 512), lambda i: (i, 0)),
        out_shape=jax.ShapeDtypeStruct((rows, 512), F32),
        scratch_shapes=[pltpu.VMEM((SEQ_TILE + 2 * halo, D_GROUP), F32)],
        compiler_params=_cparams(1), name="mixab",
    )(pa, pa, pa, pb, *prm)


def _scan_tile(n_b, n_l, b, ph, j):
    lat = b * n_l + jnp.where(ph == 0, j - 1, n_l - j)
    return jnp.where(j == 0, n_b * n_l + b, lat)


def _scan_specs(n_b, n_l, rows, cols):
    tid = functools.partial(_scan_tile, n_b, n_l)
    halo = SUBLANES
    per = SEQ_TILE // halo
    cur = pl.BlockSpec((SEQ_TILE, cols), lambda b, ph, j: (tid(b, ph, j), 0))
    prev = pl.BlockSpec((halo, cols),
                        lambda b, ph, j: (jnp.maximum(tid(b, ph, j) * per - 1, 0), 0))
    nxt = pl.BlockSpec((halo, cols),
                       lambda b, ph, j: (jnp.minimum((tid(b, ph, j) + 1) * per, rows // halo - 1), 0))
    return cur, prev, nxt


def _scan_out_spec(n_b, n_l, cols):
    def imap(b, ph, j):
        return (jnp.where(ph == 0, n_b * n_l + b, _scan_tile(n_b, n_l, b, 1, j)), 0)
    return pl.BlockSpec((SEQ_TILE, cols), imap)


def _scan_flags(n_l):
    ph = pl.program_id(1)
    j = pl.program_id(2)
    is_ctx = j == 0
    tpos = jnp.where(ph == 0, j - 1, n_l - j)
    first = jnp.logical_or(is_ctx, tpos == 0)
    last = jnp.logical_or(is_ctx, tpos == n_l - 1)
    slot = jnp.where(is_ctx, 0, tpos + 1)
    return ph, j, first, last, slot


def _mamba_body(n_l, cur_ref, prev_ref, next_ref, dt_ref, cw, cb, dtb, alog, dsk, ng,
                out, ext, yf, st, xbc_s, a_s, dt_s, y_s):
    ph, j, first, last, slot = _scan_flags(n_l)
    fwd = ph == 0

    @pl.when(j == 0)
    def _():
        st[...] = jnp.zeros_like(st)

    zx = D_GROUP
    halo = SUBLANES
    ext[0:halo, :] = jnp.where(first, 0.0, prev_ref[:, zx:])
    ext[halo:halo + SEQ_TILE, :] = cur_ref[:, zx:]
    ext[halo + SEQ_TILE:, :] = jnp.where(last, 0.0, next_ref[:, zx:])
    acc = jnp.zeros((SEQ_TILE, C_XBC), F32) + cb[...]
    for jj in range(CONV_C):
        acc = acc + cw[jj:jj + 1, :] * ext[pl.ds(halo - (CONV_C - 1) // 2 + jj, SEQ_TILE), :]
    xbc_s[...] = _silu(acc)
    dt = jax.nn.softplus(dt_ref[...] + dtb[ph])
    dt_s[...] = dt
    a_s[...] = dt * (-jnp.exp(alog[ph]))

    L = CHUNK_C
    r_i = _iota((L, L), 0)
    c_i = _iota((L, L), 1)
    ltri = (r_i >= c_i).astype(F32)
    mask = jnp.where(fwd, r_i - c_i, c_i - r_i) >= 0
    sgn = jnp.where(fwd, 1.0, -1.0)
    hd = HEAD_DIM_C

    def chunk(it, carry):
        q = jnp.where(fwd, it, SEQ_TILE // L - 1 - it)
        rows = pl.ds(pl.multiple_of(q * L, L), L)
        a_q = a_s[rows, :]
        cs = _dot_mask_lhs(ltri, a_q)
        tot = cs[L - 1:L, :]
        e = jnp.where(fwd, cs, cs - a_q)
        e_t = e.T
        dt_q = dt_s[rows, :]
        xbc = xbc_s[rows, :]
        ys = []
        for g in range(2):
            bm = xbc[:, D_GROUP + g * STATE_C:D_GROUP + (g + 1) * STATE_C]
            cm = xbc[:, D_GROUP + 2 * STATE_C + g * STATE_C:D_GROUP + 2 * STATE_C + (g + 1) * STATE_C]
            bm_t = bm.T.astype(BF16)
            cm_b = cm.astype(BF16)
            gmat = jnp.dot(cm_b, bm_t, preferred_element_type=F32)
            for h in (2 * g, 2 * g + 1):
                ecol = e[:, h:h + 1]
                erow = e_t[h:h + 1, :]
                totc = tot[:, h:h + 1]
                xdt = xbc[:, h * hd:(h + 1) * hd] * dt_q[:, h:h + 1]
                lmat = jnp.exp(jnp.where(mask, sgn * (ecol - erow), -1e30))
                offs = jnp.exp(jnp.where(fwd, ecol, totc - ecol))
                stw = jnp.exp(jnp.where(fwd, totc - ecol, ecol))
                s_prev = st[h]
                y = _dot(gmat * lmat, xdt) + offs * _dot(cm_b, s_prev)
                st[h] = jnp.exp(totc) * s_prev + _dot(bm_t, xdt * stw)
                ys.append(y)
        y_s[rows, :] = jnp.concatenate(ys, axis=1)
        return carry

    lax.fori_loop(0, SEQ_TILE // L, chunk, 0)

    @pl.when(fwd)
    def _():
        yf[slot] = y_s[...]

    @pl.when(ph == 1)
    def _():
        y = yf[slot] + y_s[...] + dsk[...] * xbc_s[:, :D_GROUP]
        t = y * _silu(cur_ref[:, :D_GROUP])
        out[...] = t * lax.rsqrt(jnp.mean(t * t, axis=-1, keepdims=True) + EPS) * ng[...]


def _mamba(pc, prm, n_b, n_l):
    rows = pc.shape[0]
    cur, prev, nxt = _scan_specs(n_b, n_l, rows, D_GROUP + C_XBC)
    tid = functools.partial(_scan_tile, n_b, n_l)
    dt_spec = pl.BlockSpec((SEQ_TILE, LANES),
                           lambda b, ph, j: (tid(b, ph, j), (D_GROUP + C_XBC) // LANES + ph))
    full = lambda a: pl.BlockSpec(a.shape, lambda b, ph, j: (0,) * a.ndim)
    return pl.pallas_call(
        functools.partial(_mamba_body, n_l),
        grid=(n_b, 2, n_l + 1),
        in_specs=[cur, prev, nxt, dt_spec] + [full(a) for a in prm],
        out_specs=_scan_out_spec(n_b, n_l, D_GROUP),
        out_shape=jax.ShapeDtypeStruct((rows, D_GROUP), F32),
        scratch_shapes=[pltpu.VMEM((SEQ_TILE + 2 * SUBLANES, C_XBC), F32),
                        pltpu.VMEM((n_l + 1, SEQ_TILE, D_GROUP), F32),
                        pltpu.VMEM((HEADS_C, STATE_C, HEAD_DIM_C), F32),
                        pltpu.VMEM((SEQ_TILE, C_XBC), F32),
                        pltpu.VMEM((SEQ_TILE, LANES), F32),
                        pltpu.VMEM((SEQ_TILE, LANES), F32),
                        pltpu.VMEM((SEQ_TILE, D_GROUP), F32)],
        compiler_params=_cparams(3), name="mamba",
    )(pc, pc, pc, pc, *prm)


def _rwkv_body(n_l, cur_ref, prev_ref, next_ref, mup, mun, w0, w2, a0, a2, g2, kkw, kaw, rkw,
               gng, gnb, out, yf, st, r_s, v_s, kap_s, alp_s, kd_s, lw_s, y_s,
               phi_s, psi_s, sin_s, left_s, amat_s, ara_s, tinv_s):
    ph, j, first, last, slot = _scan_flags(n_l)
    fwd = ph == 0

    @pl.when(j == 0)
    def _():
        st[...] = jnp.zeros_like(st)

    cur = cur_ref[...]
    prv = jnp.concatenate([jnp.where(first, 0.0, prev_ref[SUBLANES - 1:SUBLANES, :]),
                           cur[:SEQ_TILE - 1, :]], axis=0)
    nxt = jnp.concatenate([cur[1:, :], jnp.where(last, 0.0, next_ref[0:1, :])], axis=0)
    p = cur + mup[...] * (prv - cur) + mun[...] * (nxt - cur)
    G = D_GROUP
    r = p[:, :G]
    k = p[:, G:2 * G]
    v = p[:, 2 * G:3 * G]
    blk = _head_block_ones(G, HEAD_DIM_D)
    kkr = k * kkw[...]
    kk = kkr * lax.rsqrt(_dot_mask_rhs(kkr * kkr, blk) + 1e-12)

    def rate(d_static=None):
        if d_static is None:
            ad = jnp.where(fwd, p[:, 3 * G + 2 * LORA_W:3 * G + 2 * LORA_W + LORA_A],
                           p[:, 3 * G + 2 * LORA_W + LORA_A:3 * G + 2 * LORA_W + 2 * LORA_A])
            a = jax.nn.sigmoid(a0[ph] + _dot(ad, a2[ph]))
        else:
            o = 3 * G + 2 * LORA_W + d_static * LORA_A
            a = jax.nn.sigmoid(a0[d_static] + _dot(p[:, o:o + LORA_A], a2[d_static]))
        return a, k * (1.0 + (a - 1.0) * kaw[...])

    a_d, kd_d = rate()
    wd = jnp.where(fwd, p[:, 3 * G:3 * G + LORA_W], p[:, 3 * G + LORA_W:3 * G + 2 * LORA_W])
    w = w0[ph] + _dot(jnp.tanh(wd), w2[ph])
    lw_s[...] = -math.exp(-0.5) * jax.nn.sigmoid(w)
    r_s[...] = r
    v_s[...] = v
    kap_s[...] = kk
    alp_s[...] = a_d * kk
    kd_s[...] = kd_d

    C = CHUNK_D
    r_i = _iota((C, C), 0)
    c_i = _iota((C, C), 1)
    lag = jnp.where(fwd, r_i - c_i, c_i - r_i)
    tri = lag >= 0
    tri_f = tri.astype(F32)
    strict = lag > 0
    eye = (r_i == c_i).astype(F32)
    hd = HEAD_DIM_D

    n_q = SEQ_TILE // C
    eye_k = (_iota((hd, hd), 0) == _iota((hd, hd), 1)).astype(F32)

    units = [(q, h) for q in range(n_q) for h in range(HEADS_D)]
    rows_of = lambda q: slice(q * C, (q + 1) * C)
    lanes_of = lambda h: slice(h * hd, (h + 1) * hd)
    left, right, p_tot = {}, {}, {}
    for q in range(n_q):
        rows = rows_of(q)
        lw = lw_s[rows, :]
        incl = _dot_mask_lhs(tri_f, lw)
        tot = jnp.where(fwd, incl[C - 1:C, :], incl[0:1, :])
        p_inv = jnp.exp(-incl)
        kap_h = kap_s[rows, :] * jnp.exp(incl - lw)
        r_h = r_s[rows, :] * jnp.exp(incl)
        alp_b = alp_s[rows, :] * p_inv
        k_b = kd_s[rows, :] * p_inv
        ptq = jnp.exp(tot)
        for h in range(HEADS_D):
            sl = lanes_of(h)
            left[q, h] = jnp.concatenate([kap_h[:, sl], r_h[:, sl]], axis=0)
            right[q, h] = jnp.concatenate([alp_b[:, sl], k_b[:, sl]], axis=0)
            p_tot[q, h] = ptq[:, sl]
            left_s[q * HEADS_D + h] = left[q, h]
    m1 = {u: _dot(left[u], right[u], NT) for u in units}
    a_vk, pw, tinv = {}, {}, {}
    for u in units:
        qh = u[0] * HEADS_D + u[1]
        a_vk[u] = jnp.where(strict, m1[u][:C, C:], 0.0)
        amat_s[qh] = jnp.concatenate([a_vk[u], jnp.where(tri, m1[u][C:, C:], 0.0)], axis=0)
        ara_s[qh] = jnp.where(tri, m1[u][C:, :C], 0.0)
        pw[u] = jnp.where(strict, -m1[u][:C, :C], 0.0)
        tinv[u] = eye + pw[u]
    pw = {u: _dot(pw[u], pw[u]) for u in units}
    for _ in range(5):
        both = {u: _dot(jnp.concatenate([pw[u], tinv[u]], axis=0), pw[u]) for u in units}
        tinv = {u: tinv[u] + both[u][C:] for u in units}
        pw = {u: both[u][:C] for u in units}
    wmat = {u: _dot(tinv[u], right[u][:C], TN) for u in units}
    kw_aw = {u: _dot(jnp.concatenate([left[u][:C], a_vk[u]], axis=1), wmat[u], TN) for u in units}
    vk = {u: _dot(v_s[rows_of(u[0]), lanes_of(u[1])], right[u][C:] - kw_aw[u][hd:], TN)
          for u in units}
    for u in units:
        qh = u[0] * HEADS_D + u[1]
        phi_s[qh] = (eye_k - kw_aw[u][:hd]) * p_tot[u]
        psi_s[qh] = vk[u] * p_tot[u]
        tinv_s[qh] = tinv[u]

    for it in range(n_q):
        q = jnp.where(fwd, it, n_q - 1 - it)
        for h in range(HEADS_D):
            qh = q * HEADS_D + h
            s_in = st[h]
            sin_s[qh] = s_in
            st[h] = _dot(s_in, phi_s[qh]) + psi_s[qh]

    qh_of = lambda u: u[0] * HEADS_D + u[1]
    x0 = {u: _dot(left_s[qh_of(u)], sin_s[qh_of(u)], NT) for u in units}
    x1 = {u: _dot(amat_s[qh_of(u)], v_s[rows_of(u[0]), lanes_of(u[1])]) for u in units}
    uu = {u: _dot(tinv_s[qh_of(u)], x0[u][:C] + x1[u][:C]) for u in units}
    au = {u: _dot(ara_s[qh_of(u)], uu[u]) for u in units}
    for q in range(n_q):
        y_s[rows_of(q), :] = jnp.concatenate(
            [x0[q, h][C:] + x1[q, h][C:] - au[q, h] for h in range(HEADS_D)], axis=1)

    @pl.when(fwd)
    def _():
        yf[slot] = y_s[...]

    @pl.when(ph == 1)
    def _():
        y = yf[slot] + y_s[...]
        inv = 1.0 / HEAD_DIM_D
        mu = _dot_mask_rhs(y, blk) * inv
        yc = y - mu
        var = _dot_mask_rhs(yc * yc, blk) * inv
        yn = yc * lax.rsqrt(var + GN_EPS_D) * gng[...] + gnb[...]
        _, kd_f = rate(0)
        bonus = _dot_mask_rhs(r * (kd_f + kd_d) * rkw[...], blk) * v
        gate = _dot(jax.nn.sigmoid(p[:, 3 * G + 2 * LORA_W + 2 * LORA_A:]), g2[...])
        out[...] = (yn + bonus) * gate


def _rwkv(pd, prm, n_b, n_l):
    rows = pd.shape[0]
    cur, prev, nxt = _scan_specs(n_b, n_l, rows, D_COLS)
    full = lambda a: pl.BlockSpec(a.shape, lambda b, ph, j: (0,) * a.ndim)
    tile = lambda: pltpu.VMEM((SEQ_TILE, D_GROUP), F32)
    per_qh = lambda r, c: pltpu.VMEM((SEQ_TILE // CHUNK_D * HEADS_D, r, c), F32)
    return pl.pallas_call(
        functools.partial(_rwkv_body, n_l),
        grid=(n_b, 2, n_l + 1),
        in_specs=[cur, prev, nxt] + [full(a) for a in prm],
        out_specs=_scan_out_spec(n_b, n_l, D_GROUP),
        out_shape=jax.ShapeDtypeStruct((rows, D_GROUP), F32),
        scratch_shapes=[pltpu.VMEM((n_l + 1, SEQ_TILE, D_GROUP), F32),
                        pltpu.VMEM((HEADS_D, HEAD_DIM_D, HEAD_DIM_D), F32)] + [tile() for _ in range(7)]
        + [per_qh(HEAD_DIM_D, HEAD_DIM_D) for _ in range(3)]
        + [per_qh(2 * CHUNK_D, HEAD_DIM_D), per_qh(2 * CHUNK_D, CHUNK_D),
           per_qh(CHUNK_D, CHUNK_D), per_qh(CHUNK_D, CHUNK_D)],
        compiler_params=_cparams(3), name="rwkv",
    )(pd, pd, pd, *prm)


def _outproj_body(yab, yc, yd, h_ref, gate, sh, sc, g, wab, wc, wd, oh, of):
    mix = (_dot(yab[...], wab[...]) + _dot(yc[...], wc[...]) + _dot(yd[...], wd[...]))
    h = h_ref[...] + gate[...] * mix
    oh[...] = h
    of[...] = _norm_mod(h, g[...], sh[...], sc[...]).astype(BF16)


def _outproj(yab, yc, yd, h, mods, g, ws, n_lat, per_seq):
    rows = h.shape[0]
    tile = lambda c: pl.BlockSpec((TOK_TILE, c), lambda i: (i, 0))
    full = lambda a: pl.BlockSpec(a.shape, lambda i: (0,) * a.ndim)
    return pl.pallas_call(
        _outproj_body, grid=(rows // TOK_TILE,),
        in_specs=[tile(512), tile(D_GROUP), tile(D_GROUP), tile(D_MODEL),
                  _mod_spec(n_lat, per_seq, 2), _mod_spec(n_lat, per_seq, 3),
                  _mod_spec(n_lat, per_seq, 4), full(g)] + [full(a) for a in ws],
        out_specs=[tile(D_MODEL), tile(D_MODEL)],
        out_shape=[jax.ShapeDtypeStruct((rows, D_MODEL), F32),
                   jax.ShapeDtypeStruct((rows, D_MODEL), BF16)],
        compiler_params=_cparams(1), name="outproj",
    )(yab, yc, yd, h, mods, mods, mods, g, *ws)


def _cmpx(lst, i, j):
    a, b = lst[i], lst[j]
    lst[i] = jnp.maximum(a, b)
    lst[j] = jnp.minimum(a, b)


def _bitonic_sort_desc(lst):
    n = len(lst)
    k = 2
    while k <= n:
        j = k // 2
        while j >= 1:
            for i in range(n):
                p = i ^ j
                if p > i:
                    if (i & k) == 0:
                        _cmpx(lst, i, p)
                    else:
                        _cmpx(lst, p, i)
            j //= 2
        k *= 2


def _bitonic_merge_desc(lst):
    n = len(lst)
    j = n // 2
    while j >= 1:
        for i in range(n):
            p = i ^ j
            if p > i:
                _cmpx(lst, i, p)
        j //= 2


def _merge_top(a, b):
    n = len(a)
    c = [jnp.maximum(a[i], b[n - 1 - i]) for i in range(n)]
    _bitonic_merge_desc(c)
    return c


def _merge_sublanes(lst):
    for shift in (4, 2, 1):
        lst = _merge_top(lst, [pltpu.roll(a, shift, axis=0) for a in lst])
    return lst


def _count_leading(pred, t):
    sel = jnp.where
    c1 = pred(t[7])
    c2 = pred(sel(c1, t[11], t[3]))
    c3 = pred(sel(c1, sel(c2, t[13], t[9]), sel(c2, t[5], t[1])))
    c4 = pred(sel(c1, sel(c2, sel(c3, t[14], t[12]), sel(c3, t[10], t[8])),
                  sel(c2, sel(c3, t[6], t[4]), sel(c3, t[2], t[0]))))
    c5 = pred(t[15])
    return (sel(c1, 8.0, 0.0) + sel(c2, 4.0, 0.0) + sel(c3, 2.0, 0.0) + sel(c4, 1.0, 0.0)
            + sel(c5, 1.0, 0.0))


def _top16_rows(s):
    lst = [s[SUBLANES * v:SUBLANES * (v + 1), :] for v in range(s.shape[0] // SUBLANES)]
    _bitonic_sort_desc(lst)
    return _merge_sublanes(lst)


def _topk_body(f_ref, wq_ref, k1_ref, k2_ref, e1_o, n_o, r2_o, e2_o, q_s):
    q_s[...] = lax.dot_general(wq_ref[...], f_ref[...], (NT, ((), ())),
                               preferred_element_type=F32).astype(BF16)
    half = PEER_DK // 2
    T = f_ref.shape[0]
    sub = _iota((SUBLANES, T), 0)

    def stack(rows):
        out = rows[SUBLANES - 1]
        for b in range(SUBLANES - 2, -1, -1):
            out = jnp.where(sub == b, rows[b], out)
        return out

    def rep(a):
        return jnp.concatenate([a] * (N_KEYS // SUBLANES), axis=0)

    def head(h, carry):
        base = pl.multiple_of(h * PEER_DK, PEER_DK)
        s1 = jnp.dot(k1_ref[...], q_s[pl.ds(base, half), :], preferred_element_type=F32)
        s2 = jnp.dot(k2_ref[...], q_s[pl.ds(base + half, half), :], preferred_element_type=F32)
        t1 = _top16_rows(s1)
        t2 = _top16_rows(s2)
        lo = stack(t2[:SUBLANES])
        hi = stack(t2[SUBLANES:])
        top = _merge_sublanes(_merge_top([t + lo for t in t1], [t + hi for t in t1]))
        z = jnp.exp(top[0] - top[0])
        for kk in range(1, PEER_TOPK):
            z = z + jnp.exp(top[kk] - top[0])
        thr = rep(top[PEER_TOPK - 1])
        t2r = [rep(t) for t in t2]
        r2 = _count_leading(lambda t: t > s2, t2r)
        n1 = _count_leading(lambda t: s1 + t >= thr, t2r)
        r2_o[h] = _pack_rows(r2)
        n_o[h] = _dup_bf16(n1)
        e1_o[h] = _dup_bf16(jnp.exp(s1 - rep(t1[0])) / rep(z))
        e2_o[h] = _pack_rows(jnp.exp(s2 - rep(t2[0])))
        return carry

    lax.fori_loop(0, PEER_HEADS, head, 0)


def _topk(f, wq_t, k1, k2):
    rows = f.shape[0]
    T = TOPK_TILE
    full = lambda a: pl.BlockSpec(a.shape, lambda i: (0,) * a.ndim)
    big = pl.BlockSpec((PEER_HEADS, N_KEYS, T), lambda i: (0, 0, i))
    big_shape = lambda dt: jax.ShapeDtypeStruct((PEER_HEADS, N_KEYS, rows), dt)
    packed = pl.BlockSpec((PEER_HEADS, N_KEYS // 2, T), lambda i: (0, 0, i))
    packed_shape = jax.ShapeDtypeStruct((PEER_HEADS, N_KEYS // 2, rows), U32)
    return pl.pallas_call(
        _topk_body, grid=(rows // T,),
        in_specs=[pl.BlockSpec((T, D_MODEL), lambda i: (i, 0)), full(wq_t), full(k1), full(k2)],
        out_specs=[big, big, packed, packed],
        out_shape=[big_shape(U32), big_shape(U32), packed_shape, packed_shape],
        scratch_shapes=[pltpu.VMEM((PEER_HEADS * PEER_DK, T), BF16)],
        compiler_params=_cparams(1), name="peer_topk",
    )(f, wq_t, k1, k2)


def _peer_body(final, n_e, f_ref, u_ref, v_ref, e1_ref, n_ref, r2_ref, e2_ref, h_ref,
               gate_ref, fg_ref, out, st_s, at_s, acc):
    s = pl.program_id(0)
    j_up = jnp.maximum(s - 1, 0) % n_e
    cur = s % 2

    @pl.when(s == 0)
    def _():
        at_s[...] = jnp.zeros_like(at_s)

    @pl.when(j_up == 0)
    def _():
        acc[...] = jnp.zeros_like(acc)

    st_s[...] = _pack_rows(lax.dot_general(u_ref[...], f_ref[...], (NT, ((), ())),
                                           preferred_element_type=F32))
    acc[...] += lax.dot_general(_unpack_rows(at_s[1 - cur]), v_ref[...], (TN, ((), ())),
                                preferred_element_type=F32)

    blk = (N_KEYS // 2, LANES)
    group = 2
    for lg in range(TOK_TILE // LANES):
        ln = slice(lg * LANES, (lg + 1) * LANES)
        for i0 in range(0, EXP_TILE // N_KEYS, group):
            g = [None] * group
            for h in range(PEER_HEADS):
                r2_blk = _unpack_rows(r2_ref[h, :, ln])
                e2_blk = _unpack_rows(e2_ref[h, :, ln])
                for r in range(group):
                    ii = i0 + r
                    n_row = _unpack_rows(jnp.broadcast_to(n_ref[h, ii:ii + 1, ln], blk))
                    e1_row = _unpack_rows(jnp.broadcast_to(e1_ref[h, ii:ii + 1, ln], blk))
                    w = jnp.where(r2_blk < n_row, e2_blk, 0.0) * e1_row
                    g[r] = w if g[r] is None else g[r] + w
            for r in range(group):
                rows = slice((i0 + r) * N_KEYS // 2, (i0 + r + 1) * N_KEYS // 2)
                at_s[cur, rows, ln] = _pack_rows(
                    _gelu_tanh(_unpack_rows(st_s[rows, ln])) * g[r])

    @pl.when(jnp.logical_and(s >= 1, j_up == n_e - 1))
    def _():
        h = h_ref[...] + gate_ref[...] * acc[...]
        if final:
            h = h * lax.rsqrt(jnp.mean(h * h, axis=-1, keepdims=True) + EPS) * fg_ref[...]
        out[...] = h


def _peer(f, u_b, v_b, layer, e1, n1, r2, e2, h, mods, fg, n_lat, per_seq, n_tok_tiles, final):
    n_e = N_EXPERTS // EXP_TILE
    rpt = EXP_TILE // N_KEYS
    n_steps = n_tok_tiles * n_e

    def tile(s, lag):
        t = jnp.clip(s - lag, 0, n_steps - 1)
        return t // n_e, t % n_e

    tok = lambda lag: pl.BlockSpec((TOK_TILE, D_MODEL), lambda s: (tile(s, lag)[0], 0))
    exp = lambda lag: pl.BlockSpec((None, EXP_TILE, D_MODEL), lambda s: (layer, tile(s, lag)[1], 0))
    rowblk = pl.BlockSpec((PEER_HEADS, rpt, TOK_TILE), lambda s: (0, tile(s, 0)[1], tile(s, 0)[0]))
    allkeys = pl.BlockSpec((PEER_HEADS, N_KEYS // 2, TOK_TILE), lambda s: (0, 0, tile(s, 0)[0]))

    def gate_map(s):
        i = tile(s, 1)[0]
        row = jnp.where(i < n_lat, i // per_seq, 2)
        return (row * 6 + 5, 0, 0)

    return pl.pallas_call(
        functools.partial(_peer_body, final, n_e),
        grid=(n_steps + 1,),
        in_specs=[tok(0), exp(0), exp(1), rowblk, rowblk, allkeys, allkeys, tok(1),
                  pl.BlockSpec((None, 1, D_MODEL), gate_map),
                  pl.BlockSpec((1, D_MODEL), lambda s: (0, 0))],
        out_specs=tok(1),
        out_shape=jax.ShapeDtypeStruct((n_tok_tiles * TOK_TILE, D_MODEL), F32),
        scratch_shapes=[pltpu.VMEM((EXP_TILE // 2, TOK_TILE), U32),
                        pltpu.VMEM((2, EXP_TILE // 2, TOK_TILE), U32),
                        pltpu.VMEM((TOK_TILE, D_MODEL), F32)],
        compiler_params=_cparams(1), name="peer",
    )(f, u_b, v_b, e1, n1, r2, e2, h, mods, fg)


def _cast_body(u_ref, v_ref, uo, vo):
    uo[...] = u_ref[...].astype(BF16)
    vo[...] = v_ref[...].astype(BF16)


def _cast_tables(u, v):
    L, n, d = u.shape
    rows = 1024
    spec = pl.BlockSpec((None, rows, d), lambda l, i: (l, i, 0))
    shape = jax.ShapeDtypeStruct((L, n, d), BF16)
    return pl.pallas_call(
        _cast_body, grid=(L, n // rows), in_specs=[spec, spec], out_specs=[spec, spec],
        out_shape=[shape, shape], compiler_params=_cparams(2), name="cast_tables",
    )(u, v)


def _sincos_2d(t_len):
    rows = t_len // GRID_W
    q = D_MODEL // 4
    freq = 10000.0 ** (-jnp.arange(q, dtype=F32) / q)
    ar = jnp.arange(rows, dtype=F32)[:, None] * freq
    ac = jnp.arange(GRID_W, dtype=F32)[:, None] * freq
    per_row = lambda a: jnp.repeat(a, GRID_W, axis=0)
    per_col = lambda a: jnp.tile(a, (rows, 1))
    return jnp.concatenate([per_row(jnp.sin(ar)), per_row(jnp.cos(ar)),
                            per_col(jnp.sin(ac)), per_col(jnp.cos(ac))], -1)


def _row(a):
    return a.reshape(1, -1).astype(F32)


def _pad_lanes(a, width=LANES):
    return jnp.pad(a, ((0, 0), (0, width - a.shape[-1])))


def kernel(x, c, ctx, c_ctx, ada_w, ada_b, norm_mix_g, norm_ffn_g, w_in, w_out, a_conv_w, a_conv_b, a_ln_g, a_ln_b, b_ln_g, b_ln_b, b_ws, b_bs, c_conv_w, c_conv_b, c_dt_bias, c_a_log, c_d, c_norm_g, d_mu_prev, d_mu_next, d_w0, d_w2, d_a0, d_a2, d_g2, d_k_k, d_k_a, d_r_k, d_gn_g, d_gn_b, peer_wq, peer_k1, peer_k2, peer_u, peer_v, final_g):
    n_b, t_lat, _ = x.shape
    t_ctx = ctx.shape[1]
    assert n_b == 2 and t_ctx == SEQ_TILE and t_lat % TOK_TILE == 0
    n_l = t_lat // SEQ_TILE
    per_seq = t_lat // TOK_TILE
    n_lat = n_b * per_seq
    n_lat256 = n_b * n_l

    cs = jnp.zeros((MOD_ROWS, D_MODEL), F32).at[:n_b].set(c).at[n_b].set(c_ctx)
    mods_all = _ada(cs, ada_w, ada_b)
    pos = _sincos_2d(t_lat)
    u_bf, v_bf = _cast_tables(peer_u, peer_v)
    x2 = x.reshape(n_b * t_lat, D_MODEL)
    ctx2 = ctx.reshape(n_b * t_ctx, D_MODEL)

    h = None
    for i in range(DEPTH):
        last_layer = i == DEPTH - 1
        mods = mods_all[i].reshape(MOD_ROWS * 6, 1, D_MODEL)
        w = w_in[i]
        dtc = OFF_C + D_GROUP + C_XBC
        w_c = jnp.concatenate([w[:, OFF_C:dtc], _pad_lanes(w[:, dtc:dtc + HEADS_C]),
                               _pad_lanes(w[:, dtc + HEADS_C:dtc + 2 * HEADS_C])], axis=1)
        ws_in = tuple(a.astype(BF16) for a in (w[:, :OFF_B], w[:, OFF_B:OFF_C], w_c, w[:, OFF_D:]))
        g_mix = _row(norm_mix_g[i])
        if i == 0:
            h, pa, pb, pc, pd = _inproj((x2, ctx2, pos), mods, g_mix, ws_in, n_lat, per_seq, True)
        else:
            pa, pb, pc, pd = _inproj(h, mods, g_mix, ws_in, n_lat, per_seq, False)

        conv_a = jnp.pad(a_conv_w[i], ((0, 32 - CONV_A), (0, 0)))
        bsm = jnp.repeat(b_bs[i].T, D_GROUP // HEADS_B, axis=1)
        yab = _mixab(pa, pb, (conv_a, _row(a_conv_b[i]), _row(a_ln_g[i]), _row(a_ln_b[i]),
                              _row(b_ln_g[i]), _row(b_ln_b[i]), b_ws[i].astype(BF16), bsm),
                     n_lat256, n_l)

        conv_c = jnp.pad(c_conv_w[i], ((0, SUBLANES - CONV_C), (0, 0)))
        dtb = _pad_lanes(c_dt_bias[i]).reshape(2, 1, LANES)
        alog = _pad_lanes(c_a_log[i]).reshape(2, 1, LANES)
        dsk = _row(jnp.repeat(c_d[i], HEAD_DIM_C))
        yc = _mamba(pc, (conv_c, _row(c_conv_b[i]), dtb, alog, dsk, _row(c_norm_g[i])), n_b, n_l)

        yd = _rwkv(pd, (_row(d_mu_prev[i]), _row(d_mu_next[i]), d_w0[i].reshape(2, 1, D_GROUP),
                        d_w2[i].astype(BF16), d_a0[i].reshape(2, 1, D_GROUP), d_a2[i].astype(BF16),
                        d_g2[i].astype(BF16), _row(d_k_k[i]), _row(d_k_a[i]), _row(d_r_k[i]),
                        _row(d_gn_g[i]), _row(d_gn_b[i])), n_b, n_l)

        wo = w_out[i].astype(BF16)
        h, f = _outproj(yab, yc, yd, h, mods, _row(norm_ffn_g[i]),
                        (wo[:512], wo[512:768], wo[768:]), n_lat, per_seq)

        e1, n1, r2, e2 = _topk(f, peer_wq[i].T.astype(BF16), peer_k1[i].astype(BF16),
                               peer_k2[i].astype(BF16))
        n_tok = n_lat if last_layer else n_lat + 1
        h = _peer(f, u_bf, v_bf, i, e1, n1, r2, e2, h,
                  mods, _row(final_g), n_lat, per_seq, n_tok, last_layer)
    return h.reshape(n_b, t_lat, D_MODEL)
```

```python
import functools
import math

import jax
import jax.numpy as jnp
from jax import lax
from jax.experimental import pallas as pl
from jax.experimental.pallas import tpu as pltpu

F32 = jnp.float32
BF16 = jnp.bfloat16
U32 = jnp.uint32
HIGHEST = lax.Precision.HIGHEST

D_MODEL = 1024
DEPTH = 2
GRID_W = 64
EPS = 1e-6
D_GROUP = 256
CONV_A = 31
CHUNK_B = 128
HEADS_B = 4
HEADS_C = 4
HEAD_DIM_C = 64
STATE_C = 128
CONV_C = 5
CHUNK_C = 128
HEADS_D = 4
HEAD_DIM_D = 64
LORA_W = 64
LORA_A = 64
LORA_G = 128
GN_EPS_D = 64e-5
N_KEYS = 128
N_EXPERTS = N_KEYS * N_KEYS
PEER_HEADS = 8
PEER_DK = 256
PEER_TOPK = 16

C_XBC = D_GROUP + 2 * 2 * STATE_C
OFF_B = 2 * D_GROUP
OFF_C = OFF_B + 2 * D_GROUP
OFF_D = OFF_C + D_GROUP + C_XBC + 2 * HEADS_C
D_COLS = 3 * D_GROUP + 2 * LORA_W + 2 * LORA_A + LORA_G
PC_COLS = D_GROUP + C_XBC + 2 * 128

LANES = 128
SUBLANES = 8
SEQ_TILE = 256
TOK_TILE = 512
TOPK_TILE = 512
MOD_ROWS = SUBLANES
CHUNK_D = 64
EXP_TILE = 2048
VMEM_LIMIT = 56 * 1024 * 1024


def _cparams(n_axes):
    return pltpu.CompilerParams(dimension_semantics=("arbitrary",) * n_axes,
                                vmem_limit_bytes=VMEM_LIMIT)


def _silu(x):
    return x * jax.nn.sigmoid(x)


def _dot(a, b, dims=None):
    a = a.astype(BF16)
    b = b.astype(BF16)
    if dims is None:
        return jnp.dot(a, b, preferred_element_type=F32)
    return lax.dot_general(a, b, (dims, ((), ())), preferred_element_type=F32)


def _dot_hi(a, b):
    return jnp.dot(a, b, precision=HIGHEST, preferred_element_type=F32)


def _split_bf16(x, terms):
    parts = []
    for _ in range(terms):
        part = x.astype(BF16)
        parts.append(part)
        x = x - part.astype(F32)
    return parts


def _dot_mask_rhs(a, mask, terms=3):
    m = mask.astype(BF16)
    return sum(jnp.dot(p, m, preferred_element_type=F32) for p in _split_bf16(a, terms))


def _dot_mask_lhs(mask, b, terms=3):
    m = mask.astype(BF16)
    return sum(jnp.dot(m, p, preferred_element_type=F32) for p in _split_bf16(b, terms))


NT = ((1,), (1,))
TN = ((0,), (0,))


def _gelu_tanh(x):
    k1 = math.sqrt(2.0 / math.pi)
    k3 = 0.044715 * k1
    hx = 0.5 * x
    return hx + hx * jnp.tanh(x * (k1 + k3 * (x * x)))


def _pack_rows(x):
    return pltpu.bitcast(x.astype(BF16), U32)


def _unpack_rows(x):
    return pltpu.bitcast(x, BF16)


def _dup_bf16(x):
    hi = pltpu.bitcast(x.astype(BF16).astype(F32), U32)
    return hi | (hi >> 16)


def _iota(shape, axis):
    return lax.broadcasted_iota(jnp.int32, shape, axis)


def _head_block_ones(n, width):
    return (_iota((n, n), 0) // width == _iota((n, n), 1) // width).astype(F32)


def _ada_body(cs_ref, w_ref, b_ref, o_ref):
    o_ref[...] = _dot_hi(_silu(cs_ref[...]), w_ref[...]) + b_ref[...]


def _ada(cs, ada_w, ada_b):
    L = ada_w.shape[0]
    nb = 6 * D_MODEL // 4
    return pl.pallas_call(
        _ada_body,
        grid=(L, 6 * D_MODEL // nb),
        in_specs=[pl.BlockSpec((MOD_ROWS, D_MODEL), lambda l, n: (0, 0)),
                  pl.BlockSpec((None, D_MODEL, nb), lambda l, n: (l, 0, n)),
                  pl.BlockSpec((None, 1, nb), lambda l, n: (l, 0, n))],
        out_specs=pl.BlockSpec((None, MOD_ROWS, nb), lambda l, n: (l, 0, n)),
        out_shape=jax.ShapeDtypeStruct((L, MOD_ROWS, 6 * D_MODEL), F32),
        compiler_params=_cparams(2),
        name="ada",
    )(cs, ada_w, ada_b.reshape(L, 1, 6 * D_MODEL))


def _norm_mod(h, g, shift, scale):
    xn = h * lax.rsqrt(jnp.mean(h * h, axis=-1, keepdims=True) + EPS) * g
    return xn * (1.0 + scale) + shift


def _inproj_tail(h, sh_ref, sc_ref, g_ref, wa, wb, wc, wd, oa, ob, oc, od):
    xm = _norm_mod(h, g_ref[...], sh_ref[...], sc_ref[...]).astype(BF16)
    oa[...] = jnp.dot(xm, wa[...], preferred_element_type=F32)
    ob[...] = jnp.dot(xm, wb[...], preferred_element_type=F32)
    oc[...] = jnp.dot(xm, wc[...], preferred_element_type=F32)
    od[...] = jnp.dot(xm, wd[...], preferred_element_type=F32)


def _inproj_first_body(n_lat, x_ref, ctx_ref, pos_ref, sh_ref, sc_ref, g_ref, wa, wb, wc, wd,
                       oh, oa, ob, oc, od):
    i = pl.program_id(0)
    h = jnp.where(i < n_lat, x_ref[...] + pos_ref[...], ctx_ref[...])
    oh[...] = h
    _inproj_tail(h, sh_ref, sc_ref, g_ref, wa, wb, wc, wd, oa, ob, oc, od)


def _inproj_body(h_ref, sh_ref, sc_ref, g_ref, wa, wb, wc, wd, oa, ob, oc, od):
    _inproj_tail(h_ref[...], sh_ref, sc_ref, g_ref, wa, wb, wc, wd, oa, ob, oc, od)


def _mod_spec(n_lat, per_seq, k):
    def imap(i):
        row = jnp.where(i < n_lat, i // per_seq, 2)
        return (row * 6 + k, 0, 0)
    return pl.BlockSpec((None, 1, D_MODEL), imap)


def _inproj(h_or_parts, mods, g, ws, n_lat, per_seq, first):
    wa, wb, wc, wd = ws
    n_tiles = n_lat + 1
    rows = n_tiles * TOK_TILE
    tile = lambda c: pl.BlockSpec((TOK_TILE, c), lambda i: (i, 0))
    full = lambda a: pl.BlockSpec(a.shape, lambda i: (0,) * a.ndim)
    common_specs = [_mod_spec(n_lat, per_seq, 0), _mod_spec(n_lat, per_seq, 1), full(g),
                    full(wa), full(wb), full(wc), full(wd)]
    out_specs = [tile(512), tile(512), tile(PC_COLS), tile(D_COLS)]
    out_shape = [jax.ShapeDtypeStruct((rows, c), F32) for c in (512, 512, PC_COLS, D_COLS)]
    if first:
        x2, ctx2, pos = h_or_parts
        in_specs = [pl.BlockSpec((TOK_TILE, D_MODEL), lambda i: (jnp.minimum(i, n_lat - 1), 0)),
                    pl.BlockSpec((TOK_TILE, D_MODEL), lambda i: (0, 0)),
                    pl.BlockSpec((TOK_TILE, D_MODEL), lambda i: (i % per_seq, 0))] + common_specs
        return pl.pallas_call(
            functools.partial(_inproj_first_body, n_lat),
            grid=(n_tiles,), in_specs=in_specs,
            out_specs=[tile(D_MODEL)] + out_specs,
            out_shape=[jax.ShapeDtypeStruct((rows, D_MODEL), F32)] + out_shape,
            compiler_params=_cparams(1), name="inproj_first",
        )(x2, ctx2, pos, mods, mods, g, wa, wb, wc, wd)
    return pl.pallas_call(
        _inproj_body, grid=(n_tiles,), in_specs=[tile(D_MODEL)] + common_specs,
        out_specs=out_specs, out_shape=out_shape,
        compiler_params=_cparams(1), name="inproj",
    )(h_or_parts, mods, mods, g, wa, wb, wc, wd)


def _layernorm(x, g, b, eps=1e-5):
    mu = jnp.mean(x, axis=-1, keepdims=True)
    xc = x - mu
    var = jnp.mean(xc * xc, axis=-1, keepdims=True)
    return xc * lax.rsqrt(var + eps) * g + b


def _mixab_body(n_lat, n_l, pa, pa_prev, pa_next, pb, cw, cb, alg, alb, blg, blb, ws, bsm,
                out, ext):
    i = pl.program_id(0)
    is_ctx = i >= n_lat
    tpos = i % n_l
    first = jnp.logical_or(is_ctx, tpos == 0)
    last = jnp.logical_or(is_ctx, tpos == n_l - 1)

    def glu(x):
        return x[:, :D_GROUP] * jax.nn.sigmoid(x[:, D_GROUP:])

    halo = 16
    ext[0:halo, :] = jnp.where(first, 0.0, glu(pa_prev[...]))
    ext[halo:halo + SEQ_TILE, :] = glu(pa[...])
    ext[halo + SEQ_TILE:, :] = jnp.where(last, 0.0, glu(pa_next[...]))
    acc = jnp.zeros((SEQ_TILE, D_GROUP), F32) + cb[...]
    for j in range(CONV_A):
        acc = acc + cw[j:j + 1, :] * ext[pl.ds(halo - (CONV_A - 1) // 2 + j, SEQ_TILE), :]
    out[:, :D_GROUP] = _silu(_layernorm(acc, alg[...], alb[...]))

    x = pb[...]
    u = x[:, :D_GROUP]
    v = _layernorm(x[:, D_GROUP:], blg[...], blb[...]).astype(BF16)
    hd = D_GROUP // HEADS_B
    rows = []
    for c in range(SEQ_TILE // CHUNK_B):
        vc = v[c * CHUNK_B:(c + 1) * CHUNK_B, :]
        heads = [jnp.dot(ws[hh], vc[:, hh * hd:(hh + 1) * hd], preferred_element_type=F32)
                 for hh in range(HEADS_B)]
        rows.append(jnp.concatenate(heads, axis=1) + bsm[...])
    out[:, D_GROUP:] = u * jnp.concatenate(rows, axis=0)


def _mixab(pa, pb, prm, n_lat, n_l):
    rows = pa.shape[0]
    n_tiles = rows // SEQ_TILE
    halo = 16
    per = SEQ_TILE // halo
    full = lambda a: pl.BlockSpec(a.shape, lambda i: (0,) * a.ndim)
    in_specs = [pl.BlockSpec((SEQ_TILE, 512), lambda i: (i, 0)),
                pl.BlockSpec((halo, 512), lambda i: (jnp.maximum(i * per - 1, 0), 0)),
                pl.BlockSpec((halo, 512), lambda i: (jnp.minimum((i + 1) * per, rows // halo - 1), 0)),
                pl.BlockSpec((SEQ_TILE, 512), lambda i: (i, 0))] + [full(a) for a in prm]
    return pl.pallas_call(
        functools.partial(_mixab_body, n_lat, n_l),
        grid=(n_tiles,), in_specs=in_specs,
        out_specs=pl.BlockSpec((SEQ_TILE, 512), lambda i: (i, 0)),
        out_shape=jax.ShapeDtypeStruct((rows, 512), F32),
        scratch_shapes=[pltpu.VMEM((SEQ_TILE + 2 * halo, D_GROUP), F32)],
        compiler_params=_cparams(1), name="mixab",
    )(pa, pa, pa, pb, *prm)


def _scan_tile(n_b, n_l, b, ph, j):
    lat = b * n_l + jnp.where(ph == 0, j - 1, n_l - j)
    return jnp.where(j == 0, n_b * n_l + b, lat)


def _scan_specs(n_b, n_l, rows, cols):
    tid = functools.partial(_scan_tile, n_b, n_l)
    halo = SUBLANES
    per = SEQ_TILE // halo
    cur = pl.BlockSpec((SEQ_TILE, cols), lambda b, ph, j: (tid(b, ph, j), 0))
    prev = pl.BlockSpec((halo, cols),
                        lambda b, ph, j: (jnp.maximum(tid(b, ph, j) * per - 1, 0), 0))
    nxt = pl.BlockSpec((halo, cols),
                       lambda b, ph, j: (jnp.minimum((tid(b, ph, j) + 1) * per, rows // halo - 1), 0))
    return cur, prev, nxt


def _scan_out_spec(n_b, n_l, cols):
    def imap(b, ph, j):
        return (jnp.where(ph == 0, n_b * n_l + b, _scan_tile(n_b, n_l, b, 1, j)), 0)
    return pl.BlockSpec((SEQ_TILE, cols), imap)


def _scan_flags(n_l):
    ph = pl.program_id(1)
    j = pl.program_id(2)
    is_ctx = j == 0
    tpos = jnp.where(ph == 0, j - 1, n_l - j)
    first = jnp.logical_or(is_ctx, tpos == 0)
    last = jnp.logical_or(is_ctx, tpos == n_l - 1)
    slot = jnp.where(is_ctx, 0, tpos + 1)
    return ph, j, first, last, slot


def _mamba_body(n_l, cur_ref, prev_ref, next_ref, dt_ref, cw, cb, dtb, alog, dsk, ng,
                out, ext, yf, st, xbc_s, a_s, dt_s, y_s):
    ph, j, first, last, slot = _scan_flags(n_l)
    fwd = ph == 0

    @pl.when(j == 0)
    def _():
        st[...] = jnp.zeros_like(st)

    zx = D_GROUP
    halo = SUBLANES
    ext[0:halo, :] = jnp.where(first, 0.0, prev_ref[:, zx:])
    ext[halo:halo + SEQ_TILE, :] = cur_ref[:, zx:]
    ext[halo + SEQ_TILE:, :] = jnp.where(last, 0.0, next_ref[:, zx:])
    acc = jnp.zeros((SEQ_TILE, C_XBC), F32) + cb[...]
    for jj in range(CONV_C):
        acc = acc + cw[jj:jj + 1, :] * ext[pl.ds(halo - (CONV_C - 1) // 2 + jj, SEQ_TILE), :]
    xbc_s[...] = _silu(acc)
    dt = jax.nn.softplus(dt_ref[...] + dtb[ph])
    dt_s[...] = dt
    a_s[...] = dt * (-jnp.exp(alog[ph]))

    L = CHUNK_C
    r_i = _iota((L, L), 0)
    c_i = _iota((L, L), 1)
    ltri = (r_i >= c_i).astype(F32)
    mask = jnp.where(fwd, r_i - c_i, c_i - r_i) >= 0
    sgn = jnp.where(fwd, 1.0, -1.0)
    hd = HEAD_DIM_C

    def chunk(it, carry):
        q = jnp.where(fwd, it, SEQ_TILE // L - 1 - it)
        rows = pl.ds(pl.multiple_of(q * L, L), L)
        a_q = a_s[rows, :]
        cs = _dot_mask_lhs(ltri, a_q)
        tot = cs[L - 1:L, :]
        e = jnp.where(fwd, cs, cs - a_q)
        e_t = e.T
        dt_q = dt_s[rows, :]
        xbc = xbc_s[rows, :]
        heads = range(HEADS_C)
        grp = lambda h: h // (HEADS_C // 2)
        bm_t, cm_b, gmat = [], [], []
        for g in range(2):
            bm = xbc[:, D_GROUP + g * STATE_C:D_GROUP + (g + 1) * STATE_C]
            cm = xbc[:, D_GROUP + 2 * STATE_C + g * STATE_C:D_GROUP + 2 * STATE_C + (g + 1) * STATE_C]
            bm_t.append(bm.T.astype(BF16))
            cm_b.append(cm.astype(BF16))
        for g in range(2):
            gmat.append(jnp.dot(cm_b[g], bm_t[g], preferred_element_type=F32))
        xdt, lmat, offs, stw, dec, s_prev = [], [], [], [], [], []
        for h in heads:
            ecol = e[:, h:h + 1]
            erow = e_t[h:h + 1, :]
            totc = tot[:, h:h + 1]
            xdt.append(xbc[:, h * hd:(h + 1) * hd] * dt_q[:, h:h + 1])
            lmat.append(jnp.exp(jnp.where(mask, sgn * (ecol - erow), -1e30)))
            offs.append(jnp.exp(jnp.where(fwd, ecol, totc - ecol)))
            stw.append(jnp.exp(jnp.where(fwd, totc - ecol, ecol)))
            dec.append(jnp.exp(totc))
            s_prev.append(st[h])
        y_off = [_dot(cm_b[grp(h)], s_prev[h]) for h in heads]
        s_add = [_dot(bm_t[grp(h)], xdt[h] * stw[h]) for h in heads]
        y_dia = [_dot(gmat[grp(h)] * lmat[h], xdt[h]) for h in heads]
        for h in heads:
            st[h] = dec[h] * s_prev[h] + s_add[h]
        y_s[rows, :] = jnp.concatenate([y_dia[h] + offs[h] * y_off[h] for h in heads], axis=1)
        return carry

    lax.fori_loop(0, SEQ_TILE // L, chunk, 0)

    @pl.when(fwd)
    def _():
        yf[slot] = y_s[...]

    @pl.when(ph == 1)
    def _():
        y = yf[slot] + y_s[...] + dsk[...] * xbc_s[:, :D_GROUP]
        t = y * _silu(cur_ref[:, :D_GROUP])
        out[...] = t * lax.rsqrt(jnp.mean(t * t, axis=-1, keepdims=True) + EPS) * ng[...]


def _mamba(pc, prm, n_b, n_l):
    rows = pc.shape[0]
    cur, prev, nxt = _scan_specs(n_b, n_l, rows, D_GROUP + C_XBC)
    tid = functools.partial(_scan_tile, n_b, n_l)
    dt_spec = pl.BlockSpec((SEQ_TILE, LANES),
                           lambda b, ph, j: (tid(b, ph, j), (D_GROUP + C_XBC) // LANES + ph))
    full = lambda a: pl.BlockSpec(a.shape, lambda b, ph, j: (0,) * a.ndim)
    return pl.pallas_call(
        functools.partial(_mamba_body, n_l),
        grid=(n_b, 2, n_l + 1),
        in_specs=[cur, prev, nxt, dt_spec] + [full(a) for a in prm],
        out_specs=_scan_out_spec(n_b, n_l, D_GROUP),
        out_shape=jax.ShapeDtypeStruct((rows, D_GROUP), F32),
        scratch_shapes=[pltpu.VMEM((SEQ_TILE + 2 * SUBLANES, C_XBC), F32),
                        pltpu.VMEM((n_l + 1, SEQ_TILE, D_GROUP), F32),
                        pltpu.VMEM((HEADS_C, STATE_C, HEAD_DIM_C), F32),
                        pltpu.VMEM((SEQ_TILE, C_XBC), F32),
                        pltpu.VMEM((SEQ_TILE, LANES), F32),
                        pltpu.VMEM((SEQ_TILE, LANES), F32),
                        pltpu.VMEM((SEQ_TILE, D_GROUP), F32)],
        compiler_params=_cparams(3), name="mamba",
    )(pc, pc, pc, pc, *prm)


def _rwkv_body(n_l, cur_ref, prev_ref, next_ref, mup, mun, w0, w2, a0, a2, g2, kkw, kaw, rkw,
               gng, gnb, out, yf, st, r_s, v_s, kap_s, alp_s, kd_s, lw_s, y_s,
               phi_s, psi_s, sin_s, left_s, amat_s, ara_s, tinv_s):
    ph, j, first, last, slot = _scan_flags(n_l)
    fwd = ph == 0

    @pl.when(j == 0)
    def _():
        st[...] = jnp.zeros_like(st)

    cur = cur_ref[...]
    prv = jnp.concatenate([jnp.where(first, 0.0, prev_ref[SUBLANES - 1:SUBLANES, :]),
                           cur[:SEQ_TILE - 1, :]], axis=0)
    nxt = jnp.concatenate([cur[1:, :], jnp.where(last, 0.0, next_ref[0:1, :])], axis=0)
    p = cur + mup[...] * (prv - cur) + mun[...] * (nxt - cur)
    G = D_GROUP
    r = p[:, :G]
    k = p[:, G:2 * G]
    v = p[:, 2 * G:3 * G]
    blk = _head_block_ones(G, HEAD_DIM_D)
    kkr = k * kkw[...]
    kk = kkr * lax.rsqrt(_dot_mask_rhs(kkr * kkr, blk) + 1e-12)

    def rate(d_static=None):
        if d_static is None:
            ad = jnp.where(fwd, p[:, 3 * G + 2 * LORA_W:3 * G + 2 * LORA_W + LORA_A],
                           p[:, 3 * G + 2 * LORA_W + LORA_A:3 * G + 2 * LORA_W + 2 * LORA_A])
            a = jax.nn.sigmoid(a0[ph] + _dot(ad, a2[ph]))
        else:
            o = 3 * G + 2 * LORA_W + d_static * LORA_A
            a = jax.nn.sigmoid(a0[d_static] + _dot(p[:, o:o + LORA_A], a2[d_static]))
        return a, k * (1.0 + (a - 1.0) * kaw[...])

    a_d, kd_d = rate()
    wd = jnp.where(fwd, p[:, 3 * G:3 * G + LORA_W], p[:, 3 * G + LORA_W:3 * G + 2 * LORA_W])
    w = w0[ph] + _dot(jnp.tanh(wd), w2[ph])
    lw_s[...] = -math.exp(-0.5) * jax.nn.sigmoid(w)
    r_s[...] = r
    v_s[...] = v
    kap_s[...] = kk
    alp_s[...] = a_d * kk
    kd_s[...] = kd_d

    C = CHUNK_D
    r_i = _iota((C, C), 0)
    c_i = _iota((C, C), 1)
    lag = jnp.where(fwd, r_i - c_i, c_i - r_i)
    tri = lag >= 0
    tri_f = tri.astype(F32)
    strict = lag > 0
    eye = (r_i == c_i).astype(F32)
    hd = HEAD_DIM_D

    n_q = SEQ_TILE // C
    eye_k = (_iota((hd, hd), 0) == _iota((hd, hd), 1)).astype(F32)

    units = [(q, h) for q in range(n_q) for h in range(HEADS_D)]
    rows_of = lambda q: slice(q * C, (q + 1) * C)
    lanes_of = lambda h: slice(h * hd, (h + 1) * hd)
    left, right, p_tot = {}, {}, {}
    for q in range(n_q):
        rows = rows_of(q)
        lw = lw_s[rows, :]
        incl = _dot_mask_lhs(tri_f, lw)
        tot = jnp.where(fwd, incl[C - 1:C, :], incl[0:1, :])
        p_inv = jnp.exp(-incl)
        kap_h = kap_s[rows, :] * jnp.exp(incl - lw)
        r_h = r_s[rows, :] * jnp.exp(incl)
        alp_b = alp_s[rows, :] * p_inv
        k_b = kd_s[rows, :] * p_inv
        ptq = jnp.exp(tot)
        for h in range(HEADS_D):
            sl = lanes_of(h)
            left[q, h] = jnp.concatenate([kap_h[:, sl], r_h[:, sl]], axis=0)
            right[q, h] = jnp.concatenate([alp_b[:, sl], k_b[:, sl]], axis=0)
            p_tot[q, h] = ptq[:, sl]
            left_s[q * HEADS_D + h] = left[q, h]
    m1 = {u: _dot(left[u], right[u], NT) for u in units}
    a_vk, pw, tinv = {}, {}, {}
    for u in units:
        qh = u[0] * HEADS_D + u[1]
        a_vk[u] = jnp.where(strict, m1[u][:C, C:], 0.0)
        amat_s[qh] = jnp.concatenate([a_vk[u], jnp.where(tri, m1[u][C:, C:], 0.0)], axis=0)
        ara_s[qh] = jnp.where(tri, m1[u][C:, :C], 0.0)
        pw[u] = jnp.where(strict, -m1[u][:C, :C], 0.0)
        tinv[u] = eye + pw[u]
    pw = {u: _dot(pw[u], pw[u]) for u in units}
    for _ in range(5):
        both = {u: _dot(jnp.concatenate([pw[u], tinv[u]], axis=0), pw[u]) for u in units}
        tinv = {u: tinv[u] + both[u][C:] for u in units}
        pw = {u: both[u][:C] for u in units}
    wmat = {u: _dot(tinv[u], right[u][:C], TN) for u in units}
    kw_aw = {u: _dot(jnp.concatenate([left[u][:C], a_vk[u]], axis=1), wmat[u], TN) for u in units}
    vk = {u: _dot(v_s[rows_of(u[0]), lanes_of(u[1])], right[u][C:] - kw_aw[u][hd:], TN)
          for u in units}
    for u in units:
        qh = u[0] * HEADS_D + u[1]
        phi_s[qh] = (eye_k - kw_aw[u][:hd]) * p_tot[u]
        psi_s[qh] = vk[u] * p_tot[u]
        tinv_s[qh] = tinv[u]

    for it in range(n_q):
        q = jnp.where(fwd, it, n_q - 1 - it)
        for h in range(HEADS_D):
            qh = q * HEADS_D + h
            s_in = st[h]
            sin_s[qh] = s_in
            st[h] = _dot(s_in, phi_s[qh]) + psi_s[qh]

    qh_of = lambda u: u[0] * HEADS_D + u[1]
    x0 = {u: _dot(left_s[qh_of(u)], sin_s[qh_of(u)], NT) for u in units}
    x1 = {u: _dot(amat_s[qh_of(u)], v_s[rows_of(u[0]), lanes_of(u[1])]) for u in units}
    uu = {u: _dot(tinv_s[qh_of(u)], x0[u][:C] + x1[u][:C]) for u in units}
    au = {u: _dot(ara_s[qh_of(u)], uu[u]) for u in units}
    for q in range(n_q):
        y_s[rows_of(q), :] = jnp.concatenate(
            [x0[q, h][C:] + x1[q, h][C:] - au[q, h] for h in range(HEADS_D)], axis=1)

    @pl.when(fwd)
    def _():
        yf[slot] = y_s[...]

    @pl.when(ph == 1)
    def _():
        y = yf[slot] + y_s[...]
        inv = 1.0 / HEAD_DIM_D
        mu = _dot_mask_rhs(y, blk) * inv
        yc = y - mu
        var = _dot_mask_rhs(yc * yc, blk) * inv
        yn = yc * lax.rsqrt(var + GN_EPS_D) * gng[...] + gnb[...]
        _, kd_f = rate(0)
        bonus = _dot_mask_rhs(r * (kd_f + kd_d) * rkw[...], blk) * v
        gate = _dot(jax.nn.sigmoid(p[:, 3 * G + 2 * LORA_W + 2 * LORA_A:]), g2[...])
        out[...] = (yn + bonus) * gate


def _rwkv(pd, prm, n_b, n_l):
    rows = pd.shape[0]
    cur, prev, nxt = _scan_specs(n_b, n_l, rows, D_COLS)
    full = lambda a: pl.BlockSpec(a.shape, lambda b, ph, j: (0,) * a.ndim)
    tile = lambda: pltpu.VMEM((SEQ_TILE, D_GROUP), F32)
    per_qh = lambda r, c: pltpu.VMEM((SEQ_TILE // CHUNK_D * HEADS_D, r, c), F32)
    return pl.pallas_call(
        functools.partial(_rwkv_body, n_l),
        grid=(n_b, 2, n_l + 1),
        in_specs=[cur, prev, nxt] + [full(a) for a in prm],
        out_specs=_scan_out_spec(n_b, n_l, D_GROUP),
        out_shape=jax.ShapeDtypeStruct((rows, D_GROUP), F32),
        scratch_shapes=[pltpu.VMEM((n_l + 1, SEQ_TILE, D_GROUP), F32),
                        pltpu.VMEM((HEADS_D, HEAD_DIM_D, HEAD_DIM_D), F32)] + [tile() for _ in range(7)]
        + [per_qh(HEAD_DIM_D, HEAD_DIM_D) for _ in range(3)]
        + [per_qh(2 * CHUNK_D, HEAD_DIM_D), per_qh(2 * CHUNK_D, CHUNK_D),
           per_qh(CHUNK_D, CHUNK_D), per_qh(CHUNK_D, CHUNK_D)],
        compiler_params=_cparams(3), name="rwkv",
    )(pd, pd, pd, *prm)


def _outproj_body(yab, yc, yd, h_ref, gate, sh, sc, g, wab, wc, wd, oh, of):
    mix = (_dot(yab[...], wab[...]) + _dot(yc[...], wc[...]) + _dot(yd[...], wd[...]))
    h = h_ref[...] + gate[...] * mix
    oh[...] = h
    of[...] = _norm_mod(h, g[...], sh[...], sc[...]).astype(BF16)


def _outproj(yab, yc, yd, h, mods, g, ws, n_lat, per_seq):
    rows = h.shape[0]
    tile = lambda c: pl.BlockSpec((TOK_TILE, c), lambda i: (i, 0))
    full = lambda a: pl.BlockSpec(a.shape, lambda i: (0,) * a.ndim)
    return pl.pallas_call(
        _outproj_body, grid=(rows // TOK_TILE,),
        in_specs=[tile(512), tile(D_GROUP), tile(D_GROUP), tile(D_MODEL),
                  _mod_spec(n_lat, per_seq, 2), _mod_spec(n_lat, per_seq, 3),
                  _mod_spec(n_lat, per_seq, 4), full(g)] + [full(a) for a in ws],
        out_specs=[tile(D_MODEL), tile(D_MODEL)],
        out_shape=[jax.ShapeDtypeStruct((rows, D_MODEL), F32),
                   jax.ShapeDtypeStruct((rows, D_MODEL), BF16)],
        compiler_params=_cparams(1), name="outproj",
    )(yab, yc, yd, h, mods, mods, mods, g, *ws)


def _cmpx(lst, i, j):
    a, b = lst[i], lst[j]
    lst[i] = jnp.maximum(a, b)
    lst[j] = jnp.minimum(a, b)


def _bitonic_sort_desc(lst):
    n = len(lst)
    k = 2
    while k <= n:
        j = k // 2
        while j >= 1:
            for i in range(n):
                p = i ^ j
                if p > i:
                    if (i & k) == 0:
                        _cmpx(lst, i, p)
                    else:
                        _cmpx(lst, p, i)
            j //= 2
        k *= 2


def _bitonic_merge_desc(lst):
    n = len(lst)
    j = n // 2
    while j >= 1:
        for i in range(n):
            p = i ^ j
            if p > i:
                _cmpx(lst, i, p)
        j //= 2


def _merge_top(a, b):
    n = len(a)
    c = [jnp.maximum(a[i], b[n - 1 - i]) for i in range(n)]
    _bitonic_merge_desc(c)
    return c


def _merge_sublanes(lst):
    for shift in (4, 2, 1):
        lst = _merge_top(lst, [pltpu.roll(a, shift, axis=0) for a in lst])
    return lst


def _count_leading(pred, t):
    sel = jnp.where
    c1 = pred(t[7])
    c2 = pred(sel(c1, t[11], t[3]))
    c3 = pred(sel(c1, sel(c2, t[13], t[9]), sel(c2, t[5], t[1])))
    c4 = pred(sel(c1, sel(c2, sel(c3, t[14], t[12]), sel(c3, t[10], t[8])),
                  sel(c2, sel(c3, t[6], t[4]), sel(c3, t[2], t[0]))))
    c5 = pred(t[15])
    return (sel(c1, 8.0, 0.0) + sel(c2, 4.0, 0.0) + sel(c3, 2.0, 0.0) + sel(c4, 1.0, 0.0)
            + sel(c5, 1.0, 0.0))


def _top16_rows(s):
    lst = [s[SUBLANES * v:SUBLANES * (v + 1), :] for v in range(s.shape[0] // SUBLANES)]
    _bitonic_sort_desc(lst)
    return _merge_sublanes(lst)


def _topk_body(f_ref, wq_ref, k1_ref, k2_ref, e1_o, n_o, r2_o, e2_o, q_s):
    q_s[...] = lax.dot_general(wq_ref[...], f_ref[...], (NT, ((), ())),
                               preferred_element_type=F32).astype(BF16)
    half = PEER_DK // 2
    T = f_ref.shape[0]
    sub = _iota((SUBLANES, T), 0)

    def stack(rows):
        out = rows[SUBLANES - 1]
        for b in range(SUBLANES - 2, -1, -1):
            out = jnp.where(sub == b, rows[b], out)
        return out

    def rep(a):
        return jnp.concatenate([a] * (N_KEYS // SUBLANES), axis=0)

    def head(h, carry):
        base = pl.multiple_of(h * PEER_DK, PEER_DK)
        s1 = jnp.dot(k1_ref[...], q_s[pl.ds(base, half), :], preferred_element_type=F32)
        s2 = jnp.dot(k2_ref[...], q_s[pl.ds(base + half, half), :], preferred_element_type=F32)
        t1 = _top16_rows(s1)
        t2 = _top16_rows(s2)
        lo = stack(t2[:SUBLANES])
        hi = stack(t2[SUBLANES:])
        top = _merge_sublanes(_merge_top([t + lo for t in t1], [t + hi for t in t1]))
        z = jnp.exp(top[0] - top[0])
        for kk in range(1, PEER_TOPK):
            z = z + jnp.exp(top[kk] - top[0])
        thr = rep(top[PEER_TOPK - 1])
        t2r = [rep(t) for t in t2]
        r2 = _count_leading(lambda t: t > s2, t2r)
        n1 = _count_leading(lambda t: s1 + t >= thr, t2r)
        r2_o[h] = _pack_rows(r2)
        n_o[h] = _dup_bf16(n1)
        e1_o[h] = _dup_bf16(jnp.exp(s1 - rep(t1[0])) / rep(z))
        e2_o[h] = _pack_rows(jnp.exp(s2 - rep(t2[0])))
        return carry

    lax.fori_loop(0, PEER_HEADS, head, 0)


def _topk(f, wq_t, k1, k2):
    rows = f.shape[0]
    T = TOPK_TILE
    full = lambda a: pl.BlockSpec(a.shape, lambda i: (0,) * a.ndim)
    big = pl.BlockSpec((PEER_HEADS, N_KEYS, T), lambda i: (0, 0, i))
    big_shape = lambda dt: jax.ShapeDtypeStruct((PEER_HEADS, N_KEYS, rows), dt)
    packed = pl.BlockSpec((PEER_HEADS, N_KEYS // 2, T), lambda i: (0, 0, i))
    packed_shape = jax.ShapeDtypeStruct((PEER_HEADS, N_KEYS // 2, rows), U32)
    return pl.pallas_call(
        _topk_body, grid=(rows // T,),
        in_specs=[pl.BlockSpec((T, D_MODEL), lambda i: (i, 0)), full(wq_t), full(k1), full(k2)],
        out_specs=[big, big, packed, packed],
        out_shape=[big_shape(U32), big_shape(U32), packed_shape, packed_shape],
        scratch_shapes=[pltpu.VMEM((PEER_HEADS * PEER_DK, T), BF16)],
        compiler_params=_cparams(1), name="peer_topk",
    )(f, wq_t, k1, k2)


def _peer_body(final, n_e, f_ref, u_ref, v_ref, e1_ref, n_ref, r2_ref, e2_ref, h_ref,
               gate_ref, fg_ref, out, st_s, at_s, acc):
    s = pl.program_id(0)
    j_up = jnp.maximum(s - 1, 0) % n_e
    cur = s % 2

    @pl.when(s == 0)
    def _():
        at_s[...] = jnp.zeros_like(at_s)

    @pl.when(j_up == 0)
    def _():
        acc[...] = jnp.zeros_like(acc)

    st_s[...] = _pack_rows(lax.dot_general(u_ref[...], f_ref[...], (NT, ((), ())),
                                           preferred_element_type=F32))
    acc[...] += lax.dot_general(_unpack_rows(at_s[1 - cur]), v_ref[...], (TN, ((), ())),
                                preferred_element_type=F32)

    blk = (N_KEYS // 2, LANES)
    group = 2
    for lg in range(TOK_TILE // LANES):
        ln = slice(lg * LANES, (lg + 1) * LANES)
        for i0 in range(0, EXP_TILE // N_KEYS, group):
            g = [None] * group
            for h in range(PEER_HEADS):
                r2_blk = _unpack_rows(r2_ref[h, :, ln])
                e2_blk = _unpack_rows(e2_ref[h, :, ln])
                for r in range(group):
                    ii = i0 + r
                    n_row = _unpack_rows(jnp.broadcast_to(n_ref[h, ii:ii + 1, ln], blk))
                    e1_row = _unpack_rows(jnp.broadcast_to(e1_ref[h, ii:ii + 1, ln], blk))
                    w = jnp.where(r2_blk < n_row, e2_blk, 0.0) * e1_row
                    g[r] = w if g[r] is None else g[r] + w
            for r in range(group):
                rows = slice((i0 + r) * N_KEYS // 2, (i0 + r + 1) * N_KEYS // 2)
                at_s[cur, rows, ln] = _pack_rows(
                    _gelu_tanh(_unpack_rows(st_s[rows, ln])) * g[r])

    @pl.when(jnp.logical_and(s >= 1, j_up == n_e - 1))
    def _():
        h = h_ref[...] + gate_ref[...] * acc[...]
        if final:
            h = h * lax.rsqrt(jnp.mean(h * h, axis=-1, keepdims=True) + EPS) * fg_ref[...]
        out[...] = h


def _peer(f, u_b, v_b, layer, e1, n1, r2, e2, h, mods, fg, n_lat, per_seq, n_tok_tiles, final):
    n_e = N_EXPERTS // EXP_TILE
    rpt = EXP_TILE // N_KEYS
    n_steps = n_tok_tiles * n_e

    def tile(s, lag):
        t = jnp.clip(s - lag, 0, n_steps - 1)
        return t // n_e, t % n_e

    tok = lambda lag: pl.BlockSpec((TOK_TILE, D_MODEL), lambda s: (tile(s, lag)[0], 0))
    exp = lambda lag: pl.BlockSpec((None, EXP_TILE, D_MODEL), lambda s: (layer, tile(s, lag)[1], 0))
    rowblk = pl.BlockSpec((PEER_HEADS, rpt, TOK_TILE), lambda s: (0, tile(s, 0)[1], tile(s, 0)[0]))
    allkeys = pl.BlockSpec((PEER_HEADS, N_KEYS // 2, TOK_TILE), lambda s: (0, 0, tile(s, 0)[0]))

    def gate_map(s):
        i = tile(s, 1)[0]
        row = jnp.where(i < n_lat, i // per_seq, 2)
        return (row * 6 + 5, 0, 0)

    return pl.pallas_call(
        functools.partial(_peer_body, final, n_e),
        grid=(n_steps + 1,),
        in_specs=[tok(0), exp(0), exp(1), rowblk, rowblk, allkeys, allkeys, tok(1),
                  pl.BlockSpec((None, 1, D_MODEL), gate_map),
                  pl.BlockSpec((1, D_MODEL), lambda s: (0, 0))],
        out_specs=tok(1),
        out_shape=jax.ShapeDtypeStruct((n_tok_tiles * TOK_TILE, D_MODEL), F32),
        scratch_shapes=[pltpu.VMEM((EXP_TILE // 2, TOK_TILE), U32),
                        pltpu.VMEM((2, EXP_TILE // 2, TOK_TILE), U32),
                        pltpu.VMEM((TOK_TILE, D_MODEL), F32)],
        compiler_params=_cparams(1), name="peer",
    )(f, u_b, v_b, e1, n1, r2, e2, h, mods, fg)


def _cast_body(u_ref, v_ref, uo, vo):
    uo[...] = u_ref[...].astype(BF16)
    vo[...] = v_ref[...].astype(BF16)


def _cast_tables(u, v):
    L, n, d = u.shape
    rows = 1024
    spec = pl.BlockSpec((None, rows, d), lambda l, i: (l, i, 0))
    shape = jax.ShapeDtypeStruct((L, n, d), BF16)
    return pl.pallas_call(
        _cast_body, grid=(L, n // rows), in_specs=[spec, spec], out_specs=[spec, spec],
        out_shape=[shape, shape], compiler_params=_cparams(2), name="cast_tables",
    )(u, v)


def _sincos_2d(t_len):
    rows = t_len // GRID_W
    q = D_MODEL // 4
    freq = 10000.0 ** (-jnp.arange(q, dtype=F32) / q)
    ar = jnp.arange(rows, dtype=F32)[:, None] * freq
    ac = jnp.arange(GRID_W, dtype=F32)[:, None] * freq
    per_row = lambda a: jnp.repeat(a, GRID_W, axis=0)
    per_col = lambda a: jnp.tile(a, (rows, 1))
    return jnp.concatenate([per_row(jnp.sin(ar)), per_row(jnp.cos(ar)),
                            per_col(jnp.sin(ac)), per_col(jnp.cos(ac))], -1)


def _row(a):
    return a.reshape(1, -1).astype(F32)


def _pad_lanes(a, width=LANES):
    return jnp.pad(a, ((0, 0), (0, width - a.shape[-1])))


def kernel(x, c, ctx, c_ctx, ada_w, ada_b, norm_mix_g, norm_ffn_g, w_in, w_out, a_conv_w, a_conv_b, a_ln_g, a_ln_b, b_ln_g, b_ln_b, b_ws, b_bs, c_conv_w, c_conv_b, c_dt_bias, c_a_log, c_d, c_norm_g, d_mu_prev, d_mu_next, d_w0, d_w2, d_a0, d_a2, d_g2, d_k_k, d_k_a, d_r_k, d_gn_g, d_gn_b, peer_wq, peer_k1, peer_k2, peer_u, peer_v, final_g):
    n_b, t_lat, _ = x.shape
    t_ctx = ctx.shape[1]
    assert n_b == 2 and t_ctx == SEQ_TILE and t_lat % TOK_TILE == 0
    n_l = t_lat // SEQ_TILE
    per_seq = t_lat // TOK_TILE
    n_lat = n_b * per_seq
    n_lat256 = n_b * n_l

    cs = jnp.zeros((MOD_ROWS, D_MODEL), F32).at[:n_b].set(c).at[n_b].set(c_ctx)
    mods_all = _ada(cs, ada_w, ada_b)
    pos = _sincos_2d(t_lat)
    u_bf, v_bf = _cast_tables(peer_u, peer_v)
    x2 = x.reshape(n_b * t_lat, D_MODEL)
    ctx2 = ctx.reshape(n_b * t_ctx, D_MODEL)

    h = None
    for i in range(DEPTH):
        last_layer = i == DEPTH - 1
        mods = mods_all[i].reshape(MOD_ROWS * 6, 1, D_MODEL)
        w = w_in[i]
        dtc = OFF_C + D_GROUP + C_XBC
        w_c = jnp.concatenate([w[:, OFF_C:dtc], _pad_lanes(w[:, dtc:dtc + HEADS_C]),
                               _pad_lanes(w[:, dtc + HEADS_C:dtc + 2 * HEADS_C])], axis=1)
        ws_in = tuple(a.astype(BF16) for a in (w[:, :OFF_B], w[:, OFF_B:OFF_C], w_c, w[:, OFF_D:]))
        g_mix = _row(norm_mix_g[i])
        if i == 0:
            h, pa, pb, pc, pd = _inproj((x2, ctx2, pos), mods, g_mix, ws_in, n_lat, per_seq, True)
        else:
            pa, pb, pc, pd = _inproj(h, mods, g_mix, ws_in, n_lat, per_seq, False)

        conv_a = jnp.pad(a_conv_w[i], ((0, 32 - CONV_A), (0, 0)))
        bsm = jnp.repeat(b_bs[i].T, D_GROUP // HEADS_B, axis=1)
        yab = _mixab(pa, pb, (conv_a, _row(a_conv_b[i]), _row(a_ln_g[i]), _row(a_ln_b[i]),
                              _row(b_ln_g[i]), _row(b_ln_b[i]), b_ws[i].astype(BF16), bsm),
                     n_lat256, n_l)

        conv_c = jnp.pad(c_conv_w[i], ((0, SUBLANES - CONV_C), (0, 0)))
        dtb = _pad_lanes(c_dt_bias[i]).reshape(2, 1, LANES)
        alog = _pad_lanes(c_a_log[i]).reshape(2, 1, LANES)
        dsk = _row(jnp.repeat(c_d[i], HEAD_DIM_C))
        yc = _mamba(pc, (conv_c, _row(c_conv_b[i]), dtb, alog, dsk, _row(c_norm_g[i])), n_b, n_l)

        yd = _rwkv(pd, (_row(d_mu_prev[i]), _row(d_mu_next[i]), d_w0[i].reshape(2, 1, D_GROUP),
                        d_w2[i].astype(BF16), d_a0[i].reshape(2, 1, D_GROUP), d_a2[i].astype(BF16),
                        d_g2[i].astype(BF16), _row(d_k_k[i]), _row(d_k_a[i]), _row(d_r_k[i]),
                        _row(d_gn_g[i]), _row(d_gn_b[i])), n_b, n_l)

        wo = w_out[i].astype(BF16)
        h, f = _outproj(yab, yc, yd, h, mods, _row(norm_ffn_g[i]),
                        (wo[:512], wo[512:768], wo[768:]), n_lat, per_seq)

        e1, n1, r2, e2 = _topk(f, peer_wq[i].T.astype(BF16), peer_k1[i].astype(BF16),
                               peer_k2[i].astype(BF16))
        n_tok = n_lat if last_layer else n_lat + 1
        h = _peer(f, u_bf, v_bf, i, e1, n1, r2, e2, h,
                  mods, _row(final_g), n_lat, per_seq, n_tok, last_layer)
    return h.reshape(n_b, t_lat, D_MODEL)
```

```python
import functools
import math

import jax
import jax.numpy as jnp
from jax import lax
from jax.experimental import pallas as pl
from jax.experimental.pallas import tpu as pltpu

F32 = jnp.float32
BF16 = jnp.bfloat16
U32 = jnp.uint32
HIGHEST = lax.Precision.HIGHEST

D_MODEL = 1024
DEPTH = 2
GRID_W = 64
EPS = 1e-6
D_GROUP = 256
CONV_A = 31
CHUNK_B = 128
HEADS_B = 4
HEADS_C = 4
HEAD_DIM_C = 64
STATE_C = 128
CONV_C = 5
CHUNK_C = 128
HEADS_D = 4
HEAD_DIM_D = 64
LORA_W = 64
LORA_A = 64
LORA_G = 128
GN_EPS_D = 64e-5
N_KEYS = 128
N_EXPERTS = N_KEYS * N_KEYS
PEER_HEADS = 8
PEER_DK = 256
PEER_TOPK = 16

C_XBC = D_GROUP + 2 * 2 * STATE_C
OFF_B = 2 * D_GROUP
OFF_C = OFF_B + 2 * D_GROUP
OFF_D = OFF_C + D_GROUP + C_XBC + 2 * HEADS_C
D_COLS = 3 * D_GROUP + 2 * LORA_W + 2 * LORA_A + LORA_G
PC_COLS = D_GROUP + C_XBC + 2 * 128

LANES = 128
SUBLANES = 8
SEQ_TILE = 256
TOK_TILE = 512
TOPK_TILE = 512
MOD_ROWS = SUBLANES
CHUNK_D = 64
EXP_TILE = 2048
PEER_TABLE_BUFFERS = 3
VMEM_LIMIT = 56 * 1024 * 1024


def _cparams(n_axes):
    return pltpu.CompilerParams(dimension_semantics=("arbitrary",) * n_axes,
                                vmem_limit_bytes=VMEM_LIMIT)


def _silu(x):
    return x * jax.nn.sigmoid(x)


def _dot(a, b, dims=None):
    a = a.astype(BF16)
    b = b.astype(BF16)
    if dims is None:
        return jnp.dot(a, b, preferred_element_type=F32)
    return lax.dot_general(a, b, (dims, ((), ())), preferred_element_type=F32)


def _dot_hi(a, b):
    return jnp.dot(a, b, precision=HIGHEST, preferred_element_type=F32)


def _split_bf16(x, terms):
    parts = []
    for _ in range(terms):
        part = x.astype(BF16)
        parts.append(part)
        x = x - part.astype(F32)
    return parts


def _dot_mask_rhs(a, mask, terms=3):
    m = mask.astype(BF16)
    return sum(jnp.dot(p, m, preferred_element_type=F32) for p in _split_bf16(a, terms))


def _dot_mask_lhs(mask, b, terms=3):
    m = mask.astype(BF16)
    return sum(jnp.dot(m, p, preferred_element_type=F32) for p in _split_bf16(b, terms))


NT = ((1,), (1,))
TN = ((0,), (0,))


def _gelu_tanh(x):
    k1 = math.sqrt(2.0 / math.pi)
    k3 = 0.044715 * k1
    hx = 0.5 * x
    return hx + hx * jnp.tanh(x * (k1 + k3 * (x * x)))


def _pack_rows(x):
    return pltpu.bitcast(x.astype(BF16), U32)


def _unpack_rows(x):
    return pltpu.bitcast(x, BF16)


def _dup_bf16(x):
    hi = pltpu.bitcast(x.astype(BF16).astype(F32), U32)
    return hi | (hi >> 16)


def _iota(shape, axis):
    return lax.broadcasted_iota(jnp.int32, shape, axis)


def _head_block_ones(n, width):
    return (_iota((n, n), 0) // width == _iota((n, n), 1) // width).astype(F32)


def _ada_body(cs_ref, w_ref, b_ref, o_ref):
    o_ref[...] = _dot_hi(_silu(cs_ref[...]), w_ref[...]) + b_ref[...]


def _ada(cs, ada_w, ada_b):
    L = ada_w.shape[0]
    nb = 6 * D_MODEL // 4
    return pl.pallas_call(
        _ada_body,
        grid=(L, 6 * D_MODEL // nb),
        in_specs=[pl.BlockSpec((MOD_ROWS, D_MODEL), lambda l, n: (0, 0)),
                  pl.BlockSpec((None, D_MODEL, nb), lambda l, n: (l, 0, n)),
                  pl.BlockSpec((None, 1, nb), lambda l, n: (l, 0, n))],
        out_specs=pl.BlockSpec((None, MOD_ROWS, nb), lambda l, n: (l, 0, n)),
        out_shape=jax.ShapeDtypeStruct((L, MOD_ROWS, 6 * D_MODEL), F32),
        compiler_params=_cparams(2),
        name="ada",
    )(cs, ada_w, ada_b.reshape(L, 1, 6 * D_MODEL))


def _norm_mod(h, g, shift, scale):
    xn = h * lax.rsqrt(jnp.mean(h * h, axis=-1, keepdims=True) + EPS) * g
    return xn * (1.0 + scale) + shift


def _inproj_tail(h, sh_ref, sc_ref, g_ref, wa, wb, wc, wd, oa, ob, oc, od):
    xm = _norm_mod(h, g_ref[...], sh_ref[...], sc_ref[...]).astype(BF16)
    oa[...] = jnp.dot(xm, wa[...], preferred_element_type=F32)
    ob[...] = jnp.dot(xm, wb[...], preferred_element_type=F32)
    oc[...] = jnp.dot(xm, wc[...], preferred_element_type=F32)
    od[...] = jnp.dot(xm, wd[...], preferred_element_type=F32)


def _inproj_first_body(n_lat, x_ref, ctx_ref, pos_ref, sh_ref, sc_ref, g_ref, wa, wb, wc, wd,
                       oh, oa, ob, oc, od):
    i = pl.program_id(0)
    h = jnp.where(i < n_lat, x_ref[...] + pos_ref[...], ctx_ref[...])
    oh[...] = h
    _inproj_tail(h, sh_ref, sc_ref, g_ref, wa, wb, wc, wd, oa, ob, oc, od)


def _inproj_body(h_ref, sh_ref, sc_ref, g_ref, wa, wb, wc, wd, oa, ob, oc, od):
    _inproj_tail(h_ref[...], sh_ref, sc_ref, g_ref, wa, wb, wc, wd, oa, ob, oc, od)


def _mod_spec(n_lat, per_seq, k):
    def imap(i):
        row = jnp.where(i < n_lat, i // per_seq, 2)
        return (row * 6 + k, 0, 0)
    return pl.BlockSpec((None, 1, D_MODEL), imap)


def _inproj(h_or_parts, mods, g, ws, n_lat, per_seq, first):
    wa, wb, wc, wd = ws
    n_tiles = n_lat + 1
    rows = n_tiles * TOK_TILE
    tile = lambda c: pl.BlockSpec((TOK_TILE, c), lambda i: (i, 0))
    full = lambda a: pl.BlockSpec(a.shape, lambda i: (0,) * a.ndim)
    common_specs = [_mod_spec(n_lat, per_seq, 0), _mod_spec(n_lat, per_seq, 1), full(g),
                    full(wa), full(wb), full(wc), full(wd)]
    out_specs = [tile(512), tile(512), tile(PC_COLS), tile(D_COLS)]
    out_shape = [jax.ShapeDtypeStruct((rows, c), F32) for c in (512, 512, PC_COLS, D_COLS)]
    if first:
        x2, ctx2, pos = h_or_parts
        in_specs = [pl.BlockSpec((TOK_TILE, D_MODEL), lambda i: (jnp.minimum(i, n_lat - 1), 0)),
                    pl.BlockSpec((TOK_TILE, D_MODEL), lambda i: (0, 0)),
                    pl.BlockSpec((TOK_TILE, D_MODEL), lambda i: (i % per_seq, 0))] + common_specs
        return pl.pallas_call(
            functools.partial(_inproj_first_body, n_lat),
            grid=(n_tiles,), in_specs=in_specs,
            out_specs=[tile(D_MODEL)] + out_specs,
            out_shape=[jax.ShapeDtypeStruct((rows, D_MODEL), F32)] + out_shape,
            compiler_params=_cparams(1), name="inproj_first",
        )(x2, ctx2, pos, mods, mods, g, wa, wb, wc, wd)
    return pl.pallas_call(
        _inproj_body, grid=(n_tiles,), in_specs=[tile(D_MODEL)] + common_specs,
        out_specs=out_specs, out_shape=out_shape,
        compiler_params=_cparams(1), name="inproj",
    )(h_or_parts, mods, mods, g, wa, wb, wc, wd)


def _layernorm(x, g, b, eps=1e-5):
    mu = jnp.mean(x, axis=-1, keepdims=True)
    xc = x - mu
    var = jnp.mean(xc * xc, axis=-1, keepdims=True)
    return xc * lax.rsqrt(var + eps) * g + b


def _mixab_body(n_lat, n_l, pa, pa_prev, pa_next, pb, cw, cb, alg, alb, blg, blb, ws, bsm,
                out, ext):
    i = pl.program_id(0)
    is_ctx = i >= n_lat
    tpos = i % n_l
    first = jnp.logical_or(is_ctx, tpos == 0)
    last = jnp.logical_or(is_ctx, tpos == n_l - 1)

    def glu(x):
        return x[:, :D_GROUP] * jax.nn.sigmoid(x[:, D_GROUP:])

    halo = 16
    ext[0:halo, :] = jnp.where(first, 0.0, glu(pa_prev[...]))
    ext[halo:halo + SEQ_TILE, :] = glu(pa[...])
    ext[halo + SEQ_TILE:, :] = jnp.where(last, 0.0, glu(pa_next[...]))
    acc = jnp.zeros((SEQ_TILE, D_GROUP), F32) + cb[...]
    for j in range(CONV_A):
        acc = acc + cw[j:j + 1, :] * ext[pl.ds(halo - (CONV_A - 1) // 2 + j, SEQ_TILE), :]
    out[:, :D_GROUP] = _silu(_layernorm(acc, alg[...], alb[...]))

    x = pb[...]
    u = x[:, :D_GROUP]
    v = _layernorm(x[:, D_GROUP:], blg[...], blb[...]).astype(BF16)
    hd = D_GROUP // HEADS_B
    rows = []
    for c in range(SEQ_TILE // CHUNK_B):
        vc = v[c * CHUNK_B:(c + 1) * CHUNK_B, :]
        heads = [jnp.dot(ws[hh], vc[:, hh * hd:(hh + 1) * hd], preferred_element_type=F32)
                 for hh in range(HEADS_B)]
        rows.append(jnp.concatenate(heads, axis=1) + bsm[...])
    out[:, D_GROUP:] = u * jnp.concatenate(rows, axis=0)


def _mixab(pa, pb, prm, n_lat, n_l):
    rows = pa.shape[0]
    n_tiles = rows // SEQ_TILE
    halo = 16
    per = SEQ_TILE // halo
    full = lambda a: pl.BlockSpec(a.shape, lambda i: (0,) * a.ndim)
    in_specs = [pl.BlockSpec((SEQ_TILE, 512), lambda i: (i, 0)),
                pl.BlockSpec((halo, 512), lambda i: (jnp.maximum(i * per - 1, 0), 0)),
                pl.BlockSpec((halo, 512), lambda i: (jnp.minimum((i + 1) * per, rows // halo - 1), 0)),
                pl.BlockSpec((SEQ_TILE, 512), lambda i: (i, 0))] + [full(a) for a in prm]
    return pl.pallas_call(
        functools.partial(_mixab_body, n_lat, n_l),
        grid=(n_tiles,), in_specs=in_specs,
        out_specs=pl.BlockSpec((SEQ_TILE, 512), lambda i: (i, 0)),
        out_shape=jax.ShapeDtypeStruct((rows, 512), F32),
        scratch_shapes=[pltpu.VMEM((SEQ_TILE + 2 * halo, D_GROUP), F32)],
        compiler_params=_cparams(1), name="mixab",
    )(pa, pa, pa, pb, *prm)


def _scan_tile(n_b, n_l, b, ph, j):
    lat = b * n_l + jnp.where(ph == 0, j - 1, n_l - j)
    return jnp.where(j == 0, n_b * n_l + b, lat)


def _scan_specs(n_b, n_l, rows, cols):
    tid = functools.partial(_scan_tile, n_b, n_l)
    halo = SUBLANES
    per = SEQ_TILE // halo
    cur = pl.BlockSpec((SEQ_TILE, cols), lambda b, ph, j: (tid(b, ph, j), 0))
    prev = pl.BlockSpec((halo, cols),
                        lambda b, ph, j: (jnp.maximum(tid(b, ph, j) * per - 1, 0), 0))
    nxt = pl.BlockSpec((halo, cols),
                       lambda b, ph, j: (jnp.minimum((tid(b, ph, j) + 1) * per, rows // halo - 1), 0))
    return cur, prev, nxt


def _scan_out_spec(n_b, n_l, cols):
    def imap(b, ph, j):
        return (jnp.where(ph == 0, n_b * n_l + b, _scan_tile(n_b, n_l, b, 1, j)), 0)
    return pl.BlockSpec((SEQ_TILE, cols), imap)


def _scan_flags(n_l):
    ph = pl.program_id(1)
    j = pl.program_id(2)
    is_ctx = j == 0
    tpos = jnp.where(ph == 0, j - 1, n_l - j)
    first = jnp.logical_or(is_ctx, tpos == 0)
    last = jnp.logical_or(is_ctx, tpos == n_l - 1)
    slot = jnp.where(is_ctx, 0, tpos + 1)
    return ph, j, first, last, slot


def _mamba_body(n_l, cur_ref, prev_ref, next_ref, dt_ref, cw, cb, dtb, alog, dsk, ng,
                out, ext, yf, st, xbc_s, a_s, dt_s, y_s):
    ph, j, first, last, slot = _scan_flags(n_l)
    fwd = ph == 0

    @pl.when(j == 0)
    def _():
        st[...] = jnp.zeros_like(st)

    zx = D_GROUP
    halo = SUBLANES
    ext[0:halo, :] = jnp.where(first, 0.0, prev_ref[:, zx:])
    ext[halo:halo + SEQ_TILE, :] = cur_ref[:, zx:]
    ext[halo + SEQ_TILE:, :] = jnp.where(last, 0.0, next_ref[:, zx:])
    acc = jnp.zeros((SEQ_TILE, C_XBC), F32) + cb[...]
    for jj in range(CONV_C):
        acc = acc + cw[jj:jj + 1, :] * ext[pl.ds(halo - (CONV_C - 1) // 2 + jj, SEQ_TILE), :]
    xbc_s[...] = _silu(acc)
    dt = jax.nn.softplus(dt_ref[...] + dtb[ph])
    dt_s[...] = dt
    a_s[...] = dt * (-jnp.exp(alog[ph]))

    L = CHUNK_C
    r_i = _iota((L, L), 0)
    c_i = _iota((L, L), 1)
    ltri = (r_i >= c_i).astype(F32)
    mask = jnp.where(fwd, r_i - c_i, c_i - r_i) >= 0
    sgn = jnp.where(fwd, 1.0, -1.0)
    hd = HEAD_DIM_C

    def chunk(it, carry):
        q = jnp.where(fwd, it, SEQ_TILE // L - 1 - it)
        rows = pl.ds(pl.multiple_of(q * L, L), L)
        a_q = a_s[rows, :]
        cs = _dot_mask_lhs(ltri, a_q)
        tot = cs[L - 1:L, :]
        e = jnp.where(fwd, cs, cs - a_q)
        e_t = e.T
        dt_q = dt_s[rows, :]
        xbc = xbc_s[rows, :]
        ys = []
        for g in range(2):
            bm = xbc[:, D_GROUP + g * STATE_C:D_GROUP + (g + 1) * STATE_C]
            cm = xbc[:, D_GROUP + 2 * STATE_C + g * STATE_C:D_GROUP + 2 * STATE_C + (g + 1) * STATE_C]
            bm_t = bm.T.astype(BF16)
            cm_b = cm.astype(BF16)
            gmat = jnp.dot(cm_b, bm_t, preferred_element_type=F32)
            for h in (2 * g, 2 * g + 1):
                ecol = e[:, h:h + 1]
                erow = e_t[h:h + 1, :]
                totc = tot[:, h:h + 1]
                xdt = xbc[:, h * hd:(h + 1) * hd] * dt_q[:, h:h + 1]
                lmat = jnp.exp(jnp.where(mask, sgn * (ecol - erow), -1e30))
                offs = jnp.exp(jnp.where(fwd, ecol, totc - ecol))
                stw = jnp.exp(jnp.where(fwd, totc - ecol, ecol))
                s_prev = st[h]
                y = _dot(gmat * lmat, xdt) + offs * _dot(cm_b, s_prev)
                st[h] = jnp.exp(totc) * s_prev + _dot(bm_t, xdt * stw)
                ys.append(y)
        y_s[rows, :] = jnp.concatenate(ys, axis=1)
        return carry

    lax.fori_loop(0, SEQ_TILE // L, chunk, 0)

    @pl.when(fwd)
    def _():
        yf[slot] = y_s[...]

    @pl.when(ph == 1)
    def _():
        y = yf[slot] + y_s[...] + dsk[...] * xbc_s[:, :D_GROUP]
        t = y * _silu(cur_ref[:, :D_GROUP])
        out[...] = t * lax.rsqrt(jnp.mean(t * t, axis=-1, keepdims=True) + EPS) * ng[...]


def _mamba(pc, prm, n_b, n_l):
    rows = pc.shape[0]
    cur, prev, nxt = _scan_specs(n_b, n_l, rows, D_GROUP + C_XBC)
    tid = functools.partial(_scan_tile, n_b, n_l)
    dt_spec = pl.BlockSpec((SEQ_TILE, LANES),
                           lambda b, ph, j: (tid(b, ph, j), (D_GROUP + C_XBC) // LANES + ph))
    full = lambda a: pl.BlockSpec(a.shape, lambda b, ph, j: (0,) * a.ndim)
    return pl.pallas_call(
        functools.partial(_mamba_body, n_l),
        grid=(n_b, 2, n_l + 1),
        in_specs=[cur, prev, nxt, dt_spec] + [full(a) for a in prm],
        out_specs=_scan_out_spec(n_b, n_l, D_GROUP),
        out_shape=jax.ShapeDtypeStruct((rows, D_GROUP), F32),
        scratch_shapes=[pltpu.VMEM((SEQ_TILE + 2 * SUBLANES, C_XBC), F32),
                        pltpu.VMEM((n_l + 1, SEQ_TILE, D_GROUP), F32),
                        pltpu.VMEM((HEADS_C, STATE_C, HEAD_DIM_C), F32),
                        pltpu.VMEM((SEQ_TILE, C_XBC), F32),
                        pltpu.VMEM((SEQ_TILE, LANES), F32),
                        pltpu.VMEM((SEQ_TILE, LANES), F32),
                        pltpu.VMEM((SEQ_TILE, D_GROUP), F32)],
        compiler_params=_cparams(3), name="mamba",
    )(pc, pc, pc, pc, *prm)


def _rwkv_body(n_l, cur_ref, prev_ref, next_ref, mup, mun, w0, w2, a0, a2, g2, kkw, kaw, rkw,
               gng, gnb, out, yf, st, r_s, v_s, kap_s, alp_s, kd_s, lw_s, y_s,
               phi_s, psi_s, sin_s, left_s, amat_s, ara_s, tinv_s):
    ph, j, first, last, slot = _scan_flags(n_l)
    fwd = ph == 0

    @pl.when(j == 0)
    def _():
        st[...] = jnp.zeros_like(st)

    cur = cur_ref[...]
    prv = jnp.concatenate([jnp.where(first, 0.0, prev_ref[SUBLANES - 1:SUBLANES, :]),
                           cur[:SEQ_TILE - 1, :]], axis=0)
    nxt = jnp.concatenate([cur[1:, :], jnp.where(last, 0.0, next_ref[0:1, :])], axis=0)
    p = cur + mup[...] * (prv - cur) + mun[...] * (nxt - cur)
    G = D_GROUP
    r = p[:, :G]
    k = p[:, G:2 * G]
    v = p[:, 2 * G:3 * G]
    blk = _head_block_ones(G, HEAD_DIM_D)
    kkr = k * kkw[...]
    kk = kkr * lax.rsqrt(_dot_mask_rhs(kkr * kkr, blk) + 1e-12)

    def rate(d_static=None):
        if d_static is None:
            ad = jnp.where(fwd, p[:, 3 * G + 2 * LORA_W:3 * G + 2 * LORA_W + LORA_A],
                           p[:, 3 * G + 2 * LORA_W + LORA_A:3 * G + 2 * LORA_W + 2 * LORA_A])
            a = jax.nn.sigmoid(a0[ph] + _dot(ad, a2[ph]))
        else:
            o = 3 * G + 2 * LORA_W + d_static * LORA_A
            a = jax.nn.sigmoid(a0[d_static] + _dot(p[:, o:o + LORA_A], a2[d_static]))
        return a, k * (1.0 + (a - 1.0) * kaw[...])

    a_d, kd_d = rate()
    wd = jnp.where(fwd, p[:, 3 * G:3 * G + LORA_W], p[:, 3 * G + LORA_W:3 * G + 2 * LORA_W])
    w = w0[ph] + _dot(jnp.tanh(wd), w2[ph])
    lw_s[...] = -math.exp(-0.5) * jax.nn.sigmoid(w)
    r_s[...] = r
    v_s[...] = v
    kap_s[...] = kk
    alp_s[...] = a_d * kk
    kd_s[...] = kd_d

    C = CHUNK_D
    r_i = _iota((C, C), 0)
    c_i = _iota((C, C), 1)
    lag = jnp.where(fwd, r_i - c_i, c_i - r_i)
    tri = lag >= 0
    tri_f = tri.astype(F32)
    strict = lag > 0
    eye = (r_i == c_i).astype(F32)
    hd = HEAD_DIM_D

    n_q = SEQ_TILE // C
    eye_k = (_iota((hd, hd), 0) == _iota((hd, hd), 1)).astype(F32)

    units = [(q, h) for q in range(n_q) for h in range(HEADS_D)]
    rows_of = lambda q: slice(q * C, (q + 1) * C)
    lanes_of = lambda h: slice(h * hd, (h + 1) * hd)
    left, right, p_tot = {}, {}, {}
    for q in range(n_q):
        rows = rows_of(q)
        lw = lw_s[rows, :]
        incl = _dot_mask_lhs(tri_f, lw)
        tot = jnp.where(fwd, incl[C - 1:C, :], incl[0:1, :])
        p_inv = jnp.exp(-incl)
        kap_h = kap_s[rows, :] * jnp.exp(incl - lw)
        r_h = r_s[rows, :] * jnp.exp(incl)
        alp_b = alp_s[rows, :] * p_inv
        k_b = kd_s[rows, :] * p_inv
        ptq = jnp.exp(tot)
        for h in range(HEADS_D):
            sl = lanes_of(h)
            left[q, h] = jnp.concatenate([kap_h[:, sl], r_h[:, sl]], axis=0)
            right[q, h] = jnp.concatenate([alp_b[:, sl], k_b[:, sl]], axis=0)
            p_tot[q, h] = ptq[:, sl]
            left_s[q * HEADS_D + h] = left[q, h]
    m1 = {u: _dot(left[u], right[u], NT) for u in units}
    a_vk, pw, tinv = {}, {}, {}
    for u in units:
        qh = u[0] * HEADS_D + u[1]
        a_vk[u] = jnp.where(strict, m1[u][:C, C:], 0.0)
        amat_s[qh] = jnp.concatenate([a_vk[u], jnp.where(tri, m1[u][C:, C:], 0.0)], axis=0)
        ara_s[qh] = jnp.where(tri, m1[u][C:, :C], 0.0)
        pw[u] = jnp.where(strict, -m1[u][:C, :C], 0.0)
        tinv[u] = eye + pw[u]
    pw = {u: _dot(pw[u], pw[u]) for u in units}
    for _ in range(5):
        both = {u: _dot(jnp.concatenate([pw[u], tinv[u]], axis=0), pw[u]) for u in units}
        tinv = {u: tinv[u] + both[u][C:] for u in units}
        pw = {u: both[u][:C] for u in units}
    wmat = {u: _dot(tinv[u], right[u][:C], TN) for u in units}
    kw_aw = {u: _dot(jnp.concatenate([left[u][:C], a_vk[u]], axis=1), wmat[u], TN) for u in units}
    vk = {u: _dot(v_s[rows_of(u[0]), lanes_of(u[1])], right[u][C:] - kw_aw[u][hd:], TN)
          for u in units}
    for u in units:
        qh = u[0] * HEADS_D + u[1]
        phi_s[qh] = (eye_k - kw_aw[u][:hd]) * p_tot[u]
        psi_s[qh] = vk[u] * p_tot[u]
        tinv_s[qh] = tinv[u]

    for it in range(n_q):
        q = jnp.where(fwd, it, n_q - 1 - it)
        for h in range(HEADS_D):
            qh = q * HEADS_D + h
            s_in = st[h]
            sin_s[qh] = s_in
            st[h] = _dot(s_in, phi_s[qh]) + psi_s[qh]

    qh_of = lambda u: u[0] * HEADS_D + u[1]
    x0 = {u: _dot(left_s[qh_of(u)], sin_s[qh_of(u)], NT) for u in units}
    x1 = {u: _dot(amat_s[qh_of(u)], v_s[rows_of(u[0]), lanes_of(u[1])]) for u in units}
    uu = {u: _dot(tinv_s[qh_of(u)], x0[u][:C] + x1[u][:C]) for u in units}
    au = {u: _dot(ara_s[qh_of(u)], uu[u]) for u in units}
    for q in range(n_q):
        y_s[rows_of(q), :] = jnp.concatenate(
            [x0[q, h][C:] + x1[q, h][C:] - au[q, h] for h in range(HEADS_D)], axis=1)

    @pl.when(fwd)
    def _():
        yf[slot] = y_s[...]

    @pl.when(ph == 1)
    def _():
        y = yf[slot] + y_s[...]
        inv = 1.0 / HEAD_DIM_D
        mu = _dot_mask_rhs(y, blk) * inv
        yc = y - mu
        var = _dot_mask_rhs(yc * yc, blk) * inv
        yn = yc * lax.rsqrt(var + GN_EPS_D) * gng[...] + gnb[...]
        _, kd_f = rate(0)
        bonus = _dot_mask_rhs(r * (kd_f + kd_d) * rkw[...], blk) * v
        gate = _dot(jax.nn.sigmoid(p[:, 3 * G + 2 * LORA_W + 2 * LORA_A:]), g2[...])
        out[...] = (yn + bonus) * gate


def _rwkv(pd, prm, n_b, n_l):
    rows = pd.shape[0]
    cur, prev, nxt = _scan_specs(n_b, n_l, rows, D_COLS)
    full = lambda a: pl.BlockSpec(a.shape, lambda b, ph, j: (0,) * a.ndim)
    tile = lambda: pltpu.VMEM((SEQ_TILE, D_GROUP), F32)
    per_qh = lambda r, c: pltpu.VMEM((SEQ_TILE // CHUNK_D * HEADS_D, r, c), F32)
    return pl.pallas_call(
        functools.partial(_rwkv_body, n_l),
        grid=(n_b, 2, n_l + 1),
        in_specs=[cur, prev, nxt] + [full(a) for a in prm],
        out_specs=_scan_out_spec(n_b, n_l, D_GROUP),
        out_shape=jax.ShapeDtypeStruct((rows, D_GROUP), F32),
        scratch_shapes=[pltpu.VMEM((n_l + 1, SEQ_TILE, D_GROUP), F32),
                        pltpu.VMEM((HEADS_D, HEAD_DIM_D, HEAD_DIM_D), F32)] + [tile() for _ in range(7)]
        + [per_qh(HEAD_DIM_D, HEAD_DIM_D) for _ in range(3)]
        + [per_qh(2 * CHUNK_D, HEAD_DIM_D), per_qh(2 * CHUNK_D, CHUNK_D),
           per_qh(CHUNK_D, CHUNK_D), per_qh(CHUNK_D, CHUNK_D)],
        compiler_params=_cparams(3), name="rwkv",
    )(pd, pd, pd, *prm)


def _outproj_body(yab, yc, yd, h_ref, gate, sh, sc, g, wab, wc, wd, oh, of):
    mix = (_dot(yab[...], wab[...]) + _dot(yc[...], wc[...]) + _dot(yd[...], wd[...]))
    h = h_ref[...] + gate[...] * mix
    oh[...] = h
    of[...] = _norm_mod(h, g[...], sh[...], sc[...]).astype(BF16)


def _outproj(yab, yc, yd, h, mods, g, ws, n_lat, per_seq):
    rows = h.shape[0]
    tile = lambda c: pl.BlockSpec((TOK_TILE, c), lambda i: (i, 0))
    full = lambda a: pl.BlockSpec(a.shape, lambda i: (0,) * a.ndim)
    return pl.pallas_call(
        _outproj_body, grid=(rows // TOK_TILE,),
        in_specs=[tile(512), tile(D_GROUP), tile(D_GROUP), tile(D_MODEL),
                  _mod_spec(n_lat, per_seq, 2), _mod_spec(n_lat, per_seq, 3),
                  _mod_spec(n_lat, per_seq, 4), full(g)] + [full(a) for a in ws],
        out_specs=[tile(D_MODEL), tile(D_MODEL)],
        out_shape=[jax.ShapeDtypeStruct((rows, D_MODEL), F32),
                   jax.ShapeDtypeStruct((rows, D_MODEL), BF16)],
        compiler_params=_cparams(1), name="outproj",
    )(yab, yc, yd, h, mods, mods, mods, g, *ws)


def _cmpx(lst, i, j):
    a, b = lst[i], lst[j]
    lst[i] = jnp.maximum(a, b)
    lst[j] = jnp.minimum(a, b)


def _bitonic_sort_desc(lst):
    n = len(lst)
    k = 2
    while k <= n:
        j = k // 2
        while j >= 1:
            for i in range(n):
                p = i ^ j
                if p > i:
                    if (i & k) == 0:
                        _cmpx(lst, i, p)
                    else:
                        _cmpx(lst, p, i)
            j //= 2
        k *= 2


def _bitonic_merge_desc(lst):
    n = len(lst)
    j = n // 2
    while j >= 1:
        for i in range(n):
            p = i ^ j
            if p > i:
                _cmpx(lst, i, p)
        j //= 2


def _merge_top(a, b):
    n = len(a)
    c = [jnp.maximum(a[i], b[n - 1 - i]) for i in range(n)]
    _bitonic_merge_desc(c)
    return c


def _merge_sublanes(lst):
    for shift in (4, 2, 1):
        lst = _merge_top(lst, [pltpu.roll(a, shift, axis=0) for a in lst])
    return lst


def _count_leading(pred, t):
    sel = jnp.where
    c1 = pred(t[7])
    c2 = pred(sel(c1, t[11], t[3]))
    c3 = pred(sel(c1, sel(c2, t[13], t[9]), sel(c2, t[5], t[1])))
    c4 = pred(sel(c1, sel(c2, sel(c3, t[14], t[12]), sel(c3, t[10], t[8])),
                  sel(c2, sel(c3, t[6], t[4]), sel(c3, t[2], t[0]))))
    c5 = pred(t[15])
    return (sel(c1, 8.0, 0.0) + sel(c2, 4.0, 0.0) + sel(c3, 2.0, 0.0) + sel(c4, 1.0, 0.0)
            + sel(c5, 1.0, 0.0))


def _top16_rows(s):
    lst = [s[SUBLANES * v:SUBLANES * (v + 1), :] for v in range(s.shape[0] // SUBLANES)]
    _bitonic_sort_desc(lst)
    return _merge_sublanes(lst)


def _topk_body(f_ref, wq_ref, k1_ref, k2_ref, e1_o, n_o, r2_o, e2_o, q_s):
    q_s[...] = lax.dot_general(wq_ref[...], f_ref[...], (NT, ((), ())),
                               preferred_element_type=F32).astype(BF16)
    half = PEER_DK // 2
    T = f_ref.shape[0]
    sub = _iota((SUBLANES, T), 0)

    def stack(rows):
        out = rows[SUBLANES - 1]
        for b in range(SUBLANES - 2, -1, -1):
            out = jnp.where(sub == b, rows[b], out)
        return out

    def rep(a):
        return jnp.concatenate([a] * (N_KEYS // SUBLANES), axis=0)

    def head(h, carry):
        base = pl.multiple_of(h * PEER_DK, PEER_DK)
        s1 = jnp.dot(k1_ref[...], q_s[pl.ds(base, half), :], preferred_element_type=F32)
        s2 = jnp.dot(k2_ref[...], q_s[pl.ds(base + half, half), :], preferred_element_type=F32)
        t1 = _top16_rows(s1)
        t2 = _top16_rows(s2)
        lo = stack(t2[:SUBLANES])
        hi = stack(t2[SUBLANES:])
        top = _merge_sublanes(_merge_top([t + lo for t in t1], [t + hi for t in t1]))
        z = jnp.exp(top[0] - top[0])
        for kk in range(1, PEER_TOPK):
            z = z + jnp.exp(top[kk] - top[0])
        thr = rep(top[PEER_TOPK - 1])
        t2r = [rep(t) for t in t2]
        r2 = _count_leading(lambda t: t > s2, t2r)
        n1 = _count_leading(lambda t: s1 + t >= thr, t2r)
        r2_o[h] = _pack_rows(r2)
        n_o[h] = _dup_bf16(n1)
        e1_o[h] = _dup_bf16(jnp.exp(s1 - rep(t1[0])) / rep(z))
        e2_o[h] = _pack_rows(jnp.exp(s2 - rep(t2[0])))
        return carry

    lax.fori_loop(0, PEER_HEADS, head, 0)


def _topk(f, wq_t, k1, k2):
    rows = f.shape[0]
    T = TOPK_TILE
    full = lambda a: pl.BlockSpec(a.shape, lambda i: (0,) * a.ndim)
    big = pl.BlockSpec((PEER_HEADS, N_KEYS, T), lambda i: (0, 0, i))
    big_shape = lambda dt: jax.ShapeDtypeStruct((PEER_HEADS, N_KEYS, rows), dt)
    packed = pl.BlockSpec((PEER_HEADS, N_KEYS // 2, T), lambda i: (0, 0, i))
    packed_shape = jax.ShapeDtypeStruct((PEER_HEADS, N_KEYS // 2, rows), U32)
    return pl.pallas_call(
        _topk_body, grid=(rows // T,),
        in_specs=[pl.BlockSpec((T, D_MODEL), lambda i: (i, 0)), full(wq_t), full(k1), full(k2)],
        out_specs=[big, big, packed, packed],
        out_shape=[big_shape(U32), big_shape(U32), packed_shape, packed_shape],
        scratch_shapes=[pltpu.VMEM((PEER_HEADS * PEER_DK, T), BF16)],
        compiler_params=_cparams(1), name="peer_topk",
    )(f, wq_t, k1, k2)


def _peer_body(final, n_e, n_steps, layer, f_ref, u_hbm, v_hbm, e1_ref, n_ref, r2_ref, e2_ref,
               h_ref, gate_ref, fg_ref, out, st_s, at_s, acc, u_buf, v_buf, sem):
    s = pl.program_id(0)
    j_up = jnp.maximum(s - 1, 0) % n_e
    cur = s % 2

    def table_copies(t):
        slot = t % PEER_TABLE_BUFFERS
        e_dn = pl.multiple_of((jnp.clip(t, 0, n_steps - 1) % n_e) * EXP_TILE, EXP_TILE)
        e_up = pl.multiple_of((jnp.clip(t - 1, 0, n_steps - 1) % n_e) * EXP_TILE, EXP_TILE)
        return (pltpu.make_async_copy(u_hbm.at[layer, pl.ds(e_dn, EXP_TILE), :], u_buf.at[slot],
                                      sem.at[0, slot]),
                pltpu.make_async_copy(v_hbm.at[layer, pl.ds(e_up, EXP_TILE), :], v_buf.at[slot],
                                      sem.at[1, slot]))

    lookahead = PEER_TABLE_BUFFERS - 1

    @pl.when(s == 0)
    def _():
        at_s[...] = jnp.zeros_like(at_s)
        for t in range(lookahead):
            for cp in table_copies(t):
                cp.start()

    @pl.when(s + lookahead < pl.num_programs(0))
    def _():
        for cp in table_copies(s + lookahead):
            cp.start()

    @pl.when(j_up == 0)
    def _():
        acc[...] = jnp.zeros_like(acc)

    for cp in table_copies(s):
        cp.wait()
    slot = s % PEER_TABLE_BUFFERS
    st_s[...] = _pack_rows(lax.dot_general(u_buf[slot], f_ref[...], (NT, ((), ())),
                                           preferred_element_type=F32))
    acc[...] += lax.dot_general(_unpack_rows(at_s[1 - cur]), v_buf[slot], (TN, ((), ())),
                                preferred_element_type=F32)

    blk = (N_KEYS // 2, LANES)
    group = 2
    for lg in range(TOK_TILE // LANES):
        ln = slice(lg * LANES, (lg + 1) * LANES)
        for i0 in range(0, EXP_TILE // N_KEYS, group):
            g = [None] * group
            for h in range(PEER_HEADS):
                r2_blk = _unpack_rows(r2_ref[h, :, ln])
                e2_blk = _unpack_rows(e2_ref[h, :, ln])
                for r in range(group):
                    ii = i0 + r
                    n_row = _unpack_rows(jnp.broadcast_to(n_ref[h, ii:ii + 1, ln], blk))
                    e1_row = _unpack_rows(jnp.broadcast_to(e1_ref[h, ii:ii + 1, ln], blk))
                    w = jnp.where(r2_blk < n_row, e2_blk, 0.0) * e1_row
                    g[r] = w if g[r] is None else g[r] + w
            for r in range(group):
                rows = slice((i0 + r) * N_KEYS // 2, (i0 + r + 1) * N_KEYS // 2)
                at_s[cur, rows, ln] = _pack_rows(
                    _gelu_tanh(_unpack_rows(st_s[rows, ln])) * g[r])

    @pl.when(jnp.logical_and(s >= 1, j_up == n_e - 1))
    def _():
        h = h_ref[...] + gate_ref[...] * acc[...]
        if final:
            h = h * lax.rsqrt(jnp.mean(h * h, axis=-1, keepdims=True) + EPS) * fg_ref[...]
        out[...] = h


def _peer(f, u_b, v_b, layer, e1, n1, r2, e2, h, mods, fg, n_lat, per_seq, n_tok_tiles, final):
    n_e = N_EXPERTS // EXP_TILE
    rpt = EXP_TILE // N_KEYS
    n_steps = n_tok_tiles * n_e

    def tile(s, lag):
        t = jnp.clip(s - lag, 0, n_steps - 1)
        return t // n_e, t % n_e

    tok = lambda lag: pl.BlockSpec((TOK_TILE, D_MODEL), lambda s: (tile(s, lag)[0], 0))
    table = pl.BlockSpec(memory_space=pl.ANY)
    table_buf = pltpu.VMEM((PEER_TABLE_BUFFERS, EXP_TILE, D_MODEL), BF16)
    rowblk = pl.BlockSpec((PEER_HEADS, rpt, TOK_TILE), lambda s: (0, tile(s, 0)[1], tile(s, 0)[0]))
    allkeys = pl.BlockSpec((PEER_HEADS, N_KEYS // 2, TOK_TILE), lambda s: (0, 0, tile(s, 0)[0]))

    def gate_map(s):
        i = tile(s, 1)[0]
        row = jnp.where(i < n_lat, i // per_seq, 2)
        return (row * 6 + 5, 0, 0)

    return pl.pallas_call(
        functools.partial(_peer_body, final, n_e, n_steps, layer),
        grid=(n_steps + 1,),
        in_specs=[tok(0), table, table, rowblk, rowblk, allkeys, allkeys, tok(1),
                  pl.BlockSpec((None, 1, D_MODEL), gate_map),
                  pl.BlockSpec((1, D_MODEL), lambda s: (0, 0))],
        out_specs=tok(1),
        out_shape=jax.ShapeDtypeStruct((n_tok_tiles * TOK_TILE, D_MODEL), F32),
        scratch_shapes=[pltpu.VMEM((EXP_TILE // 2, TOK_TILE), U32),
                        pltpu.VMEM((2, EXP_TILE // 2, TOK_TILE), U32),
                        pltpu.VMEM((TOK_TILE, D_MODEL), F32),
                        table_buf, table_buf,
                        pltpu.SemaphoreType.DMA((2, PEER_TABLE_BUFFERS))],
        compiler_params=_cparams(1), name="peer",
    )(f, u_b, v_b, e1, n1, r2, e2, h, mods, fg)


def _cast_body(u_ref, v_ref, uo, vo):
    uo[...] = u_ref[...].astype(BF16)
    vo[...] = v_ref[...].astype(BF16)


def _cast_tables(u, v):
    L, n, d = u.shape
    rows = 1024
    spec = pl.BlockSpec((None, rows, d), lambda l, i: (l, i, 0))
    shape = jax.ShapeDtypeStruct((L, n, d), BF16)
    return pl.pallas_call(
        _cast_body, grid=(L, n // rows), in_specs=[spec, spec], out_specs=[spec, spec],
        out_shape=[shape, shape], compiler_params=_cparams(2), name="cast_tables",
    )(u, v)


def _sincos_2d(t_len):
    rows = t_len // GRID_W
    q = D_MODEL // 4
    freq = 10000.0 ** (-jnp.arange(q, dtype=F32) / q)
    ar = jnp.arange(rows, dtype=F32)[:, None] * freq
    ac = jnp.arange(GRID_W, dtype=F32)[:, None] * freq
    per_row = lambda a: jnp.repeat(a, GRID_W, axis=0)
    per_col = lambda a: jnp.tile(a, (rows, 1))
    return jnp.concatenate([per_row(jnp.sin(ar)), per_row(jnp.cos(ar)),
                            per_col(jnp.sin(ac)), per_col(jnp.cos(ac))], -1)


def _row(a):
    return a.reshape(1, -1).astype(F32)


def _pad_lanes(a, width=LANES):
    return jnp.pad(a, ((0, 0), (0, width - a.shape[-1])))


def kernel(x, c, ctx, c_ctx, ada_w, ada_b, norm_mix_g, norm_ffn_g, w_in, w_out, a_conv_w, a_conv_b, a_ln_g, a_ln_b, b_ln_g, b_ln_b, b_ws, b_bs, c_conv_w, c_conv_b, c_dt_bias, c_a_log, c_d, c_norm_g, d_mu_prev, d_mu_next, d_w0, d_w2, d_a0, d_a2, d_g2, d_k_k, d_k_a, d_r_k, d_gn_g, d_gn_b, peer_wq, peer_k1, peer_k2, peer_u, peer_v, final_g):
    n_b, t_lat, _ = x.shape
    t_ctx = ctx.shape[1]
    assert n_b == 2 and t_ctx == SEQ_TILE and t_lat % TOK_TILE == 0
    n_l = t_lat // SEQ_TILE
    per_seq = t_lat // TOK_TILE
    n_lat = n_b * per_seq
    n_lat256 = n_b * n_l

    cs = jnp.zeros((MOD_ROWS, D_MODEL), F32).at[:n_b].set(c).at[n_b].set(c_ctx)
    mods_all = _ada(cs, ada_w, ada_b)
    pos = _sincos_2d(t_lat)
    u_bf, v_bf = _cast_tables(peer_u, peer_v)
    x2 = x.reshape(n_b * t_lat, D_MODEL)
    ctx2 = ctx.reshape(n_b * t_ctx, D_MODEL)

    h = None
    for i in range(DEPTH):
        last_layer = i == DEPTH - 1
        mods = mods_all[i].reshape(MOD_ROWS * 6, 1, D_MODEL)
        w = w_in[i]
        dtc = OFF_C + D_GROUP + C_XBC
        w_c = jnp.concatenate([w[:, OFF_C:dtc], _pad_lanes(w[:, dtc:dtc + HEADS_C]),
                               _pad_lanes(w[:, dtc + HEADS_C:dtc + 2 * HEADS_C])], axis=1)
        ws_in = tuple(a.astype(BF16) for a in (w[:, :OFF_B], w[:, OFF_B:OFF_C], w_c, w[:, OFF_D:]))
        g_mix = _row(norm_mix_g[i])
        if i == 0:
            h, pa, pb, pc, pd = _inproj((x2, ctx2, pos), mods, g_mix, ws_in, n_lat, per_seq, True)
        else:
            pa, pb, pc, pd = _inproj(h, mods, g_mix, ws_in, n_lat, per_seq, False)

        conv_a = jnp.pad(a_conv_w[i], ((0, 32 - CONV_A), (0, 0)))
        bsm = jnp.repeat(b_bs[i].T, D_GROUP // HEADS_B, axis=1)
        yab = _mixab(pa, pb, (conv_a, _row(a_conv_b[i]), _row(a_ln_g[i]), _row(a_ln_b[i]),
                              _row(b_ln_g[i]), _row(b_ln_b[i]), b_ws[i].astype(BF16), bsm),
                     n_lat256, n_l)

        conv_c = jnp.pad(c_conv_w[i], ((0, SUBLANES - CONV_C), (0, 0)))
        dtb = _pad_lanes(c_dt_bias[i]).reshape(2, 1, LANES)
        alog = _pad_lanes(c_a_log[i]).reshape(2, 1, LANES)
        dsk = _row(jnp.repeat(c_d[i], HEAD_DIM_C))
        yc = _mamba(pc, (conv_c, _row(c_conv_b[i]), dtb, alog, dsk, _row(c_norm_g[i])), n_b, n_l)

        yd = _rwkv(pd, (_row(d_mu_prev[i]), _row(d_mu_next[i]), d_w0[i].reshape(2, 1, D_GROUP),
                        d_w2[i].astype(BF16), d_a0[i].reshape(2, 1, D_GROUP), d_a2[i].astype(BF16),
                        d_g2[i].astype(BF16), _row(d_k_k[i]), _row(d_k_a[i]), _row(d_r_k[i]),
                        _row(d_gn_g[i]), _row(d_gn_b[i])), n_b, n_l)

        wo = w_out[i].astype(BF16)
        h, f = _outproj(yab, yc, yd, h, mods, _row(norm_ffn_g[i]),
                        (wo[:512], wo[512:768], wo[768:]), n_lat, per_seq)

        e1, n1, r2, e2 = _topk(f, peer_wq[i].T.astype(BF16), peer_k1[i].astype(BF16),
                               peer_k2[i].astype(BF16))
        n_tok = n_lat if last_layer else n_lat + 1
        h = _peer(f, u_bf, v_bf, i, e1, n1, r2, e2, h,
                  mods, _row(final_g), n_lat, per_seq, n_tok, last_layer)
    return h.reshape(n_b, t_lat, D_MODEL)
```

```python
import functools
import math

import jax
import jax.numpy as jnp
from jax import lax
from jax.experimental import pallas as pl
from jax.experimental.pallas import tpu as pltpu

F32 = jnp.float32
BF16 = jnp.bfloat16
U32 = jnp.uint32
HIGHEST = lax.Precision.HIGHEST

D_MODEL = 1024
DEPTH = 2
GRID_W = 64
EPS = 1e-6
D_GROUP = 256
CONV_A = 31
CHUNK_B = 128
HEADS_B = 4
HEADS_C = 4
HEAD_DIM_C = 64
STATE_C = 128
CONV_C = 5
CHUNK_C = 128
HEADS_D = 4
HEAD_DIM_D = 64
LORA_W = 64
LORA_A = 64
LORA_G = 128
GN_EPS_D = 64e-5
N_KEYS = 128
N_EXPERTS = N_KEYS * N_KEYS
PEER_HEADS = 8
PEER_DK = 256
PEER_TOPK = 16

C_XBC = D_GROUP + 2 * 2 * STATE_C
OFF_B = 2 * D_GROUP
OFF_C = OFF_B + 2 * D_GROUP
OFF_D = OFF_C + D_GROUP + C_XBC + 2 * HEADS_C
D_COLS = 3 * D_GROUP + 2 * LORA_W + 2 * LORA_A + LORA_G
PC_COLS = D_GROUP + C_XBC + 2 * 128

LANES = 128
SUBLANES = 8
SEQ_TILE = 256
TOK_TILE = 512
TOPK_TILE = 512
MOD_ROWS = SUBLANES
CHUNK_D = 64
EXP_TILE = 2048
VMEM_LIMIT = 56 * 1024 * 1024


def _cparams(n_axes):
    return pltpu.CompilerParams(dimension_semantics=("arbitrary",) * n_axes,
                                vmem_limit_bytes=VMEM_LIMIT)


def _silu(x):
    return x * jax.nn.sigmoid(x)


def _dot(a, b, dims=None):
    a = a.astype(BF16)
    b = b.astype(BF16)
    if dims is None:
        return jnp.dot(a, b, preferred_element_type=F32)
    return lax.dot_general(a, b, (dims, ((), ())), preferred_element_type=F32)


def _dot_hi(a, b):
    return jnp.dot(a, b, precision=HIGHEST, preferred_element_type=F32)


def _split_bf16(x, terms):
    parts = []
    for _ in range(terms):
        part = x.astype(BF16)
        parts.append(part)
        x = x - part.astype(F32)
    return parts


def _dot_mask_rhs(a, mask, terms=3):
    m = mask.astype(BF16)
    return sum(jnp.dot(p, m, preferred_element_type=F32) for p in _split_bf16(a, terms))


def _dot_mask_lhs(mask, b, terms=3):
    m = mask.astype(BF16)
    return sum(jnp.dot(m, p, preferred_element_type=F32) for p in _split_bf16(b, terms))


NT = ((1,), (1,))
TN = ((0,), (0,))


def _gelu_tanh(x):
    k1 = math.sqrt(2.0 / math.pi)
    k3 = 0.044715 * k1
    hx = 0.5 * x
    return hx + hx * jnp.tanh(x * (k1 + k3 * (x * x)))


def _pack_rows(x):
    return pltpu.bitcast(x.astype(BF16), U32)


def _unpack_rows(x):
    return pltpu.bitcast(x, BF16)


def _dup_bf16(x):
    hi = pltpu.bitcast(x.astype(BF16).astype(F32), U32)
    return hi | (hi >> 16)


def _iota(shape, axis):
    return lax.broadcasted_iota(jnp.int32, shape, axis)


def _head_block_ones(n, width):
    return (_iota((n, n), 0) // width == _iota((n, n), 1) // width).astype(F32)


def _ada_body(cs_ref, w_ref, b_ref, o_ref):
    o_ref[...] = _dot_hi(_silu(cs_ref[...]), w_ref[...]) + b_ref[...]


def _ada(cs, ada_w, ada_b):
    L = ada_w.shape[0]
    nb = 6 * D_MODEL // 4
    return pl.pallas_call(
        _ada_body,
        grid=(L, 6 * D_MODEL // nb),
        in_specs=[pl.BlockSpec((MOD_ROWS, D_MODEL), lambda l, n: (0, 0)),
                  pl.BlockSpec((None, D_MODEL, nb), lambda l, n: (l, 0, n)),
                  pl.BlockSpec((None, 1, nb), lambda l, n: (l, 0, n))],
        out_specs=pl.BlockSpec((None, MOD_ROWS, nb), lambda l, n: (l, 0, n)),
        out_shape=jax.ShapeDtypeStruct((L, MOD_ROWS, 6 * D_MODEL), F32),
        compiler_params=_cparams(2),
        name="ada",
    )(cs, ada_w, ada_b.reshape(L, 1, 6 * D_MODEL))


def _norm_mod(h, g, shift, scale):
    xn = h * lax.rsqrt(jnp.mean(h * h, axis=-1, keepdims=True) + EPS) * g
    return xn * (1.0 + scale) + shift


def _inproj_tail(h, sh_ref, sc_ref, g_ref, wa, wb, wc, wd, oa, ob, oc, od):
    xm = _norm_mod(h, g_ref[...], sh_ref[...], sc_ref[...]).astype(BF16)
    oa[...] = jnp.dot(xm, wa[...], preferred_element_type=F32)
    ob[...] = jnp.dot(xm, wb[...], preferred_element_type=F32)
    oc[...] = jnp.dot(xm, wc[...], preferred_element_type=F32)
    od[...] = jnp.dot(xm, wd[...], preferred_element_type=F32)


def _inproj_first_body(n_lat, x_ref, ctx_ref, pos_ref, sh_ref, sc_ref, g_ref, wa, wb, wc, wd,
                       oh, oa, ob, oc, od):
    i = pl.program_id(0)
    h = jnp.where(i < n_lat, x_ref[...] + pos_ref[...], ctx_ref[...])
    oh[...] = h
    _inproj_tail(h, sh_ref, sc_ref, g_ref, wa, wb, wc, wd, oa, ob, oc, od)


def _inproj_body(h_ref, sh_ref, sc_ref, g_ref, wa, wb, wc, wd, oa, ob, oc, od):
    _inproj_tail(h_ref[...], sh_ref, sc_ref, g_ref, wa, wb, wc, wd, oa, ob, oc, od)


def _mod_spec(n_lat, per_seq, k):
    def imap(i):
        row = jnp.where(i < n_lat, i // per_seq, 2)
        return (row * 6 + k, 0, 0)
    return pl.BlockSpec((None, 1, D_MODEL), imap)


def _inproj(h_or_parts, mods, g, ws, n_lat, per_seq, first):
    wa, wb, wc, wd = ws
    n_tiles = n_lat + 1
    rows = n_tiles * TOK_TILE
    tile = lambda c: pl.BlockSpec((TOK_TILE, c), lambda i: (i, 0))
    full = lambda a: pl.BlockSpec(a.shape, lambda i: (0,) * a.ndim)
    common_specs = [_mod_spec(n_lat, per_seq, 0), _mod_spec(n_lat, per_seq, 1), full(g),
                    full(wa), full(wb), full(wc), full(wd)]
    out_specs = [tile(512), tile(512), tile(PC_COLS), tile(D_COLS)]
    out_shape = [jax.ShapeDtypeStruct((rows, c), F32) for c in (512, 512, PC_COLS, D_COLS)]
    if first:
        x2, ctx2, pos = h_or_parts
        in_specs = [pl.BlockSpec((TOK_TILE, D_MODEL), lambda i: (jnp.minimum(i, n_lat - 1), 0)),
                    pl.BlockSpec((TOK_TILE, D_MODEL), lambda i: (0, 0)),
                    pl.BlockSpec((TOK_TILE, D_MODEL), lambda i: (i % per_seq, 0))] + common_specs
        return pl.pallas_call(
            functools.partial(_inproj_first_body, n_lat),
            grid=(n_tiles,), in_specs=in_specs,
            out_specs=[tile(D_MODEL)] + out_specs,
            out_shape=[jax.ShapeDtypeStruct((rows, D_MODEL), F32)] + out_shape,
            compiler_params=_cparams(1), name="inproj_first",
        )(x2, ctx2, pos, mods, mods, g, wa, wb, wc, wd)
    return pl.pallas_call(
        _inproj_body, grid=(n_tiles,), in_specs=[tile(D_MODEL)] + common_specs,
        out_specs=out_specs, out_shape=out_shape,
        compiler_params=_cparams(1), name="inproj",
    )(h_or_parts, mods, mods, g, wa, wb, wc, wd)


def _layernorm(x, g, b, eps=1e-5):
    mu = jnp.mean(x, axis=-1, keepdims=True)
    xc = x - mu
    var = jnp.mean(xc * xc, axis=-1, keepdims=True)
    return xc * lax.rsqrt(var + eps) * g + b


def _mixab_body(n_lat, n_l, pa, pa_prev, pa_next, pb, cw, cb, alg, alb, blg, blb, ws, bsm,
                out, ext):
    i = pl.program_id(0)
    is_ctx = i >= n_lat
    tpos = i % n_l
    first = jnp.logical_or(is_ctx, tpos == 0)
    last = jnp.logical_or(is_ctx, tpos == n_l - 1)

    def glu(x):
        return x[:, :D_GROUP] * jax.nn.sigmoid(x[:, D_GROUP:])

    halo = 16
    ext[0:halo, :] = jnp.where(first, 0.0, glu(pa_prev[...]))
    ext[halo:halo + SEQ_TILE, :] = glu(pa[...])
    ext[halo + SEQ_TILE:, :] = jnp.where(last, 0.0, glu(pa_next[...]))
    acc = jnp.zeros((SEQ_TILE, D_GROUP), F32) + cb[...]
    for j in range(CONV_A):
        acc = acc + cw[j:j + 1, :] * ext[pl.ds(halo - (CONV_A - 1) // 2 + j, SEQ_TILE), :]
    out[:, :D_GROUP] = _silu(_layernorm(acc, alg[...], alb[...]))

    x = pb[...]
    u = x[:, :D_GROUP]
    v = _layernorm(x[:, D_GROUP:], blg[...], blb[...]).astype(BF16)
    hd = D_GROUP // HEADS_B
    rows = []
    for c in range(SEQ_TILE // CHUNK_B):
        vc = v[c * CHUNK_B:(c + 1) * CHUNK_B, :]
        heads = [jnp.dot(ws[hh], vc[:, hh * hd:(hh + 1) * hd], preferred_element_type=F32)
                 for hh in range(HEADS_B)]
        rows.append(jnp.concatenate(heads, axis=1) + bsm[...])
    out[:, D_GROUP:] = u * jnp.concatenate(rows, axis=0)


def _mixab(pa, pb, prm, n_lat, n_l):
    rows = pa.shape[0]
    n_tiles = rows // SEQ_TILE
    halo = 16
    per = SEQ_TILE // halo
    full = lambda a: pl.BlockSpec(a.shape, lambda i: (0,) * a.ndim)
    in_specs = [pl.BlockSpec((SEQ_TILE, 512), lambda i: (i, 0)),
                pl.BlockSpec((halo, 512), lambda i: (jnp.maximum(i * per - 1, 0), 0)),
                pl.BlockSpec((halo, 512), lambda i: (jnp.minimum((i + 1) * per, rows // halo - 1), 0)),
                pl.BlockSpec((SEQ_TILE, 512), lambda i: (i, 0))] + [full(a) for a in prm]
    return pl.pallas_call(
        functools.partial(_mixab_body, n_lat, n_l),
        grid=(n_tiles,), in_specs=in_specs,
        out_specs=pl.BlockSpec((SEQ_TILE, 512), lambda i: (i, 0)),
        out_shape=jax.ShapeDtypeStruct((rows, 512), F32),
        scratch_shapes=[pltpu.VMEM((SEQ_TILE + 2 * halo, D_GROUP), F32)],
        compiler_params=_cparams(1), name="mixab",
    )(pa, pa, pa, pb, *prm)


def _scan_tile(n_b, n_l, b, ph, j):
    lat = b * n_l + jnp.where(ph == 0, j - 1, n_l - j)
    return jnp.where(j == 0, n_b * n_l + b, lat)


def _scan_specs(n_b, n_l, rows, cols):
    tid = functools.partial(_scan_tile, n_b, n_l)
    halo = SUBLANES
    per = SEQ_TILE // halo
    cur = pl.BlockSpec((SEQ_TILE, cols), lambda b, ph, j: (tid(b, ph, j), 0))
    prev = pl.BlockSpec((halo, cols),
                        lambda b, ph, j: (jnp.maximum(tid(b, ph, j) * per - 1, 0), 0))
    nxt = pl.BlockSpec((halo, cols),
                       lambda b, ph, j: (jnp.minimum((tid(b, ph, j) + 1) * per, rows // halo - 1), 0))
    return cur, prev, nxt


def _scan_out_spec(n_b, n_l, cols):
    def imap(b, ph, j):
        return (jnp.where(ph == 0, n_b * n_l + b, _scan_tile(n_b, n_l, b, 1, j)), 0)
    return pl.BlockSpec((SEQ_TILE, cols), imap)


def _scan_flags(n_l):
    ph = pl.program_id(1)
    j = pl.program_id(2)
    is_ctx = j == 0
    tpos = jnp.where(ph == 0, j - 1, n_l - j)
    first = jnp.logical_or(is_ctx, tpos == 0)
    last = jnp.logical_or(is_ctx, tpos == n_l - 1)
    slot = jnp.where(is_ctx, 0, tpos + 1)
    return ph, j, first, last, slot


def _mamba_body(n_l, cur_ref, prev_ref, next_ref, dt_ref, cw, cb, dtb, alog, dsk, ng,
                out, ext, yf, st, xbc_s, a_s, dt_s, y_s):
    ph, j, first, last, slot = _scan_flags(n_l)
    fwd = ph == 0

    @pl.when(j == 0)
    def _():
        st[...] = jnp.zeros_like(st)

    zx = D_GROUP
    halo = SUBLANES
    ext[0:halo, :] = jnp.where(first, 0.0, prev_ref[:, zx:])
    ext[halo:halo + SEQ_TILE, :] = cur_ref[:, zx:]
    ext[halo + SEQ_TILE:, :] = jnp.where(last, 0.0, next_ref[:, zx:])
    acc = jnp.zeros((SEQ_TILE, C_XBC), F32) + cb[...]
    for jj in range(CONV_C):
        acc = acc + cw[jj:jj + 1, :] * ext[pl.ds(halo - (CONV_C - 1) // 2 + jj, SEQ_TILE), :]
    xbc_s[...] = _silu(acc)
    dt = jax.nn.softplus(dt_ref[...] + dtb[ph])
    dt_s[...] = dt
    a_s[...] = dt * (-jnp.exp(alog[ph]))

    L = CHUNK_C
    r_i = _iota((L, L), 0)
    c_i = _iota((L, L), 1)
    ltri = (r_i >= c_i).astype(F32)
    mask = jnp.where(fwd, r_i - c_i, c_i - r_i) >= 0
    sgn = jnp.where(fwd, 1.0, -1.0)
    hd = HEAD_DIM_C

    def chunk(it, carry):
        q = jnp.where(fwd, it, SEQ_TILE // L - 1 - it)
        rows = pl.ds(pl.multiple_of(q * L, L), L)
        a_q = a_s[rows, :]
        cs = _dot_mask_lhs(ltri, a_q)
        tot = cs[L - 1:L, :]
        e = jnp.where(fwd, cs, cs - a_q)
        e_t = e.T
        dt_q = dt_s[rows, :]
        xbc = xbc_s[rows, :]
        ys = []
        for g in range(2):
            bm = xbc[:, D_GROUP + g * STATE_C:D_GROUP + (g + 1) * STATE_C]
            cm = xbc[:, D_GROUP + 2 * STATE_C + g * STATE_C:D_GROUP + 2 * STATE_C + (g + 1) * STATE_C]
            bm_t = bm.T.astype(BF16)
            cm_b = cm.astype(BF16)
            gmat = jnp.dot(cm_b, bm_t, preferred_element_type=F32)
            for h in (2 * g, 2 * g + 1):
                ecol = e[:, h:h + 1]
                erow = e_t[h:h + 1, :]
                totc = tot[:, h:h + 1]
                xdt = xbc[:, h * hd:(h + 1) * hd] * dt_q[:, h:h + 1]
                lmat = jnp.exp(jnp.where(mask, sgn * (ecol - erow), -1e30))
                offs = jnp.exp(jnp.where(fwd, ecol, totc - ecol))
                stw = jnp.exp(jnp.where(fwd, totc - ecol, ecol))
                s_prev = st[h]
                y = _dot(gmat * lmat, xdt) + offs * _dot(cm_b, s_prev)
                st[h] = jnp.exp(totc) * s_prev + _dot(bm_t, xdt * stw)
                ys.append(y)
        y_s[rows, :] = jnp.concatenate(ys, axis=1)
        return carry

    lax.fori_loop(0, SEQ_TILE // L, chunk, 0)

    @pl.when(fwd)
    def _():
        yf[slot] = y_s[...]

    @pl.when(ph == 1)
    def _():
        y = yf[slot] + y_s[...] + dsk[...] * xbc_s[:, :D_GROUP]
        t = y * _silu(cur_ref[:, :D_GROUP])
        out[...] = t * lax.rsqrt(jnp.mean(t * t, axis=-1, keepdims=True) + EPS) * ng[...]


def _mamba(pc, prm, n_b, n_l):
    rows = pc.shape[0]
    cur, prev, nxt = _scan_specs(n_b, n_l, rows, D_GROUP + C_XBC)
    tid = functools.partial(_scan_tile, n_b, n_l)
    dt_spec = pl.BlockSpec((SEQ_TILE, LANES),
                           lambda b, ph, j: (tid(b, ph, j), (D_GROUP + C_XBC) // LANES + ph))
    full = lambda a: pl.BlockSpec(a.shape, lambda b, ph, j: (0,) * a.ndim)
    return pl.pallas_call(
        functools.partial(_mamba_body, n_l),
        grid=(n_b, 2, n_l + 1),
        in_specs=[cur, prev, nxt, dt_spec] + [full(a) for a in prm],
        out_specs=_scan_out_spec(n_b, n_l, D_GROUP),
        out_shape=jax.ShapeDtypeStruct((rows, D_GROUP), F32),
        scratch_shapes=[pltpu.VMEM((SEQ_TILE + 2 * SUBLANES, C_XBC), F32),
                        pltpu.VMEM((n_l + 1, SEQ_TILE, D_GROUP), F32),
                        pltpu.VMEM((HEADS_C, STATE_C, HEAD_DIM_C), F32),
                        pltpu.VMEM((SEQ_TILE, C_XBC), F32),
                        pltpu.VMEM((SEQ_TILE, LANES), F32),
                        pltpu.VMEM((SEQ_TILE, LANES), F32),
                        pltpu.VMEM((SEQ_TILE, D_GROUP), F32)],
        compiler_params=_cparams(3), name="mamba",
    )(pc, pc, pc, pc, *prm)


def _rwkv_body(n_l, cur_ref, prev_ref, next_ref, mup, mun, w0, w2, a0, a2, g2, kkw, kaw, rkw,
               gng, gnb, out, yf, st, r_s, v_s, kap_s, alp_s, kd_s, lw_s, y_s,
               phi_s, psi_s, sin_s, left_s, amat_s, ara_s, tinv_s):
    ph, j, first, last, slot = _scan_flags(n_l)
    fwd = ph == 0

    @pl.when(j == 0)
    def _():
        st[...] = jnp.zeros_like(st)

    cur = cur_ref[...]
    prv = jnp.concatenate([jnp.where(first, 0.0, prev_ref[SUBLANES - 1:SUBLANES, :]),
                           cur[:SEQ_TILE - 1, :]], axis=0)
    nxt = jnp.concatenate([cur[1:, :], jnp.where(last, 0.0, next_ref[0:1, :])], axis=0)
    p = cur + mup[...] * (prv - cur) + mun[...] * (nxt - cur)
    G = D_GROUP
    r = p[:, :G]
    k = p[:, G:2 * G]
    v = p[:, 2 * G:3 * G]
    blk = _head_block_ones(G, HEAD_DIM_D)
    kkr = k * kkw[...]
    kk = kkr * lax.rsqrt(_dot_mask_rhs(kkr * kkr, blk) + 1e-12)

    def rate(d_static=None):
        if d_static is None:
            ad = jnp.where(fwd, p[:, 3 * G + 2 * LORA_W:3 * G + 2 * LORA_W + LORA_A],
                           p[:, 3 * G + 2 * LORA_W + LORA_A:3 * G + 2 * LORA_W + 2 * LORA_A])
            a = jax.nn.sigmoid(a0[ph] + _dot(ad, a2[ph]))
        else:
            o = 3 * G + 2 * LORA_W + d_static * LORA_A
            a = jax.nn.sigmoid(a0[d_static] + _dot(p[:, o:o + LORA_A], a2[d_static]))
        return a, k * (1.0 + (a - 1.0) * kaw[...])

    a_d, kd_d = rate()
    wd = jnp.where(fwd, p[:, 3 * G:3 * G + LORA_W], p[:, 3 * G + LORA_W:3 * G + 2 * LORA_W])
    w = w0[ph] + _dot(jnp.tanh(wd), w2[ph])
    lw_s[...] = -math.exp(-0.5) * jax.nn.sigmoid(w)
    r_s[...] = r
    v_s[...] = v
    kap_s[...] = kk
    alp_s[...] = a_d * kk
    kd_s[...] = kd_d

    C = CHUNK_D
    r_i = _iota((C, C), 0)
    c_i = _iota((C, C), 1)
    lag = jnp.where(fwd, r_i - c_i, c_i - r_i)
    tri = lag >= 0
    tri_f = tri.astype(F32)
    strict = lag > 0
    eye = (r_i == c_i).astype(F32)
    hd = HEAD_DIM_D

    n_q = SEQ_TILE // C
    eye_k = (_iota((hd, hd), 0) == _iota((hd, hd), 1)).astype(F32)

    units = [(q, h) for q in range(n_q) for h in range(HEADS_D)]
    rows_of = lambda q: slice(q * C, (q + 1) * C)
    lanes_of = lambda h: slice(h * hd, (h + 1) * hd)
    left, right, p_tot = {}, {}, {}
    for q in range(n_q):
        rows = rows_of(q)
        lw = lw_s[rows, :]
        incl = _dot_mask_lhs(tri_f, lw)
        tot = jnp.where(fwd, incl[C - 1:C, :], incl[0:1, :])
        p_inv = jnp.exp(-incl)
        kap_h = kap_s[rows, :] * jnp.exp(incl - lw)
        r_h = r_s[rows, :] * jnp.exp(incl)
        alp_b = alp_s[rows, :] * p_inv
        k_b = kd_s[rows, :] * p_inv
        ptq = jnp.exp(tot)
        for h in range(HEADS_D):
            sl = lanes_of(h)
            left[q, h] = jnp.concatenate([kap_h[:, sl], r_h[:, sl]], axis=0)
            right[q, h] = jnp.concatenate([alp_b[:, sl], k_b[:, sl]], axis=0)
            p_tot[q, h] = ptq[:, sl]
            left_s[q * HEADS_D + h] = left[q, h]
    m1 = {u: _dot(left[u], right[u], NT) for u in units}
    a_vk, pw, tinv = {}, {}, {}
    for u in units:
        qh = u[0] * HEADS_D + u[1]
        a_vk[u] = jnp.where(strict, m1[u][:C, C:], 0.0)
        amat_s[qh] = jnp.concatenate([a_vk[u], jnp.where(tri, m1[u][C:, C:], 0.0)], axis=0)
        ara_s[qh] = jnp.where(tri, m1[u][C:, :C], 0.0)
        pw[u] = jnp.where(strict, -m1[u][:C, :C], 0.0)
        tinv[u] = eye + pw[u]
    pw = {u: _dot(pw[u], pw[u]) for u in units}
    for _ in range(5):
        both = {u: _dot(jnp.concatenate([pw[u], tinv[u]], axis=0), pw[u]) for u in units}
        tinv = {u: tinv[u] + both[u][C:] for u in units}
        pw = {u: both[u][:C] for u in units}
    wmat = {u: _dot(tinv[u], right[u][:C], TN) for u in units}
    kw_aw = {u: _dot(jnp.concatenate([left[u][:C], a_vk[u]], axis=1), wmat[u], TN) for u in units}
    vk = {u: _dot(v_s[rows_of(u[0]), lanes_of(u[1])], right[u][C:] - kw_aw[u][hd:], TN)
          for u in units}
    for u in units:
        qh = u[0] * HEADS_D + u[1]
        phi_s[qh] = (eye_k - kw_aw[u][:hd]) * p_tot[u]
        psi_s[qh] = vk[u] * p_tot[u]
        tinv_s[qh] = tinv[u]

    for it in range(n_q):
        q = jnp.where(fwd, it, n_q - 1 - it)
        for h in range(HEADS_D):
            qh = q * HEADS_D + h
            s_in = st[h]
            sin_s[qh] = s_in
            st[h] = _dot(s_in, phi_s[qh]) + psi_s[qh]

    qh_of = lambda u: u[0] * HEADS_D + u[1]
    x0 = {u: _dot(left_s[qh_of(u)], sin_s[qh_of(u)], NT) for u in units}
    x1 = {u: _dot(amat_s[qh_of(u)], v_s[rows_of(u[0]), lanes_of(u[1])]) for u in units}
    uu = {u: _dot(tinv_s[qh_of(u)], x0[u][:C] + x1[u][:C]) for u in units}
    au = {u: _dot(ara_s[qh_of(u)], uu[u]) for u in units}
    for q in range(n_q):
        y_s[rows_of(q), :] = jnp.concatenate(
            [x0[q, h][C:] + x1[q, h][C:] - au[q, h] for h in range(HEADS_D)], axis=1)

    @pl.when(fwd)
    def _():
        yf[slot] = y_s[...]

    @pl.when(ph == 1)
    def _():
        y = yf[slot] + y_s[...]
        inv = 1.0 / HEAD_DIM_D
        mu = _dot_mask_rhs(y, blk) * inv
        yc = y - mu
        var = _dot_mask_rhs(yc * yc, blk) * inv
        yn = yc * lax.rsqrt(var + GN_EPS_D) * gng[...] + gnb[...]
        _, kd_f = rate(0)
        bonus = _dot_mask_rhs(r * (kd_f + kd_d) * rkw[...], blk) * v
        gate = _dot(jax.nn.sigmoid(p[:, 3 * G + 2 * LORA_W + 2 * LORA_A:]), g2[...])
        out[...] = (yn + bonus) * gate


def _rwkv(pd, prm, n_b, n_l):
    rows = pd.shape[0]
    cur, prev, nxt = _scan_specs(n_b, n_l, rows, D_COLS)
    full = lambda a: pl.BlockSpec(a.shape, lambda b, ph, j: (0,) * a.ndim)
    tile = lambda: pltpu.VMEM((SEQ_TILE, D_GROUP), F32)
    per_qh = lambda r, c: pltpu.VMEM((SEQ_TILE // CHUNK_D * HEADS_D, r, c), F32)
    return pl.pallas_call(
        functools.partial(_rwkv_body, n_l),
        grid=(n_b, 2, n_l + 1),
        in_specs=[cur, prev, nxt] + [full(a) for a in prm],
        out_specs=_scan_out_spec(n_b, n_l, D_GROUP),
        out_shape=jax.ShapeDtypeStruct((rows, D_GROUP), F32),
        scratch_shapes=[pltpu.VMEM((n_l + 1, SEQ_TILE, D_GROUP), F32),
                        pltpu.VMEM((HEADS_D, HEAD_DIM_D, HEAD_DIM_D), F32)] + [tile() for _ in range(7)]
        + [per_qh(HEAD_DIM_D, HEAD_DIM_D) for _ in range(3)]
        + [per_qh(2 * CHUNK_D, HEAD_DIM_D), per_qh(2 * CHUNK_D, CHUNK_D),
           per_qh(CHUNK_D, CHUNK_D), per_qh(CHUNK_D, CHUNK_D)],
        compiler_params=_cparams(3), name="rwkv",
    )(pd, pd, pd, *prm)


def _outproj_body(yab, yc, yd, h_ref, gate, sh, sc, g, wab, wc, wd, oh, of):
    mix = (_dot(yab[...], wab[...]) + _dot(yc[...], wc[...]) + _dot(yd[...], wd[...]))
    h = h_ref[...] + gate[...] * mix
    oh[...] = h
    of[...] = _norm_mod(h, g[...], sh[...], sc[...]).astype(BF16)


def _outproj(yab, yc, yd, h, mods, g, ws, n_lat, per_seq):
    rows = h.shape[0]
    tile = lambda c: pl.BlockSpec((TOK_TILE, c), lambda i: (i, 0))
    full = lambda a: pl.BlockSpec(a.shape, lambda i: (0,) * a.ndim)
    return pl.pallas_call(
        _outproj_body, grid=(rows // TOK_TILE,),
        in_specs=[tile(512), tile(D_GROUP), tile(D_GROUP), tile(D_MODEL),
                  _mod_spec(n_lat, per_seq, 2), _mod_spec(n_lat, per_seq, 3),
                  _mod_spec(n_lat, per_seq, 4), full(g)] + [full(a) for a in ws],
        out_specs=[tile(D_MODEL), tile(D_MODEL)],
        out_shape=[jax.ShapeDtypeStruct((rows, D_MODEL), F32),
                   jax.ShapeDtypeStruct((rows, D_MODEL), BF16)],
        compiler_params=_cparams(1), name="outproj",
    )(yab, yc, yd, h, mods, mods, mods, g, *ws)


def _cmpx(lst, i, j):
    a, b = lst[i], lst[j]
    lst[i] = jnp.maximum(a, b)
    lst[j] = jnp.minimum(a, b)


def _bitonic_sort_desc(lst):
    n = len(lst)
    k = 2
    while k <= n:
        j = k // 2
        while j >= 1:
            for i in range(n):
                p = i ^ j
                if p > i:
                    if (i & k) == 0:
                        _cmpx(lst, i, p)
                    else:
                        _cmpx(lst, p, i)
            j //= 2
        k *= 2


def _bitonic_merge_desc(lst):
    n = len(lst)
    j = n // 2
    while j >= 1:
        for i in range(n):
            p = i ^ j
            if p > i:
                _cmpx(lst, i, p)
        j //= 2


def _merge_top(a, b):
    n = len(a)
    c = [jnp.maximum(a[i], b[n - 1 - i]) for i in range(n)]
    _bitonic_merge_desc(c)
    return c


def _merge_sublanes(lst):
    for shift in (4, 2, 1):
        lst = _merge_top(lst, [pltpu.roll(a, shift, axis=0) for a in lst])
    return lst


def _count_leading(pred, t):
    sel = jnp.where
    c1 = pred(t[7])
    c2 = pred(sel(c1, t[11], t[3]))
    c3 = pred(sel(c1, sel(c2, t[13], t[9]), sel(c2, t[5], t[1])))
    c4 = pred(sel(c1, sel(c2, sel(c3, t[14], t[12]), sel(c3, t[10], t[8])),
                  sel(c2, sel(c3, t[6], t[4]), sel(c3, t[2], t[0]))))
    c5 = pred(t[15])
    return (sel(c1, 8.0, 0.0) + sel(c2, 4.0, 0.0) + sel(c3, 2.0, 0.0) + sel(c4, 1.0, 0.0)
            + sel(c5, 1.0, 0.0))


def _top16_rows(s):
    lst = [s[SUBLANES * v:SUBLANES * (v + 1), :] for v in range(s.shape[0] // SUBLANES)]
    _bitonic_sort_desc(lst)
    return _merge_sublanes(lst)


def _topk_body(f_ref, wq_ref, k1_ref, k2_ref, e1_o, n_o, r2_o, e2_o, q_s):
    q_s[...] = lax.dot_general(wq_ref[...], f_ref[...], (NT, ((), ())),
                               preferred_element_type=F32).astype(BF16)
    half = PEER_DK // 2
    T = f_ref.shape[0]
    sub = _iota((SUBLANES, T), 0)

    def stack(rows):
        out = rows[SUBLANES - 1]
        for b in range(SUBLANES - 2, -1, -1):
            out = jnp.where(sub == b, rows[b], out)
        return out

    def rep(a):
        return jnp.concatenate([a] * (N_KEYS // SUBLANES), axis=0)

    def head(h, carry):
        base = pl.multiple_of(h * PEER_DK, PEER_DK)
        s1 = jnp.dot(k1_ref[...], q_s[pl.ds(base, half), :], preferred_element_type=F32)
        s2 = jnp.dot(k2_ref[...], q_s[pl.ds(base + half, half), :], preferred_element_type=F32)
        t1 = _top16_rows(s1)
        t2 = _top16_rows(s2)
        lo = stack(t2[:SUBLANES])
        hi = stack(t2[SUBLANES:])
        top = _merge_sublanes(_merge_top([t + lo for t in t1], [t + hi for t in t1]))
        z = jnp.exp(top[0] - top[0])
        for kk in range(1, PEER_TOPK):
            z = z + jnp.exp(top[kk] - top[0])
        thr = rep(top[PEER_TOPK - 1])
        t2r = [rep(t) for t in t2]
        r2 = _count_leading(lambda t: t > s2, t2r)
        n1 = _count_leading(lambda t: s1 + t >= thr, t2r)
        r2_o[h] = _pack_rows(r2)
        n_o[h] = _dup_bf16(n1)
        e1_o[h] = _dup_bf16(jnp.exp(s1 - rep(t1[0])) / rep(z))
        e2_o[h] = _pack_rows(jnp.exp(s2 - rep(t2[0])))
        return carry

    lax.fori_loop(0, PEER_HEADS, head, 0)


def _topk(f, wq_t, k1, k2):
    rows = f.shape[0]
    T = TOPK_TILE
    full = lambda a: pl.BlockSpec(a.shape, lambda i: (0,) * a.ndim)
    big = pl.BlockSpec((PEER_HEADS, N_KEYS, T), lambda i: (0, 0, i))
    big_shape = lambda dt: jax.ShapeDtypeStruct((PEER_HEADS, N_KEYS, rows), dt)
    packed = pl.BlockSpec((PEER_HEADS, N_KEYS // 2, T), lambda i: (0, 0, i))
    packed_shape = jax.ShapeDtypeStruct((PEER_HEADS, N_KEYS // 2, rows), U32)
    return pl.pallas_call(
        _topk_body, grid=(rows // T,),
        in_specs=[pl.BlockSpec((T, D_MODEL), lambda i: (i, 0)), full(wq_t), full(k1), full(k2)],
        out_specs=[big, big, packed, packed],
        out_shape=[big_shape(U32), big_shape(U32), packed_shape, packed_shape],
        scratch_shapes=[pltpu.VMEM((PEER_HEADS * PEER_DK, T), BF16)],
        compiler_params=_cparams(1), name="peer_topk",
    )(f, wq_t, k1, k2)


def _peer_body(final, n_e, f_ref, u_ref, v_ref, e1_ref, n_ref, r2_ref, e2_ref, h_ref,
               gate_ref, fg_ref, out, st_s, at_s, acc):
    s = pl.program_id(0)
    j_up = jnp.maximum(s - 1, 0) % n_e
    cur = s % 2

    @pl.when(s == 0)
    def _():
        at_s[...] = jnp.zeros_like(at_s)

    @pl.when(j_up == 0)
    def _():
        acc[...] = jnp.zeros_like(acc)

    n_blk = 4
    e_blk = EXP_TILE // n_blk
    for e in range(n_blk):
        st_s[e * e_blk // 2:(e + 1) * e_blk // 2, :] = _pack_rows(lax.dot_general(
            u_ref[e * e_blk:(e + 1) * e_blk, :], f_ref[...], (NT, ((), ())),
            preferred_element_type=F32))
    acc[...] += lax.dot_general(_unpack_rows(at_s[1 - cur]), v_ref[...], (TN, ((), ())),
                                preferred_element_type=F32)

    blk = (N_KEYS // 2, LANES)
    group = 2
    for lg in range(TOK_TILE // LANES):
        ln = slice(lg * LANES, (lg + 1) * LANES)
        for i0 in range(0, EXP_TILE // N_KEYS, group):
            g = [None] * group
            for h in range(PEER_HEADS):
                r2_blk = _unpack_rows(r2_ref[h, :, ln])
                e2_blk = _unpack_rows(e2_ref[h, :, ln])
                for r in range(group):
                    ii = i0 + r
                    n_row = _unpack_rows(jnp.broadcast_to(n_ref[h, ii:ii + 1, ln], blk))
                    e1_row = _unpack_rows(jnp.broadcast_to(e1_ref[h, ii:ii + 1, ln], blk))
                    w = jnp.where(r2_blk < n_row, e2_blk, 0.0) * e1_row
                    g[r] = w if g[r] is None else g[r] + w
            for r in range(group):
                rows = slice((i0 + r) * N_KEYS // 2, (i0 + r + 1) * N_KEYS // 2)
                at_s[cur, rows, ln] = _pack_rows(
                    _gelu_tanh(_unpack_rows(st_s[rows, ln])) * g[r])

    @pl.when(jnp.logical_and(s >= 1, j_up == n_e - 1))
    def _():
        h = h_ref[...] + gate_ref[...] * acc[...]
        if final:
            h = h * lax.rsqrt(jnp.mean(h * h, axis=-1, keepdims=True) + EPS) * fg_ref[...]
        out[...] = h


def _peer(f, u_b, v_b, layer, e1, n1, r2, e2, h, mods, fg, n_lat, per_seq, n_tok_tiles, final):
    n_e = N_EXPERTS // EXP_TILE
    rpt = EXP_TILE // N_KEYS
    n_steps = n_tok_tiles * n_e

    def tile(s, lag):
        t = jnp.clip(s - lag, 0, n_steps - 1)
        return t // n_e, t % n_e

    tok = lambda lag: pl.BlockSpec((TOK_TILE, D_MODEL), lambda s: (tile(s, lag)[0], 0))
    exp = lambda lag: pl.BlockSpec((None, EXP_TILE, D_MODEL), lambda s: (layer, tile(s, lag)[1], 0))
    rowblk = pl.BlockSpec((PEER_HEADS, rpt, TOK_TILE), lambda s: (0, tile(s, 0)[1], tile(s, 0)[0]))
    allkeys = pl.BlockSpec((PEER_HEADS, N_KEYS // 2, TOK_TILE), lambda s: (0, 0, tile(s, 0)[0]))

    def gate_map(s):
        i = tile(s, 1)[0]
        row = jnp.where(i < n_lat, i // per_seq, 2)
        return (row * 6 + 5, 0, 0)

    return pl.pallas_call(
        functools.partial(_peer_body, final, n_e),
        grid=(n_steps + 1,),
        in_specs=[tok(0), exp(0), exp(1), rowblk, rowblk, allkeys, allkeys, tok(1),
                  pl.BlockSpec((None, 1, D_MODEL), gate_map),
                  pl.BlockSpec((1, D_MODEL), lambda s: (0, 0))],
        out_specs=tok(1),
        out_shape=jax.ShapeDtypeStruct((n_tok_tiles * TOK_TILE, D_MODEL), F32),
        scratch_shapes=[pltpu.VMEM((EXP_TILE // 2, TOK_TILE), U32),
                        pltpu.VMEM((2, EXP_TILE // 2, TOK_TILE), U32),
                        pltpu.VMEM((TOK_TILE, D_MODEL), F32)],
        compiler_params=_cparams(1), name="peer",
    )(f, u_b, v_b, e1, n1, r2, e2, h, mods, fg)


def _cast_body(u_ref, v_ref, uo, vo):
    uo[...] = u_ref[...].astype(BF16)
    vo[...] = v_ref[...].astype(BF16)


def _cast_tables(u, v):
    L, n, d = u.shape
    rows = 1024
    spec = pl.BlockSpec((None, rows, d), lambda l, i: (l, i, 0))
    shape = jax.ShapeDtypeStruct((L, n, d), BF16)
    return pl.pallas_call(
        _cast_body, grid=(L, n // rows), in_specs=[spec, spec], out_specs=[spec, spec],
        out_shape=[shape, shape], compiler_params=_cparams(2), name="cast_tables",
    )(u, v)


def _sincos_2d(t_len):
    rows = t_len // GRID_W
    q = D_MODEL // 4
    freq = 10000.0 ** (-jnp.arange(q, dtype=F32) / q)
    ar = jnp.arange(rows, dtype=F32)[:, None] * freq
    ac = jnp.arange(GRID_W, dtype=F32)[:, None] * freq
    per_row = lambda a: jnp.repeat(a, GRID_W, axis=0)
    per_col = lambda a: jnp.tile(a, (rows, 1))
    return jnp.concatenate([per_row(jnp.sin(ar)), per_row(jnp.cos(ar)),
                            per_col(jnp.sin(ac)), per_col(jnp.cos(ac))], -1)


def _row(a):
    return a.reshape(1, -1).astype(F32)


def _pad_lanes(a, width=LANES):
    return jnp.pad(a, ((0, 0), (0, width - a.shape[-1])))


def kernel(x, c, ctx, c_ctx, ada_w, ada_b, norm_mix_g, norm_ffn_g, w_in, w_out, a_conv_w, a_conv_b, a_ln_g, a_ln_b, b_ln_g, b_ln_b, b_ws, b_bs, c_conv_w, c_conv_b, c_dt_bias, c_a_log, c_d, c_norm_g, d_mu_prev, d_mu_next, d_w0, d_w2, d_a0, d_a2, d_g2, d_k_k, d_k_a, d_r_k, d_gn_g, d_gn_b, peer_wq, peer_k1, peer_k2, peer_u, peer_v, final_g):
    n_b, t_lat, _ = x.shape
    t_ctx = ctx.shape[1]
    assert n_b == 2 and t_ctx == SEQ_TILE and t_lat % TOK_TILE == 0
    n_l = t_lat // SEQ_TILE
    per_seq = t_lat // TOK_TILE
    n_lat = n_b * per_seq
    n_lat256 = n_b * n_l

    cs = jnp.zeros((MOD_ROWS, D_MODEL), F32).at[:n_b].set(c).at[n_b].set(c_ctx)
    mods_all = _ada(cs, ada_w, ada_b)
    pos = _sincos_2d(t_lat)
    u_bf, v_bf = _cast_tables(peer_u, peer_v)
    x2 = x.reshape(n_b * t_lat, D_MODEL)
    ctx2 = ctx.reshape(n_b * t_ctx, D_MODEL)

    h = None
    for i in range(DEPTH):
        last_layer = i == DEPTH - 1
        mods = mods_all[i].reshape(MOD_ROWS * 6, 1, D_MODEL)
        w = w_in[i]
        dtc = OFF_C + D_GROUP + C_XBC
        w_c = jnp.concatenate([w[:, OFF_C:dtc], _pad_lanes(w[:, dtc:dtc + HEADS_C]),
                               _pad_lanes(w[:, dtc + HEADS_C:dtc + 2 * HEADS_C])], axis=1)
        ws_in = tuple(a.astype(BF16) for a in (w[:, :OFF_B], w[:, OFF_B:OFF_C], w_c, w[:, OFF_D:]))
        g_mix = _row(norm_mix_g[i])
        if i == 0:
            h, pa, pb, pc, pd = _inproj((x2, ctx2, pos), mods, g_mix, ws_in, n_lat, per_seq, True)
        else:
            pa, pb, pc, pd = _inproj(h, mods, g_mix, ws_in, n_lat, per_seq, False)

        conv_a = jnp.pad(a_conv_w[i], ((0, 32 - CONV_A), (0, 0)))
        bsm = jnp.repeat(b_bs[i].T, D_GROUP // HEADS_B, axis=1)
        yab = _mixab(pa, pb, (conv_a, _row(a_conv_b[i]), _row(a_ln_g[i]), _row(a_ln_b[i]),
                              _row(b_ln_g[i]), _row(b_ln_b[i]), b_ws[i].astype(BF16), bsm),
                     n_lat256, n_l)

        conv_c = jnp.pad(c_conv_w[i], ((0, SUBLANES - CONV_C), (0, 0)))
        dtb = _pad_lanes(c_dt_bias[i]).reshape(2, 1, LANES)
        alog = _pad_lanes(c_a_log[i]).reshape(2, 1, LANES)
        dsk = _row(jnp.repeat(c_d[i], HEAD_DIM_C))
        yc = _mamba(pc, (conv_c, _row(c_conv_b[i]), dtb, alog, dsk, _row(c_norm_g[i])), n_b, n_l)

        yd = _rwkv(pd, (_row(d_mu_prev[i]), _row(d_mu_next[i]), d_w0[i].reshape(2, 1, D_GROUP),
                        d_w2[i].astype(BF16), d_a0[i].reshape(2, 1, D_GROUP), d_a2[i].astype(BF16),
                        d_g2[i].astype(BF16), _row(d_k_k[i]), _row(d_k_a[i]), _row(d_r_k[i]),
                        _row(d_gn_g[i]), _row(d_gn_b[i])), n_b, n_l)

        wo = w_out[i].astype(BF16)
        h, f = _outproj(yab, yc, yd, h, mods, _row(norm_ffn_g[i]),
                        (wo[:512], wo[512:768], wo[768:]), n_lat, per_seq)

        e1, n1, r2, e2 = _topk(f, peer_wq[i].T.astype(BF16), peer_k1[i].astype(BF16),
                               peer_k2[i].astype(BF16))
        n_tok = n_lat if last_layer else n_lat + 1
        h = _peer(f, u_bf, v_bf, i, e1, n1, r2, e2, h,
                  mods, _row(final_g), n_lat, per_seq, n_tok, last_layer)
    return h.reshape(n_b, t_lat, D_MODEL)
```
